```python
import jax, jax.numpy as jnp
from jax import lax
import numpy as np

D_MODEL = 1024
BATCH = 4
SEQ = 4096
DEPTH = 1
DEC_BATCH = 128
DEC_SEQ = 8
PAST_LEN = 8192
PAGE_SIZE = 128

CHUNK = 128
A_GROUPS = 4
A_WIDTH = 1024
A_GROUP_WIDTH = A_WIDTH // A_GROUPS
HEAD_DIM = 64
H_SLOT = 8
DILATION_GROUPS = ((128, 1), (512, 4), (2048, 16))
N_GROUPS_B = len(DILATION_GROUPS)
B_HEADS = H_SLOT * N_GROUPS_B
B_WIDTH = B_HEADS * HEAD_DIM
B_OUT = H_SLOT * HEAD_DIM
N_EXPERTS = 32
TOP_K = 4
D_FF = 1024
SWIGLU_ALPHA = 1.702
SWIGLU_LIMIT = 7.0
MOE_BLOCK = 128
PLE_DIM = 256
RMS_EPS = 1e-6

kernel_name = 'hybrid_gmlp_dilated_attn_moe_step'


def rmsnorm(x, g):
    xf = x.astype(jnp.float32)
    r = lax.rsqrt(jnp.mean(xf * xf, axis=-1, keepdims=True) + RMS_EPS)
    return (xf * r).astype(x.dtype) * g


def alibi_slopes():
    return jnp.exp2(-8.0 * jnp.arange(1, B_HEADS + 1, dtype=jnp.float32) / B_HEADS)


def project_inputs(x, g_mix, w_in, g_v, g_q, g_k):
    n = rmsnorm(x, g_mix)
    z = n @ w_in
    cols = [A_WIDTH, A_WIDTH, B_WIDTH, B_WIDTH, B_WIDTH, D_MODEL, D_MODEL]
    splits = np.cumsum(cols)[:-1].tolist()
    u, va, q, k, v, ga, gb = jnp.split(z, splits, axis=-1)
    lead = x.shape[:-1]
    u = jax.nn.gelu(u)
    va = rmsnorm(jax.nn.gelu(va), g_v)
    q = rmsnorm(q.reshape(*lead, B_HEADS, HEAD_DIM), g_q)
    k = rmsnorm(k.reshape(*lead, B_HEADS, HEAD_DIM), g_k)
    v = v.reshape(*lead, B_HEADS, HEAD_DIM)
    return u, va, q, k, v, jax.nn.sigmoid(ga), jax.nn.sigmoid(gb)


def spatial_gating(u, va, w_s, b_s):
    bsz, s, _ = va.shape
    vc = va.reshape(bsz, s // CHUNK, CHUNK, A_GROUPS, A_GROUP_WIDTH)
    ws = jnp.where(jnp.tril(jnp.ones((CHUNK, CHUNK), bool))[None], w_s, 0)
    mixed = jnp.einsum('gij,bcjgd->bcigd', ws, vc) + b_s.T[None, None, :, :, None]
    return u * mixed.reshape(bsz, s, A_WIDTH)


def dilated_attn_prompt(q, k, v, slopes, window, dilation):
    bsz, s, h, dh = q.shape
    band = window // dilation
    n = s // dilation
    nb = -(-n // band)
    n_pad = nb * band

    def to_sub(t):
        t = t.reshape(bsz, n, dilation, h, dh).transpose(0, 2, 1, 3, 4)
        return jnp.pad(t, ((0, 0), (0, 0), (0, n_pad - n), (0, 0), (0, 0)))

    def windows(t):
        tp = jnp.pad(t, ((0, 0), (0, 0), (band, 0), (0, 0), (0, 0)))
        tp = tp.reshape(bsz, dilation, nb + 1, band, h, dh)
        return jnp.concatenate([tp[:, :, :-1], tp[:, :, 1:]], axis=3)

    qb = to_sub(q).reshape(bsz, dilation, nb, band, h, dh)
    kw = windows(to_sub(k))
    vw = windows(to_sub(v))
    sc = jnp.einsum('brnihd,brnjhd->brnhij', qb, kw).astype(jnp.float32) * HEAD_DIM ** -0.5
    qi = jnp.arange(band)[:, None]
    kj = jnp.arange(2 * band)[None, :]
    dist = qi + band - kj
    key_sub = jnp.arange(nb)[:, None, None] * band - band + kj[None]
    valid = (dist >= 0)[None] & (dist <= band)[None] & (key_sub >= 0)
    bias = -slopes[:, None, None] * (dist * dilation).astype(jnp.float32)[None]
    sc = jnp.where(valid[None, None, :, None], sc + bias[None, None, None], -jnp.inf)
    m = jnp.max(sc, axis=-1, keepdims=True)
    e = jnp.exp(sc - m)
    den = jnp.sum(e, axis=-1, keepdims=True)
    o = jnp.einsum('brnhij,brnjhd->brnihd', (e / den).astype(v.dtype), vw)
    lse = (m + jnp.log(den))[..., 0]
    o = o.reshape(bsz, dilation, n_pad, h, dh)[:, :, :n].transpose(0, 2, 1, 3, 4).reshape(bsz, s, h, dh)
    lse = lse.transpose(0, 1, 2, 4, 3).reshape(bsz, dilation, n_pad, h)[:, :, :n]
    lse = lse.transpose(0, 2, 1, 3).reshape(bsz, s, h)
    return o, lse


def dilated_attn_sample(q, k_new, v_new, kv_buf, slopes, window, dilation):
    buf_len = kv_buf.shape[1]
    t = q.shape[1]
    band = window // dilation
    kv = jnp.concatenate([kv_buf, jnp.stack([k_new, v_new], axis=2)], axis=1)
    i = jnp.arange(t)[:, None]
    j = jnp.arange(band + 1)[None, :]
    idx = buf_len + i - j * dilation
    valid = idx >= 0
    g = kv[:, jnp.clip(idx, 0)]
    sc = jnp.einsum('bthd,btjhd->bhtj', q, g[:, :, :, 0]).astype(jnp.float32) * HEAD_DIM ** -0.5
    sc = sc - slopes[:, None, None] * (j * dilation).astype(jnp.float32)[None]
    sc = jnp.where(valid[None, None], sc, -jnp.inf)
    m = jnp.max(sc, axis=-1, keepdims=True)
    e = jnp.exp(sc - m)
    den = jnp.sum(e, axis=-1, keepdims=True)
    o = jnp.einsum('bhtj,btjhd->bthd', (e / den).astype(v_new.dtype), g[:, :, :, 1])
    lse = (m + jnp.log(den))[..., 0].transpose(0, 2, 1)
    return o, lse


def combine_groups(outs, lses):
    w = jax.nn.softmax(jnp.stack(lses, axis=0), axis=0)
    o = jnp.einsum('gbsh,gbshd->bshd', w.astype(outs[0].dtype), jnp.stack(outs, axis=0))
    return o.reshape(*o.shape[:2], B_OUT)


def moe_ffn(x, w_router, b_router, w_gate_up, b_gate_up, w_down, b_down):
    lead = x.shape[:-1]
    xt = x.reshape(-1, D_MODEL)
    n_tok = xt.shape[0]
    logits = (xt @ w_router + b_router).astype(jnp.float32)
    top_val, top_idx = lax.top_k(logits, TOP_K)
    gates = jax.nn.softmax(top_val, axis=-1)
    n_assign = n_tok * TOP_K
    expert = top_idx.reshape(n_assign)
    token = jnp.repeat(jnp.arange(n_tok, dtype=jnp.int32), TOP_K)
    order = jnp.argsort(expert)
    e_sorted = expert[order]
    tok_sorted = token[order]
    gate_sorted = gates.reshape(n_assign)[order]
    counts = jnp.bincount(expert, length=N_EXPERTS)
    starts = jnp.cumsum(counts) - counts
    pcounts = (counts + MOE_BLOCK - 1) // MOE_BLOCK * MOE_BLOCK
    pends = jnp.cumsum(pcounts)
    pstarts = pends - pcounts
    dest = pstarts[e_sorted] + jnp.arange(n_assign) - starts[e_sorted]
    n_blocks = -(-n_assign // MOE_BLOCK) + N_EXPERTS
    n_slots = n_blocks * MOE_BLOCK
    slot_tok = jnp.full((n_slots,), n_tok, jnp.int32).at[dest].set(tok_sorted)
    block_expert = jnp.minimum(
        jnp.searchsorted(pends, jnp.arange(n_blocks) * MOE_BLOCK, side='right'), N_EXPERTS - 1)
    x_pad = jnp.concatenate([xt, jnp.zeros((1, D_MODEL), xt.dtype)], axis=0)
    xb = x_pad[slot_tok].reshape(n_blocks, MOE_BLOCK, D_MODEL)

    def expert_block(args):
        xi, e = args
        hh = xi @ w_gate_up[e] + b_gate_up[e]
        h_glu = jnp.minimum(hh[:, 0::2], SWIGLU_LIMIT)
        h_lin = jnp.clip(hh[:, 1::2], -SWIGLU_LIMIT, SWIGLU_LIMIT)
        act = h_glu * jax.nn.sigmoid(SWIGLU_ALPHA * h_glu) * (h_lin + 1)
        return act @ w_down[e] + b_down[e]

    yb = lax.map(expert_block, (xb, block_expert)).reshape(n_slots, D_MODEL)
    y = yb[dest] * gate_sorted[:, None].astype(x.dtype)
    out = jax.ops.segment_sum(y, tok_sorted, num_segments=n_tok)
    return out.reshape(*lead, D_MODEL)


def finish_layer(h, out_a, o_b, ga, gb, p, w_branch_a, w_branch_b, w_out, g_moe, w_router, b_router,
                 w_gate_up, b_gate_up, w_down, b_down, g_ple, w_ple_gate, w_ple_proj):
    mix = ga * (out_a @ w_branch_a) + gb * (o_b @ w_branch_b)
    h = h + mix @ w_out
    h = h + moe_ffn(rmsnorm(h, g_moe), w_router, b_router, w_gate_up, b_gate_up, w_down, b_down)
    h = h + jax.nn.sigmoid(rmsnorm(h, g_ple) @ w_ple_gate) * (p @ w_ple_proj)
    return h


def setup_inputs(seed: int = 0) -> dict:
    key = jax.random.key(seed)
    ks = jax.random.split(key, 32)
    f32 = jnp.float32

    def nrm(k, shape, scale):
        return jax.random.normal(k, shape, f32) * scale

    def gain(k, shape):
        return 1.0 + 0.1 * jax.random.normal(k, shape, f32)

    buf = [min(w, PAST_LEN) for w, _ in DILATION_GROUPS]
    in_cols = 2 * A_WIDTH + 3 * B_WIDTH + 2 * D_MODEL
    return {
        'x_prompt': nrm(ks[0], (BATCH, SEQ, D_MODEL), 1.0),
        'x_sample': nrm(ks[1], (DEC_BATCH, DEC_SEQ, D_MODEL), 1.0),
        'cache_kv_w128': nrm(ks[2], (DEPTH, DEC_BATCH, buf[0], 2, H_SLOT, HEAD_DIM), 1.0),
        'cache_kv_w512': nrm(ks[3], (DEPTH, DEC_BATCH, buf[1], 2, H_SLOT, HEAD_DIM), 1.0),
        'cache_kv_w2048': nrm(ks[4], (DEPTH, DEC_BATCH, buf[2], 2, H_SLOT, HEAD_DIM), 1.0),
        'p_prompt': nrm(ks[5], (DEPTH, BATCH, SEQ, PLE_DIM), 1.0),
        'p_sample': nrm(ks[6], (DEPTH, DEC_BATCH, DEC_SEQ, PLE_DIM), 1.0),
        'g_mix': gain(ks[7], (DEPTH, D_MODEL)),
        'w_in': nrm(ks[8], (DEPTH, D_MODEL, in_cols), D_MODEL ** -0.5),
        'g_v': gain(ks[9], (DEPTH, A_WIDTH)),
        'g_q': gain(ks[10], (DEPTH, HEAD_DIM)),
        'g_k': gain(ks[11], (DEPTH, HEAD_DIM)),
        'w_spatial': nrm(ks[12], (DEPTH, A_GROUPS, CHUNK, CHUNK), CHUNK ** -0.5),
        'b_spatial': 1.0 + nrm(ks[13], (DEPTH, A_GROUPS, CHUNK), 0.1),
        'w_branch_a': nrm(ks[14], (DEPTH, A_WIDTH, D_MODEL), A_WIDTH ** -0.5),
        'w_branch_b': nrm(ks[15], (DEPTH, B_OUT, D_MODEL), B_OUT ** -0.5),
        'w_out': nrm(ks[16], (DEPTH, D_MODEL, D_MODEL), D_MODEL ** -0.5),
        'g_moe': gain(ks[17], (DEPTH, D_MODEL)),
        'w_router': nrm(ks[18], (DEPTH, D_MODEL, N_EXPERTS), D_MODEL ** -0.5),
        'b_router': nrm(ks[19], (DEPTH, N_EXPERTS), 0.01),
        'w_gate_up': nrm(ks[20], (DEPTH, N_EXPERTS, D_MODEL, 2 * D_FF), D_MODEL ** -0.5),
        'b_gate_up': nrm(ks[21], (DEPTH, N_EXPERTS, 2 * D_FF), 0.01),
        'w_down': nrm(ks[22], (DEPTH, N_EXPERTS, D_FF, D_MODEL), D_FF ** -0.5),
        'b_down': nrm(ks[23], (DEPTH, N_EXPERTS, D_MODEL), 0.01),
        'g_ple': gain(ks[24], (DEPTH, D_MODEL)),
        'w_ple_gate': nrm(ks[25], (DEPTH, D_MODEL, D_MODEL), D_MODEL ** -0.5),
        'w_ple_proj': nrm(ks[26], (DEPTH, PLE_DIM, D_MODEL), PLE_DIM ** -0.5),
    }


def reference(x_prompt, x_sample, cache_kv_w128, cache_kv_w512, cache_kv_w2048, p_prompt, p_sample,
              g_mix, w_in, g_v, g_q, g_k, w_spatial, b_spatial, w_branch_a, w_branch_b, w_out,
              g_moe, w_router, b_router, w_gate_up, b_gate_up, w_down, b_down,
              g_ple, w_ple_gate, w_ple_proj):
    caches = (cache_kv_w128, cache_kv_w512, cache_kv_w2048)
    slopes = alibi_slopes()
    hp, hs = x_prompt, x_sample
    kv_p = [[] for _ in DILATION_GROUPS]
    kv_s = [[] for _ in DILATION_GROUPS]
    va_s = []
    for l in range(DEPTH):
        tail = (w_branch_a[l], w_branch_b[l], w_out[l], g_moe[l], w_router[l], b_router[l],
                w_gate_up[l], b_gate_up[l], w_down[l], b_down[l], g_ple[l], w_ple_gate[l], w_ple_proj[l])
        u, va, q, k, v, ga, gb = project_inputs(hp, g_mix[l], w_in[l], g_v[l], g_q[l], g_k[l])
        out_a = spatial_gating(u, va, w_spatial[l], b_spatial[l])
        outs, lses = [], []
        for gi, (win, dil) in enumerate(DILATION_GROUPS):
            hsl = slice(gi * H_SLOT, (gi + 1) * H_SLOT)
            o, lse = dilated_attn_prompt(q[:, :, hsl], k[:, :, hsl], v[:, :, hsl], slopes[hsl], win, dil)
            outs.append(o)
            lses.append(lse)
            keep = min(win, hp.shape[1])
            kv_p[gi].append(jnp.stack([k[:, -keep:, hsl], v[:, -keep:, hsl]], axis=2))
        hp = finish_layer(hp, out_a, combine_groups(outs, lses), ga, gb, p_prompt[l], *tail)

        u, va, q, k, v, ga, gb = project_inputs(hs, g_mix[l], w_in[l], g_v[l], g_q[l], g_k[l])
        t = hs.shape[1]
        t_pad = -(-t // CHUNK) * CHUNK
        pad = ((0, 0), (0, t_pad - t), (0, 0))
        out_a = spatial_gating(jnp.pad(u, pad), jnp.pad(va, pad), w_spatial[l], b_spatial[l])[:, :t]
        va_s.append(va)
        outs, lses = [], []
        for gi, (win, dil) in enumerate(DILATION_GROUPS):
            hsl = slice(gi * H_SLOT, (gi + 1) * H_SLOT)
            o, lse = dilated_attn_sample(q[:, :, hsl], k[:, :, hsl], v[:, :, hsl], caches[gi][l],
                                         slopes[hsl], win, dil)
            outs.append(o)
            lses.append(lse)
            kv_s[gi].append(jnp.stack([k[:, :, hsl], v[:, :, hsl]], axis=2))
        hs = finish_layer(hs, out_a, combine_groups(outs, lses), ga, gb, p_sample[l], *tail)
    return (hp, hs, jnp.stack(kv_p[0]), jnp.stack(kv_p[1]), jnp.stack(kv_p[2]),
            jnp.stack(kv_s[0]), jnp.stack(kv_s[1]), jnp.stack(kv_s[2]), jnp.stack(va_s))
```

```python
import functools

import numpy as np
import jax
import jax.numpy as jnp
from jax import lax
from jax.experimental import pallas as pl
from jax.experimental.pallas import tpu as pltpu
from jax.experimental.pallas import tpu_sc as plsc

F32 = jnp.float32
BF16 = jnp.bfloat16

D_MODEL = 1024
A_WIDTH = 1024
A_GROUPS = 4
A_GROUP_WIDTH = A_WIDTH // A_GROUPS
CHUNK = 128
HEAD_DIM = 64
H_SLOT = 8
GROUP_WIDTH = H_SLOT * HEAD_DIM
DILATION_GROUPS = ((128, 1), (512, 4), (2048, 16))
N_GROUPS_B = len(DILATION_GROUPS)
B_HEADS = H_SLOT * N_GROUPS_B
B_WIDTH = B_HEADS * HEAD_DIM
N_EXPERTS = 32
TOP_K = 4
D_FF = 1024
SWIGLU_ALPHA = 1.702
SWIGLU_LIMIT = 7.0
PLE_DIM = 256
RMS_EPS = 1e-6
NEG_BIG = -1e30

LANES = 128
MXU_DIM = 256
TOKEN_TILE = 512
EXPERT_BLOCK = 256
VMEM_LIMIT = 56 * 1024 * 1024

SC_CORES = 2
SC_SUBCORES = 16
SC_WORKERS = SC_CORES * SC_SUBCORES
SC_ROWS = 32

_COL_SPLITS = np.cumsum([0, A_WIDTH, A_WIDTH, B_WIDTH, B_WIDTH, B_WIDTH, D_MODEL, D_MODEL]).tolist()


def _alibi_slopes():
    return np.exp2(-8.0 * np.arange(1, B_HEADS + 1, dtype=np.float32) / B_HEADS).astype(np.float32)


def _sigmoid(x):
    return 1.0 / (1.0 + jnp.exp(-x))


def _rms(x):
    return lax.rsqrt(jnp.mean(x * x, axis=-1, keepdims=True) + RMS_EPS)


def _dot(a, b):
    return jnp.dot(a, b, preferred_element_type=F32)


def _dot_nt(a, b):
    return lax.dot_general(a, b, (((1,), (1,)), ((), ())), preferred_element_type=F32)


def _resident(shape):
    nd = len(shape)
    return pl.BlockSpec(shape, lambda *_: (0,) * nd, pipeline_mode=pl.Buffered(1))


def _proj_kernel(x_ref, gmix_ref, w_ref, gv_ref, gq_ref, gk_ref, hsum_ref,
                 u_ref, va_ref, q_ref, k_ref, v_ref, ga_ref, gb_ref):
    x = x_ref[...]
    n = ((x * _rms(x)) * gmix_ref[...]).astype(BF16)

    def section(i):
        return _dot(n, w_ref[:, _COL_SPLITS[i]:_COL_SPLITS[i + 1]])

    u_ref[...] = jax.nn.gelu(section(0)).astype(u_ref.dtype)
    va = jax.nn.gelu(section(1))
    va_ref[...] = ((va * _rms(va)) * gv_ref[...]).astype(va_ref.dtype)

    def head_norm(z, g_ref, scale):
        parts = []
        for c in range(B_WIDTH // MXU_DIM):
            zc = z[:, c * MXU_DIM:(c + 1) * MXU_DIM]
            ss = _dot((zc * zc).astype(BF16), hsum_ref[...])
            parts.append(zc * lax.rsqrt(ss * (1.0 / HEAD_DIM) + RMS_EPS))
        return jnp.concatenate(parts, axis=1) * (g_ref[...] * scale)

    q_ref[...] = head_norm(section(2), gq_ref, HEAD_DIM ** -0.5).astype(q_ref.dtype)
    k_ref[...] = head_norm(section(3), gk_ref, 1.0).astype(k_ref.dtype)
    v_ref[...] = section(4).astype(v_ref.dtype)
    ga_ref[...] = _sigmoid(section(5)).astype(ga_ref.dtype)
    gb_ref[...] = _sigmoid(section(6)).astype(gb_ref.dtype)


def _project(x, g_mix, w_in_bf, g_v, g_q_t, g_k_t, hsum, va_dtype, q_dtype):
    n_tok = x.shape[0]
    tm = TOKEN_TILE
    row = lambda w: pl.BlockSpec((tm, w), lambda i: (i, 0))
    outs = [(A_WIDTH, BF16), (A_WIDTH, va_dtype), (B_WIDTH, q_dtype), (B_WIDTH, F32), (B_WIDTH, F32),
            (D_MODEL, BF16), (D_MODEL, BF16)]
    return pl.pallas_call(
        _proj_kernel,
        grid=(n_tok // tm,),
        in_specs=[row(D_MODEL), _resident(g_mix.shape), _resident(w_in_bf.shape), _resident(g_v.shape),
                  _resident(g_q_t.shape), _resident(g_k_t.shape), _resident(hsum.shape)],
        out_specs=[row(w) for w, _ in outs],
        out_shape=[jax.ShapeDtypeStruct((n_tok, w), dt) for w, dt in outs],
        compiler_params=pltpu.CompilerParams(dimension_semantics=("arbitrary",), vmem_limit_bytes=VMEM_LIMIT),
        name="project",
    )(x, g_mix, w_in_bf, g_v, g_q_t, g_k_t, hsum)


def _attn_prompt_kernel(q_ref, kp_ref, kc_ref, vp_ref, vc_ref, o_ref, lse_ref, *, band, dil, slopes):
    j = pl.program_id(2)
    q = q_ref[...].astype(F32)
    k = jnp.concatenate([kp_ref[...], kc_ref[...]], axis=0).astype(BF16)
    v = jnp.concatenate([vp_ref[...], vc_ref[...]], axis=0).astype(BF16)
    qi = lax.broadcasted_iota(jnp.int32, (band, 2 * band), 0)
    kj = lax.broadcasted_iota(jnp.int32, (band, 2 * band), 1)
    dist = qi + band - kj
    valid = (dist >= 0) & (dist <= band) & ((kj >= band) | (j > 0))
    distf = (dist * dil).astype(F32)
    lane_q = lax.broadcasted_iota(jnp.int32, (band, LANES), 1)
    lse_all = jnp.zeros((band, LANES), F32)
    heads_per_vreg = LANES // HEAD_DIM
    for p in range(GROUP_WIDTH // LANES):
        qp = q[:, p * LANES:(p + 1) * LANES]
        kpair = k[:, p * LANES:(p + 1) * LANES]
        vpair = v[:, p * LANES:(p + 1) * LANES]
        o_pair = jnp.zeros((band, LANES), F32)
        for hh in range(heads_per_vreg):
            h = p * heads_per_vreg + hh
            in_head = (lane_q >= hh * HEAD_DIM) & (lane_q < (hh + 1) * HEAD_DIM)
            qm = jnp.where(in_head, qp, 0.0).astype(BF16)
            s = _dot_nt(qm, kpair)
            s = jnp.where(valid, s - float(slopes[h]) * distf, NEG_BIG)
            m = jnp.max(s, axis=-1, keepdims=True)
            e = jnp.exp(s - m)
            den = jnp.sum(e, axis=-1, keepdims=True)
            prob = (e * (1.0 / den)).astype(BF16)
            o_pair = jnp.where(in_head, _dot(prob, vpair), o_pair)
            lse_all = jnp.where(lane_q == h, m + jnp.log(den), lse_all)
        o_ref[:, p * LANES:(p + 1) * LANES] = o_pair.astype(o_ref.dtype)
    lse_ref[...] = lse_all


def _attend_prompt(q, k, v, gi, bsz, seq):
    win, dil = DILATION_GROUPS[gi]
    band = win // dil
    n = seq // dil
    nb = n // band
    ngrp = B_WIDTH // GROUP_WIDTH
    view = lambda t: t.reshape(bsz, n, dil * B_WIDTH)
    cur = pl.BlockSpec((None, band, GROUP_WIDTH), lambda b, r, j: (b, j, r * ngrp + gi))
    prev = pl.BlockSpec((None, band, GROUP_WIDTH), lambda b, r, j: (b, jnp.maximum(j - 1, 0), r * ngrp + gi))
    slopes = _alibi_slopes()[gi * H_SLOT:(gi + 1) * H_SLOT]
    o, lse = pl.pallas_call(
        functools.partial(_attn_prompt_kernel, band=band, dil=dil, slopes=slopes),
        grid=(bsz, dil, nb),
        in_specs=[cur, prev, cur, prev, cur],
        out_specs=[pl.BlockSpec((None, band, GROUP_WIDTH), lambda b, r, j: (b, j, r)),
                   pl.BlockSpec((None, band, LANES), lambda b, r, j: (b, j, r))],
        out_shape=[jax.ShapeDtypeStruct((bsz, n, dil * GROUP_WIDTH), BF16),
                   jax.ShapeDtypeStruct((bsz, n, dil * LANES), F32)],
        compiler_params=pltpu.CompilerParams(dimension_semantics=("arbitrary",) * 3),
        name=f"attn_prompt_w{win}",
    )(view(q), view(k), view(k), view(v), view(v))
    return o.reshape(bsz * seq, GROUP_WIDTH), lse.reshape(bsz * seq, LANES)


def _attn_sample_kernel(q_ref, kn_ref, vn_ref, c0_ref, c1_ref, c2_ref, o_ref, *, t_new, buf_lens, slopes):
    rows = H_SLOT * t_new
    pad_new = 16
    t_shift = t_new.bit_length() - 1
    row_id = lax.broadcasted_iota(jnp.int32, (rows, GROUP_WIDTH), 0)
    col_id = lax.broadcasted_iota(jnp.int32, (rows, GROUP_WIDTH), 1)
    own_head = (row_id >> t_shift) == (col_id >> (HEAD_DIM.bit_length() - 1))
    hrow = lax.broadcasted_iota(jnp.int32, (rows, 1), 0) >> t_shift
    outs, lses = [], []
    caches = (c0_ref, c1_ref, c2_ref)
    for g, (win, dil) in enumerate(DILATION_GROUPS):
        buf_len = buf_lens[g]
        lo, hi = g * GROUP_WIDTH, (g + 1) * GROUP_WIDTH
        qg = q_ref[:, lo:hi]
        qblk = jnp.where(own_head, jnp.concatenate([qg] * H_SLOT, axis=0), 0.0).astype(BF16)
        cache = caches[g][...]
        strided = cache.ndim == 3
        if strided:
            kept = cache.shape[1]
            cache = cache.reshape(cache.shape[0] * kept, cache.shape[2])
        n_c = cache.shape[0]
        kc = cache[:, :GROUP_WIDTH].astype(BF16)
        vc = cache[:, GROUP_WIDTH:].astype(BF16)
        zpad = jnp.zeros((pad_new - t_new, GROUP_WIDTH), F32)
        kn = jnp.concatenate([kn_ref[:, lo:hi], zpad], axis=0).astype(BF16)
        vn = jnp.concatenate([vn_ref[:, lo:hi], zpad], axis=0).astype(BF16)

        slope_rows = jnp.zeros((rows, 1), F32)
        for h in range(H_SLOT):
            slope_rows = jnp.where(hrow == h, float(slopes[g * H_SLOT + h]), slope_rows)

        def masked(s, pos, extra_ok):
            i_q = lax.broadcasted_iota(jnp.int32, s.shape, 0) & (t_new - 1)
            delta = buf_len + i_q - pos
            ok = (delta >= 0) & ((delta & (dil - 1)) == 0) & (delta <= win) & extra_ok
            return jnp.where(ok, s - slope_rows * delta.astype(F32), NEG_BIG)

        lc = lax.broadcasted_iota(jnp.int32, (rows, n_c), 1)
        pos_c = (lc >> t_shift) * dil + (lc & (t_new - 1)) if strided else lc
        sc = masked(_dot_nt(qblk, kc), pos_c, lc >= 0)
        ln = lax.broadcasted_iota(jnp.int32, (rows, pad_new), 1)
        sn = masked(_dot_nt(qblk, kn), buf_len + ln, ln < t_new)
        m = jnp.maximum(jnp.max(sc, axis=-1, keepdims=True), jnp.max(sn, axis=-1, keepdims=True))
        ec = jnp.exp(sc - m)
        en = jnp.exp(sn - m)
        den = jnp.sum(ec, axis=-1, keepdims=True) + jnp.sum(en, axis=-1, keepdims=True)
        inv = 1.0 / den
        outs.append(_dot((ec * inv).astype(BF16), vc) + _dot((en * inv).astype(BF16), vn))
        lses.append(m + jnp.log(den))

    mx = functools.reduce(jnp.maximum, lses)
    ws = [jnp.exp(l - mx) for l in lses]
    wsum = functools.reduce(jnp.add, ws)
    comb = sum((w / wsum) * o for w, o in zip(ws, outs))
    comb = jnp.where(own_head, comb, 0.0)
    o_tok = sum(comb[h * t_new:(h + 1) * t_new, :] for h in range(H_SLOT))
    o_ref[...] = o_tok.astype(o_ref.dtype)


def _attend_sample(q, k_new, v_new, caches, dbatch, t_new):
    cache_views, cache_specs, buf_lens = [], [], []
    for (win, dil), c in zip(DILATION_GROUPS, caches):
        buf_len = c.shape[1]
        buf_lens.append(buf_len)
        flat = c.reshape(dbatch, buf_len, 2 * GROUP_WIDTH)
        if dil >= 2 * t_new and buf_len % dil == 0 and t_new % 8 == 0:
            cache_views.append(flat.reshape(dbatch, buf_len // dil, dil, 2 * GROUP_WIDTH))
            cache_specs.append(pl.BlockSpec((None, buf_len // dil, t_new, 2 * GROUP_WIDTH),
                                            lambda b: (b, 0, 0, 0)))
        else:
            cache_views.append(flat)
            cache_specs.append(pl.BlockSpec((None, buf_len, 2 * GROUP_WIDTH), lambda b: (b, 0, 0)))
    tok = pl.BlockSpec((None, t_new, B_WIDTH), lambda b: (b, 0, 0))
    o = pl.pallas_call(
        functools.partial(_attn_sample_kernel, t_new=t_new, buf_lens=tuple(buf_lens), slopes=_alibi_slopes()),
        grid=(dbatch,),
        in_specs=[tok, tok, tok] + cache_specs,
        out_specs=pl.BlockSpec((None, t_new, GROUP_WIDTH), lambda b: (b, 0, 0)),
        out_shape=jax.ShapeDtypeStruct((dbatch, t_new, GROUP_WIDTH), BF16),
        compiler_params=pltpu.CompilerParams(dimension_semantics=("arbitrary",), vmem_limit_bytes=VMEM_LIMIT),
        name="attn_sample",
    )(q, k_new, v_new, *cache_views)
    return o.reshape(dbatch * t_new, GROUP_WIDTH)


def _topk_route(logits):
    tm = logits.shape[0]
    lane = lax.broadcasted_iota(jnp.int32, logits.shape, 1).astype(F32)
    slot = lax.broadcasted_iota(jnp.int32, (tm, TOP_K), 1)
    idx_out = jnp.zeros((tm, TOP_K), F32)
    val_out = jnp.zeros((tm, TOP_K), F32)
    work = logits
    top = None
    for r in range(TOP_K):
        mx = jnp.max(work, axis=-1, keepdims=True)
        ix = jnp.min(jnp.where(work == mx, lane, float(N_EXPERTS)), axis=-1, keepdims=True)
        top = mx if top is None else top
        idx_out = jnp.where(slot == r, ix, idx_out)
        val_out = jnp.where(slot == r, jnp.exp(mx - top), val_out)
        work = jnp.where(lane == ix, -jnp.inf, work)
    gates = val_out / jnp.sum(val_out, axis=-1, keepdims=True)
    return idx_out.astype(jnp.int32), gates


def _finish_kernel(*refs, n_attn):
    (h_ref, u_ref, va_ref, wm_ref, bias_ref), refs = refs[:5], refs[5:]
    o_refs, refs = refs[:n_attn], refs[n_attn:]
    if n_attn > 1:
        l_refs, expand_ref, refs = refs[:n_attn], refs[n_attn], refs[n_attn + 1:]
    (ga_ref, gb_ref, wa_ref, wb_ref, wo_ref, gmoe_ref, wrh_ref, wrl_ref, br_ref,
     h1_ref, n2_ref, idx_ref, gate_ref) = refs
    tm = h_ref.shape[0]

    chunks = []
    for c in range(tm // CHUNK):
        rows = slice(c * CHUNK, (c + 1) * CHUNK)
        va_c = va_ref[rows, :].astype(BF16)
        mixed = jnp.concatenate(
            [_dot(wm_ref[g], va_c[:, g * A_GROUP_WIDTH:(g + 1) * A_GROUP_WIDTH]) for g in range(A_GROUPS)], axis=1)
        chunks.append((u_ref[rows, :].astype(F32) * (mixed + bias_ref[...])).astype(BF16))
    branch_a = _dot(jnp.concatenate(chunks, axis=0), wa_ref[...])

    if n_attn > 1:
        lses = [r[...] for r in l_refs]
        mx = functools.reduce(jnp.maximum, lses)
        ws = [jnp.exp(l - mx) for l in lses]
        wsum = functools.reduce(jnp.add, ws)
        o_b = jnp.zeros((tm, GROUP_WIDTH), F32)
        for w, o_ref in zip(ws, o_refs):
            w = w / wsum
            w_hi = w.astype(BF16)
            w_lo = (w - w_hi.astype(F32)).astype(BF16)
            w_full = _dot(w_hi, expand_ref[...]) + _dot(w_lo, expand_ref[...])
            o_b = o_b + w_full * o_ref[...].astype(F32)
        o_b = o_b.astype(BF16)
    else:
        o_b = o_refs[0][...]
    branch_b = _dot(o_b, wb_ref[...])

    mix = ga_ref[...].astype(F32) * branch_a + gb_ref[...].astype(F32) * branch_b
    h1 = h_ref[...] + _dot(mix.astype(BF16), wo_ref[...])
    h1_ref[...] = h1

    n2 = (h1 * _rms(h1)) * gmoe_ref[...]
    n2_ref[...] = n2
    n_hi = n2.astype(BF16)
    n_lo = (n2 - n_hi.astype(F32)).astype(BF16)
    logits = _dot(n_hi, wrh_ref[...]) + _dot(n_lo, wrh_ref[...]) + _dot(n_hi, wrl_ref[...]) + br_ref[...]
    idx, gates = _topk_route(logits)
    idx_ref[...] = idx
    gate_ref[...] = gates


def _finish(h, u, va, wm, bias_full, attn, ga, gb, wa, wb, wo, g_moe, wr_hi, wr_lo, b_router, expand):
    n_tok = h.shape[0]
    tm = TOKEN_TILE
    row = lambda w: pl.BlockSpec((tm, w), lambda i: (i, 0))
    n_attn = len(attn)
    args = [h, u, va, wm, bias_full]
    specs = [row(D_MODEL), row(A_WIDTH), row(A_WIDTH), _resident(wm.shape), _resident(bias_full.shape)]
    args += [o for o, _ in attn]
    specs += [row(GROUP_WIDTH)] * n_attn
    if n_attn > 1:
        args += [l for _, l in attn] + [expand]
        specs += [row(LANES)] * n_attn + [_resident(expand.shape)]
    tail = [wa, wb, wo, g_moe, wr_hi, wr_lo, b_router]
    args += [ga, gb] + tail
    specs += [row(D_MODEL), row(D_MODEL)] + [_resident(t.shape) for t in tail]
    return pl.pallas_call(
        functools.partial(_finish_kernel, n_attn=n_attn),
        grid=(n_tok // tm,),
        in_specs=specs,
        out_specs=[row(D_MODEL), row(D_MODEL), row(TOP_K), row(TOP_K)],
        out_shape=[jax.ShapeDtypeStruct((n_tok, D_MODEL), F32), jax.ShapeDtypeStruct((n_tok, D_MODEL), F32),
                   jax.ShapeDtypeStruct((n_tok, TOP_K), jnp.int32), jax.ShapeDtypeStruct((n_tok, TOP_K), F32)],
        compiler_params=pltpu.CompilerParams(dimension_semantics=("arbitrary",), vmem_limit_bytes=VMEM_LIMIT),
        name=f"finish_{n_attn}",
    )(*args)


def _sc_gather_rows(table, idx):
    m = idx.shape[0]
    width = table.shape[1]
    per_worker = m // SC_WORKERS
    n_chunks = per_worker // SC_ROWS
    mesh = plsc.VectorSubcoreMesh(core_axis_name="c", subcore_axis_name="s",
                                  num_cores=SC_CORES, num_subcores=SC_SUBCORES)

    @functools.partial(
        pl.kernel, mesh=mesh,
        out_type=jax.ShapeDtypeStruct((m, width), table.dtype),
        scratch_types=[pltpu.VMEM((SC_ROWS,), jnp.int32),
                       pltpu.VMEM((SC_ROWS, width), table.dtype),
                       pltpu.SemaphoreType.DMA],
        name="sc_gather_rows",
    )
    def gather(table_hbm, idx_hbm, out_hbm, idx_v, rows_v, sem):
        wid = lax.axis_index("s") * SC_CORES + lax.axis_index("c")
        base = wid * per_worker

        @pl.loop(0, n_chunks)
        def _(c):
            off = pl.multiple_of(base + c * SC_ROWS, SC_ROWS)
            pltpu.sync_copy(idx_hbm.at[pl.ds(off, SC_ROWS)], idx_v)
            pltpu.async_copy(table_hbm.at[idx_v], rows_v, sem).wait()
            pltpu.sync_copy(rows_v, out_hbm.at[pl.ds(off, SC_ROWS)])

    return gather(table, idx)


def _expert_kernel(be_ref, nused_ref, x_ref, wg_ref, wl_ref, wd_ref, bg_ref, bl_ref, bd_ref, y_ref):
    i = pl.program_id(0)

    @pl.when(i < nused_ref[0])
    def _():
        x = x_ref[...].astype(BF16)
        h_glu = jnp.minimum(_dot(x, wg_ref[...]) + bg_ref[...], SWIGLU_LIMIT)
        h_lin = jnp.clip(_dot(x, wl_ref[...]) + bl_ref[...], -SWIGLU_LIMIT, SWIGLU_LIMIT)
        act = h_glu * _sigmoid(SWIGLU_ALPHA * h_glu) * (h_lin + 1.0)
        y_ref[...] = _dot(act.astype(BF16), wd_ref[...]) + bd_ref[...]

    @pl.when(i >= nused_ref[0])
    def _():
        y_ref[...] = jnp.zeros_like(y_ref)


def _experts(xb, block_expert, n_used, wg, wl, wd, bg, bl, bd):
    n_slots = xb.shape[0]
    n_blocks = n_slots // EXPERT_BLOCK
    w_spec = lambda k, n: pl.BlockSpec((None, k, n), lambda i, be, nu: (be[i], 0, 0))
    blk = pl.BlockSpec((EXPERT_BLOCK, D_MODEL), lambda i, be, nu: (i, 0))
    return pl.pallas_call(
        _expert_kernel,
        grid_spec=pltpu.PrefetchScalarGridSpec(
            num_scalar_prefetch=2, grid=(n_blocks,),
            in_specs=[blk, w_spec(D_MODEL, D_FF), w_spec(D_MODEL, D_FF), w_spec(D_FF, D_MODEL),
                      w_spec(1, D_FF), w_spec(1, D_FF), w_spec(1, D_MODEL)],
            out_specs=blk),
        out_shape=jax.ShapeDtypeStruct((n_slots, D_MODEL), F32),
        compiler_params=pltpu.CompilerParams(dimension_semantics=("arbitrary",), vmem_limit_bytes=VMEM_LIMIT),
        name="experts",
    )(block_expert, n_used, xb, wg, wl, wd, bg, bl, bd)


def _final_kernel(h1_ref, yg_ref, gate_ref, p_ref, gple_ref, wg_ref, wp_ref, out_ref):
    h2 = h1_ref[...]
    gates = gate_ref[...]
    for k in range(TOP_K):
        h2 = h2 + gates[:, k:k + 1] * yg_ref[:, k * D_MODEL:(k + 1) * D_MODEL]
    n3 = ((h2 * _rms(h2)) * gple_ref[...]).astype(BF16)
    gate = _sigmoid(_dot(n3, wg_ref[...]))
    out_ref[...] = h2 + gate * _dot(p_ref[...].astype(BF16), wp_ref[...])


def _final(h1, yg, gates, p, g_ple, w_ple_gate, w_ple_proj):
    n_tok = h1.shape[0]
    tm = TOKEN_TILE
    row = lambda w: pl.BlockSpec((tm, w), lambda i: (i, 0))
    return pl.pallas_call(
        _final_kernel,
        grid=(n_tok // tm,),
        in_specs=[row(D_MODEL), row(TOP_K * D_MODEL), row(TOP_K), row(PLE_DIM),
                  _resident(g_ple.shape), _resident(w_ple_gate.shape), _resident(w_ple_proj.shape)],
        out_specs=row(D_MODEL),
        out_shape=jax.ShapeDtypeStruct((n_tok, D_MODEL), F32),
        compiler_params=pltpu.CompilerParams(dimension_semantics=("arbitrary",), vmem_limit_bytes=VMEM_LIMIT),
        name="final",
    )(h1, yg, gates, p, g_ple, w_ple_gate, w_ple_proj)


def _routing_tables(idx4, n_slots):
    n_tok = idx4.shape[0]
    sel = (idx4[:, :, None] == jnp.arange(N_EXPERTS, dtype=jnp.int32)[None, None, :]).any(axis=1).astype(jnp.int32)
    cum = jnp.cumsum(sel, axis=0)
    counts = cum[-1]
    pcounts = (counts + EXPERT_BLOCK - 1) // EXPERT_BLOCK * EXPERT_BLOCK
    pends = jnp.cumsum(pcounts)
    pstarts = pends - pcounts
    slot_of = pstarts[None, :] + cum - sel
    dest4 = jnp.take_along_axis(slot_of, idx4, axis=1).astype(jnp.int32)
    token = jnp.broadcast_to(jnp.arange(n_tok, dtype=jnp.int32)[:, None], dest4.shape)
    slot_tok = jnp.zeros((n_slots,), jnp.int32).at[dest4.reshape(-1)].set(token.reshape(-1))
    n_blocks = n_slots // EXPERT_BLOCK
    block_expert = jnp.minimum(
        jnp.searchsorted(pends, jnp.arange(n_blocks, dtype=jnp.int32) * EXPERT_BLOCK, side="right"),
        N_EXPERTS - 1).astype(jnp.int32)
    n_used = (pends[-1] // EXPERT_BLOCK).astype(jnp.int32).reshape(1)
    return dest4, slot_tok, block_expert, n_used


def kernel(x_prompt, x_sample, cache_kv_w128, cache_kv_w512, cache_kv_w2048, p_prompt, p_sample, g_mix, w_in, g_v, g_q, g_k, w_spatial, b_spatial, w_branch_a, w_branch_b, w_out, g_moe, w_router, b_router, w_gate_up, b_gate_up, w_down, b_down, g_ple, w_ple_gate, w_ple_proj):
    bsz, seq, _ = x_prompt.shape
    dbatch, t_new, _ = x_sample.shape
    depth = g_mix.shape[0]
    assert depth == 1
    caches = (cache_kv_w128, cache_kv_w512, cache_kv_w2048)
    l = 0
    n_p, n_s = bsz * seq, dbatch * t_new

    row2 = lambda t: t.reshape(1, -1)
    w_in_bf = w_in[l].astype(BF16)
    g_q_t = jnp.tile(g_q[l], B_HEADS).reshape(1, B_WIDTH)
    g_k_t = jnp.tile(g_k[l], B_HEADS).reshape(1, B_WIDTH)
    hid = np.arange(MXU_DIM) // HEAD_DIM
    hsum = jnp.asarray(hid[:, None] == hid[None, :], BF16)
    expand = jnp.asarray(np.arange(LANES)[:, None] == (np.arange(GROUP_WIDTH) // HEAD_DIM)[None, :], BF16)
    tril = jnp.tril(jnp.ones((CHUNK, CHUNK), bool))
    wm_prompt = jnp.where(tril[None], w_spatial[l], 0).astype(BF16)
    bias_prompt = jnp.repeat(b_spatial[l].T, A_GROUP_WIDTH, axis=1)
    reps = CHUNK // t_new
    small = jnp.where(tril[None, :t_new, :t_new], w_spatial[l][:, :t_new, :t_new], 0)
    wm_sample = jnp.einsum("ab,gij->gaibj", jnp.eye(reps, dtype=F32), small).reshape(A_GROUPS, CHUNK, CHUNK).astype(BF16)
    bias_sample = jnp.tile(bias_prompt[:t_new], (reps, 1))
    wa, wb, wo = w_branch_a[l].astype(BF16), w_branch_b[l].astype(BF16), w_out[l].astype(BF16)
    wr_hi = w_router[l].astype(BF16)
    wr_lo = (w_router[l] - wr_hi.astype(F32)).astype(BF16)
    wg = w_gate_up[l][:, :, 0::2].astype(BF16)
    wl = w_gate_up[l][:, :, 1::2].astype(BF16)
    wd = w_down[l].astype(BF16)
    bg = b_gate_up[l][:, None, 0::2]
    bl = b_gate_up[l][:, None, 1::2]
    bd = b_down[l][:, None, :]

    proj = functools.partial(_project, g_mix=row2(g_mix[l]), w_in_bf=w_in_bf, g_v=row2(g_v[l]),
                             g_q_t=g_q_t, g_k_t=g_k_t, hsum=hsum)
    fin = functools.partial(_finish, wa=wa, wb=wb, wo=wo, g_moe=row2(g_moe[l]), wr_hi=wr_hi, wr_lo=wr_lo,
                            b_router=row2(b_router[l]), expand=expand)

    xp = x_prompt.reshape(n_p, D_MODEL)
    u_p, va_p, q_p, k_p, v_p, ga_p, gb_p = proj(xp, va_dtype=BF16, q_dtype=BF16)
    attn_p = [_attend_prompt(q_p, k_p, v_p, gi, bsz, seq) for gi in range(N_GROUPS_B)]
    h1_p, n2_p, idx_p, gate_p = fin(xp, u_p, va_p, wm_prompt, bias_prompt, attn_p, ga_p, gb_p)

    xs = x_sample.reshape(n_s, D_MODEL)
    u_s, va_s, q_s, k_s, v_s, ga_s, gb_s = proj(xs, va_dtype=F32, q_dtype=F32)
    tok3 = lambda t: t.reshape(dbatch, t_new, B_WIDTH)
    o_s = _attend_sample(tok3(q_s), tok3(k_s), tok3(v_s), [c[l] for c in caches], dbatch, t_new)
    h1_s, n2_s, idx_s, gate_s = fin(xs, u_s, va_s, wm_sample, bias_sample, [(o_s, None)], ga_s, gb_s)

    h1 = jnp.concatenate([h1_p, h1_s], axis=0)
    n2 = jnp.concatenate([n2_p, n2_s], axis=0)
    idx4 = jnp.concatenate([idx_p, idx_s], axis=0)
    gates = jnp.concatenate([gate_p, gate_s], axis=0)
    n_tok = n_p + n_s
    n_blocks = -(-n_tok * TOP_K // EXPERT_BLOCK) + N_EXPERTS
    n_slots = n_blocks * EXPERT_BLOCK
    dest4, slot_tok, block_expert, n_used = _routing_tables(idx4, n_slots)
    xb = _sc_gather_rows(n2, slot_tok)
    yb = _experts(xb, block_expert, n_used, wg, wl, wd, bg, bl, bd)
    yg = _sc_gather_rows(yb, dest4.reshape(-1)).reshape(n_tok, TOP_K * D_MODEL)
    p_all = jnp.concatenate([p_prompt[l].reshape(n_p, PLE_DIM), p_sample[l].reshape(n_s, PLE_DIM)], axis=0)
    out = _final(h1, yg, gates, p_all, row2(g_ple[l]), w_ple_gate[l].astype(BF16), w_ple_proj[l].astype(BF16))
    y_prompt = out[:n_p].reshape(bsz, seq, D_MODEL)
    y_sample = out[n_p:].reshape(dbatch, t_new, D_MODEL)

    def kv_pack(k, v, nb, t, keep, gi):
        sl = slice(gi * GROUP_WIDTH, (gi + 1) * GROUP_WIDTH)
        k3 = k.reshape(nb, t, B_WIDTH)[:, t - keep:, sl].reshape(nb, keep, H_SLOT, HEAD_DIM)
        v3 = v.reshape(nb, t, B_WIDTH)[:, t - keep:, sl].reshape(nb, keep, H_SLOT, HEAD_DIM)
        return jnp.stack([k3, v3], axis=2)[None]

    kv_prompt = [kv_pack(k_p, v_p, bsz, seq, min(win, seq), gi) for gi, (win, _) in enumerate(DILATION_GROUPS)]
    kv_sample = [kv_pack(k_s, v_s, dbatch, t_new, t_new, gi) for gi in range(N_GROUPS_B)]
    va_out = va_s.reshape(1, dbatch, t_new, A_WIDTH)
    return (y_prompt, y_sample, *kv_prompt, *kv_sample, va_out)
```

```python
import functools

import numpy as np
import jax
import jax.numpy as jnp
from jax import lax
from jax.experimental import pallas as pl
from jax.experimental.pallas import tpu as pltpu
from jax.experimental.pallas import tpu_sc as plsc

F32 = jnp.float32
BF16 = jnp.bfloat16

D_MODEL = 1024
A_WIDTH = 1024
A_GROUPS = 4
A_GROUP_WIDTH = A_WIDTH // A_GROUPS
CHUNK = 128
HEAD_DIM = 64
H_SLOT = 8
GROUP_WIDTH = H_SLOT * HEAD_DIM
DILATION_GROUPS = ((128, 1), (512, 4), (2048, 16))
N_GROUPS_B = len(DILATION_GROUPS)
B_HEADS = H_SLOT * N_GROUPS_B
B_WIDTH = B_HEADS * HEAD_DIM
N_EXPERTS = 32
TOP_K = 4
D_FF = 1024
SWIGLU_ALPHA = 1.702
SWIGLU_LIMIT = 7.0
PLE_DIM = 256
RMS_EPS = 1e-6
NEG_BIG = -1e30

LANES = 128
SUBLANES = 8
MXU_DIM = 256
HEADS_PER_VREG = LANES // HEAD_DIM
PAIRS = GROUP_WIDTH // LANES
TOKEN_TILE = 512
EXPERT_BLOCK = 256
VMEM_LIMIT = 56 * 1024 * 1024

SC_CORES = 2
SC_SUBCORES = 16
SC_WORKERS = SC_CORES * SC_SUBCORES
SC_ROWS = 32

_COL_SPLITS = np.cumsum([0, A_WIDTH, A_WIDTH, B_WIDTH, B_WIDTH, B_WIDTH, D_MODEL, D_MODEL]).tolist()


def _alibi_slopes():
    return np.exp2(-8.0 * np.arange(1, B_HEADS + 1, dtype=np.float32) / B_HEADS).astype(np.float32)


def _sigmoid(x):
    return 1.0 / (1.0 + jnp.exp(-x))


def _rms(x):
    return lax.rsqrt(jnp.mean(x * x, axis=-1, keepdims=True) + RMS_EPS)


def _dot(a, b):
    return jnp.dot(a, b, preferred_element_type=F32)


def _dot_nt(a, b):
    return lax.dot_general(a, b, (((1,), (1,)), ((), ())), preferred_element_type=F32)


def _resident(shape):
    nd = len(shape)
    return pl.BlockSpec(shape, lambda *_: (0,) * nd, pipeline_mode=pl.Buffered(1))


def _params(n_axes=1):
    return pltpu.CompilerParams(dimension_semantics=("arbitrary",) * n_axes, vmem_limit_bytes=VMEM_LIMIT)


def _proj_kernel(x_ref, gmix_ref, w_ref, gv_ref, gq_ref, gk_ref, hsum_ref,
                 u_ref, va_ref, q_ref, k_ref, v_ref, ga_ref, gb_ref):
    x = x_ref[...]
    n = ((x * _rms(x)) * gmix_ref[...]).astype(BF16)

    def section(i):
        return _dot(n, w_ref[:, _COL_SPLITS[i]:_COL_SPLITS[i + 1]])

    u_ref[...] = jax.nn.gelu(section(0)).astype(u_ref.dtype)
    va = jax.nn.gelu(section(1))
    va_ref[...] = ((va * _rms(va)) * gv_ref[...]).astype(va_ref.dtype)

    def head_norm(z, g_ref, scale):
        parts = []
        for c in range(B_WIDTH // MXU_DIM):
            zc = z[:, c * MXU_DIM:(c + 1) * MXU_DIM]
            ss = _dot((zc * zc).astype(BF16), hsum_ref[...])
            parts.append(zc * lax.rsqrt(ss * (1.0 / HEAD_DIM) + RMS_EPS))
        return jnp.concatenate(parts, axis=1) * (g_ref[...] * scale)

    q_ref[...] = head_norm(section(2), gq_ref, HEAD_DIM ** -0.5)
    k_ref[...] = head_norm(section(3), gk_ref, 1.0)
    v_ref[...] = section(4)
    ga_ref[...] = _sigmoid(section(5)).astype(ga_ref.dtype)
    gb_ref[...] = _sigmoid(section(6)).astype(gb_ref.dtype)


def _project(x, g_mix, w_in_bf, g_v, g_q_t, g_k_t, hsum, va_dtype):
    n_tok = x.shape[0]
    tm = TOKEN_TILE
    row = lambda w: pl.BlockSpec((tm, w), lambda i: (i, 0))
    outs = [(A_WIDTH, BF16), (A_WIDTH, va_dtype), (B_WIDTH, F32), (B_WIDTH, F32), (B_WIDTH, F32),
            (D_MODEL, BF16), (D_MODEL, BF16)]
    return pl.pallas_call(
        _proj_kernel,
        grid=(n_tok // tm,),
        in_specs=[row(D_MODEL), _resident(g_mix.shape), _resident(w_in_bf.shape), _resident(g_v.shape),
                  _resident(g_q_t.shape), _resident(g_k_t.shape), _resident(hsum.shape)],
        out_specs=[row(w) for w, _ in outs],
        out_shape=[jax.ShapeDtypeStruct((n_tok, w), dt) for w, dt in outs],
        compiler_params=_params(),
        name="project",
    )(x, g_mix, w_in_bf, g_v, g_q_t, g_k_t, hsum)


def _band_mask(band, first_block):
    qi = lax.broadcasted_iota(jnp.int32, (band, 2 * band), 0)
    kj = lax.broadcasted_iota(jnp.int32, (band, 2 * band), 1)
    dist = qi + band - kj
    valid = (dist >= 0) & (dist <= band) & ((kj >= band) | jnp.logical_not(first_block))
    return valid, dist


def _pair_attention(q2, k, v, valid, distf, slopes2):
    band = q2.shape[0]
    lane = lax.broadcasted_iota(jnp.int32, (band, LANES), 1)
    o_pair = jnp.zeros((band, LANES), F32)
    l_pair = jnp.zeros((band, LANES), F32)
    for hh in range(HEADS_PER_VREG):
        in_head = (lane >= hh * HEAD_DIM) & (lane < (hh + 1) * HEAD_DIM)
        qm = jnp.where(in_head, q2, 0.0).astype(BF16)
        s = _dot_nt(qm, k)
        s = jnp.where(valid, s - slopes2[hh] * distf, NEG_BIG)
        m = jnp.max(s, axis=-1, keepdims=True)
        e = jnp.exp(s - m)
        den = jnp.sum(e, axis=-1, keepdims=True)
        prob = (e * (1.0 / den)).astype(BF16)
        o_pair = jnp.where(in_head, _dot(prob, v), o_pair)
        l_pair = jnp.where(in_head, m + jnp.log(den), l_pair)
    return o_pair, l_pair


def _attn_dense_kernel(q_ref, kp_ref, kc_ref, vp_ref, vc_ref, o_ref, l_ref, *, band, slopes):
    valid, dist = _band_mask(band, pl.program_id(1) == 0)
    distf = dist.astype(F32)
    for p in range(PAIRS):
        cols = slice(p * LANES, (p + 1) * LANES)
        k = jnp.concatenate([kp_ref[:, cols], kc_ref[:, cols]], axis=0).astype(BF16)
        v = jnp.concatenate([vp_ref[:, cols], vc_ref[:, cols]], axis=0).astype(BF16)
        sl = [float(slopes[p * HEADS_PER_VREG + hh]) for hh in range(HEADS_PER_VREG)]
        o_pair, l_pair = _pair_attention(q_ref[:, cols], k, v, valid, distf, sl)
        o_ref[:, cols] = o_pair.astype(o_ref.dtype)
        l_ref[:, cols] = l_pair


def _attn_dilated_kernel(slopes_ref, q_ref, kp_ref, kc_ref, vp_ref, vc_ref, o_ref, l_ref, *, band, dil):
    p = pl.program_id(2)
    valid, dist = _band_mask(band, pl.program_id(1) == 0)
    distf = (dist * dil).astype(F32)
    sl = [slopes_ref[p * HEADS_PER_VREG + hh] for hh in range(HEADS_PER_VREG)]

    def residue(r, carry):
        rows = pl.ds(r, band, stride=dil)
        k = jnp.concatenate([kp_ref[rows, :], kc_ref[rows, :]], axis=0).astype(BF16)
        v = jnp.concatenate([vp_ref[rows, :], vc_ref[rows, :]], axis=0).astype(BF16)
        o_pair, l_pair = _pair_attention(q_ref[rows, :], k, v, valid, distf, sl)
        o_ref[rows, :] = o_pair
        l_ref[rows, :] = l_pair
        return carry

    lax.fori_loop(0, dil, residue, 0)


def _attend_prompt(q, k, v, gi, bsz, seq):
    win, dil = DILATION_GROUPS[gi]
    band = win // dil
    slopes = _alibi_slopes()[gi * H_SLOT:(gi + 1) * H_SLOT]
    if dil == 1:
        cur = pl.BlockSpec((None, band, GROUP_WIDTH), lambda b, j: (b, j, gi))
        prev = pl.BlockSpec((None, band, GROUP_WIDTH), lambda b, j: (b, jnp.maximum(j - 1, 0), gi))
        out = pl.BlockSpec((None, band, GROUP_WIDTH), lambda b, j: (b, j, 0))
        o, lse = pl.pallas_call(
            functools.partial(_attn_dense_kernel, band=band, slopes=slopes),
            grid=(bsz, seq // band),
            in_specs=[cur, prev, cur, prev, cur],
            out_specs=[out, out],
            out_shape=[jax.ShapeDtypeStruct((bsz, seq, GROUP_WIDTH), BF16),
                       jax.ShapeDtypeStruct((bsz, seq, GROUP_WIDTH), F32)],
            compiler_params=_params(2),
            name=f"attn_prompt_w{win}",
        )(q, k, k, v, v)
    else:
        col = lambda p: gi * PAIRS + p
        cur = pl.BlockSpec((None, win, LANES), lambda b, j, p, s: (b, j, col(p)))
        prev = pl.BlockSpec((None, win, LANES), lambda b, j, p, s: (b, jnp.maximum(j - 1, 0), col(p)))
        out = pl.BlockSpec((None, win, LANES), lambda b, j, p, s: (b, j, p))
        o, lse = pl.pallas_call(
            functools.partial(_attn_dilated_kernel, band=band, dil=dil),
            grid_spec=pltpu.PrefetchScalarGridSpec(
                num_scalar_prefetch=1, grid=(bsz, seq // win, PAIRS),
                in_specs=[cur, prev, cur, prev, cur], out_specs=[out, out]),
            out_shape=[jax.ShapeDtypeStruct((bsz, seq, GROUP_WIDTH), F32)] * 2,
            compiler_params=_params(3),
            name=f"attn_prompt_w{win}",
        )(jnp.asarray(slopes), q, k, k, v, v)
    return o.reshape(bsz * seq, GROUP_WIDTH), lse.reshape(bsz * seq, GROUP_WIDTH)


def _attn_sample_kernel(q_ref, kn_ref, vn_ref, c0_ref, c1_ref, c2_ref, o_ref, *, t_new, slopes):
    rows = HEADS_PER_VREG * t_new
    pad_new = 16
    caches = (c0_ref, c1_ref, c2_ref)
    lane = lax.broadcasted_iota(jnp.int32, (t_new, LANES), 1)
    row16 = lax.broadcasted_iota(jnp.int32, (rows, 1), 0)
    first_head = row16 < t_new

    masks = []
    for g, (win, dil) in enumerate(DILATION_GROUPS):
        buf_len = caches[g].shape[-1]

        def mask_of(n_keys, first_pos, n_real):
            i_q = lax.broadcasted_iota(jnp.int32, (rows, n_keys), 0) & (t_new - 1)
            key = lax.broadcasted_iota(jnp.int32, (rows, n_keys), 1)
            delta = buf_len + i_q - (first_pos + key)
            ok = (delta >= 0) & ((delta & (dil - 1)) == 0) & (delta <= win) & (key < n_real)
            return ok, delta.astype(F32)

        masks.append((mask_of(buf_len, 0, buf_len), mask_of(pad_new, buf_len, t_new)))

    for p in range(PAIRS):
        outs, lses = [], []
        for g in range(N_GROUPS_B):
            cols = slice(g * GROUP_WIDTH + p * LANES, g * GROUP_WIDTH + (p + 1) * LANES)
            buf_len = caches[g].shape[-1]
            heads = slice(p * HEADS_PER_VREG, (p + 1) * HEADS_PER_VREG)
            kt = caches[g][0, heads].reshape(LANES, buf_len).astype(BF16)
            vt = caches[g][1, heads].reshape(LANES, buf_len).astype(BF16)
            qp = q_ref[:, cols]
            q2 = jnp.concatenate([jnp.where(lane < HEAD_DIM, qp, 0.0), jnp.where(lane >= HEAD_DIM, qp, 0.0)],
                                 axis=0).astype(BF16)
            zpad = jnp.zeros((pad_new - t_new, LANES), F32)
            kn = jnp.concatenate([kn_ref[:, cols], zpad], axis=0).astype(BF16)
            vn = jnp.concatenate([vn_ref[:, cols], zpad], axis=0).astype(BF16)
            h0 = g * H_SLOT + p * HEADS_PER_VREG
            slope_rows = jnp.where(first_head, float(slopes[h0]), float(slopes[h0 + 1]))
            (ok_c, delta_c), (ok_n, delta_n) = masks[g]
            sc = jnp.where(ok_c, _dot(q2, kt) - slope_rows * delta_c, NEG_BIG)
            sn = jnp.where(ok_n, _dot_nt(q2, kn) - slope_rows * delta_n, NEG_BIG)
            m = jnp.maximum(jnp.max(sc, axis=-1, keepdims=True), jnp.max(sn, axis=-1, keepdims=True))
            ec = jnp.exp(sc - m)
            en = jnp.exp(sn - m)
            den = jnp.sum(ec, axis=-1, keepdims=True) + jnp.sum(en, axis=-1, keepdims=True)
            inv = 1.0 / den
            outs.append(_dot_nt((ec * inv).astype(BF16), vt) + _dot((en * inv).astype(BF16), vn))
            lses.append(m + jnp.log(den))
        mx = functools.reduce(jnp.maximum, lses)
        ws = [jnp.exp(l - mx) for l in lses]
        wsum = functools.reduce(jnp.add, ws)
        comb = sum((w / wsum) * o for w, o in zip(ws, outs))
        o_pair = jnp.where(lane < HEAD_DIM, comb[:t_new], comb[t_new:])
        o_ref[:, p * LANES:(p + 1) * LANES] = o_pair.astype(o_ref.dtype)


def _attend_sample(q, k_new, v_new, caches_t, dbatch, t_new):
    assert t_new & (t_new - 1) == 0 and t_new <= 8
    tok = pl.BlockSpec((None, t_new, B_WIDTH), lambda b: (b, 0, 0))
    cache_specs = [pl.BlockSpec((None,) + c.shape[1:], lambda b: (b, 0, 0, 0, 0)) for c in caches_t]
    o = pl.pallas_call(
        functools.partial(_attn_sample_kernel, t_new=t_new, slopes=_alibi_slopes()),
        grid=(dbatch,),
        in_specs=[tok, tok, tok] + cache_specs,
        out_specs=pl.BlockSpec((None, t_new, GROUP_WIDTH), lambda b: (b, 0, 0)),
        out_shape=jax.ShapeDtypeStruct((dbatch, t_new, GROUP_WIDTH), BF16),
        compiler_params=_params(),
        name="attn_sample",
    )(q, k_new, v_new, *caches_t)
    return o.reshape(dbatch * t_new, GROUP_WIDTH)


def _kv_tail_kernel(k_ref, v_ref, o_ref):
    o_ref[0] = k_ref[...].T
    o_ref[1] = v_ref[...].T


def _kv_tail(k, v, gi, bsz, seq):
    keep = min(DILATION_GROUPS[gi][0], seq)
    pb = min(keep, 512)
    first = (seq - keep) // pb
    src = pl.BlockSpec((None, pb, GROUP_WIDTH), lambda b, j: (b, first + j, gi))
    return pl.pallas_call(
        _kv_tail_kernel,
        grid=(bsz, keep // pb),
        in_specs=[src, src],
        out_specs=pl.BlockSpec((None, 2, GROUP_WIDTH, pb), lambda b, j: (b, 0, 0, j)),
        out_shape=jax.ShapeDtypeStruct((bsz, 2, GROUP_WIDTH, keep), F32),
        compiler_params=_params(2),
        name=f"kv_tail_w{DILATION_GROUPS[gi][0]}",
    )(k, v)


def _kv_sample_kernel(k_ref, v_ref, o_ref, *, t_new, dbatch):
    for t in range(t_new):
        rows = pl.ds(t, dbatch, stride=t_new)
        o_ref[t, 0] = k_ref[rows, :].T
        o_ref[t, 1] = v_ref[rows, :].T


def _kv_sample(k, v, dbatch, t_new):
    src = pl.BlockSpec((dbatch * t_new, LANES), lambda s: (0, s))
    return pl.pallas_call(
        functools.partial(_kv_sample_kernel, t_new=t_new, dbatch=dbatch),
        grid=(B_WIDTH // LANES,),
        in_specs=[src, src],
        out_specs=pl.BlockSpec((None, t_new, 2, LANES, dbatch), lambda s: (s // PAIRS, 0, 0, s % PAIRS, 0)),
        out_shape=jax.ShapeDtypeStruct((N_GROUPS_B, t_new, 2, GROUP_WIDTH, dbatch), F32),
        compiler_params=_params(),
        name="kv_sample",
    )(k, v)


def _topk_route(logits):
    tm = logits.shape[0]
    lane = lax.broadcasted_iota(jnp.int32, logits.shape, 1).astype(F32)
    slot = lax.broadcasted_iota(jnp.int32, (tm, TOP_K), 1)
    val_out = jnp.zeros((tm, TOP_K), F32)
    work = logits
    ids, top = [], None
    for r in range(TOP_K):
        mx = jnp.max(work, axis=-1, keepdims=True)
        ix = jnp.min(jnp.where(work == mx, lane, float(N_EXPERTS)), axis=-1, keepdims=True)
        top = mx if top is None else top
        ids.append(ix)
        val_out = jnp.where(slot == r, jnp.exp(mx - top), val_out)
        work = jnp.where(lane == ix, -jnp.inf, work)
    gates = val_out / jnp.sum(val_out, axis=-1, keepdims=True)
    return ids, gates


def _finish_kernel(*refs, n_attn, n_alias, n_valid):
    n_in = 5 + n_attn + (n_attn if n_attn > 1 else 0) + 11
    ins, outs = refs[:n_in], refs[n_in + n_alias:]
    step = pl.program_id(0)

    @pl.when(step < n_valid)
    def _():
        _finish_tile(ins, outs, n_attn)

    @pl.when(step >= n_valid)
    def _():
        for ref in outs[:5]:
            ref[...] = jnp.zeros_like(ref)


def _finish_tile(ins, outs, n_attn):
    (h_ref, u_ref, va_ref, wm_ref, bias_ref), refs = ins[:5], ins[5:]
    o_refs, refs = refs[:n_attn], refs[n_attn:]
    n_lse = n_attn if n_attn > 1 else 0
    l_refs, refs = refs[:n_lse], refs[n_lse:]
    ga_ref, gb_ref, wa_ref, wb_ref, wo_ref, gmoe_ref, wrh_ref, wrl_ref, br_ref, tri_ref, cnt_in_ref = refs
    h1_ref, n2_ref, idx_ref, gate_ref, pos_ref, cnt_out_ref, carry_ref = outs
    tm = h_ref.shape[0]

    chunks = []
    for c in range(tm // CHUNK):
        rows = slice(c * CHUNK, (c + 1) * CHUNK)
        va_c = va_ref[rows, :].astype(BF16)
        mixed = jnp.concatenate(
            [_dot(wm_ref[g], va_c[:, g * A_GROUP_WIDTH:(g + 1) * A_GROUP_WIDTH]) for g in range(A_GROUPS)], axis=1)
        chunks.append((u_ref[rows, :].astype(F32) * (mixed + bias_ref[...])).astype(BF16))
    branch_a = _dot(jnp.concatenate(chunks, axis=0), wa_ref[...])

    if n_attn > 1:
        lses = [r[...] for r in l_refs]
        mx = functools.reduce(jnp.maximum, lses)
        ws = [jnp.exp(l - mx) for l in lses]
        wsum = functools.reduce(jnp.add, ws)
        o_b = (sum(w * o_ref[...].astype(F32) for w, o_ref in zip(ws, o_refs)) / wsum).astype(BF16)
    else:
        o_b = o_refs[0][...]
    branch_b = _dot(o_b, wb_ref[...])

    mix = ga_ref[...].astype(F32) * branch_a + gb_ref[...].astype(F32) * branch_b
    h1 = h_ref[...] + _dot(mix.astype(BF16), wo_ref[...])
    h1_ref[...] = h1

    n2 = (h1 * _rms(h1)) * gmoe_ref[...]
    n2_ref[...] = n2
    n_hi = n2.astype(BF16)
    n_lo = (n2 - n_hi.astype(F32)).astype(BF16)
    logits = _dot(n_hi, wrh_ref[...]) + _dot(n_lo, wrh_ref[...]) + _dot(n_hi, wrl_ref[...]) + br_ref[...]
    ids, gates = _topk_route(logits)
    gate_ref[...] = gates

    @pl.when(pl.program_id(0) == 0)
    def _():
        carry_ref[...] = cnt_in_ref[...]

    lane = lax.broadcasted_iota(jnp.int32, logits.shape, 1).astype(F32)
    slot = lax.broadcasted_iota(jnp.int32, (tm, TOP_K), 1)
    sel = sum(jnp.where(lane == ix, 1.0, 0.0) for ix in ids)
    before = _dot(tri_ref[...], sel.astype(BF16)) + carry_ref[...]
    idx_out = jnp.zeros((tm, TOP_K), F32)
    pos_out = jnp.zeros((tm, TOP_K), F32)
    for r, ix in enumerate(ids):
        idx_out = jnp.where(slot == r, ix, idx_out)
        rank = jnp.sum(jnp.where(lane == ix, before, 0.0), axis=-1, keepdims=True)
        pos_out = jnp.where(slot == r, rank, pos_out)
    idx_ref[...] = idx_out.astype(jnp.int32)
    pos_ref[...] = pos_out.astype(jnp.int32)
    carry_ref[...] = carry_ref[...] + jnp.sum(sel, axis=0, keepdims=True)
    cnt_out_ref[...] = carry_ref[...]


def _finish(h, u, va, wm, bias_full, attn, ga, gb, consts, cnt_in, n_total, tile_offset, prev_outs):
    n_tok = h.shape[0]
    tm = TOKEN_TILE
    n_valid = n_tok // tm
    n_steps = n_valid if prev_outs is not None else n_total // tm - tile_offset
    row = lambda w: pl.BlockSpec((tm, w), lambda i: (jnp.minimum(i, n_valid - 1), 0))
    orow = lambda w: pl.BlockSpec((tm, w), lambda i: (i + tile_offset, 0))
    n_attn = len(attn)
    args = [h, u, va, wm, bias_full] + [o for o, _ in attn]
    specs = [row(D_MODEL), row(A_WIDTH), row(A_WIDTH), _resident(wm.shape), _resident(bias_full.shape)]
    specs += [row(GROUP_WIDTH)] * n_attn
    if n_attn > 1:
        args += [l for _, l in attn]
        specs += [row(GROUP_WIDTH)] * n_attn
    args += [ga, gb] + list(consts) + [cnt_in]
    specs += [row(D_MODEL), row(D_MODEL)] + [_resident(t.shape) for t in consts] + [_resident(cnt_in.shape)]
    aliases = {}
    if prev_outs is not None:
        for k, t in enumerate(prev_outs):
            aliases[len(args)] = k
            args.append(t)
            specs.append(pl.BlockSpec(memory_space=pl.ANY))
    widths = [(D_MODEL, F32), (D_MODEL, F32), (TOP_K, jnp.int32), (TOP_K, F32), (TOP_K, jnp.int32)]
    outs = pl.pallas_call(
        functools.partial(_finish_kernel, n_attn=n_attn, n_alias=len(aliases), n_valid=n_valid),
        grid=(n_steps,),
        in_specs=specs,
        out_specs=[orow(w) for w, _ in widths] + [_resident(cnt_in.shape)],
        out_shape=[jax.ShapeDtypeStruct((n_total, w), dt) for w, dt in widths]
                  + [jax.ShapeDtypeStruct(cnt_in.shape, F32)],
        scratch_shapes=[pltpu.VMEM(cnt_in.shape, F32)],
        input_output_aliases=aliases,
        compiler_params=_params(),
        name=f"finish_{n_attn}",
    )(*args)
    return outs[:5], outs[5]


def _sc_gather_rows(table, idx):
    m = idx.shape[0]
    width = table.shape[1]
    per_worker = m // SC_WORKERS
    n_chunks = per_worker // SC_ROWS
    mesh = plsc.VectorSubcoreMesh(core_axis_name="c", subcore_axis_name="s",
                                  num_cores=SC_CORES, num_subcores=SC_SUBCORES)

    @functools.partial(
        pl.kernel, mesh=mesh,
        out_type=jax.ShapeDtypeStruct((m, width), table.dtype),
        scratch_types=[pltpu.VMEM((SC_ROWS,), jnp.int32),
                       pltpu.VMEM((SC_ROWS, width), table.dtype),
                       pltpu.SemaphoreType.DMA],
        name="sc_gather_rows",
    )
    def gather(table_hbm, idx_hbm, out_hbm, idx_v, rows_v, sem):
        wid = lax.axis_index("s") * SC_CORES + lax.axis_index("c")
        base = wid * per_worker

        @pl.loop(0, n_chunks)
        def _(c):
            off = pl.multiple_of(base + c * SC_ROWS, SC_ROWS)
            pltpu.sync_copy(idx_hbm.at[pl.ds(off, SC_ROWS)], idx_v)
            pltpu.async_copy(table_hbm.at[idx_v], rows_v, sem).wait()
            pltpu.sync_copy(rows_v, out_hbm.at[pl.ds(off, SC_ROWS)])

    return gather(table, idx)


def _expert_kernel(be_ref, nused_ref, x_ref, wgu_ref, wd_ref, bgu_ref, bd_ref, perm_ref, y_ref, wgu_s, wd_s):
    i = pl.program_id(0)
    active = i < nused_ref[0]
    fresh = (i == 0) | (be_ref[i] != be_ref[jnp.maximum(i - 1, 0)])

    @pl.when(active & fresh)
    def _():
        wgu_s[...] = wgu_ref[...].astype(BF16)
        wd_s[...] = _dot(perm_ref[...], wd_ref[...].astype(BF16)).astype(BF16)

    @pl.when(active)
    def _():
        x = x_ref[...].astype(BF16)
        hh = _dot(x, wgu_s[...]) + bgu_ref[...]
        partner = pltpu.roll(hh, 2 * D_FF - 1, axis=1)
        h_glu = jnp.minimum(hh, SWIGLU_LIMIT)
        h_lin = jnp.clip(partner, -SWIGLU_LIMIT, SWIGLU_LIMIT)
        act = h_glu * _sigmoid(SWIGLU_ALPHA * h_glu) * (h_lin + 1.0)
        lane = lax.broadcasted_iota(jnp.int32, act.shape, 1)
        act = jnp.where((lane & 1) == 0, act, 0.0)
        merged = act[:, :D_FF] + pltpu.roll(act[:, D_FF:], 1, axis=1)
        y_ref[...] = _dot(merged.astype(BF16), wd_s[...]) + bd_ref[...]

    @pl.when(jnp.logical_not(active))
    def _():
        y_ref[...] = jnp.zeros_like(y_ref)


def _experts(xb, block_expert, n_used, w_gate_up, w_down, b_gate_up, b_down, perm):
    n_slots = xb.shape[0]
    n_blocks = n_slots // EXPERT_BLOCK
    w_spec = lambda k, n: pl.BlockSpec((None, k, n), lambda i, be, nu: (be[i], 0, 0))
    blk = pl.BlockSpec((EXPERT_BLOCK, D_MODEL), lambda i, be, nu: (i, 0))
    return pl.pallas_call(
        _expert_kernel,
        grid_spec=pltpu.PrefetchScalarGridSpec(
            num_scalar_prefetch=2, grid=(n_blocks,),
            in_specs=[blk, w_spec(D_MODEL, 2 * D_FF), w_spec(D_FF, D_MODEL), w_spec(1, 2 * D_FF), w_spec(1, D_MODEL),
                      pl.BlockSpec(perm.shape, lambda i, be, nu: (0, 0), pipeline_mode=pl.Buffered(1))],
            out_specs=blk,
            scratch_shapes=[pltpu.VMEM((D_MODEL, 2 * D_FF), BF16), pltpu.VMEM((D_FF, D_MODEL), BF16)]),
        out_shape=jax.ShapeDtypeStruct((n_slots, D_MODEL), F32),
        compiler_params=_params(),
        name="experts",
    )(block_expert, n_used, xb, w_gate_up, w_down, b_gate_up, b_down, perm)


def _final_kernel(h1_ref, yg_ref, gate_ref, p_ref, gple_ref, wg_ref, wp_ref, out_ref):
    h2 = h1_ref[...]
    gates = gate_ref[...]
    for k in range(TOP_K):
        h2 = h2 + gates[:, k:k + 1] * yg_ref[k]
    n3 = ((h2 * _rms(h2)) * gple_ref[...]).astype(BF16)
    gate = _sigmoid(_dot(n3, wg_ref[...]))
    out_ref[...] = h2 + gate * _dot(p_ref[...].astype(BF16), wp_ref[...])


def _final(h1, yg, gates, p, tile_offset, g_ple, w_ple_gate, w_ple_proj):
    n_tok = p.shape[0]
    tm = TOKEN_TILE
    return pl.pallas_call(
        _final_kernel,
        grid=(n_tok // tm,),
        in_specs=[pl.BlockSpec((tm, D_MODEL), lambda i: (i + tile_offset, 0)),
                  pl.BlockSpec((TOP_K, tm, D_MODEL), lambda i: (0, i + tile_offset, 0)),
                  pl.BlockSpec((tm, TOP_K), lambda i: (i + tile_offset, 0)),
                  pl.BlockSpec((tm, PLE_DIM), lambda i: (i, 0)),
                  _resident(g_ple.shape), _resident(w_ple_gate.shape), _resident(w_ple_proj.shape)],
        out_specs=pl.BlockSpec((tm, D_MODEL), lambda i: (i, 0)),
        out_shape=jax.ShapeDtypeStruct((n_tok, D_MODEL), F32),
        compiler_params=_params(),
        name="final",
    )(h1, yg, gates, p, g_ple, w_ple_gate, w_ple_proj)


def _routing_tables(idx4, pos4, counts, n_slots):
    n_tok = idx4.shape[0]
    counts = counts.reshape(N_EXPERTS).astype(jnp.int32)
    pcounts = (counts + EXPERT_BLOCK - 1) // EXPERT_BLOCK * EXPERT_BLOCK
    pends = jnp.cumsum(pcounts)
    pstarts = pends - pcounts
    experts = jnp.arange(N_EXPERTS, dtype=jnp.int32)
    start4 = jnp.sum(jnp.where(idx4[:, :, None] == experts, pstarts, 0), axis=-1)
    dest4 = (start4 + pos4).astype(jnp.int32)
    token = jnp.broadcast_to(jnp.arange(n_tok, dtype=jnp.int32)[:, None], dest4.shape)
    slot_tok = jnp.zeros((n_slots,), jnp.int32).at[dest4.reshape(-1)].set(token.reshape(-1))
    n_blocks = n_slots // EXPERT_BLOCK
    block_start = jnp.arange(n_blocks, dtype=jnp.int32) * EXPERT_BLOCK
    block_expert = jnp.minimum(jnp.sum(block_start[:, None] >= pends[None, :], axis=1), N_EXPERTS - 1).astype(jnp.int32)
    n_used = (pends[-1] // EXPERT_BLOCK).astype(jnp.int32).reshape(1)
    return dest4, slot_tok, block_expert, n_used


def kernel(x_prompt, x_sample, cache_kv_w128, cache_kv_w512, cache_kv_w2048, p_prompt, p_sample, g_mix, w_in, g_v, g_q, g_k, w_spatial, b_spatial, w_branch_a, w_branch_b, w_out, g_moe, w_router, b_router, w_gate_up, b_gate_up, w_down, b_down, g_ple, w_ple_gate, w_ple_proj):
    bsz, seq, _ = x_prompt.shape
    dbatch, t_new, _ = x_sample.shape
    assert g_mix.shape[0] == 1
    caches = (cache_kv_w128, cache_kv_w512, cache_kv_w2048)
    l = 0
    n_p, n_s = bsz * seq, dbatch * t_new
    n_tok = n_p + n_s
    assert n_p % TOKEN_TILE == 0 and n_s % TOKEN_TILE == 0

    row2 = lambda t: t.reshape(1, -1)
    w_in_bf = w_in[l].astype(BF16)
    g_q_t = jnp.tile(g_q[l], B_HEADS).reshape(1, B_WIDTH)
    g_k_t = jnp.tile(g_k[l], B_HEADS).reshape(1, B_WIDTH)
    hid = np.arange(MXU_DIM) // HEAD_DIM
    hsum = jnp.asarray(hid[:, None] == hid[None, :], BF16)
    tril = jnp.tril(jnp.ones((CHUNK, CHUNK), bool))
    wm_prompt = jnp.where(tril[None], w_spatial[l], 0).astype(BF16)
    bias_prompt = jnp.repeat(b_spatial[l].T, A_GROUP_WIDTH, axis=1)
    reps = CHUNK // t_new
    small = jnp.where(tril[None, :t_new, :t_new], w_spatial[l][:, :t_new, :t_new], 0)
    wm_sample = jnp.einsum("ab,gij->gaibj", jnp.eye(reps, dtype=F32), small).reshape(A_GROUPS, CHUNK, CHUNK).astype(BF16)
    bias_sample = jnp.tile(bias_prompt[:t_new], (reps, 1))
    wr_hi = w_router[l].astype(BF16)
    wr_lo = (w_router[l] - wr_hi.astype(F32)).astype(BF16)
    tri = jnp.asarray(np.tril(np.ones((TOKEN_TILE, TOKEN_TILE), np.float32), -1), BF16)
    consts = (w_branch_a[l].astype(BF16), w_branch_b[l].astype(BF16), w_out[l].astype(BF16), row2(g_moe[l]),
              wr_hi, wr_lo, row2(b_router[l]), tri)
    half = D_FF // 2
    perm_np = np.zeros((D_FF, D_FF), np.float32)
    perm_np[2 * np.arange(half), np.arange(half)] = 1.0
    perm_np[2 * np.arange(half) + 1, half + np.arange(half)] = 1.0
    perm = jnp.asarray(perm_np, BF16)

    proj = functools.partial(_project, g_mix=row2(g_mix[l]), w_in_bf=w_in_bf, g_v=row2(g_v[l]),
                             g_q_t=g_q_t, g_k_t=g_k_t, hsum=hsum)

    xp = x_prompt.reshape(n_p, D_MODEL)
    u_p, va_p, q_p, k_p, v_p, ga_p, gb_p = proj(xp, va_dtype=BF16)
    seq3 = lambda t: t.reshape(bsz, seq, B_WIDTH)
    attn_p = [_attend_prompt(seq3(q_p), seq3(k_p), seq3(v_p), gi, bsz, seq) for gi in range(N_GROUPS_B)]
    zero_counts = jnp.zeros((1, N_EXPERTS), F32)
    outs_p, cnt_p = _finish(xp, u_p, va_p, wm_prompt, bias_prompt, attn_p, ga_p, gb_p, consts, zero_counts,
                            n_tok, 0, None)

    xs = x_sample.reshape(n_s, D_MODEL)
    u_s, va_s, q_s, k_s, v_s, ga_s, gb_s = proj(xs, va_dtype=F32)
    tok3 = lambda t: t.reshape(dbatch, t_new, B_WIDTH)
    caches_t = [jnp.transpose(c[l], (0, 2, 3, 4, 1)) for c in caches]
    o_s = _attend_sample(tok3(q_s), tok3(k_s), tok3(v_s), caches_t, dbatch, t_new)
    (h1, n2, idx4, gates, pos4), counts = _finish(xs, u_s, va_s, wm_sample, bias_sample, [(o_s, None)], ga_s, gb_s,
                                                  consts, cnt_p, n_tok, n_p // TOKEN_TILE, outs_p)

    n_blocks = -(-n_tok * TOP_K // EXPERT_BLOCK) + N_EXPERTS
    n_slots = n_blocks * EXPERT_BLOCK
    dest4, slot_tok, block_expert, n_used = _routing_tables(idx4, pos4, counts, n_slots)
    xb = _sc_gather_rows(n2, slot_tok)
    yb = _experts(xb, block_expert, n_used, w_gate_up[l], w_down[l], b_gate_up[l][:, None, :], b_down[l][:, None, :], perm)
    yg = _sc_gather_rows(yb, dest4.T.reshape(-1)).reshape(TOP_K, n_tok, D_MODEL)
    fin = functools.partial(_final, g_ple=row2(g_ple[l]), w_ple_gate=w_ple_gate[l].astype(BF16),
                            w_ple_proj=w_ple_proj[l].astype(BF16))
    y_prompt = fin(h1, yg, gates, p_prompt[l].reshape(n_p, PLE_DIM), 0).reshape(bsz, seq, D_MODEL)
    y_sample = fin(h1, yg, gates, p_sample[l].reshape(n_s, PLE_DIM), n_p // TOKEN_TILE).reshape(dbatch, t_new, D_MODEL)

    kv_prompt = []
    for gi in range(N_GROUPS_B):
        t = _kv_tail(seq3(k_p), seq3(v_p), gi, bsz, seq)
        keep = t.shape[-1]
        kv_prompt.append(jnp.transpose(t.reshape(bsz, 2, H_SLOT, HEAD_DIM, keep), (0, 4, 1, 2, 3))[None])
    kvs = _kv_sample(k_s, v_s, dbatch, t_new)
    kv_sample = [jnp.transpose(kvs[gi].reshape(t_new, 2, H_SLOT, HEAD_DIM, dbatch), (4, 0, 1, 2, 3))[None]
                 for gi in range(N_GROUPS_B)]
    va_out = va_s.reshape(1, dbatch, t_new, A_WIDTH)
    return (y_prompt, y_sample, *kv_prompt, *kv_sample, va_out)
```

```python
import functools

import numpy as np
import jax
import jax.numpy as jnp
from jax import lax
from jax.experimental import pallas as pl
from jax.experimental.pallas import tpu as pltpu
from jax.experimental.pallas import tpu_sc as plsc

F32 = jnp.float32
BF16 = jnp.bfloat16

D_MODEL = 1024
A_WIDTH = 1024
A_GROUPS = 4
A_GROUP_WIDTH = A_WIDTH // A_GROUPS
CHUNK = 128
HEAD_DIM = 64
H_SLOT = 8
GROUP_WIDTH = H_SLOT * HEAD_DIM
DILATION_GROUPS = ((128, 1), (512, 4), (2048, 16))
N_GROUPS_B = len(DILATION_GROUPS)
B_HEADS = H_SLOT * N_GROUPS_B
B_WIDTH = B_HEADS * HEAD_DIM
N_EXPERTS = 32
TOP_K = 4
D_FF = 1024
SWIGLU_ALPHA = 1.702
SWIGLU_LIMIT = 7.0
PLE_DIM = 256
RMS_EPS = 1e-6
NEG_BIG = -1e30

LANES = 128
SUBLANES = 8
MXU_DIM = 256
HEADS_PER_VREG = LANES // HEAD_DIM
PAIRS = GROUP_WIDTH // LANES
RESIDUE_UNROLL = 4
TOKEN_TILE = 512
EXPERT_BLOCK = 256
VMEM_LIMIT = 56 * 1024 * 1024

SC_CORES = 2
SC_SUBCORES = 16
SC_WORKERS = SC_CORES * SC_SUBCORES
SC_ROWS = 32

_COL_SPLITS = np.cumsum([0, A_WIDTH, A_WIDTH, B_WIDTH, B_WIDTH, B_WIDTH, D_MODEL, D_MODEL]).tolist()


def _alibi_slopes():
    return np.exp2(-8.0 * np.arange(1, B_HEADS + 1, dtype=np.float32) / B_HEADS).astype(np.float32)


def _sigmoid(x):
    return 1.0 / (1.0 + jnp.exp(-x))


def _rms(x):
    return lax.rsqrt(jnp.mean(x * x, axis=-1, keepdims=True) + RMS_EPS)


def _dot(a, b):
    return jnp.dot(a, b, preferred_element_type=F32)


def _dot_nt(a, b):
    return lax.dot_general(a, b, (((1,), (1,)), ((), ())), preferred_element_type=F32)


def _resident(shape):
    nd = len(shape)
    return pl.BlockSpec(shape, lambda *_: (0,) * nd, pipeline_mode=pl.Buffered(1))


def _params(n_axes=1):
    return pltpu.CompilerParams(dimension_semantics=("arbitrary",) * n_axes, vmem_limit_bytes=VMEM_LIMIT)


def _proj_kernel(x_ref, gmix_ref, w_ref, gv_ref, gq_ref, gk_ref, hsum_ref,
                 u_ref, va_ref, q_ref, k_ref, v_ref, ga_ref, gb_ref):
    x = x_ref[...]
    n = ((x * _rms(x)) * gmix_ref[...]).astype(BF16)

    def section(i):
        return _dot(n, w_ref[:, _COL_SPLITS[i]:_COL_SPLITS[i + 1]])

    u_ref[...] = jax.nn.gelu(section(0)).astype(u_ref.dtype)
    va = jax.nn.gelu(section(1))
    va_ref[...] = ((va * _rms(va)) * gv_ref[...]).astype(va_ref.dtype)

    def head_norm(z, g_ref, scale):
        parts = []
        for c in range(B_WIDTH // MXU_DIM):
            zc = z[:, c * MXU_DIM:(c + 1) * MXU_DIM]
            ss = _dot((zc * zc).astype(BF16), hsum_ref[...])
            parts.append(zc * lax.rsqrt(ss * (1.0 / HEAD_DIM) + RMS_EPS))
        return jnp.concatenate(parts, axis=1) * (g_ref[...] * scale)

    q_ref[...] = head_norm(section(2), gq_ref, HEAD_DIM ** -0.5)
    k_ref[...] = head_norm(section(3), gk_ref, 1.0)
    v_ref[...] = section(4)
    ga_ref[...] = _sigmoid(section(5)).astype(ga_ref.dtype)
    gb_ref[...] = _sigmoid(section(6)).astype(gb_ref.dtype)


def _project(x, g_mix, w_in_bf, g_v, g_q_t, g_k_t, hsum, va_dtype):
    n_tok = x.shape[0]
    tm = TOKEN_TILE
    row = lambda w: pl.BlockSpec((tm, w), lambda i: (i, 0))
    outs = [(A_WIDTH, BF16), (A_WIDTH, va_dtype), (B_WIDTH, F32), (B_WIDTH, F32), (B_WIDTH, F32),
            (D_MODEL, BF16), (D_MODEL, BF16)]
    return pl.pallas_call(
        _proj_kernel,
        grid=(n_tok // tm,),
        in_specs=[row(D_MODEL), _resident(g_mix.shape), _resident(w_in_bf.shape), _resident(g_v.shape),
                  _resident(g_q_t.shape), _resident(g_k_t.shape), _resident(hsum.shape)],
        out_specs=[row(w) for w, _ in outs],
        out_shape=[jax.ShapeDtypeStruct((n_tok, w), dt) for w, dt in outs],
        compiler_params=_params(),
        name="project",
    )(x, g_mix, w_in_bf, g_v, g_q_t, g_k_t, hsum)


def _band_mask(band, first_block):
    qi = lax.broadcasted_iota(jnp.int32, (band, 2 * band), 0)
    kj = lax.broadcasted_iota(jnp.int32, (band, 2 * band), 1)
    dist = qi + band - kj
    valid = (dist >= 0) & (dist <= band) & ((kj >= band) | jnp.logical_not(first_block))
    return valid, dist


def _pair_attention(q2, k, v, valid, distf, slopes2):
    band = q2.shape[0]
    lane = lax.broadcasted_iota(jnp.int32, (band, LANES), 1)
    o_pair = jnp.zeros((band, LANES), F32)
    l_pair = jnp.zeros((band, LANES), F32)
    for hh in range(HEADS_PER_VREG):
        in_head = (lane >= hh * HEAD_DIM) & (lane < (hh + 1) * HEAD_DIM)
        qm = jnp.where(in_head, q2, 0.0).astype(BF16)
        s = _dot_nt(qm, k)
        s = jnp.where(valid, s - slopes2[hh] * distf, NEG_BIG)
        m = jnp.max(s, axis=-1, keepdims=True)
        e = jnp.exp(s - m)
        den = jnp.sum(e, axis=-1, keepdims=True)
        prob = (e * (1.0 / den)).astype(BF16)
        o_pair = jnp.where(in_head, _dot(prob, v), o_pair)
        l_pair = jnp.where(in_head, m + jnp.log(den), l_pair)
    return o_pair, l_pair


def _attn_dense_kernel(q_ref, kp_ref, kc_ref, vp_ref, vc_ref, o_ref, l_ref, *, band, slopes):
    valid, dist = _band_mask(band, pl.program_id(1) == 0)
    distf = dist.astype(F32)
    for p in range(PAIRS):
        cols = slice(p * LANES, (p + 1) * LANES)
        k = jnp.concatenate([kp_ref[:, cols], kc_ref[:, cols]], axis=0).astype(BF16)
        v = jnp.concatenate([vp_ref[:, cols], vc_ref[:, cols]], axis=0).astype(BF16)
        sl = [float(slopes[p * HEADS_PER_VREG + hh]) for hh in range(HEADS_PER_VREG)]
        o_pair, l_pair = _pair_attention(q_ref[:, cols], k, v, valid, distf, sl)
        o_ref[:, cols] = o_pair.astype(o_ref.dtype)
        l_ref[:, cols] = l_pair


def _attn_dilated_kernel(slopes_ref, q_ref, kp_ref, kc_ref, vp_ref, vc_ref, o_ref, l_ref, *, band, dil):
    p = pl.program_id(2)
    valid, dist = _band_mask(band, pl.program_id(1) == 0)
    distf = (dist * dil).astype(F32)
    sl = [slopes_ref[p * HEADS_PER_VREG + hh] for hh in range(HEADS_PER_VREG)]

    def residues(it, carry):
        for u in range(RESIDUE_UNROLL):
            rows = pl.ds(it * RESIDUE_UNROLL + u, band, stride=dil)
            k = jnp.concatenate([kp_ref[rows, :], kc_ref[rows, :]], axis=0).astype(BF16)
            v = jnp.concatenate([vp_ref[rows, :], vc_ref[rows, :]], axis=0).astype(BF16)
            o_pair, l_pair = _pair_attention(q_ref[rows, :], k, v, valid, distf, sl)
            o_ref[rows, :] = o_pair
            l_ref[rows, :] = l_pair
        return carry

    lax.fori_loop(0, dil // RESIDUE_UNROLL, residues, 0)


def _attend_prompt(q, k, v, gi, bsz, seq):
    win, dil = DILATION_GROUPS[gi]
    band = win // dil
    slopes = _alibi_slopes()[gi * H_SLOT:(gi + 1) * H_SLOT]
    if dil == 1:
        cur = pl.BlockSpec((None, band, GROUP_WIDTH), lambda b, j: (b, j, gi))
        prev = pl.BlockSpec((None, band, GROUP_WIDTH), lambda b, j: (b, jnp.maximum(j - 1, 0), gi))
        out = pl.BlockSpec((None, band, GROUP_WIDTH), lambda b, j: (b, j, 0))
        o, lse = pl.pallas_call(
            functools.partial(_attn_dense_kernel, band=band, slopes=slopes),
            grid=(bsz, seq // band),
            in_specs=[cur, prev, cur, prev, cur],
            out_specs=[out, out],
            out_shape=[jax.ShapeDtypeStruct((bsz, seq, GROUP_WIDTH), BF16),
                       jax.ShapeDtypeStruct((bsz, seq, GROUP_WIDTH), F32)],
            compiler_params=_params(2),
            name=f"attn_prompt_w{win}",
        )(q, k, k, v, v)
    else:
        col = lambda p: gi * PAIRS + p
        cur = pl.BlockSpec((None, win, LANES), lambda b, j, p, s: (b, j, col(p)))
        prev = pl.BlockSpec((None, win, LANES), lambda b, j, p, s: (b, jnp.maximum(j - 1, 0), col(p)))
        out = pl.BlockSpec((None, win, LANES), lambda b, j, p, s: (b, j, p))
        o, lse = pl.pallas_call(
            functools.partial(_attn_dilated_kernel, band=band, dil=dil),
            grid_spec=pltpu.PrefetchScalarGridSpec(
                num_scalar_prefetch=1, grid=(bsz, seq // win, PAIRS),
                in_specs=[cur, prev, cur, prev, cur], out_specs=[out, out]),
            out_shape=[jax.ShapeDtypeStruct((bsz, seq, GROUP_WIDTH), F32)] * 2,
            compiler_params=_params(3),
            name=f"attn_prompt_w{win}",
        )(jnp.asarray(slopes), q, k, k, v, v)
    return o.reshape(bsz * seq, GROUP_WIDTH), lse.reshape(bsz * seq, GROUP_WIDTH)


def _attn_sample_kernel(q_ref, kn_ref, vn_ref, c0_ref, c1_ref, c2_ref, o_ref, *, t_new, slopes):
    rows = H_SLOT * t_new
    pad_new = 16
    t_shift = t_new.bit_length() - 1
    caches = (c0_ref, c1_ref, c2_ref)
    row_id = lax.broadcasted_iota(jnp.int32, (rows, GROUP_WIDTH), 0)
    col_id = lax.broadcasted_iota(jnp.int32, (rows, GROUP_WIDTH), 1)
    own_head = (row_id >> t_shift) == (col_id >> (HEAD_DIM.bit_length() - 1))
    head_of_row = lax.broadcasted_iota(jnp.int32, (rows, 1), 0) >> t_shift
    zpad = jnp.zeros((pad_new - t_new, GROUP_WIDTH), F32)
    outs, lses = [], []
    for g, (win, dil) in enumerate(DILATION_GROUPS):
        buf_len = caches[g].shape[-1]
        cols = slice(g * GROUP_WIDTH, (g + 1) * GROUP_WIDTH)

        def mask_of(n_keys, first_pos, n_real):
            i_q = lax.broadcasted_iota(jnp.int32, (rows, n_keys), 0) & (t_new - 1)
            key = lax.broadcasted_iota(jnp.int32, (rows, n_keys), 1)
            delta = buf_len + i_q - (first_pos + key)
            ok = (delta >= 0) & ((delta & (dil - 1)) == 0) & (delta <= win) & (key < n_real)
            return ok, delta.astype(F32)

        kt = caches[g][0].reshape(GROUP_WIDTH, buf_len).astype(BF16)
        vt = caches[g][1].reshape(GROUP_WIDTH, buf_len).astype(BF16)
        qblk = jnp.where(own_head, jnp.concatenate([q_ref[:, cols]] * H_SLOT, axis=0), 0.0).astype(BF16)
        kn = jnp.concatenate([kn_ref[:, cols], zpad], axis=0).astype(BF16)
        vn = jnp.concatenate([vn_ref[:, cols], zpad], axis=0).astype(BF16)
        slope_rows = jnp.zeros((rows, 1), F32)
        for h in range(H_SLOT):
            slope_rows = jnp.where(head_of_row == h, float(slopes[g * H_SLOT + h]), slope_rows)
        ok_c, delta_c = mask_of(buf_len, 0, buf_len)
        ok_n, delta_n = mask_of(pad_new, buf_len, t_new)
        sc = jnp.where(ok_c, _dot(qblk, kt) - slope_rows * delta_c, NEG_BIG)
        sn = jnp.where(ok_n, _dot_nt(qblk, kn) - slope_rows * delta_n, NEG_BIG)
        m = jnp.maximum(jnp.max(sc, axis=-1, keepdims=True), jnp.max(sn, axis=-1, keepdims=True))
        ec = jnp.exp(sc - m)
        en = jnp.exp(sn - m)
        den = jnp.sum(ec, axis=-1, keepdims=True) + jnp.sum(en, axis=-1, keepdims=True)
        inv = 1.0 / den
        outs.append(_dot_nt((ec * inv).astype(BF16), vt) + _dot((en * inv).astype(BF16), vn))
        lses.append(m + jnp.log(den))
    mx = functools.reduce(jnp.maximum, lses)
    ws = [jnp.exp(l - mx) for l in lses]
    wsum = functools.reduce(jnp.add, ws)
    comb = sum((w / wsum) * o for w, o in zip(ws, outs))
    comb = jnp.where(own_head, comb, 0.0)
    o_tok = sum(comb[h * t_new:(h + 1) * t_new, :] for h in range(H_SLOT))
    o_ref[...] = o_tok.astype(o_ref.dtype)


def _attend_sample(q, k_new, v_new, caches_t, dbatch, t_new):
    assert t_new & (t_new - 1) == 0 and t_new <= 8
    tok = pl.BlockSpec((None, t_new, B_WIDTH), lambda b: (b, 0, 0))
    cache_specs = [pl.BlockSpec((None,) + c.shape[1:], lambda b: (b, 0, 0, 0, 0)) for c in caches_t]
    o = pl.pallas_call(
        functools.partial(_attn_sample_kernel, t_new=t_new, slopes=_alibi_slopes()),
        grid=(dbatch,),
        in_specs=[tok, tok, tok] + cache_specs,
        out_specs=pl.BlockSpec((None, t_new, GROUP_WIDTH), lambda b: (b, 0, 0)),
        out_shape=jax.ShapeDtypeStruct((dbatch, t_new, GROUP_WIDTH), BF16),
        compiler_params=_params(),
        name="attn_sample",
    )(q, k_new, v_new, *caches_t)
    return o.reshape(dbatch * t_new, GROUP_WIDTH)


def _kv_tail_kernel(k_ref, v_ref, o_ref):
    o_ref[0] = k_ref[...].T
    o_ref[1] = v_ref[...].T


def _kv_tail(k, v, gi, bsz, seq):
    keep = min(DILATION_GROUPS[gi][0], seq)
    pb = min(keep, 512)
    first = (seq - keep) // pb
    src = pl.BlockSpec((None, pb, GROUP_WIDTH), lambda b, j: (b, first + j, gi))
    return pl.pallas_call(
        _kv_tail_kernel,
        grid=(bsz, keep // pb),
        in_specs=[src, src],
        out_specs=pl.BlockSpec((None, 2, GROUP_WIDTH, pb), lambda b, j: (b, 0, 0, j)),
        out_shape=jax.ShapeDtypeStruct((bsz, 2, GROUP_WIDTH, keep), F32),
        compiler_params=_params(2),
        name=f"kv_tail_w{DILATION_GROUPS[gi][0]}",
    )(k, v)


def _kv_sample_kernel(k_ref, v_ref, o_ref, *, t_new, dbatch):
    for t in range(t_new):
        rows = pl.ds(t, dbatch, stride=t_new)
        o_ref[t, 0] = k_ref[rows, :].T
        o_ref[t, 1] = v_ref[rows, :].T


def _kv_sample(k, v, dbatch, t_new):
    src = pl.BlockSpec((dbatch * t_new, LANES), lambda s: (0, s))
    return pl.pallas_call(
        functools.partial(_kv_sample_kernel, t_new=t_new, dbatch=dbatch),
        grid=(B_WIDTH // LANES,),
        in_specs=[src, src],
        out_specs=pl.BlockSpec((None, t_new, 2, LANES, dbatch), lambda s: (s // PAIRS, 0, 0, s % PAIRS, 0)),
        out_shape=jax.ShapeDtypeStruct((N_GROUPS_B, t_new, 2, GROUP_WIDTH, dbatch), F32),
        compiler_params=_params(),
        name="kv_sample",
    )(k, v)


def _topk_route(logits):
    tm = logits.shape[0]
    lane = lax.broadcasted_iota(jnp.int32, logits.shape, 1).astype(F32)
    slot = lax.broadcasted_iota(jnp.int32, (tm, TOP_K), 1)
    val_out = jnp.zeros((tm, TOP_K), F32)
    work = logits
    ids, top = [], None
    for r in range(TOP_K):
        mx = jnp.max(work, axis=-1, keepdims=True)
        ix = jnp.min(jnp.where(work == mx, lane, float(N_EXPERTS)), axis=-1, keepdims=True)
        top = mx if top is None else top
        ids.append(ix)
        val_out = jnp.where(slot == r, jnp.exp(mx - top), val_out)
        work = jnp.where(lane == ix, -jnp.inf, work)
    gates = val_out / jnp.sum(val_out, axis=-1, keepdims=True)
    return ids, gates


def _finish_kernel(*refs, n_attn, n_alias, n_valid):
    n_in = 5 + n_attn + (n_attn if n_attn > 1 else 0) + 11
    ins, outs = refs[:n_in], refs[n_in + n_alias:]
    step = pl.program_id(0)

    @pl.when(step < n_valid)
    def _():
        _finish_tile(ins, outs, n_attn)

    @pl.when(step >= n_valid)
    def _():
        for ref in outs[:5]:
            ref[...] = jnp.zeros_like(ref)


def _finish_tile(ins, outs, n_attn):
    (h_ref, u_ref, va_ref, wm_ref, bias_ref), refs = ins[:5], ins[5:]
    o_refs, refs = refs[:n_attn], refs[n_attn:]
    n_lse = n_attn if n_attn > 1 else 0
    l_refs, refs = refs[:n_lse], refs[n_lse:]
    ga_ref, gb_ref, wa_ref, wb_ref, wo_ref, gmoe_ref, wrh_ref, wrl_ref, br_ref, tri_ref, cnt_in_ref = refs
    h1_ref, n2_ref, idx_ref, gate_ref, pos_ref, cnt_out_ref, carry_ref = outs
    tm = h_ref.shape[0]

    chunks = []
    for c in range(tm // CHUNK):
        rows = slice(c * CHUNK, (c + 1) * CHUNK)
        va_c = va_ref[rows, :].astype(BF16)
        mixed = jnp.concatenate(
            [_dot(wm_ref[g], va_c[:, g * A_GROUP_WIDTH:(g + 1) * A_GROUP_WIDTH]) for g in range(A_GROUPS)], axis=1)
        chunks.append((u_ref[rows, :].astype(F32) * (mixed + bias_ref[...])).astype(BF16))
    branch_a = _dot(jnp.concatenate(chunks, axis=0), wa_ref[...])

    if n_attn > 1:
        lses = [r[...] for r in l_refs]
        mx = functools.reduce(jnp.maximum, lses)
        ws = [jnp.exp(l - mx) for l in lses]
        wsum = functools.reduce(jnp.add, ws)
        o_b = (sum(w * o_ref[...].astype(F32) for w, o_ref in zip(ws, o_refs)) / wsum).astype(BF16)
    else:
        o_b = o_refs[0][...]
    branch_b = _dot(o_b, wb_ref[...])

    mix = ga_ref[...].astype(F32) * branch_a + gb_ref[...].astype(F32) * branch_b
    h1 = h_ref[...] + _dot(mix.astype(BF16), wo_ref[...])
    h1_ref[...] = h1

    n2 = (h1 * _rms(h1)) * gmoe_ref[...]
    n2_ref[...] = n2
    n_hi = n2.astype(BF16)
    n_lo = (n2 - n_hi.astype(F32)).astype(BF16)
    logits = _dot(n_hi, wrh_ref[...]) + _dot(n_lo, wrh_ref[...]) + _dot(n_hi, wrl_ref[...]) + br_ref[...]
    ids, gates = _topk_route(logits)
    gate_ref[...] = gates

    @pl.when(pl.program_id(0) == 0)
    def _():
        carry_ref[...] = cnt_in_ref[...]

    lane = lax.broadcasted_iota(jnp.int32, logits.shape, 1).astype(F32)
    slot = lax.broadcasted_iota(jnp.int32, (tm, TOP_K), 1)
    sel = sum(jnp.where(lane == ix, 1.0, 0.0) for ix in ids)
    before = _dot(tri_ref[...], sel.astype(BF16)) + carry_ref[...]
    idx_out = jnp.zeros((tm, TOP_K), F32)
    pos_out = jnp.zeros((tm, TOP_K), F32)
    for r, ix in enumerate(ids):
        idx_out = jnp.where(slot == r, ix, idx_out)
        rank = jnp.sum(jnp.where(lane == ix, before, 0.0), axis=-1, keepdims=True)
        pos_out = jnp.where(slot == r, rank, pos_out)
    idx_ref[...] = idx_out.astype(jnp.int32)
    pos_ref[...] = pos_out.astype(jnp.int32)
    carry_ref[...] = carry_ref[...] + jnp.sum(sel, axis=0, keepdims=True)
    cnt_out_ref[...] = carry_ref[...]


def _finish(h, u, va, wm, bias_full, attn, ga, gb, consts, cnt_in, n_total, tile_offset, prev_outs):
    n_tok = h.shape[0]
    tm = TOKEN_TILE
    n_valid = n_tok // tm
    n_steps = n_valid if prev_outs is not None else n_total // tm - tile_offset
    row = lambda w: pl.BlockSpec((tm, w), lambda i: (jnp.minimum(i, n_valid - 1), 0))
    orow = lambda w: pl.BlockSpec((tm, w), lambda i: (i + tile_offset, 0))
    n_attn = len(attn)
    args = [h, u, va, wm, bias_full] + [o for o, _ in attn]
    specs = [row(D_MODEL), row(A_WIDTH), row(A_WIDTH), _resident(wm.shape), _resident(bias_full.shape)]
    specs += [row(GROUP_WIDTH)] * n_attn
    if n_attn > 1:
        args += [l for _, l in attn]
        specs += [row(GROUP_WIDTH)] * n_attn
    args += [ga, gb] + list(consts) + [cnt_in]
    specs += [row(D_MODEL), row(D_MODEL)] + [_resident(t.shape) for t in consts] + [_resident(cnt_in.shape)]
    aliases = {}
    if prev_outs is not None:
        for k, t in enumerate(prev_outs):
            aliases[len(args)] = k
            args.append(t)
            specs.append(pl.BlockSpec(memory_space=pl.ANY))
    widths = [(D_MODEL, F32), (D_MODEL, F32), (TOP_K, jnp.int32), (TOP_K, F32), (TOP_K, jnp.int32)]
    outs = pl.pallas_call(
        functools.partial(_finish_kernel, n_attn=n_attn, n_alias=len(aliases), n_valid=n_valid),
        grid=(n_steps,),
        in_specs=specs,
        out_specs=[orow(w) for w, _ in widths] + [_resident(cnt_in.shape)],
        out_shape=[jax.ShapeDtypeStruct((n_total, w), dt) for w, dt in widths]
                  + [jax.ShapeDtypeStruct(cnt_in.shape, F32)],
        scratch_shapes=[pltpu.VMEM(cnt_in.shape, F32)],
        input_output_aliases=aliases,
        compiler_params=_params(),
        name=f"finish_{n_attn}",
    )(*args)
    return outs[:5], outs[5]


def _sc_gather_rows(table, idx):
    m = idx.shape[0]
    width = table.shape[1]
    per_worker = m // SC_WORKERS
    n_chunks = per_worker // SC_ROWS
    mesh = plsc.VectorSubcoreMesh(core_axis_name="c", subcore_axis_name="s",
                                  num_cores=SC_CORES, num_subcores=SC_SUBCORES)

    assert n_chunks % 2 == 0

    @functools.partial(
        pl.kernel, mesh=mesh,
        out_type=jax.ShapeDtypeStruct((m, width), table.dtype),
        scratch_types=[pltpu.VMEM((n_chunks, SC_ROWS), jnp.int32),
                       pltpu.VMEM((SC_ROWS, width), table.dtype),
                       pltpu.VMEM((SC_ROWS, width), table.dtype),
                       pltpu.SemaphoreType.DMA,
                       pltpu.SemaphoreType.DMA],
        name="sc_gather_rows",
    )
    def gather(table_hbm, idx_hbm, out_hbm, idx_v, rows_a, rows_b, sem_a, sem_b):
        wid = lax.axis_index("s") * SC_CORES + lax.axis_index("c")
        base = wid * per_worker
        pltpu.sync_copy(idx_hbm.at[wid], idx_v)

        def fetch(c, rows, sem):
            return pltpu.make_async_copy(table_hbm.at[idx_v.at[c]], rows, sem)

        def put(c, rows):
            off = pl.multiple_of(base + c * SC_ROWS, SC_ROWS)
            pltpu.sync_copy(rows, out_hbm.at[pl.ds(off, SC_ROWS)])

        fetch(0, rows_a, sem_a).start()

        @pl.loop(0, n_chunks, step=2)
        def _(c):
            fetch(c, rows_a, sem_a).wait()
            fetch(c + 1, rows_b, sem_b).start()
            put(c, rows_a)
            fetch(c + 1, rows_b, sem_b).wait()

            @pl.when(c + 2 < n_chunks)
            def _():
                fetch(c + 2, rows_a, sem_a).start()

            put(c + 1, rows_b)

    return gather(table, idx.reshape(SC_WORKERS, n_chunks, SC_ROWS))


def _expert_kernel(be_ref, nused_ref, x_ref, wgu_ref, wd_ref, bgu_ref, bd_ref, perm_ref, y_ref, wgu_s, wd_s):
    i = pl.program_id(0)
    active = i < nused_ref[0]
    fresh = (i == 0) | (be_ref[i] != be_ref[jnp.maximum(i - 1, 0)])

    @pl.when(active & fresh)
    def _():
        wgu_s[...] = wgu_ref[...].astype(BF16)
        half, part = D_FF // 2, MXU_DIM // 2
        for t in range(D_FF // MXU_DIM):
            src = jnp.concatenate([wd_ref[t * part:(t + 1) * part, :],
                                   wd_ref[half + t * part:half + (t + 1) * part, :]], axis=0).astype(BF16)
            wd_s[t * MXU_DIM:(t + 1) * MXU_DIM, :] = _dot(perm_ref[...], src).astype(BF16)

    @pl.when(active)
    def _():
        x = x_ref[...].astype(BF16)
        hh = _dot(x, wgu_s[...]) + bgu_ref[...]
        partner = pltpu.roll(hh, 2 * D_FF - 1, axis=1)
        h_glu = jnp.minimum(hh, SWIGLU_LIMIT)
        h_lin = jnp.clip(partner, -SWIGLU_LIMIT, SWIGLU_LIMIT)
        act = h_glu * _sigmoid(SWIGLU_ALPHA * h_glu) * (h_lin + 1.0)
        lane = lax.broadcasted_iota(jnp.int32, act.shape, 1)
        act = jnp.where((lane & 1) == 0, act, 0.0)
        merged = act[:, :D_FF] + pltpu.roll(act[:, D_FF:], 1, axis=1)
        y_ref[...] = _dot(merged.astype(BF16), wd_s[...]) + bd_ref[...]

    @pl.when(jnp.logical_not(active))
    def _():
        y_ref[...] = jnp.zeros_like(y_ref)


def _experts(xb, block_expert, n_used, w_gate_up, w_down, b_gate_up, b_down, perm):
    n_slots = xb.shape[0]
    n_blocks = n_slots // EXPERT_BLOCK
    w_spec = lambda k, n: pl.BlockSpec((None, k, n), lambda i, be, nu: (be[i], 0, 0))
    blk = pl.BlockSpec((EXPERT_BLOCK, D_MODEL), lambda i, be, nu: (i, 0))
    return pl.pallas_call(
        _expert_kernel,
        grid_spec=pltpu.PrefetchScalarGridSpec(
            num_scalar_prefetch=2, grid=(n_blocks,),
            in_specs=[blk, w_spec(D_MODEL, 2 * D_FF), w_spec(D_FF, D_MODEL), w_spec(1, 2 * D_FF), w_spec(1, D_MODEL),
                      pl.BlockSpec(perm.shape, lambda i, be, nu: (0, 0), pipeline_mode=pl.Buffered(1))],
            out_specs=blk,
            scratch_shapes=[pltpu.VMEM((D_MODEL, 2 * D_FF), BF16), pltpu.VMEM((D_FF, D_MODEL), BF16)]),
        out_shape=jax.ShapeDtypeStruct((n_slots, D_MODEL), F32),
        compiler_params=_params(),
        name="experts",
    )(block_expert, n_used, xb, w_gate_up, w_down, b_gate_up, b_down, perm)


def _final_kernel(h1_ref, yg_ref, gate_ref, p_ref, gple_ref, wg_ref, wp_ref, out_ref):
    h2 = h1_ref[...]
    gates = gate_ref[...]
    for k in range(TOP_K):
        h2 = h2 + gates[:, k:k + 1] * yg_ref[k]
    n3 = ((h2 * _rms(h2)) * gple_ref[...]).astype(BF16)
    gate = _sigmoid(_dot(n3, wg_ref[...]))
    out_ref[...] = h2 + gate * _dot(p_ref[...].astype(BF16), wp_ref[...])


def _final(h1, yg, gates, p, tile_offset, g_ple, w_ple_gate, w_ple_proj):
    n_tok = p.shape[0]
    tm = TOKEN_TILE
    return pl.pallas_call(
        _final_kernel,
        grid=(n_tok // tm,),
        in_specs=[pl.BlockSpec((tm, D_MODEL), lambda i: (i + tile_offset, 0)),
                  pl.BlockSpec((TOP_K, tm, D_MODEL), lambda i: (0, i + tile_offset, 0)),
                  pl.BlockSpec((tm, TOP_K), lambda i: (i + tile_offset, 0)),
                  pl.BlockSpec((tm, PLE_DIM), lambda i: (i, 0)),
                  _resident(g_ple.shape), _resident(w_ple_gate.shape), _resident(w_ple_proj.shape)],
        out_specs=pl.BlockSpec((tm, D_MODEL), lambda i: (i, 0)),
        out_shape=jax.ShapeDtypeStruct((n_tok, D_MODEL), F32),
        compiler_params=_params(),
        name="final",
    )(h1, yg, gates, p, g_ple, w_ple_gate, w_ple_proj)


def _routing_tables(idx4, pos4, counts, n_slots):
    n_tok = idx4.shape[0]
    counts = counts.reshape(N_EXPERTS).astype(jnp.int32)
    pcounts = (counts + EXPERT_BLOCK - 1) // EXPERT_BLOCK * EXPERT_BLOCK
    pends = jnp.cumsum(pcounts)
    pstarts = pends - pcounts
    experts = jnp.arange(N_EXPERTS, dtype=jnp.int32)
    start4 = jnp.sum(jnp.where(idx4[:, :, None] == experts, pstarts, 0), axis=-1)
    dest4 = (start4 + pos4).astype(jnp.int32)
    token = jnp.broadcast_to(jnp.arange(n_tok, dtype=jnp.int32)[:, None], dest4.shape)
    filler = jnp.arange(n_slots, dtype=jnp.int32) % n_tok
    slot_tok = filler.at[dest4.reshape(-1)].set(token.reshape(-1))
    n_blocks = n_slots // EXPERT_BLOCK
    block_start = jnp.arange(n_blocks, dtype=jnp.int32) * EXPERT_BLOCK
    block_expert = jnp.minimum(jnp.sum(block_start[:, None] >= pends[None, :], axis=1), N_EXPERTS - 1).astype(jnp.int32)
    n_used = (pends[-1] // EXPERT_BLOCK).astype(jnp.int32).reshape(1)
    return dest4, slot_tok, block_expert, n_used


def kernel(x_prompt, x_sample, cache_kv_w128, cache_kv_w512, cache_kv_w2048, p_prompt, p_sample, g_mix, w_in, g_v, g_q, g_k, w_spatial, b_spatial, w_branch_a, w_branch_b, w_out, g_moe, w_router, b_router, w_gate_up, b_gate_up, w_down, b_down, g_ple, w_ple_gate, w_ple_proj):
    bsz, seq, _ = x_prompt.shape
    dbatch, t_new, _ = x_sample.shape
    assert g_mix.shape[0] == 1
    caches = (cache_kv_w128, cache_kv_w512, cache_kv_w2048)
    l = 0
    n_p, n_s = bsz * seq, dbatch * t_new
    n_tok = n_p + n_s
    assert n_p % TOKEN_TILE == 0 and n_s % TOKEN_TILE == 0

    row2 = lambda t: t.reshape(1, -1)
    w_in_bf = w_in[l].astype(BF16)
    g_q_t = jnp.tile(g_q[l], B_HEADS).reshape(1, B_WIDTH)
    g_k_t = jnp.tile(g_k[l], B_HEADS).reshape(1, B_WIDTH)
    hid = np.arange(MXU_DIM) // HEAD_DIM
    hsum = jnp.asarray(hid[:, None] == hid[None, :], BF16)
    tril = jnp.tril(jnp.ones((CHUNK, CHUNK), bool))
    wm_prompt = jnp.where(tril[None], w_spatial[l], 0).astype(BF16)
    bias_prompt = jnp.repeat(b_spatial[l].T, A_GROUP_WIDTH, axis=1)
    reps = CHUNK // t_new
    small = jnp.where(tril[None, :t_new, :t_new], w_spatial[l][:, :t_new, :t_new], 0)
    wm_sample = jnp.einsum("ab,gij->gaibj", jnp.eye(reps, dtype=F32), small).reshape(A_GROUPS, CHUNK, CHUNK).astype(BF16)
    bias_sample = jnp.tile(bias_prompt[:t_new], (reps, 1))
    wr_hi = w_router[l].astype(BF16)
    wr_lo = (w_router[l] - wr_hi.astype(F32)).astype(BF16)
    tri = jnp.asarray(np.tril(np.ones((TOKEN_TILE, TOKEN_TILE), np.float32), -1), BF16)
    consts = (w_branch_a[l].astype(BF16), w_branch_b[l].astype(BF16), w_out[l].astype(BF16), row2(g_moe[l]),
              wr_hi, wr_lo, row2(b_router[l]), tri)
    part = MXU_DIM // 2
    perm_np = np.zeros((MXU_DIM, MXU_DIM), np.float32)
    perm_np[2 * np.arange(part), np.arange(part)] = 1.0
    perm_np[2 * np.arange(part) + 1, part + np.arange(part)] = 1.0
    perm = jnp.asarray(perm_np, BF16)

    proj = functools.partial(_project, g_mix=row2(g_mix[l]), w_in_bf=w_in_bf, g_v=row2(g_v[l]),
                             g_q_t=g_q_t, g_k_t=g_k_t, hsum=hsum)

    xp = x_prompt.reshape(n_p, D_MODEL)
    u_p, va_p, q_p, k_p, v_p, ga_p, gb_p = proj(xp, va_dtype=BF16)
    seq3 = lambda t: t.reshape(bsz, seq, B_WIDTH)
    attn_p = [_attend_prompt(seq3(q_p), seq3(k_p), seq3(v_p), gi, bsz, seq) for gi in range(N_GROUPS_B)]
    zero_counts = jnp.zeros((1, N_EXPERTS), F32)
    outs_p, cnt_p = _finish(xp, u_p, va_p, wm_prompt, bias_prompt, attn_p, ga_p, gb_p, consts, zero_counts,
                            n_tok, 0, None)

    xs = x_sample.reshape(n_s, D_MODEL)
    u_s, va_s, q_s, k_s, v_s, ga_s, gb_s = proj(xs, va_dtype=F32)
    tok3 = lambda t: t.reshape(dbatch, t_new, B_WIDTH)
    caches_t = [jnp.transpose(c[l], (0, 2, 3, 4, 1)) for c in caches]
    o_s = _attend_sample(tok3(q_s), tok3(k_s), tok3(v_s), caches_t, dbatch, t_new)
    (h1, n2, idx4, gates, pos4), counts = _finish(xs, u_s, va_s, wm_sample, bias_sample, [(o_s, None)], ga_s, gb_s,
                                                  consts, cnt_p, n_tok, n_p // TOKEN_TILE, outs_p)

    n_blocks = -(-n_tok * TOP_K // EXPERT_BLOCK) + N_EXPERTS
    n_slots = n_blocks * EXPERT_BLOCK
    dest4, slot_tok, block_expert, n_used = _routing_tables(idx4, pos4, counts, n_slots)
    xb = _sc_gather_rows(n2, slot_tok)
    yb = _experts(xb, block_expert, n_used, w_gate_up[l], w_down[l], b_gate_up[l][:, None, :], b_down[l][:, None, :], perm)
    yg = _sc_gather_rows(yb, dest4.T.reshape(-1)).reshape(TOP_K, n_tok, D_MODEL)
    fin = functools.partial(_final, g_ple=row2(g_ple[l]), w_ple_gate=w_ple_gate[l].astype(BF16),
                            w_ple_proj=w_ple_proj[l].astype(BF16))
    y_prompt = fin(h1, yg, gates, p_prompt[l].reshape(n_p, PLE_DIM), 0).reshape(bsz, seq, D_MODEL)
    y_sample = fin(h1, yg, gates, p_sample[l].reshape(n_s, PLE_DIM), n_p // TOKEN_TILE).reshape(dbatch, t_new, D_MODEL)

    kv_prompt = []
    for gi in range(N_GROUPS_B):
        t = _kv_tail(seq3(k_p), seq3(v_p), gi, bsz, seq)
        keep = t.shape[-1]
        kv_prompt.append(jnp.transpose(t.reshape(bsz, 2, H_SLOT, HEAD_DIM, keep), (0, 4, 1, 2, 3))[None])
    kvs = _kv_sample(k_s, v_s, dbatch, t_new)
    kv_sample = [jnp.transpose(kvs[gi].reshape(t_new, 2, H_SLOT, HEAD_DIM, dbatch), (4, 0, 1, 2, 3))[None]
                 for gi in range(N_GROUPS_B)]
    va_out = va_s.reshape(1, dbatch, t_new, A_WIDTH)
    return (y_prompt, y_sample, *kv_prompt, *kv_sample, va_out)
```

```python
import functools

import numpy as np
import jax
import jax.numpy as jnp
from jax import lax
from jax.experimental import pallas as pl
from jax.experimental.pallas import tpu as pltpu
from jax.experimental.pallas import tpu_sc as plsc

F32 = jnp.float32
BF16 = jnp.bfloat16

D_MODEL = 1024
A_WIDTH = 1024
A_GROUPS = 4
A_GROUP_WIDTH = A_WIDTH // A_GROUPS
CHUNK = 128
HEAD_DIM = 64
H_SLOT = 8
GROUP_WIDTH = H_SLOT * HEAD_DIM
DILATION_GROUPS = ((128, 1), (512, 4), (2048, 16))
N_GROUPS_B = len(DILATION_GROUPS)
B_HEADS = H_SLOT * N_GROUPS_B
B_WIDTH = B_HEADS * HEAD_DIM
N_EXPERTS = 32
TOP_K = 4
D_FF = 1024
SWIGLU_ALPHA = 1.702
SWIGLU_LIMIT = 7.0
PLE_DIM = 256
RMS_EPS = 1e-6
NEG_BIG = -1e30

LANES = 128
SUBLANES = 8
MXU_DIM = 256
HEADS_PER_VREG = LANES // HEAD_DIM
PAIRS = GROUP_WIDTH // LANES
RESIDUE_UNROLL = 4
TOKEN_TILE = 512
EXPERT_BLOCK = 256
VMEM_LIMIT = 56 * 1024 * 1024

SC_CORES = 2
SC_SUBCORES = 16
SC_WORKERS = SC_CORES * SC_SUBCORES
SC_ROWS = 32
SC_SCATTER_ROWS = 16

_COL_SPLITS = np.cumsum([0, A_WIDTH, A_WIDTH, B_WIDTH, B_WIDTH, B_WIDTH, D_MODEL, D_MODEL]).tolist()


def _alibi_slopes():
    return np.exp2(-8.0 * np.arange(1, B_HEADS + 1, dtype=np.float32) / B_HEADS).astype(np.float32)


def _sigmoid(x):
    return 1.0 / (1.0 + jnp.exp(-x))


def _rms(x):
    return lax.rsqrt(jnp.mean(x * x, axis=-1, keepdims=True) + RMS_EPS)


def _dot(a, b):
    return jnp.dot(a, b, preferred_element_type=F32)


def _dot_nt(a, b):
    return lax.dot_general(a, b, (((1,), (1,)), ((), ())), preferred_element_type=F32)


def _pack_bf16_pairs(x):
    w = x.shape[1] // 2
    lo = lax.bitcast_convert_type(x[:, :w].astype(BF16).astype(F32), jnp.uint32) >> 16
    hi = lax.bitcast_convert_type(x[:, w:].astype(BF16).astype(F32), jnp.uint32) & jnp.uint32(0xFFFF0000)
    return lax.bitcast_convert_type(lo | hi, jnp.int32)


def _unpack_bf16_pairs(p):
    u = lax.bitcast_convert_type(p, jnp.uint32)
    lo = lax.bitcast_convert_type(u << 16, F32)
    hi = lax.bitcast_convert_type(u & jnp.uint32(0xFFFF0000), F32)
    return jnp.concatenate([lo, hi], axis=1)


def _resident(shape):
    nd = len(shape)
    return pl.BlockSpec(shape, lambda *_: (0,) * nd, pipeline_mode=pl.Buffered(1))


def _params(n_axes=1):
    return pltpu.CompilerParams(dimension_semantics=("arbitrary",) * n_axes, vmem_limit_bytes=VMEM_LIMIT)


def _proj_kernel(x_ref, gmix_ref, w_ref, gv_ref, gq_ref, gk_ref, hsum_ref,
                 u_ref, va_ref, q_ref, k_ref, v_ref, ga_ref, gb_ref):
    x = x_ref[...]
    n = ((x * _rms(x)) * gmix_ref[...]).astype(BF16)

    def section(i):
        return _dot(n, w_ref[:, _COL_SPLITS[i]:_COL_SPLITS[i + 1]])

    u_ref[...] = jax.nn.gelu(section(0)).astype(u_ref.dtype)
    va = jax.nn.gelu(section(1))
    va_ref[...] = ((va * _rms(va)) * gv_ref[...]).astype(va_ref.dtype)

    def head_norm(z, g_ref, scale):
        parts = []
        for c in range(B_WIDTH // MXU_DIM):
            zc = z[:, c * MXU_DIM:(c + 1) * MXU_DIM]
            ss = _dot((zc * zc).astype(BF16), hsum_ref[...])
            parts.append(zc * lax.rsqrt(ss * (1.0 / HEAD_DIM) + RMS_EPS))
        return jnp.concatenate(parts, axis=1) * (g_ref[...] * scale)

    q_ref[...] = head_norm(section(2), gq_ref, HEAD_DIM ** -0.5)
    k_ref[...] = head_norm(section(3), gk_ref, 1.0)
    v_ref[...] = section(4)
    ga_ref[...] = _sigmoid(section(5)).astype(ga_ref.dtype)
    gb_ref[...] = _sigmoid(section(6)).astype(gb_ref.dtype)


def _project(x, g_mix, w_in_bf, g_v, g_q_t, g_k_t, hsum, va_dtype):
    n_tok = x.shape[0]
    tm = TOKEN_TILE
    row = lambda w: pl.BlockSpec((tm, w), lambda i: (i, 0))
    outs = [(A_WIDTH, BF16), (A_WIDTH, va_dtype), (B_WIDTH, F32), (B_WIDTH, F32), (B_WIDTH, F32),
            (D_MODEL, BF16), (D_MODEL, BF16)]
    return pl.pallas_call(
        _proj_kernel,
        grid=(n_tok // tm,),
        in_specs=[row(D_MODEL), _resident(g_mix.shape), _resident(w_in_bf.shape), _resident(g_v.shape),
                  _resident(g_q_t.shape), _resident(g_k_t.shape), _resident(hsum.shape)],
        out_specs=[row(w) for w, _ in outs],
        out_shape=[jax.ShapeDtypeStruct((n_tok, w), dt) for w, dt in outs],
        compiler_params=_params(),
        name="project",
    )(x, g_mix, w_in_bf, g_v, g_q_t, g_k_t, hsum)


def _band_mask(band, first_block):
    qi = lax.broadcasted_iota(jnp.int32, (band, 2 * band), 0)
    kj = lax.broadcasted_iota(jnp.int32, (band, 2 * band), 1)
    dist = qi + band - kj
    valid = (dist >= 0) & (dist <= band) & ((kj >= band) | jnp.logical_not(first_block))
    return valid, dist


def _pair_attention(q2, k, v, valid, distf, slopes2):
    band = q2.shape[0]
    lane = lax.broadcasted_iota(jnp.int32, (band, LANES), 1)
    o_pair = jnp.zeros((band, LANES), F32)
    l_pair = jnp.zeros((band, LANES), F32)
    for hh in range(HEADS_PER_VREG):
        in_head = (lane >= hh * HEAD_DIM) & (lane < (hh + 1) * HEAD_DIM)
        qm = jnp.where(in_head, q2, 0.0).astype(BF16)
        s = _dot_nt(qm, k)
        s = jnp.where(valid, s - slopes2[hh] * distf, NEG_BIG)
        m = jnp.max(s, axis=-1, keepdims=True)
        e = jnp.exp(s - m)
        den = jnp.sum(e, axis=-1, keepdims=True)
        prob = (e * (1.0 / den)).astype(BF16)
        o_pair = jnp.where(in_head, _dot(prob, v), o_pair)
        l_pair = jnp.where(in_head, m + jnp.log(den), l_pair)
    return o_pair, l_pair


def _attn_dense_kernel(q_ref, kp_ref, kc_ref, vp_ref, vc_ref, o_ref, l_ref, *, band, slopes):
    valid, dist = _band_mask(band, pl.program_id(1) == 0)
    distf = dist.astype(F32)
    for p in range(PAIRS):
        cols = slice(p * LANES, (p + 1) * LANES)
        k = jnp.concatenate([kp_ref[:, cols], kc_ref[:, cols]], axis=0).astype(BF16)
        v = jnp.concatenate([vp_ref[:, cols], vc_ref[:, cols]], axis=0).astype(BF16)
        sl = [float(slopes[p * HEADS_PER_VREG + hh]) for hh in range(HEADS_PER_VREG)]
        o_pair, l_pair = _pair_attention(q_ref[:, cols], k, v, valid, distf, sl)
        o_ref[:, cols] = o_pair.astype(o_ref.dtype)
        l_ref[:, cols] = l_pair


def _attn_dilated_kernel(slopes_ref, q_ref, kp_ref, kc_ref, vp_ref, vc_ref, o_ref, l_ref, *, band, dil):
    p = pl.program_id(2)
    valid, dist = _band_mask(band, pl.program_id(1) == 0)
    distf = (dist * dil).astype(F32)
    sl = [slopes_ref[p * HEADS_PER_VREG + hh] for hh in range(HEADS_PER_VREG)]

    def residues(it, carry):
        for u in range(RESIDUE_UNROLL):
            rows = pl.ds(it * RESIDUE_UNROLL + u, band, stride=dil)
            k = jnp.concatenate([kp_ref[rows, :], kc_ref[rows, :]], axis=0).astype(BF16)
            v = jnp.concatenate([vp_ref[rows, :], vc_ref[rows, :]], axis=0).astype(BF16)
            o_pair, l_pair = _pair_attention(q_ref[rows, :], k, v, valid, distf, sl)
            o_ref[rows, :] = o_pair
            l_ref[rows, :] = l_pair
        return carry

    lax.fori_loop(0, dil // RESIDUE_UNROLL, residues, 0)


def _attend_prompt(q, k, v, gi, bsz, seq):
    win, dil = DILATION_GROUPS[gi]
    band = win // dil
    slopes = _alibi_slopes()[gi * H_SLOT:(gi + 1) * H_SLOT]
    if dil == 1:
        cur = pl.BlockSpec((None, band, GROUP_WIDTH), lambda b, j: (b, j, gi))
        prev = pl.BlockSpec((None, band, GROUP_WIDTH), lambda b, j: (b, jnp.maximum(j - 1, 0), gi))
        out = pl.BlockSpec((None, band, GROUP_WIDTH), lambda b, j: (b, j, 0))
        o, lse = pl.pallas_call(
            functools.partial(_attn_dense_kernel, band=band, slopes=slopes),
            grid=(bsz, seq // band),
            in_specs=[cur, prev, cur, prev, cur],
            out_specs=[out, out],
            out_shape=[jax.ShapeDtypeStruct((bsz, seq, GROUP_WIDTH), BF16),
                       jax.ShapeDtypeStruct((bsz, seq, GROUP_WIDTH), F32)],
            compiler_params=_params(2),
            name=f"attn_prompt_w{win}",
        )(q, k, k, v, v)
    else:
        col = lambda p: gi * PAIRS + p
        cur = pl.BlockSpec((None, win, LANES), lambda b, j, p, s: (b, j, col(p)))
        prev = pl.BlockSpec((None, win, LANES), lambda b, j, p, s: (b, jnp.maximum(j - 1, 0), col(p)))
        out = pl.BlockSpec((None, win, LANES), lambda b, j, p, s: (b, j, p))
        o, lse = pl.pallas_call(
            functools.partial(_attn_dilated_kernel, band=band, dil=dil),
            grid_spec=pltpu.PrefetchScalarGridSpec(
                num_scalar_prefetch=1, grid=(bsz, seq // win, PAIRS),
                in_specs=[cur, prev, cur, prev, cur], out_specs=[out, out]),
            out_shape=[jax.ShapeDtypeStruct((bsz, seq, GROUP_WIDTH), F32)] * 2,
            compiler_params=_params(3),
            name=f"attn_prompt_w{win}",
        )(jnp.asarray(slopes), q, k, k, v, v)
    return o.reshape(bsz * seq, GROUP_WIDTH), lse.reshape(bsz * seq, GROUP_WIDTH)


def _attn_sample_kernel(q_ref, kn_ref, vn_ref, c0_ref, c1_ref, c2_ref, o_ref, *, t_new, slopes):
    rows = H_SLOT * t_new
    pad_new = 16
    t_shift = t_new.bit_length() - 1
    caches = (c0_ref, c1_ref, c2_ref)
    row_id = lax.broadcasted_iota(jnp.int32, (rows, GROUP_WIDTH), 0)
    col_id = lax.broadcasted_iota(jnp.int32, (rows, GROUP_WIDTH), 1)
    own_head = (row_id >> t_shift) == (col_id >> (HEAD_DIM.bit_length() - 1))
    head_of_row = lax.broadcasted_iota(jnp.int32, (rows, 1), 0) >> t_shift
    zpad = jnp.zeros((pad_new - t_new, GROUP_WIDTH), F32)
    outs, lses = [], []
    for g, (win, dil) in enumerate(DILATION_GROUPS):
        buf_len = caches[g].shape[-1]
        cols = slice(g * GROUP_WIDTH, (g + 1) * GROUP_WIDTH)

        def mask_of(n_keys, first_pos, n_real):
            i_q = lax.broadcasted_iota(jnp.int32, (rows, n_keys), 0) & (t_new - 1)
            key = lax.broadcasted_iota(jnp.int32, (rows, n_keys), 1)
            delta = buf_len + i_q - (first_pos + key)
            ok = (delta >= 0) & ((delta & (dil - 1)) == 0) & (delta <= win) & (key < n_real)
            return ok, delta.astype(F32)

        kt = caches[g][0].reshape(GROUP_WIDTH, buf_len).astype(BF16)
        vt = caches[g][1].reshape(GROUP_WIDTH, buf_len).astype(BF16)
        qblk = jnp.where(own_head, jnp.concatenate([q_ref[:, cols]] * H_SLOT, axis=0), 0.0).astype(BF16)
        kn = jnp.concatenate([kn_ref[:, cols], zpad], axis=0).astype(BF16)
        vn = jnp.concatenate([vn_ref[:, cols], zpad], axis=0).astype(BF16)
        slope_rows = jnp.zeros((rows, 1), F32)
        for h in range(H_SLOT):
            slope_rows = jnp.where(head_of_row == h, float(slopes[g * H_SLOT + h]), slope_rows)
        ok_c, delta_c = mask_of(buf_len, 0, buf_len)
        ok_n, delta_n = mask_of(pad_new, buf_len, t_new)
        sc = jnp.where(ok_c, _dot(qblk, kt) - slope_rows * delta_c, NEG_BIG)
        sn = jnp.where(ok_n, _dot_nt(qblk, kn) - slope_rows * delta_n, NEG_BIG)
        m = jnp.maximum(jnp.max(sc, axis=-1, keepdims=True), jnp.max(sn, axis=-1, keepdims=True))
        ec = jnp.exp(sc - m)
        en = jnp.exp(sn - m)
        den = jnp.sum(ec, axis=-1, keepdims=True) + jnp.sum(en, axis=-1, keepdims=True)
        inv = 1.0 / den
        outs.append(_dot_nt((ec * inv).astype(BF16), vt) + _dot((en * inv).astype(BF16), vn))
        lses.append(m + jnp.log(den))
    mx = functools.reduce(jnp.maximum, lses)
    ws = [jnp.exp(l - mx) for l in lses]
    wsum = functools.reduce(jnp.add, ws)
    comb = sum((w / wsum) * o for w, o in zip(ws, outs))
    comb = jnp.where(own_head, comb, 0.0)
    o_tok = sum(comb[h * t_new:(h + 1) * t_new, :] for h in range(H_SLOT))
    o_ref[...] = o_tok.astype(o_ref.dtype)


def _attend_sample(q, k_new, v_new, caches_t, dbatch, t_new):
    assert t_new & (t_new - 1) == 0 and t_new <= 8
    tok = pl.BlockSpec((None, t_new, B_WIDTH), lambda b: (b, 0, 0))
    cache_specs = [pl.BlockSpec((None,) + c.shape[1:], lambda b: (b, 0, 0, 0, 0)) for c in caches_t]
    o = pl.pallas_call(
        functools.partial(_attn_sample_kernel, t_new=t_new, slopes=_alibi_slopes()),
        grid=(dbatch,),
        in_specs=[tok, tok, tok] + cache_specs,
        out_specs=pl.BlockSpec((None, t_new, GROUP_WIDTH), lambda b: (b, 0, 0)),
        out_shape=jax.ShapeDtypeStruct((dbatch, t_new, GROUP_WIDTH), BF16),
        compiler_params=_params(),
        name="attn_sample",
    )(q, k_new, v_new, *caches_t)
    return o.reshape(dbatch * t_new, GROUP_WIDTH)


def _kv_tail_kernel(k_ref, v_ref, o_ref):
    o_ref[0] = k_ref[...].T
    o_ref[1] = v_ref[...].T


def _kv_tail(k, v, gi, bsz, seq):
    keep = min(DILATION_GROUPS[gi][0], seq)
    pb = min(keep, 512)
    first = (seq - keep) // pb
    src = pl.BlockSpec((None, pb, GROUP_WIDTH), lambda b, j: (b, first + j, gi))
    return pl.pallas_call(
        _kv_tail_kernel,
        grid=(bsz, keep // pb),
        in_specs=[src, src],
        out_specs=pl.BlockSpec((None, 2, GROUP_WIDTH, pb), lambda b, j: (b, 0, 0, j)),
        out_shape=jax.ShapeDtypeStruct((bsz, 2, GROUP_WIDTH, keep), F32),
        compiler_params=_params(2),
        name=f"kv_tail_w{DILATION_GROUPS[gi][0]}",
    )(k, v)


def _kv_sample_kernel(k_ref, v_ref, o_ref, *, t_new, dbatch):
    for t in range(t_new):
        rows = pl.ds(t, dbatch, stride=t_new)
        o_ref[t, 0] = k_ref[rows, :].T
        o_ref[t, 1] = v_ref[rows, :].T


def _kv_sample(k, v, dbatch, t_new):
    src = pl.BlockSpec((dbatch * t_new, LANES), lambda s: (0, s))
    return pl.pallas_call(
        functools.partial(_kv_sample_kernel, t_new=t_new, dbatch=dbatch),
        grid=(B_WIDTH // LANES,),
        in_specs=[src, src],
        out_specs=pl.BlockSpec((None, t_new, 2, LANES, dbatch), lambda s: (s // PAIRS, 0, 0, s % PAIRS, 0)),
        out_shape=jax.ShapeDtypeStruct((N_GROUPS_B, t_new, 2, GROUP_WIDTH, dbatch), F32),
        compiler_params=_params(),
        name="kv_sample",
    )(k, v)


def _topk_route(logits):
    tm = logits.shape[0]
    lane = lax.broadcasted_iota(jnp.int32, logits.shape, 1).astype(F32)
    slot = lax.broadcasted_iota(jnp.int32, (tm, TOP_K), 1)
    val_out = jnp.zeros((tm, TOP_K), F32)
    work = logits
    ids, top = [], None
    for r in range(TOP_K):
        mx = jnp.max(work, axis=-1, keepdims=True)
        ix = jnp.min(jnp.where(work == mx, lane, float(N_EXPERTS)), axis=-1, keepdims=True)
        top = mx if top is None else top
        ids.append(ix)
        val_out = jnp.where(slot == r, jnp.exp(mx - top), val_out)
        work = jnp.where(lane == ix, -jnp.inf, work)
    gates = val_out / jnp.sum(val_out, axis=-1, keepdims=True)
    return ids, gates


def _finish_kernel(*refs, n_attn, n_alias, n_valid):
    n_in = 5 + n_attn + (n_attn if n_attn > 1 else 0) + 11
    ins, outs = refs[:n_in], refs[n_in + n_alias:]
    step = pl.program_id(0)

    @pl.when(step < n_valid)
    def _():
        _finish_tile(ins, outs, n_attn)

    @pl.when(step >= n_valid)
    def _():
        for ref in outs[:5]:
            ref[...] = jnp.zeros_like(ref)


def _finish_tile(ins, outs, n_attn):
    (h_ref, u_ref, va_ref, wm_ref, bias_ref), refs = ins[:5], ins[5:]
    o_refs, refs = refs[:n_attn], refs[n_attn:]
    n_lse = n_attn if n_attn > 1 else 0
    l_refs, refs = refs[:n_lse], refs[n_lse:]
    ga_ref, gb_ref, wa_ref, wb_ref, wo_ref, gmoe_ref, wrh_ref, wrl_ref, br_ref, tri_ref, cnt_in_ref = refs
    h1_ref, n2_ref, idx_ref, gate_ref, pos_ref, cnt_out_ref, carry_ref = outs
    tm = h_ref.shape[0]

    chunks = []
    for c in range(tm // CHUNK):
        rows = slice(c * CHUNK, (c + 1) * CHUNK)
        va_c = va_ref[rows, :].astype(BF16)
        mixed = jnp.concatenate(
            [_dot(wm_ref[g], va_c[:, g * A_GROUP_WIDTH:(g + 1) * A_GROUP_WIDTH]) for g in range(A_GROUPS)], axis=1)
        chunks.append((u_ref[rows, :].astype(F32) * (mixed + bias_ref[...])).astype(BF16))
    branch_a = _dot(jnp.concatenate(chunks, axis=0), wa_ref[...])

    if n_attn > 1:
        lses = [r[...] for r in l_refs]
        mx = functools.reduce(jnp.maximum, lses)
        ws = [jnp.exp(l - mx) for l in lses]
        wsum = functools.reduce(jnp.add, ws)
        o_b = (sum(w * o_ref[...].astype(F32) for w, o_ref in zip(ws, o_refs)) / wsum).astype(BF16)
    else:
        o_b = o_refs[0][...]
    branch_b = _dot(o_b, wb_ref[...])

    mix = ga_ref[...].astype(F32) * branch_a + gb_ref[...].astype(F32) * branch_b
    h1 = h_ref[...] + _dot(mix.astype(BF16), wo_ref[...])
    h1_ref[...] = h1

    n2 = (h1 * _rms(h1)) * gmoe_ref[...]
    n2_ref[...] = _pack_bf16_pairs(n2)
    n_hi = n2.astype(BF16)
    n_lo = (n2 - n_hi.astype(F32)).astype(BF16)
    logits = _dot(n_hi, wrh_ref[...]) + _dot(n_lo, wrh_ref[...]) + _dot(n_hi, wrl_ref[...]) + br_ref[...]
    ids, gates = _topk_route(logits)
    gate_ref[...] = gates

    @pl.when(pl.program_id(0) == 0)
    def _():
        carry_ref[...] = cnt_in_ref[...]

    lane = lax.broadcasted_iota(jnp.int32, logits.shape, 1).astype(F32)
    slot = lax.broadcasted_iota(jnp.int32, (tm, TOP_K), 1)
    sel = sum(jnp.where(lane == ix, 1.0, 0.0) for ix in ids)
    before = _dot(tri_ref[...], sel.astype(BF16)) + carry_ref[...]
    idx_out = jnp.zeros((tm, TOP_K), F32)
    pos_out = jnp.zeros((tm, TOP_K), F32)
    for r, ix in enumerate(ids):
        idx_out = jnp.where(slot == r, ix, idx_out)
        rank = jnp.sum(jnp.where(lane == ix, before, 0.0), axis=-1, keepdims=True)
        pos_out = jnp.where(slot == r, rank, pos_out)
    idx_ref[...] = idx_out.astype(jnp.int32)
    pos_ref[...] = pos_out.astype(jnp.int32)
    carry_ref[...] = carry_ref[...] + jnp.sum(sel, axis=0, keepdims=True)
    cnt_out_ref[...] = carry_ref[...]


def _finish(h, u, va, wm, bias_full, attn, ga, gb, consts, cnt_in, n_total, tile_offset, prev_outs):
    n_tok = h.shape[0]
    tm = TOKEN_TILE
    n_valid = n_tok // tm
    n_steps = n_valid if prev_outs is not None else n_total // tm - tile_offset
    row = lambda w: pl.BlockSpec((tm, w), lambda i: (jnp.minimum(i, n_valid - 1), 0))
    orow = lambda w: pl.BlockSpec((tm, w), lambda i: (i + tile_offset, 0))
    n_attn = len(attn)
    args = [h, u, va, wm, bias_full] + [o for o, _ in attn]
    specs = [row(D_MODEL), row(A_WIDTH), row(A_WIDTH), _resident(wm.shape), _resident(bias_full.shape)]
    specs += [row(GROUP_WIDTH)] * n_attn
    if n_attn > 1:
        args += [l for _, l in attn]
        specs += [row(GROUP_WIDTH)] * n_attn
    args += [ga, gb] + list(consts) + [cnt_in]
    specs += [row(D_MODEL), row(D_MODEL)] + [_resident(t.shape) for t in consts] + [_resident(cnt_in.shape)]
    aliases = {}
    if prev_outs is not None:
        for k, t in enumerate(prev_outs):
            aliases[len(args)] = k
            args.append(t)
            specs.append(pl.BlockSpec(memory_space=pl.ANY))
    widths = [(D_MODEL, F32), (D_MODEL // 2, jnp.int32), (TOP_K, jnp.int32), (TOP_K, F32), (TOP_K, jnp.int32)]
    outs = pl.pallas_call(
        functools.partial(_finish_kernel, n_attn=n_attn, n_alias=len(aliases), n_valid=n_valid),
        grid=(n_steps,),
        in_specs=specs,
        out_specs=[orow(w) for w, _ in widths] + [_resident(cnt_in.shape)],
        out_shape=[jax.ShapeDtypeStruct((n_total, w), dt) for w, dt in widths]
                  + [jax.ShapeDtypeStruct(cnt_in.shape, F32)],
        scratch_shapes=[pltpu.VMEM(cnt_in.shape, F32)],
        input_output_aliases=aliases,
        compiler_params=_params(),
        name=f"finish_{n_attn}",
    )(*args)
    return outs[:5], outs[5]


def _sc_gather_rows(table, idx):
    m = idx.shape[0]
    width = table.shape[1]
    per_worker = m // SC_WORKERS
    n_chunks = per_worker // SC_ROWS
    mesh = plsc.VectorSubcoreMesh(core_axis_name="c", subcore_axis_name="s",
                                  num_cores=SC_CORES, num_subcores=SC_SUBCORES)

    assert n_chunks % 2 == 0

    @functools.partial(
        pl.kernel, mesh=mesh,
        out_type=jax.ShapeDtypeStruct((m, width), table.dtype),
        scratch_types=[pltpu.VMEM((n_chunks, SC_ROWS), jnp.int32),
                       pltpu.VMEM((SC_ROWS, width), table.dtype),
                       pltpu.VMEM((SC_ROWS, width), table.dtype),
                       pltpu.SemaphoreType.DMA,
                       pltpu.SemaphoreType.DMA],
        name="sc_gather_rows",
    )
    def gather(table_hbm, idx_hbm, out_hbm, idx_v, rows_a, rows_b, sem_a, sem_b):
        wid = lax.axis_index("s") * SC_CORES + lax.axis_index("c")
        base = wid * per_worker
        pltpu.sync_copy(idx_hbm.at[wid], idx_v)

        def fetch(c, rows, sem):
            return pltpu.make_async_copy(table_hbm.at[idx_v.at[c]], rows, sem)

        def put(c, rows):
            off = pl.multiple_of(base + c * SC_ROWS, SC_ROWS)
            pltpu.sync_copy(rows, out_hbm.at[pl.ds(off, SC_ROWS)])

        fetch(0, rows_a, sem_a).start()

        @pl.loop(0, n_chunks, step=2)
        def _(c):
            fetch(c, rows_a, sem_a).wait()
            fetch(c + 1, rows_b, sem_b).start()
            put(c, rows_a)
            fetch(c + 1, rows_b, sem_b).wait()

            @pl.when(c + 2 < n_chunks)
            def _():
                fetch(c + 2, rows_a, sem_a).start()

            put(c + 1, rows_b)

    return gather(table, idx.reshape(SC_WORKERS, n_chunks, SC_ROWS))


def _sc_scatter_rows(src, dest4, n_out):
    n_src, width = src.shape
    top_k = dest4.shape[1]
    per_worker = n_src // SC_WORKERS
    rows = SC_SCATTER_ROWS
    n_chunks = per_worker // rows
    assert per_worker % rows == 0 and n_chunks % 2 == 0
    dest = dest4.reshape(SC_WORKERS, n_chunks, rows, top_k).transpose(0, 1, 3, 2)
    mesh = plsc.VectorSubcoreMesh(core_axis_name="c", subcore_axis_name="s",
                                  num_cores=SC_CORES, num_subcores=SC_SUBCORES)

    @functools.partial(
        pl.kernel, mesh=mesh,
        out_type=jax.ShapeDtypeStruct((n_out, width), src.dtype),
        scratch_types=[pltpu.VMEM((n_chunks, top_k, rows), jnp.int32),
                       pltpu.VMEM((rows, width), src.dtype),
                       pltpu.VMEM((rows, width), src.dtype),
                       pltpu.SemaphoreType.DMA,
                       pltpu.SemaphoreType.DMA,
                       pltpu.SemaphoreType.DMA],
        name="sc_scatter_rows",
    )
    def scatter(src_hbm, dest_hbm, out_hbm, idx_v, rows_a, rows_b, sem_a, sem_b, sem_w):
        wid = lax.axis_index("s") * SC_CORES + lax.axis_index("c")
        base = wid * per_worker
        pltpu.sync_copy(dest_hbm.at[wid], idx_v)

        def load(c, buf, sem):
            off = pl.multiple_of(base + c * rows, rows)
            return pltpu.make_async_copy(src_hbm.at[pl.ds(off, rows)], buf, sem)

        def spread(c, buf):
            copies = [pltpu.make_async_copy(buf, out_hbm.at[idx_v.at[c, k]], sem_w) for k in range(top_k)]
            for cp in copies:
                cp.start()
            for cp in copies:
                cp.wait()

        load(0, rows_a, sem_a).start()

        @pl.loop(0, n_chunks, step=2)
        def _(c):
            load(c, rows_a, sem_a).wait()
            load(c + 1, rows_b, sem_b).start()
            spread(c, rows_a)
            load(c + 1, rows_b, sem_b).wait()

            @pl.when(c + 2 < n_chunks)
            def _():
                load(c + 2, rows_a, sem_a).start()

            spread(c + 1, rows_b)

    return scatter(src, dest)


def _expert_kernel(be_ref, nused_ref, x_ref, wgu_ref, wd_ref, bgu_ref, bd_ref, perm_ref, y_ref, wgu_s, wd_s):
    i = pl.program_id(0)
    active = i < nused_ref[0]
    fresh = (i == 0) | (be_ref[i] != be_ref[jnp.maximum(i - 1, 0)])

    @pl.when(active & fresh)
    def _():
        wgu_s[...] = wgu_ref[...].astype(BF16)
        half, part = D_FF // 2, MXU_DIM // 2
        for t in range(D_FF // MXU_DIM):
            src = jnp.concatenate([wd_ref[t * part:(t + 1) * part, :],
                                   wd_ref[half + t * part:half + (t + 1) * part, :]], axis=0).astype(BF16)
            wd_s[t * MXU_DIM:(t + 1) * MXU_DIM, :] = _dot(perm_ref[...], src).astype(BF16)

    @pl.when(active)
    def _():
        x = _unpack_bf16_pairs(x_ref[...]).astype(BF16)
        hh = _dot(x, wgu_s[...]) + bgu_ref[...]
        partner = pltpu.roll(hh, 2 * D_FF - 1, axis=1)
        h_glu = jnp.minimum(hh, SWIGLU_LIMIT)
        h_lin = jnp.clip(partner, -SWIGLU_LIMIT, SWIGLU_LIMIT)
        act = h_glu * _sigmoid(SWIGLU_ALPHA * h_glu) * (h_lin + 1.0)
        lane = lax.broadcasted_iota(jnp.int32, act.shape, 1)
        act = jnp.where((lane & 1) == 0, act, 0.0)
        merged = act[:, :D_FF] + pltpu.roll(act[:, D_FF:], 1, axis=1)
        y_ref[...] = _pack_bf16_pairs(_dot(merged.astype(BF16), wd_s[...]) + bd_ref[...])

    @pl.when(jnp.logical_not(active))
    def _():
        y_ref[...] = jnp.zeros_like(y_ref)


def _experts(xb, block_expert, n_used, w_gate_up, w_down, b_gate_up, b_down, perm):
    n_slots = xb.shape[0]
    n_blocks = n_slots // EXPERT_BLOCK
    w_spec = lambda k, n: pl.BlockSpec((None, k, n), lambda i, be, nu: (be[i], 0, 0))
    blk = pl.BlockSpec((EXPERT_BLOCK, D_MODEL // 2), lambda i, be, nu: (i, 0))
    return pl.pallas_call(
        _expert_kernel,
        grid_spec=pltpu.PrefetchScalarGridSpec(
            num_scalar_prefetch=2, grid=(n_blocks,),
            in_specs=[blk, w_spec(D_MODEL, 2 * D_FF), w_spec(D_FF, D_MODEL), w_spec(1, 2 * D_FF), w_spec(1, D_MODEL),
                      pl.BlockSpec(perm.shape, lambda i, be, nu: (0, 0), pipeline_mode=pl.Buffered(1))],
            out_specs=blk,
            scratch_shapes=[pltpu.VMEM((D_MODEL, 2 * D_FF), BF16), pltpu.VMEM((D_FF, D_MODEL), BF16)]),
        out_shape=jax.ShapeDtypeStruct((n_slots, D_MODEL // 2), jnp.int32),
        compiler_params=_params(),
        name="experts",
    )(block_expert, n_used, xb, w_gate_up, w_down, b_gate_up, b_down, perm)


def _final_kernel(h1_ref, yg_ref, gate_ref, p_ref, gple_ref, wg_ref, wp_ref, out_ref):
    h2 = h1_ref[...]
    gates = gate_ref[...]
    for k in range(TOP_K):
        h2 = h2 + gates[:, k:k + 1] * _unpack_bf16_pairs(yg_ref[k])
    n3 = ((h2 * _rms(h2)) * gple_ref[...]).astype(BF16)
    gate = _sigmoid(_dot(n3, wg_ref[...]))
    out_ref[...] = h2 + gate * _dot(p_ref[...].astype(BF16), wp_ref[...])


def _final(h1, yg, gates, p, tile_offset, g_ple, w_ple_gate, w_ple_proj):
    n_tok = p.shape[0]
    tm = TOKEN_TILE
    return pl.pallas_call(
        _final_kernel,
        grid=(n_tok // tm,),
        in_specs=[pl.BlockSpec((tm, D_MODEL), lambda i: (i + tile_offset, 0)),
                  pl.BlockSpec((TOP_K, tm, D_MODEL // 2), lambda i: (0, i + tile_offset, 0)),
                  pl.BlockSpec((tm, TOP_K), lambda i: (i + tile_offset, 0)),
                  pl.BlockSpec((tm, PLE_DIM), lambda i: (i, 0)),
                  _resident(g_ple.shape), _resident(w_ple_gate.shape), _resident(w_ple_proj.shape)],
        out_specs=pl.BlockSpec((tm, D_MODEL), lambda i: (i, 0)),
        out_shape=jax.ShapeDtypeStruct((n_tok, D_MODEL), F32),
        compiler_params=_params(),
        name="final",
    )(h1, yg, gates, p, g_ple, w_ple_gate, w_ple_proj)


def _routing_tables(idx4, pos4, counts, n_slots):
    counts = counts.reshape(N_EXPERTS).astype(jnp.int32)
    pcounts = (counts + EXPERT_BLOCK - 1) // EXPERT_BLOCK * EXPERT_BLOCK
    pends = jnp.cumsum(pcounts)
    pstarts = pends - pcounts
    experts = jnp.arange(N_EXPERTS, dtype=jnp.int32)
    start4 = jnp.sum(jnp.where(idx4[:, :, None] == experts, pstarts, 0), axis=-1)
    dest4 = (start4 + pos4).astype(jnp.int32)
    n_blocks = n_slots // EXPERT_BLOCK
    block_start = jnp.arange(n_blocks, dtype=jnp.int32) * EXPERT_BLOCK
    block_expert = jnp.minimum(jnp.sum(block_start[:, None] >= pends[None, :], axis=1), N_EXPERTS - 1).astype(jnp.int32)
    n_used = (pends[-1] // EXPERT_BLOCK).astype(jnp.int32).reshape(1)
    return dest4, block_expert, n_used


def kernel(x_prompt, x_sample, cache_kv_w128, cache_kv_w512, cache_kv_w2048, p_prompt, p_sample, g_mix, w_in, g_v, g_q, g_k, w_spatial, b_spatial, w_branch_a, w_branch_b, w_out, g_moe, w_router, b_router, w_gate_up, b_gate_up, w_down, b_down, g_ple, w_ple_gate, w_ple_proj):
    bsz, seq, _ = x_prompt.shape
    dbatch, t_new, _ = x_sample.shape
    assert g_mix.shape[0] == 1
    caches = (cache_kv_w128, cache_kv_w512, cache_kv_w2048)
    l = 0
    n_p, n_s = bsz * seq, dbatch * t_new
    n_tok = n_p + n_s
    assert n_p % TOKEN_TILE == 0 and n_s % TOKEN_TILE == 0

    row2 = lambda t: t.reshape(1, -1)
    w_in_bf = w_in[l].astype(BF16)
    g_q_t = jnp.tile(g_q[l], B_HEADS).reshape(1, B_WIDTH)
    g_k_t = jnp.tile(g_k[l], B_HEADS).reshape(1, B_WIDTH)
    hid = np.arange(MXU_DIM) // HEAD_DIM
    hsum = jnp.asarray(hid[:, None] == hid[None, :], BF16)
    tril = jnp.tril(jnp.ones((CHUNK, CHUNK), bool))
    wm_prompt = jnp.where(tril[None], w_spatial[l], 0).astype(BF16)
    bias_prompt = jnp.repeat(b_spatial[l].T, A_GROUP_WIDTH, axis=1)
    reps = CHUNK // t_new
    small = jnp.where(tril[None, :t_new, :t_new], w_spatial[l][:, :t_new, :t_new], 0)
    wm_sample = jnp.einsum("ab,gij->gaibj", jnp.eye(reps, dtype=F32), small).reshape(A_GROUPS, CHUNK, CHUNK).astype(BF16)
    bias_sample = jnp.tile(bias_prompt[:t_new], (reps, 1))
    wr_hi = w_router[l].astype(BF16)
    wr_lo = (w_router[l] - wr_hi.astype(F32)).astype(BF16)
    tri = jnp.asarray(np.tril(np.ones((TOKEN_TILE, TOKEN_TILE), np.float32), -1), BF16)
    consts = (w_branch_a[l].astype(BF16), w_branch_b[l].astype(BF16), w_out[l].astype(BF16), row2(g_moe[l]),
              wr_hi, wr_lo, row2(b_router[l]), tri)
    part = MXU_DIM // 2
    perm_np = np.zeros((MXU_DIM, MXU_DIM), np.float32)
    perm_np[2 * np.arange(part), np.arange(part)] = 1.0
    perm_np[2 * np.arange(part) + 1, part + np.arange(part)] = 1.0
    perm = jnp.asarray(perm_np, BF16)

    proj = functools.partial(_project, g_mix=row2(g_mix[l]), w_in_bf=w_in_bf, g_v=row2(g_v[l]),
                             g_q_t=g_q_t, g_k_t=g_k_t, hsum=hsum)

    xp = x_prompt.reshape(n_p, D_MODEL)
    u_p, va_p, q_p, k_p, v_p, ga_p, gb_p = proj(xp, va_dtype=BF16)
    seq3 = lambda t: t.reshape(bsz, seq, B_WIDTH)
    attn_p = [_attend_prompt(seq3(q_p), seq3(k_p), seq3(v_p), gi, bsz, seq) for gi in range(N_GROUPS_B)]
    zero_counts = jnp.zeros((1, N_EXPERTS), F32)
    outs_p, cnt_p = _finish(xp, u_p, va_p, wm_prompt, bias_prompt, attn_p, ga_p, gb_p, consts, zero_counts,
                            n_tok, 0, None)

    xs = x_sample.reshape(n_s, D_MODEL)
    u_s, va_s, q_s, k_s, v_s, ga_s, gb_s = proj(xs, va_dtype=F32)
    tok3 = lambda t: t.reshape(dbatch, t_new, B_WIDTH)
    caches_t = [jnp.transpose(c[l], (0, 2, 3, 4, 1)) for c in caches]
    o_s = _attend_sample(tok3(q_s), tok3(k_s), tok3(v_s), caches_t, dbatch, t_new)
    (h1, n2, idx4, gates, pos4), counts = _finish(xs, u_s, va_s, wm_sample, bias_sample, [(o_s, None)], ga_s, gb_s,
                                                  consts, cnt_p, n_tok, n_p // TOKEN_TILE, outs_p)

    n_blocks = -(-n_tok * TOP_K // EXPERT_BLOCK) + N_EXPERTS
    n_slots = n_blocks * EXPERT_BLOCK
    dest4, block_expert, n_used = _routing_tables(idx4, pos4, counts, n_slots)
    xb = _sc_scatter_rows(n2, dest4, n_slots)
    yb = _experts(xb, block_expert, n_used, w_gate_up[l], w_down[l], b_gate_up[l][:, None, :], b_down[l][:, None, :], perm)
    yg = _sc_gather_rows(yb, dest4.T.reshape(-1)).reshape(TOP_K, n_tok, D_MODEL // 2)
    fin = functools.partial(_final, g_ple=row2(g_ple[l]), w_ple_gate=w_ple_gate[l].astype(BF16),
                            w_ple_proj=w_ple_proj[l].astype(BF16))
    y_prompt = fin(h1, yg, gates, p_prompt[l].reshape(n_p, PLE_DIM), 0).reshape(bsz, seq, D_MODEL)
    y_sample = fin(h1, yg, gates, p_sample[l].reshape(n_s, PLE_DIM), n_p // TOKEN_TILE).reshape(dbatch, t_new, D_MODEL)

    kv_prompt = []
    for gi in range(N_GROUPS_B):
        t = _kv_tail(seq3(k_p), seq3(v_p), gi, bsz, seq)
        keep = t.shape[-1]
        kv_prompt.append(jnp.transpose(t.reshape(bsz, 2, H_SLOT, HEAD_DIM, keep), (0, 4, 1, 2, 3))[None])
    kvs = _kv_sample(k_s, v_s, dbatch, t_new)
    kv_sample = [jnp.transpose(kvs[gi].reshape(t_new, 2, H_SLOT, HEAD_DIM, dbatch), (4, 0, 1, 2, 3))[None]
                 for gi in range(N_GROUPS_B)]
    va_out = va_s.reshape(1, dbatch, t_new, A_WIDTH)
    return (y_prompt, y_sample, *kv_prompt, *kv_sample, va_out)
```

```python
import functools

import numpy as np
import jax
import jax.numpy as jnp
from jax import lax
from jax.experimental import pallas as pl
from jax.experimental.pallas import tpu as pltpu
from jax.experimental.pallas import tpu_sc as plsc

F32 = jnp.float32
BF16 = jnp.bfloat16

D_MODEL = 1024
A_WIDTH = 1024
A_GROUPS = 4
A_GROUP_WIDTH = A_WIDTH // A_GROUPS
CHUNK = 128
HEAD_DIM = 64
H_SLOT = 8
GROUP_WIDTH = H_SLOT * HEAD_DIM
DILATION_GROUPS = ((128, 1), (512, 4), (2048, 16))
N_GROUPS_B = len(DILATION_GROUPS)
B_HEADS = H_SLOT * N_GROUPS_B
B_WIDTH = B_HEADS * HEAD_DIM
N_EXPERTS = 32
TOP_K = 4
D_FF = 1024
SWIGLU_ALPHA = 1.702
SWIGLU_LIMIT = 7.0
PLE_DIM = 256
RMS_EPS = 1e-6
NEG_BIG = -1e30
LOG2_E = float(np.log2(np.e))
LN_2 = float(np.log(2.0))

LANES = 128
SUBLANES = 8
MXU_DIM = 256
HEADS_PER_VREG = LANES // HEAD_DIM
PAIRS = GROUP_WIDTH // LANES
RESIDUE_UNROLL = 4
TOKEN_TILE = 512
EXPERT_BLOCK = 256
VMEM_LIMIT = 56 * 1024 * 1024

SC_CORES = 2
SC_SUBCORES = 16
SC_WORKERS = SC_CORES * SC_SUBCORES
SC_ROWS = 32
SC_SCATTER_ROWS = 16

_COL_SPLITS = np.cumsum([0, A_WIDTH, A_WIDTH, B_WIDTH, B_WIDTH, B_WIDTH, D_MODEL, D_MODEL]).tolist()


def _alibi_slopes():
    return np.exp2(-8.0 * np.arange(1, B_HEADS + 1, dtype=np.float32) / B_HEADS).astype(np.float32)


def _sigmoid(x):
    return 1.0 / (1.0 + jnp.exp(-x))


def _rms(x):
    return lax.rsqrt(jnp.mean(x * x, axis=-1, keepdims=True) + RMS_EPS)


def _dot(a, b):
    return jnp.dot(a, b, preferred_element_type=F32)


def _dot_nt(a, b):
    return lax.dot_general(a, b, (((1,), (1,)), ((), ())), preferred_element_type=F32)


def _pack_bf16_pairs(x):
    w = x.shape[1] // 2
    lo = lax.bitcast_convert_type(x[:, :w].astype(BF16).astype(F32), jnp.uint32) >> 16
    hi = lax.bitcast_convert_type(x[:, w:].astype(BF16).astype(F32), jnp.uint32) & jnp.uint32(0xFFFF0000)
    return lax.bitcast_convert_type(lo | hi, jnp.int32)


def _unpack_bf16_pairs(p):
    u = lax.bitcast_convert_type(p, jnp.uint32)
    lo = lax.bitcast_convert_type(u << 16, F32)
    hi = lax.bitcast_convert_type(u & jnp.uint32(0xFFFF0000), F32)
    return jnp.concatenate([lo, hi], axis=1)


def _resident(shape):
    nd = len(shape)
    return pl.BlockSpec(shape, lambda *_: (0,) * nd, pipeline_mode=pl.Buffered(1))


def _params(n_axes=1):
    return pltpu.CompilerParams(dimension_semantics=("arbitrary",) * n_axes, vmem_limit_bytes=VMEM_LIMIT)


def _proj_kernel(x_ref, gmix_ref, w_ref, gv_ref, gq_ref, gk_ref, hsum_ref,
                 u_ref, va_ref, q_ref, k_ref, v_ref, ga_ref, gb_ref):
    x = x_ref[...]
    n = ((x * _rms(x)) * gmix_ref[...]).astype(BF16)

    def section(i):
        return _dot(n, w_ref[:, _COL_SPLITS[i]:_COL_SPLITS[i + 1]])

    u_ref[...] = jax.nn.gelu(section(0)).astype(u_ref.dtype)
    va = jax.nn.gelu(section(1))
    va_ref[...] = ((va * _rms(va)) * gv_ref[...]).astype(va_ref.dtype)

    def head_norm(z, g_ref, scale):
        parts = []
        for c in range(B_WIDTH // MXU_DIM):
            zc = z[:, c * MXU_DIM:(c + 1) * MXU_DIM]
            ss = _dot((zc * zc).astype(BF16), hsum_ref[...])
            parts.append(zc * lax.rsqrt(ss * (1.0 / HEAD_DIM) + RMS_EPS))
        return jnp.concatenate(parts, axis=1) * (g_ref[...] * scale)

    q_ref[...] = head_norm(section(2), gq_ref, HEAD_DIM ** -0.5 * LOG2_E)
    k_ref[...] = head_norm(section(3), gk_ref, 1.0)
    v_ref[...] = section(4)
    ga_ref[...] = _sigmoid(section(5)).astype(ga_ref.dtype)
    gb_ref[...] = _sigmoid(section(6)).astype(gb_ref.dtype)


def _project(x, g_mix, w_in_bf, g_v, g_q_t, g_k_t, hsum, va_dtype):
    n_tok = x.shape[0]
    tm = TOKEN_TILE
    row = lambda w: pl.BlockSpec((tm, w), lambda i: (i, 0))
    outs = [(A_WIDTH, BF16), (A_WIDTH, va_dtype), (B_WIDTH, F32), (B_WIDTH, F32), (B_WIDTH, F32),
            (D_MODEL, BF16), (D_MODEL, BF16)]
    return pl.pallas_call(
        _proj_kernel,
        grid=(n_tok // tm,),
        in_specs=[row(D_MODEL), _resident(g_mix.shape), _resident(w_in_bf.shape), _resident(g_v.shape),
                  _resident(g_q_t.shape), _resident(g_k_t.shape), _resident(hsum.shape)],
        out_specs=[row(w) for w, _ in outs],
        out_shape=[jax.ShapeDtypeStruct((n_tok, w), dt) for w, dt in outs],
        compiler_params=_params(),
        name="project",
    )(x, g_mix, w_in_bf, g_v, g_q_t, g_k_t, hsum)


def _band_bias(band, dil, slopes):
    qi = jnp.arange(band, dtype=jnp.int32)[:, None]
    kj = jnp.arange(2 * band, dtype=jnp.int32)[None, :]
    dist = qi + band - kj
    in_band = (dist >= 0) & (dist <= band)
    valid = jnp.stack([in_band & (kj >= band), in_band])
    penalty = (jnp.asarray(slopes, F32) * LOG2_E)[:, None, None] * (dist * dil).astype(F32)[None]
    return jnp.where(valid[:, None], -penalty[None], NEG_BIG)


def _pair_attention(q2, k, v, bias2):
    band = q2.shape[0]
    lane = lax.broadcasted_iota(jnp.int32, (band, LANES), 1)
    o_pair = jnp.zeros((band, LANES), F32)
    l_pair = jnp.zeros((band, LANES), F32)
    for hh in range(HEADS_PER_VREG):
        in_head = (lane >= hh * HEAD_DIM) & (lane < (hh + 1) * HEAD_DIM)
        qm = jnp.where(in_head, q2, 0.0).astype(BF16)
        s = _dot_nt(qm, k) + bias2[hh]
        m = jnp.max(s, axis=-1, keepdims=True)
        e = jnp.exp2(s - m)
        den = jnp.sum(e, axis=-1, keepdims=True)
        prob = (e * (1.0 / den)).astype(BF16)
        o_pair = jnp.where(in_head, _dot(prob, v), o_pair)
        l_pair = jnp.where(in_head, m * LN_2 + jnp.log(den), l_pair)
    return o_pair, l_pair


def _attn_dense_kernel(q_ref, kp_ref, kc_ref, vp_ref, vc_ref, bias_ref, o_ref, l_ref):
    for p in range(PAIRS):
        cols = slice(p * LANES, (p + 1) * LANES)
        k = jnp.concatenate([kp_ref[:, cols], kc_ref[:, cols]], axis=0).astype(BF16)
        v = jnp.concatenate([vp_ref[:, cols], vc_ref[:, cols]], axis=0).astype(BF16)
        bias2 = [bias_ref[p * HEADS_PER_VREG + hh] for hh in range(HEADS_PER_VREG)]
        o_pair, l_pair = _pair_attention(q_ref[:, cols], k, v, bias2)
        o_ref[:, cols] = o_pair.astype(o_ref.dtype)
        l_ref[:, cols] = l_pair


def _attn_dilated_kernel(q_ref, kp_ref, kc_ref, vp_ref, vc_ref, bias_ref, o_ref, l_ref, *, band, dil):
    def residues(it, carry):
        for u in range(RESIDUE_UNROLL):
            rows = pl.ds(it * RESIDUE_UNROLL + u, band, stride=dil)
            k = jnp.concatenate([kp_ref[rows, :], kc_ref[rows, :]], axis=0).astype(BF16)
            v = jnp.concatenate([vp_ref[rows, :], vc_ref[rows, :]], axis=0).astype(BF16)
            bias2 = [bias_ref[hh] for hh in range(HEADS_PER_VREG)]
            o_pair, l_pair = _pair_attention(q_ref[rows, :], k, v, bias2)
            o_ref[rows, :] = o_pair
            l_ref[rows, :] = l_pair
        return carry

    lax.fori_loop(0, dil // RESIDUE_UNROLL, residues, 0)


def _attend_prompt(q, k, v, gi, bsz, seq):
    win, dil = DILATION_GROUPS[gi]
    band = win // dil
    bias = _band_bias(band, dil, _alibi_slopes()[gi * H_SLOT:(gi + 1) * H_SLOT])
    if dil == 1:
        cur = pl.BlockSpec((None, band, GROUP_WIDTH), lambda b, j: (b, j, gi))
        prev = pl.BlockSpec((None, band, GROUP_WIDTH), lambda b, j: (b, jnp.maximum(j - 1, 0), gi))
        tab = pl.BlockSpec((None, H_SLOT, band, 2 * band), lambda b, j: (jnp.minimum(j, 1), 0, 0, 0))
        out = pl.BlockSpec((None, band, GROUP_WIDTH), lambda b, j: (b, j, 0))
        o, lse = pl.pallas_call(
            _attn_dense_kernel,
            grid=(bsz, seq // band),
            in_specs=[cur, prev, cur, prev, cur, tab],
            out_specs=[out, out],
            out_shape=[jax.ShapeDtypeStruct((bsz, seq, GROUP_WIDTH), BF16),
                       jax.ShapeDtypeStruct((bsz, seq, GROUP_WIDTH), F32)],
            compiler_params=_params(2),
            name=f"attn_prompt_w{win}",
        )(q, k, k, v, v, bias)
    else:
        col = lambda p: gi * PAIRS + p
        cur = pl.BlockSpec((None, win, LANES), lambda b, j, p: (b, j, col(p)))
        prev = pl.BlockSpec((None, win, LANES), lambda b, j, p: (b, jnp.maximum(j - 1, 0), col(p)))
        tab = pl.BlockSpec((None, HEADS_PER_VREG, band, 2 * band), lambda b, j, p: (jnp.minimum(j, 1), p, 0, 0))
        out = pl.BlockSpec((None, win, LANES), lambda b, j, p: (b, j, p))
        o, lse = pl.pallas_call(
            functools.partial(_attn_dilated_kernel, band=band, dil=dil),
            grid=(bsz, seq // win, PAIRS),
            in_specs=[cur, prev, cur, prev, cur, tab],
            out_specs=[out, out],
            out_shape=[jax.ShapeDtypeStruct((bsz, seq, GROUP_WIDTH), F32)] * 2,
            compiler_params=_params(3),
            name=f"attn_prompt_w{win}",
        )(q, k, k, v, v, bias)
    return o.reshape(bsz * seq, GROUP_WIDTH), lse.reshape(bsz * seq, GROUP_WIDTH)


def _attn_sample_kernel(q_ref, kn_ref, vn_ref, c0_ref, c1_ref, c2_ref, o_ref, *, t_new, slopes):
    rows = H_SLOT * t_new
    pad_new = 16
    t_shift = t_new.bit_length() - 1
    caches = (c0_ref, c1_ref, c2_ref)
    row_id = lax.broadcasted_iota(jnp.int32, (rows, GROUP_WIDTH), 0)
    col_id = lax.broadcasted_iota(jnp.int32, (rows, GROUP_WIDTH), 1)
    own_head = (row_id >> t_shift) == (col_id >> (HEAD_DIM.bit_length() - 1))
    head_of_row = lax.broadcasted_iota(jnp.int32, (rows, 1), 0) >> t_shift
    zpad = jnp.zeros((pad_new - t_new, GROUP_WIDTH), F32)
    outs, lses = [], []
    for g, (win, dil) in enumerate(DILATION_GROUPS):
        buf_len = caches[g].shape[-1]
        cols = slice(g * GROUP_WIDTH, (g + 1) * GROUP_WIDTH)

        def mask_of(n_keys, first_pos, n_real):
            i_q = lax.broadcasted_iota(jnp.int32, (rows, n_keys), 0) & (t_new - 1)
            key = lax.broadcasted_iota(jnp.int32, (rows, n_keys), 1)
            delta = buf_len + i_q - (first_pos + key)
            ok = (delta >= 0) & ((delta & (dil - 1)) == 0) & (delta <= win) & (key < n_real)
            return ok, delta.astype(F32)

        kt = caches[g][0].reshape(GROUP_WIDTH, buf_len).astype(BF16)
        vt = caches[g][1].reshape(GROUP_WIDTH, buf_len).astype(BF16)
        qblk = jnp.where(own_head, jnp.concatenate([q_ref[:, cols]] * H_SLOT, axis=0), 0.0).astype(BF16)
        kn = jnp.concatenate([kn_ref[:, cols], zpad], axis=0).astype(BF16)
        vn = jnp.concatenate([vn_ref[:, cols], zpad], axis=0).astype(BF16)
        slope_rows = jnp.zeros((rows, 1), F32)
        for h in range(H_SLOT):
            slope_rows = jnp.where(head_of_row == h, float(slopes[g * H_SLOT + h]) * LOG2_E, slope_rows)
        ok_c, delta_c = mask_of(buf_len, 0, buf_len)
        ok_n, delta_n = mask_of(pad_new, buf_len, t_new)
        sc = jnp.where(ok_c, _dot(qblk, kt) - slope_rows * delta_c, NEG_BIG)
        sn = jnp.where(ok_n, _dot_nt(qblk, kn) - slope_rows * delta_n, NEG_BIG)
        m = jnp.maximum(jnp.max(sc, axis=-1, keepdims=True), jnp.max(sn, axis=-1, keepdims=True))
        ec = jnp.exp2(sc - m)
        en = jnp.exp2(sn - m)
        den = jnp.sum(ec, axis=-1, keepdims=True) + jnp.sum(en, axis=-1, keepdims=True)
        inv = 1.0 / den
        outs.append(_dot_nt((ec * inv).astype(BF16), vt) + _dot((en * inv).astype(BF16), vn))
        lses.append(m * LN_2 + jnp.log(den))
    mx = functools.reduce(jnp.maximum, lses)
    ws = [jnp.exp(l - mx) for l in lses]
    wsum = functools.reduce(jnp.add, ws)
    comb = sum((w / wsum) * o for w, o in zip(ws, outs))
    comb = jnp.where(own_head, comb, 0.0)
    o_tok = sum(comb[h * t_new:(h + 1) * t_new, :] for h in range(H_SLOT))
    o_ref[...] = o_tok.astype(o_ref.dtype)


def _attend_sample(q, k_new, v_new, caches_t, dbatch, t_new):
    assert t_new & (t_new - 1) == 0 and t_new <= 8
    tok = pl.BlockSpec((None, t_new, B_WIDTH), lambda b: (b, 0, 0))
    cache_specs = [pl.BlockSpec((None,) + c.shape[1:], lambda b: (b, 0, 0, 0, 0)) for c in caches_t]
    o = pl.pallas_call(
        functools.partial(_attn_sample_kernel, t_new=t_new, slopes=_alibi_slopes()),
        grid=(dbatch,),
        in_specs=[tok, tok, tok] + cache_specs,
        out_specs=pl.BlockSpec((None, t_new, GROUP_WIDTH), lambda b: (b, 0, 0)),
        out_shape=jax.ShapeDtypeStruct((dbatch, t_new, GROUP_WIDTH), BF16),
        compiler_params=_params(),
        name="attn_sample",
    )(q, k_new, v_new, *caches_t)
    return o.reshape(dbatch * t_new, GROUP_WIDTH)


def _kv_tail_kernel(k_ref, v_ref, o_ref):
    o_ref[0] = k_ref[...].T
    o_ref[1] = v_ref[...].T


def _kv_tail(k, v, gi, bsz, seq):
    keep = min(DILATION_GROUPS[gi][0], seq)
    pb = min(keep, 512)
    first = (seq - keep) // pb
    src = pl.BlockSpec((None, pb, GROUP_WIDTH), lambda b, j: (b, first + j, gi))
    return pl.pallas_call(
        _kv_tail_kernel,
        grid=(bsz, keep // pb),
        in_specs=[src, src],
        out_specs=pl.BlockSpec((None, 2, GROUP_WIDTH, pb), lambda b, j: (b, 0, 0, j)),
        out_shape=jax.ShapeDtypeStruct((bsz, 2, GROUP_WIDTH, keep), F32),
        compiler_params=_params(2),
        name=f"kv_tail_w{DILATION_GROUPS[gi][0]}",
    )(k, v)


def _kv_sample_kernel(k_ref, v_ref, o_ref, *, t_new, dbatch):
    for t in range(t_new):
        rows = pl.ds(t, dbatch, stride=t_new)
        o_ref[t, 0] = k_ref[rows, :].T
        o_ref[t, 1] = v_ref[rows, :].T


def _kv_sample(k, v, dbatch, t_new):
    src = pl.BlockSpec((dbatch * t_new, LANES), lambda s: (0, s))
    return pl.pallas_call(
        functools.partial(_kv_sample_kernel, t_new=t_new, dbatch=dbatch),
        grid=(B_WIDTH // LANES,),
        in_specs=[src, src],
        out_specs=pl.BlockSpec((None, t_new, 2, LANES, dbatch), lambda s: (s // PAIRS, 0, 0, s % PAIRS, 0)),
        out_shape=jax.ShapeDtypeStruct((N_GROUPS_B, t_new, 2, GROUP_WIDTH, dbatch), F32),
        compiler_params=_params(),
        name="kv_sample",
    )(k, v)


def _topk_route(logits):
    tm = logits.shape[0]
    lane = lax.broadcasted_iota(jnp.int32, logits.shape, 1).astype(F32)
    slot = lax.broadcasted_iota(jnp.int32, (tm, TOP_K), 1)
    val_out = jnp.zeros((tm, TOP_K), F32)
    work = logits
    ids, top = [], None
    for r in range(TOP_K):
        mx = jnp.max(work, axis=-1, keepdims=True)
        ix = jnp.min(jnp.where(work == mx, lane, float(N_EXPERTS)), axis=-1, keepdims=True)
        top = mx if top is None else top
        ids.append(ix)
        val_out = jnp.where(slot == r, jnp.exp(mx - top), val_out)
        work = jnp.where(lane == ix, -jnp.inf, work)
    gates = val_out / jnp.sum(val_out, axis=-1, keepdims=True)
    return ids, gates


def _finish_kernel(*refs, n_attn, n_alias, n_valid):
    n_in = 5 + n_attn + (n_attn if n_attn > 1 else 0) + 11
    ins, outs = refs[:n_in], refs[n_in + n_alias:]
    step = pl.program_id(0)

    @pl.when(step < n_valid)
    def _():
        _finish_tile(ins, outs, n_attn)

    @pl.when(step >= n_valid)
    def _():
        for ref in outs[:5]:
            ref[...] = jnp.zeros_like(ref)


def _finish_tile(ins, outs, n_attn):
    (h_ref, u_ref, va_ref, wm_ref, bias_ref), refs = ins[:5], ins[5:]
    o_refs, refs = refs[:n_attn], refs[n_attn:]
    n_lse = n_attn if n_attn > 1 else 0
    l_refs, refs = refs[:n_lse], refs[n_lse:]
    ga_ref, gb_ref, wa_ref, wb_ref, wo_ref, gmoe_ref, wrh_ref, wrl_ref, br_ref, tri_ref, cnt_in_ref = refs
    h1_ref, n2_ref, idx_ref, gate_ref, pos_ref, cnt_out_ref, carry_ref = outs
    tm = h_ref.shape[0]

    chunks = []
    for c in range(tm // CHUNK):
        rows = slice(c * CHUNK, (c + 1) * CHUNK)
        va_c = va_ref[rows, :].astype(BF16)
        mixed = jnp.concatenate(
            [_dot(wm_ref[g], va_c[:, g * A_GROUP_WIDTH:(g + 1) * A_GROUP_WIDTH]) for g in range(A_GROUPS)], axis=1)
        chunks.append((u_ref[rows, :].astype(F32) * (mixed + bias_ref[...])).astype(BF16))
    branch_a = _dot(jnp.concatenate(chunks, axis=0), wa_ref[...])

    if n_attn > 1:
        lses = [r[...] for r in l_refs]
        mx = functools.reduce(jnp.maximum, lses)
        ws = [jnp.exp(l - mx) for l in lses]
        wsum = functools.reduce(jnp.add, ws)
        o_b = (sum(w * o_ref[...].astype(F32) for w, o_ref in zip(ws, o_refs)) / wsum).astype(BF16)
    else:
        o_b = o_refs[0][...]
    branch_b = _dot(o_b, wb_ref[...])

    mix = ga_ref[...].astype(F32) * branch_a + gb_ref[...].astype(F32) * branch_b
    h1 = h_ref[...] + _dot(mix.astype(BF16), wo_ref[...])
    h1_ref[...] = h1

    n2 = (h1 * _rms(h1)) * gmoe_ref[...]
    n2_ref[...] = _pack_bf16_pairs(n2)
    n_hi = n2.astype(BF16)
    n_lo = (n2 - n_hi.astype(F32)).astype(BF16)
    logits = _dot(n_hi, wrh_ref[...]) + _dot(n_lo, wrh_ref[...]) + _dot(n_hi, wrl_ref[...]) + br_ref[...]
    ids, gates = _topk_route(logits)
    gate_ref[...] = gates

    @pl.when(pl.program_id(0) == 0)
    def _():
        carry_ref[...] = cnt_in_ref[...]

    lane = lax.broadcasted_iota(jnp.int32, logits.shape, 1).astype(F32)
    slot = lax.broadcasted_iota(jnp.int32, (tm, TOP_K), 1)
    sel = sum(jnp.where(lane == ix, 1.0, 0.0) for ix in ids)
    before = _dot(tri_ref[...], sel.astype(BF16)) + carry_ref[...]
    idx_out = jnp.zeros((tm, TOP_K), F32)
    pos_out = jnp.zeros((tm, TOP_K), F32)
    for r, ix in enumerate(ids):
        idx_out = jnp.where(slot == r, ix, idx_out)
        rank = jnp.sum(jnp.where(lane == ix, before, 0.0), axis=-1, keepdims=True)
        pos_out = jnp.where(slot == r, rank, pos_out)
    idx_ref[...] = idx_out.astype(jnp.int32)
    pos_ref[...] = pos_out.astype(jnp.int32)
    carry_ref[...] = carry_ref[...] + jnp.sum(sel, axis=0, keepdims=True)
    cnt_out_ref[...] = carry_ref[...]


def _finish(h, u, va, wm, bias_full, attn, ga, gb, consts, cnt_in, n_total, tile_offset, prev_outs):
    n_tok = h.shape[0]
    tm = TOKEN_TILE
    n_valid = n_tok // tm
    n_steps = n_valid if prev_outs is not None else n_total // tm - tile_offset
    row = lambda w: pl.BlockSpec((tm, w), lambda i: (jnp.minimum(i, n_valid - 1), 0))
    orow = lambda w: pl.BlockSpec((tm, w), lambda i: (i + tile_offset, 0))
    n_attn = len(attn)
    args = [h, u, va, wm, bias_full] + [o for o, _ in attn]
    specs = [row(D_MODEL), row(A_WIDTH), row(A_WIDTH), _resident(wm.shape), _resident(bias_full.shape)]
    specs += [row(GROUP_WIDTH)] * n_attn
    if n_attn > 1:
        args += [l for _, l in attn]
        specs += [row(GROUP_WIDTH)] * n_attn
    args += [ga, gb] + list(consts) + [cnt_in]
    specs += [row(D_MODEL), row(D_MODEL)] + [_resident(t.shape) for t in consts] + [_resident(cnt_in.shape)]
    aliases = {}
    if prev_outs is not None:
        for k, t in enumerate(prev_outs):
            aliases[len(args)] = k
            args.append(t)
            specs.append(pl.BlockSpec(memory_space=pl.ANY))
    widths = [(D_MODEL, F32), (D_MODEL // 2, jnp.int32), (TOP_K, jnp.int32), (TOP_K, F32), (TOP_K, jnp.int32)]
    outs = pl.pallas_call(
        functools.partial(_finish_kernel, n_attn=n_attn, n_alias=len(aliases), n_valid=n_valid),
        grid=(n_steps,),
        in_specs=specs,
        out_specs=[orow(w) for w, _ in widths] + [_resident(cnt_in.shape)],
        out_shape=[jax.ShapeDtypeStruct((n_total, w), dt) for w, dt in widths]
                  + [jax.ShapeDtypeStruct(cnt_in.shape, F32)],
        scratch_shapes=[pltpu.VMEM(cnt_in.shape, F32)],
        input_output_aliases=aliases,
        compiler_params=_params(),
        name=f"finish_{n_attn}",
    )(*args)
    return outs[:5], outs[5]


def _sc_gather_rows(table, idx):
    m = idx.shape[0]
    width = table.shape[1]
    per_worker = m // SC_WORKERS
    n_chunks = per_worker // SC_ROWS
    mesh = plsc.VectorSubcoreMesh(core_axis_name="c", subcore_axis_name="s",
                                  num_cores=SC_CORES, num_subcores=SC_SUBCORES)

    assert n_chunks % 2 == 0

    @functools.partial(
        pl.kernel, mesh=mesh,
        out_type=jax.ShapeDtypeStruct((m, width), table.dtype),
        scratch_types=[pltpu.VMEM((n_chunks, SC_ROWS), jnp.int32),
                       pltpu.VMEM((SC_ROWS, width), table.dtype),
                       pltpu.VMEM((SC_ROWS, width), table.dtype),
                       pltpu.SemaphoreType.DMA,
                       pltpu.SemaphoreType.DMA],
        name="sc_gather_rows",
    )
    def gather(table_hbm, idx_hbm, out_hbm, idx_v, rows_a, rows_b, sem_a, sem_b):
        wid = lax.axis_index("s") * SC_CORES + lax.axis_index("c")
        base = wid * per_worker
        pltpu.sync_copy(idx_hbm.at[wid], idx_v)

        def fetch(c, rows, sem):
            return pltpu.make_async_copy(table_hbm.at[idx_v.at[c]], rows, sem)

        def put(c, rows):
            off = pl.multiple_of(base + c * SC_ROWS, SC_ROWS)
            pltpu.sync_copy(rows, out_hbm.at[pl.ds(off, SC_ROWS)])

        fetch(0, rows_a, sem_a).start()

        @pl.loop(0, n_chunks, step=2)
        def _(c):
            fetch(c, rows_a, sem_a).wait()
            fetch(c + 1, rows_b, sem_b).start()
            put(c, rows_a)
            fetch(c + 1, rows_b, sem_b).wait()

            @pl.when(c + 2 < n_chunks)
            def _():
                fetch(c + 2, rows_a, sem_a).start()

            put(c + 1, rows_b)

    return gather(table, idx.reshape(SC_WORKERS, n_chunks, SC_ROWS))


def _sc_scatter_rows(src, dest4, n_out):
    n_src, width = src.shape
    top_k = dest4.shape[1]
    per_worker = n_src // SC_WORKERS
    rows = SC_SCATTER_ROWS
    n_chunks = per_worker // rows
    assert per_worker % rows == 0 and n_chunks % 2 == 0
    dest = dest4.reshape(SC_WORKERS, n_chunks, rows, top_k).transpose(0, 1, 3, 2)
    mesh = plsc.VectorSubcoreMesh(core_axis_name="c", subcore_axis_name="s",
                                  num_cores=SC_CORES, num_subcores=SC_SUBCORES)

    @functools.partial(
        pl.kernel, mesh=mesh,
        out_type=jax.ShapeDtypeStruct((n_out, width), src.dtype),
        scratch_types=[pltpu.VMEM((n_chunks, top_k, rows), jnp.int32),
                       pltpu.VMEM((rows, width), src.dtype),
                       pltpu.VMEM((rows, width), src.dtype),
                       pltpu.SemaphoreType.DMA,
                       pltpu.SemaphoreType.DMA,
                       pltpu.SemaphoreType.DMA],
        name="sc_scatter_rows",
    )
    def scatter(src_hbm, dest_hbm, out_hbm, idx_v, rows_a, rows_b, sem_a, sem_b, sem_w):
        wid = lax.axis_index("s") * SC_CORES + lax.axis_index("c")
        base = wid * per_worker
        pltpu.sync_copy(dest_hbm.at[wid], idx_v)

        def load(c, buf, sem):
            off = pl.multiple_of(base + c * rows, rows)
            return pltpu.make_async_copy(src_hbm.at[pl.ds(off, rows)], buf, sem)

        def spread(c, buf):
            copies = [pltpu.make_async_copy(buf, out_hbm.at[idx_v.at[c, k]], sem_w) for k in range(top_k)]
            for cp in copies:
                cp.start()
            for cp in copies:
                cp.wait()

        load(0, rows_a, sem_a).start()

        @pl.loop(0, n_chunks, step=2)
        def _(c):
            load(c, rows_a, sem_a).wait()
            load(c + 1, rows_b, sem_b).start()
            spread(c, rows_a)
            load(c + 1, rows_b, sem_b).wait()

            @pl.when(c + 2 < n_chunks)
            def _():
                load(c + 2, rows_a, sem_a).start()

            spread(c + 1, rows_b)

    return scatter(src, dest)


def _expert_kernel(be_ref, slot_ref, next_ref, nused_ref, x_ref, wgu_hbm, wd_hbm, bgu_ref, bd_ref, perm_ref, y_ref,
                   wgu_f, wd_f, wgu_s, wd_s, sems):
    i = pl.program_id(0)
    active = i < nused_ref[0]
    expert = be_ref[i]
    fresh = (i == 0) | (expert != be_ref[jnp.maximum(i - 1, 0)])

    def fetch(e, slot):
        return (pltpu.make_async_copy(wgu_hbm.at[e], wgu_f.at[slot], sems.at[slot, 0]),
                pltpu.make_async_copy(wd_hbm.at[e], wd_f.at[slot], sems.at[slot, 1]))

    @pl.when(active & fresh)
    def _():
        slot = slot_ref[i]

        @pl.when(i == 0)
        def _():
            for cp in fetch(expert, slot):
                cp.start()

        for cp in fetch(expert, slot):
            cp.wait()
        nxt = next_ref[i]

        @pl.when(nxt >= 0)
        def _():
            for cp in fetch(nxt, 1 - slot):
                cp.start()

        wgu_s[...] = wgu_f[slot].astype(BF16)
        half, part = D_FF // 2, MXU_DIM // 2
        for t in range(D_FF // MXU_DIM):
            src = jnp.concatenate([wd_f[slot, t * part:(t + 1) * part, :],
                                   wd_f[slot, half + t * part:half + (t + 1) * part, :]], axis=0).astype(BF16)
            wd_s[t * MXU_DIM:(t + 1) * MXU_DIM, :] = _dot(perm_ref[...], src).astype(BF16)

    @pl.when(active)
    def _():
        x = _unpack_bf16_pairs(x_ref[...]).astype(BF16)
        hh = _dot(x, wgu_s[...]) + bgu_ref[...]
        partner = pltpu.roll(hh, 2 * D_FF - 1, axis=1)
        h_glu = jnp.minimum(hh, SWIGLU_LIMIT)
        h_lin = jnp.clip(partner, -SWIGLU_LIMIT, SWIGLU_LIMIT)
        act = h_glu * _sigmoid(SWIGLU_ALPHA * h_glu) * (h_lin + 1.0)
        lane = lax.broadcasted_iota(jnp.int32, (act.shape[0], D_FF), 1)
        merged = jnp.where((lane & 1) == 0, act[:, :D_FF], pltpu.roll(act[:, D_FF:], 1, axis=1))
        y_ref[...] = _pack_bf16_pairs(_dot(merged.astype(BF16), wd_s[...]) + bd_ref[...])

    @pl.when(jnp.logical_not(active))
    def _():
        y_ref[...] = jnp.zeros_like(y_ref)


def _experts(xb, block_expert, block_slot, block_next, n_used, w_gate_up, w_down, b_gate_up, b_down, perm):
    n_slots = xb.shape[0]
    n_blocks = n_slots // EXPERT_BLOCK
    by_expert = lambda k, n: pl.BlockSpec((None, k, n), lambda i, be, sl, nx, nu: (be[i], 0, 0))
    blk = pl.BlockSpec((EXPERT_BLOCK, D_MODEL // 2), lambda i, be, sl, nx, nu: (i, 0))
    hbm = pl.BlockSpec(memory_space=pl.ANY)
    return pl.pallas_call(
        _expert_kernel,
        grid_spec=pltpu.PrefetchScalarGridSpec(
            num_scalar_prefetch=4, grid=(n_blocks,),
            in_specs=[blk, hbm, hbm, by_expert(1, 2 * D_FF), by_expert(1, D_MODEL),
                      pl.BlockSpec(perm.shape, lambda i, be, sl, nx, nu: (0, 0), pipeline_mode=pl.Buffered(1))],
            out_specs=blk,
            scratch_shapes=[pltpu.VMEM((2, D_MODEL, 2 * D_FF), F32), pltpu.VMEM((2, D_FF, D_MODEL), F32),
                            pltpu.VMEM((D_MODEL, 2 * D_FF), BF16), pltpu.VMEM((D_FF, D_MODEL), BF16),
                            pltpu.SemaphoreType.DMA((2, 2))]),
        out_shape=jax.ShapeDtypeStruct((n_slots, D_MODEL // 2), jnp.int32),
        compiler_params=_params(),
        name="experts",
    )(block_expert, block_slot, block_next, n_used, xb, w_gate_up, w_down, b_gate_up, b_down, perm)


def _final_kernel(h1_ref, yg_ref, gate_ref, p_ref, gple_ref, wg_ref, wp_ref, out_ref):
    h2 = h1_ref[...]
    gates = gate_ref[...]
    for k in range(TOP_K):
        h2 = h2 + gates[:, k:k + 1] * _unpack_bf16_pairs(yg_ref[k])
    n3 = ((h2 * _rms(h2)) * gple_ref[...]).astype(BF16)
    gate = _sigmoid(_dot(n3, wg_ref[...]))
    out_ref[...] = h2 + gate * _dot(p_ref[...].astype(BF16), wp_ref[...])


def _final(h1, yg, gates, p, tile_offset, g_ple, w_ple_gate, w_ple_proj):
    n_tok = p.shape[0]
    tm = TOKEN_TILE
    return pl.pallas_call(
        _final_kernel,
        grid=(n_tok // tm,),
        in_specs=[pl.BlockSpec((tm, D_MODEL), lambda i: (i + tile_offset, 0)),
                  pl.BlockSpec((TOP_K, tm, D_MODEL // 2), lambda i: (0, i + tile_offset, 0)),
                  pl.BlockSpec((tm, TOP_K), lambda i: (i + tile_offset, 0)),
                  pl.BlockSpec((tm, PLE_DIM), lambda i: (i, 0)),
                  _resident(g_ple.shape), _resident(w_ple_gate.shape), _resident(w_ple_proj.shape)],
        out_specs=pl.BlockSpec((tm, D_MODEL), lambda i: (i, 0)),
        out_shape=jax.ShapeDtypeStruct((n_tok, D_MODEL), F32),
        compiler_params=_params(),
        name="final",
    )(h1, yg, gates, p, g_ple, w_ple_gate, w_ple_proj)


def _routing_tables(idx4, pos4, counts, n_slots):
    counts = counts.reshape(N_EXPERTS).astype(jnp.int32)
    pcounts = (counts + EXPERT_BLOCK - 1) // EXPERT_BLOCK * EXPERT_BLOCK
    pends = jnp.cumsum(pcounts)
    pstarts = pends - pcounts
    experts = jnp.arange(N_EXPERTS, dtype=jnp.int32)
    start4 = jnp.sum(jnp.where(idx4[:, :, None] == experts, pstarts, 0), axis=-1)
    dest4 = (start4 + pos4).astype(jnp.int32)
    n_blocks = n_slots // EXPERT_BLOCK
    block_start = jnp.arange(n_blocks, dtype=jnp.int32) * EXPERT_BLOCK
    block_expert = jnp.minimum(jnp.sum(block_start[:, None] >= pends[None, :], axis=1), N_EXPERTS - 1).astype(jnp.int32)
    n_used = (pends[-1] // EXPERT_BLOCK).astype(jnp.int32).reshape(1)
    used = counts > 0
    slot_e = (jnp.cumsum(used.astype(jnp.int32)) - 1) & 1
    later_used = used[None, :] & (experts[None, :] > experts[:, None])
    next_e = jnp.min(jnp.where(later_used, experts[None, :], N_EXPERTS), axis=1)
    next_e = jnp.where(next_e == N_EXPERTS, -1, next_e)
    of_block = block_expert[:, None] == experts[None, :]
    block_slot = jnp.sum(jnp.where(of_block, slot_e[None, :], 0), axis=1).astype(jnp.int32)
    block_next = jnp.sum(jnp.where(of_block, next_e[None, :], 0), axis=1).astype(jnp.int32)
    return dest4, block_expert, block_slot, block_next, n_used


def kernel(x_prompt, x_sample, cache_kv_w128, cache_kv_w512, cache_kv_w2048, p_prompt, p_sample, g_mix, w_in, g_v, g_q, g_k, w_spatial, b_spatial, w_branch_a, w_branch_b, w_out, g_moe, w_router, b_router, w_gate_up, b_gate_up, w_down, b_down, g_ple, w_ple_gate, w_ple_proj):
    bsz, seq, _ = x_prompt.shape
    dbatch, t_new, _ = x_sample.shape
    assert g_mix.shape[0] == 1
    caches = (cache_kv_w128, cache_kv_w512, cache_kv_w2048)
    l = 0
    n_p, n_s = bsz * seq, dbatch * t_new
    n_tok = n_p + n_s
    assert n_p % TOKEN_TILE == 0 and n_s % TOKEN_TILE == 0

    row2 = lambda t: t.reshape(1, -1)
    w_in_bf = w_in[l].astype(BF16)
    g_q_t = jnp.tile(g_q[l], B_HEADS).reshape(1, B_WIDTH)
    g_k_t = jnp.tile(g_k[l], B_HEADS).reshape(1, B_WIDTH)
    hid = np.arange(MXU_DIM) // HEAD_DIM
    hsum = jnp.asarray(hid[:, None] == hid[None, :], BF16)
    tril = jnp.tril(jnp.ones((CHUNK, CHUNK), bool))
    wm_prompt = jnp.where(tril[None], w_spatial[l], 0).astype(BF16)
    bias_prompt = jnp.repeat(b_spatial[l].T, A_GROUP_WIDTH, axis=1)
    reps = CHUNK // t_new
    small = jnp.where(tril[None, :t_new, :t_new], w_spatial[l][:, :t_new, :t_new], 0)
    wm_sample = jnp.einsum("ab,gij->gaibj", jnp.eye(reps, dtype=F32), small).reshape(A_GROUPS, CHUNK, CHUNK).astype(BF16)
    bias_sample = jnp.tile(bias_prompt[:t_new], (reps, 1))
    wr_hi = w_router[l].astype(BF16)
    wr_lo = (w_router[l] - wr_hi.astype(F32)).astype(BF16)
    tri = jnp.asarray(np.tril(np.ones((TOKEN_TILE, TOKEN_TILE), np.float32), -1), BF16)
    consts = (w_branch_a[l].astype(BF16), w_branch_b[l].astype(BF16), w_out[l].astype(BF16), row2(g_moe[l]),
              wr_hi, wr_lo, row2(b_router[l]), tri)
    part = MXU_DIM // 2
    perm_np = np.zeros((MXU_DIM, MXU_DIM), np.float32)
    perm_np[2 * np.arange(part), np.arange(part)] = 1.0
    perm_np[2 * np.arange(part) + 1, part + np.arange(part)] = 1.0
    perm = jnp.asarray(perm_np, BF16)

    proj = functools.partial(_project, g_mix=row2(g_mix[l]), w_in_bf=w_in_bf, g_v=row2(g_v[l]),
                             g_q_t=g_q_t, g_k_t=g_k_t, hsum=hsum)

    xp = x_prompt.reshape(n_p, D_MODEL)
    u_p, va_p, q_p, k_p, v_p, ga_p, gb_p = proj(xp, va_dtype=BF16)
    seq3 = lambda t: t.reshape(bsz, seq, B_WIDTH)
    attn_p = [_attend_prompt(seq3(q_p), seq3(k_p), seq3(v_p), gi, bsz, seq) for gi in range(N_GROUPS_B)]
    zero_counts = jnp.zeros((1, N_EXPERTS), F32)
    outs_p, cnt_p = _finish(xp, u_p, va_p, wm_prompt, bias_prompt, attn_p, ga_p, gb_p, consts, zero_counts,
                            n_tok, 0, None)

    xs = x_sample.reshape(n_s, D_MODEL)
    u_s, va_s, q_s, k_s, v_s, ga_s, gb_s = proj(xs, va_dtype=F32)
    tok3 = lambda t: t.reshape(dbatch, t_new, B_WIDTH)
    caches_t = [jnp.transpose(c[l], (0, 2, 3, 4, 1)) for c in caches]
    o_s = _attend_sample(tok3(q_s), tok3(k_s), tok3(v_s), caches_t, dbatch, t_new)
    (h1, n2, idx4, gates, pos4), counts = _finish(xs, u_s, va_s, wm_sample, bias_sample, [(o_s, None)], ga_s, gb_s,
                                                  consts, cnt_p, n_tok, n_p // TOKEN_TILE, outs_p)

    n_blocks = -(-n_tok * TOP_K // EXPERT_BLOCK) + N_EXPERTS
    n_slots = n_blocks * EXPERT_BLOCK
    dest4, block_expert, block_slot, block_next, n_used = _routing_tables(idx4, pos4, counts, n_slots)
    xb = _sc_scatter_rows(n2, dest4, n_slots)
    yb = _experts(xb, block_expert, block_slot, block_next, n_used, w_gate_up[l], w_down[l],
                  b_gate_up[l][:, None, :], b_down[l][:, None, :], perm)
    yg = _sc_gather_rows(yb, dest4.T.reshape(-1)).reshape(TOP_K, n_tok, D_MODEL // 2)
    fin = functools.partial(_final, g_ple=row2(g_ple[l]), w_ple_gate=w_ple_gate[l].astype(BF16),
                            w_ple_proj=w_ple_proj[l].astype(BF16))
    y_prompt = fin(h1, yg, gates, p_prompt[l].reshape(n_p, PLE_DIM), 0).reshape(bsz, seq, D_MODEL)
    y_sample = fin(h1, yg, gates, p_sample[l].reshape(n_s, PLE_DIM), n_p // TOKEN_TILE).reshape(dbatch, t_new, D_MODEL)

    kv_prompt = []
    for gi in range(N_GROUPS_B):
        t = _kv_tail(seq3(k_p), seq3(v_p), gi, bsz, seq)
        keep = t.shape[-1]
        kv_prompt.append(jnp.transpose(t.reshape(bsz, 2, H_SLOT, HEAD_DIM, keep), (0, 4, 1, 2, 3))[None])
    kvs = _kv_sample(k_s, v_s, dbatch, t_new)
    kv_sample = [jnp.transpose(kvs[gi].reshape(t_new, 2, H_SLOT, HEAD_DIM, dbatch), (4, 0, 1, 2, 3))[None]
                 for gi in range(N_GROUPS_B)]
    va_out = va_s.reshape(1, dbatch, t_new, A_WIDTH)
    return (y_prompt, y_sample, *kv_prompt, *kv_sample, va_out)
```

```python
import functools

import numpy as np
import jax
import jax.numpy as jnp
from jax import lax
from jax.experimental import pallas as pl
from jax.experimental.pallas import tpu as pltpu
from jax.experimental.pallas import tpu_sc as plsc

F32 = jnp.float32
BF16 = jnp.bfloat16

D_MODEL = 1024
A_WIDTH = 1024
A_GROUPS = 4
A_GROUP_WIDTH = A_WIDTH // A_GROUPS
CHUNK = 128
HEAD_DIM = 64
H_SLOT = 8
GROUP_WIDTH = H_SLOT * HEAD_DIM
DILATION_GROUPS = ((128, 1), (512, 4), (2048, 16))
N_GROUPS_B = len(DILATION_GROUPS)
B_HEADS = H_SLOT * N_GROUPS_B
B_WIDTH = B_HEADS * HEAD_DIM
N_EXPERTS = 32
TOP_K = 4
D_FF = 1024
SWIGLU_ALPHA = 1.702
SWIGLU_LIMIT = 7.0
PLE_DIM = 256
RMS_EPS = 1e-6
NEG_BIG = -1e30
LOG2_E = float(np.log2(np.e))
LN_2 = float(np.log(2.0))

LANES = 128
SUBLANES = 8
MXU_DIM = 256
HEADS_PER_VREG = LANES // HEAD_DIM
PAIRS = GROUP_WIDTH // LANES
RESIDUE_UNROLL = 4
TOKEN_TILE = 512
EXPERT_BLOCK = 256
VMEM_LIMIT = 56 * 1024 * 1024

SC_CORES = 2
SC_SUBCORES = 16
SC_WORKERS = SC_CORES * SC_SUBCORES
SC_ROWS = 32
SC_SCATTER_ROWS = 16

_COL_SPLITS = np.cumsum([0, A_WIDTH, A_WIDTH, B_WIDTH, B_WIDTH, B_WIDTH, D_MODEL, D_MODEL]).tolist()


def _alibi_slopes():
    return np.exp2(-8.0 * np.arange(1, B_HEADS + 1, dtype=np.float32) / B_HEADS).astype(np.float32)


def _sigmoid(x):
    return 1.0 / (1.0 + jnp.exp(-x))


def _rms(x):
    return lax.rsqrt(jnp.mean(x * x, axis=-1, keepdims=True) + RMS_EPS)


def _dot(a, b):
    return jnp.dot(a, b, preferred_element_type=F32)


def _dot_nt(a, b):
    return lax.dot_general(a, b, (((1,), (1,)), ((), ())), preferred_element_type=F32)


def _pack_bf16_pairs(x):
    w = x.shape[1] // 2
    lo = lax.bitcast_convert_type(x[:, :w].astype(BF16).astype(F32), jnp.uint32) >> 16
    hi = lax.bitcast_convert_type(x[:, w:].astype(BF16).astype(F32), jnp.uint32) & jnp.uint32(0xFFFF0000)
    return lax.bitcast_convert_type(lo | hi, jnp.int32)


def _unpack_bf16_pairs(p):
    u = lax.bitcast_convert_type(p, jnp.uint32)
    lo = lax.bitcast_convert_type(u << 16, F32)
    hi = lax.bitcast_convert_type(u & jnp.uint32(0xFFFF0000), F32)
    return jnp.concatenate([lo, hi], axis=1)


def _resident(shape):
    nd = len(shape)
    return pl.BlockSpec(shape, lambda *_: (0,) * nd, pipeline_mode=pl.Buffered(1))


def _params(n_axes=1):
    return pltpu.CompilerParams(dimension_semantics=("arbitrary",) * n_axes, vmem_limit_bytes=VMEM_LIMIT)


def _proj_kernel(x_ref, gmix_ref, w_ref, gv_ref, gq_ref, gk_ref, hsum_ref,
                 u_ref, va_ref, q_ref, k_ref, v_ref, ga_ref, gb_ref):
    x = x_ref[...]
    n = ((x * _rms(x)) * gmix_ref[...]).astype(BF16)

    def section(i):
        return _dot(n, w_ref[:, _COL_SPLITS[i]:_COL_SPLITS[i + 1]])

    u_ref[...] = jax.nn.gelu(section(0)).astype(u_ref.dtype)
    va = jax.nn.gelu(section(1))
    va_ref[...] = ((va * _rms(va)) * gv_ref[...]).astype(va_ref.dtype)

    def head_norm(z, g_ref, scale):
        parts = []
        for c in range(B_WIDTH // MXU_DIM):
            zc = z[:, c * MXU_DIM:(c + 1) * MXU_DIM]
            ss = _dot((zc * zc).astype(BF16), hsum_ref[...])
            parts.append(zc * lax.rsqrt(ss * (1.0 / HEAD_DIM) + RMS_EPS))
        return jnp.concatenate(parts, axis=1) * (g_ref[...] * scale)

    q_ref[...] = head_norm(section(2), gq_ref, HEAD_DIM ** -0.5 * LOG2_E)
    k_ref[...] = head_norm(section(3), gk_ref, 1.0)
    v_ref[...] = section(4)
    ga_ref[...] = _sigmoid(section(5)).astype(ga_ref.dtype)
    gb_ref[...] = _sigmoid(section(6)).astype(gb_ref.dtype)


def _project(x, g_mix, w_in_bf, g_v, g_q_t, g_k_t, hsum, va_dtype):
    n_tok = x.shape[0]
    tm = TOKEN_TILE
    row = lambda w: pl.BlockSpec((tm, w), lambda i: (i, 0))
    outs = [(A_WIDTH, BF16), (A_WIDTH, va_dtype), (B_WIDTH, F32), (B_WIDTH, F32), (B_WIDTH, F32),
            (D_MODEL, BF16), (D_MODEL, BF16)]
    return pl.pallas_call(
        _proj_kernel,
        grid=(n_tok // tm,),
        in_specs=[row(D_MODEL), _resident(g_mix.shape), _resident(w_in_bf.shape), _resident(g_v.shape),
                  _resident(g_q_t.shape), _resident(g_k_t.shape), _resident(hsum.shape)],
        out_specs=[row(w) for w, _ in outs],
        out_shape=[jax.ShapeDtypeStruct((n_tok, w), dt) for w, dt in outs],
        compiler_params=_params(),
        name="project",
    )(x, g_mix, w_in_bf, g_v, g_q_t, g_k_t, hsum)


def _band_bias(band, dil, slopes):
    qi = jnp.arange(band, dtype=jnp.int32)[:, None]
    kj = jnp.arange(2 * band, dtype=jnp.int32)[None, :]
    dist = qi + band - kj
    in_band = (dist >= 0) & (dist <= band)
    valid = jnp.stack([in_band & (kj >= band), in_band])
    penalty = (jnp.asarray(slopes, F32) * LOG2_E)[:, None, None] * (dist * dil).astype(F32)[None]
    return jnp.where(valid[:, None], -penalty[None], NEG_BIG)


def _pair_attention(q2, k, v, bias_pair):
    band = q2.shape[0]
    first = lax.broadcasted_iota(jnp.int32, (band, LANES), 1) < HEAD_DIM
    qs = jnp.concatenate([jnp.where(first, q2, 0.0), jnp.where(first, 0.0, q2)], axis=0).astype(BF16)
    s = _dot_nt(qs, k) + bias_pair.reshape(HEADS_PER_VREG * band, 2 * band)
    m = jnp.max(s, axis=-1, keepdims=True)
    e = jnp.exp2(s - m)
    den = jnp.sum(e, axis=-1, keepdims=True)
    o2 = _dot((e * (1.0 / den)).astype(BF16), v)
    lse = m * LN_2 + jnp.log(den)
    return jnp.where(first, o2[:band], o2[band:]), jnp.where(first, lse[:band], lse[band:])


def _attn_dense_kernel(q_ref, kp_ref, kc_ref, vp_ref, vc_ref, bias_ref, o_ref, l_ref):
    for p in range(PAIRS):
        cols = slice(p * LANES, (p + 1) * LANES)
        k = jnp.concatenate([kp_ref[:, cols], kc_ref[:, cols]], axis=0).astype(BF16)
        v = jnp.concatenate([vp_ref[:, cols], vc_ref[:, cols]], axis=0).astype(BF16)
        bias2 = bias_ref[p * HEADS_PER_VREG:(p + 1) * HEADS_PER_VREG]
        o_pair, l_pair = _pair_attention(q_ref[:, cols], k, v, bias2)
        o_ref[:, cols] = o_pair.astype(o_ref.dtype)
        l_ref[:, cols] = l_pair


def _attn_dilated_kernel(q_ref, kp_ref, kc_ref, vp_ref, vc_ref, bias_ref, o_ref, l_ref, *, band, dil):
    def residues(it, carry):
        for u in range(RESIDUE_UNROLL):
            rows = pl.ds(it * RESIDUE_UNROLL + u, band, stride=dil)
            k = jnp.concatenate([kp_ref[rows, :], kc_ref[rows, :]], axis=0).astype(BF16)
            v = jnp.concatenate([vp_ref[rows, :], vc_ref[rows, :]], axis=0).astype(BF16)
            o_pair, l_pair = _pair_attention(q_ref[rows, :], k, v, bias_ref[...])
            o_ref[rows, :] = o_pair
            l_ref[rows, :] = l_pair
        return carry

    lax.fori_loop(0, dil // RESIDUE_UNROLL, residues, 0)


def _attend_prompt(q, k, v, gi, bsz, seq):
    win, dil = DILATION_GROUPS[gi]
    band = win // dil
    bias = _band_bias(band, dil, _alibi_slopes()[gi * H_SLOT:(gi + 1) * H_SLOT])
    if dil == 1:
        cur = pl.BlockSpec((None, band, GROUP_WIDTH), lambda b, j: (b, j, gi))
        prev = pl.BlockSpec((None, band, GROUP_WIDTH), lambda b, j: (b, jnp.maximum(j - 1, 0), gi))
        tab = pl.BlockSpec((None, H_SLOT, band, 2 * band), lambda b, j: (jnp.minimum(j, 1), 0, 0, 0))
        out = pl.BlockSpec((None, band, GROUP_WIDTH), lambda b, j: (b, j, 0))
        o, lse = pl.pallas_call(
            _attn_dense_kernel,
            grid=(bsz, seq // band),
            in_specs=[cur, prev, cur, prev, cur, tab],
            out_specs=[out, out],
            out_shape=[jax.ShapeDtypeStruct((bsz, seq, GROUP_WIDTH), BF16),
                       jax.ShapeDtypeStruct((bsz, seq, GROUP_WIDTH), F32)],
            compiler_params=_params(2),
            name=f"attn_prompt_w{win}",
        )(q, k, k, v, v, bias)
    else:
        col = lambda p: gi * PAIRS + p
        cur = pl.BlockSpec((None, win, LANES), lambda b, j, p: (b, j, col(p)))
        prev = pl.BlockSpec((None, win, LANES), lambda b, j, p: (b, jnp.maximum(j - 1, 0), col(p)))
        tab = pl.BlockSpec((None, HEADS_PER_VREG, band, 2 * band), lambda b, j, p: (jnp.minimum(j, 1), p, 0, 0))
        out = pl.BlockSpec((None, win, LANES), lambda b, j, p: (b, j, p))
        o, lse = pl.pallas_call(
            functools.partial(_attn_dilated_kernel, band=band, dil=dil),
            grid=(bsz, seq // win, PAIRS),
            in_specs=[cur, prev, cur, prev, cur, tab],
            out_specs=[out, out],
            out_shape=[jax.ShapeDtypeStruct((bsz, seq, GROUP_WIDTH), F32)] * 2,
            compiler_params=_params(3),
            name=f"attn_prompt_w{win}",
        )(q, k, k, v, v, bias)
    return o.reshape(bsz * seq, GROUP_WIDTH), lse.reshape(bsz * seq, GROUP_WIDTH)


def _attn_sample_kernel(q_ref, kn_ref, vn_ref, c0_ref, c1_ref, c2_ref, o_ref, *, t_new, slopes):
    rows = H_SLOT * t_new
    pad_new = 16
    t_shift = t_new.bit_length() - 1
    caches = (c0_ref, c1_ref, c2_ref)
    row_id = lax.broadcasted_iota(jnp.int32, (rows, GROUP_WIDTH), 0)
    col_id = lax.broadcasted_iota(jnp.int32, (rows, GROUP_WIDTH), 1)
    own_head = (row_id >> t_shift) == (col_id >> (HEAD_DIM.bit_length() - 1))
    head_of_row = lax.broadcasted_iota(jnp.int32, (rows, 1), 0) >> t_shift
    zpad = jnp.zeros((pad_new - t_new, GROUP_WIDTH), F32)
    outs, lses = [], []
    for g, (win, dil) in enumerate(DILATION_GROUPS):
        buf_len = caches[g].shape[-1]
        cols = slice(g * GROUP_WIDTH, (g + 1) * GROUP_WIDTH)

        def mask_of(n_keys, first_pos, n_real):
            i_q = lax.broadcasted_iota(jnp.int32, (rows, n_keys), 0) & (t_new - 1)
            key = lax.broadcasted_iota(jnp.int32, (rows, n_keys), 1)
            delta = buf_len + i_q - (first_pos + key)
            ok = (delta >= 0) & ((delta & (dil - 1)) == 0) & (delta <= win) & (key < n_real)
            return ok, delta.astype(F32)

        kt = caches[g][0].reshape(GROUP_WIDTH, buf_len).astype(BF16)
        vt = caches[g][1].reshape(GROUP_WIDTH, buf_len).astype(BF16)
        qblk = jnp.where(own_head, jnp.concatenate([q_ref[:, cols]] * H_SLOT, axis=0), 0.0).astype(BF16)
        kn = jnp.concatenate([kn_ref[:, cols], zpad], axis=0).astype(BF16)
        vn = jnp.concatenate([vn_ref[:, cols], zpad], axis=0).astype(BF16)
        slope_rows = jnp.zeros((rows, 1), F32)
        for h in range(H_SLOT):
            slope_rows = jnp.where(head_of_row == h, float(slopes[g * H_SLOT + h]) * LOG2_E, slope_rows)
        ok_c, delta_c = mask_of(buf_len, 0, buf_len)
        ok_n, delta_n = mask_of(pad_new, buf_len, t_new)
        sc = jnp.where(ok_c, _dot(qblk, kt) - slope_rows * delta_c, NEG_BIG)
        sn = jnp.where(ok_n, _dot_nt(qblk, kn) - slope_rows * delta_n, NEG_BIG)
        m = jnp.maximum(jnp.max(sc, axis=-1, keepdims=True), jnp.max(sn, axis=-1, keepdims=True))
        ec = jnp.exp2(sc - m)
        en = jnp.exp2(sn - m)
        den = jnp.sum(ec, axis=-1, keepdims=True) + jnp.sum(en, axis=-1, keepdims=True)
        inv = 1.0 / den
        outs.append(_dot_nt((ec * inv).astype(BF16), vt) + _dot((en * inv).astype(BF16), vn))
        lses.append(m * LN_2 + jnp.log(den))
    mx = functools.reduce(jnp.maximum, lses)
    ws = [jnp.exp(l - mx) for l in lses]
    wsum = functools.reduce(jnp.add, ws)
    comb = sum((w / wsum) * o for w, o in zip(ws, outs))
    comb = jnp.where(own_head, comb, 0.0)
    o_tok = sum(comb[h * t_new:(h + 1) * t_new, :] for h in range(H_SLOT))
    o_ref[...] = o_tok.astype(o_ref.dtype)


def _attend_sample(q, k_new, v_new, caches_t, dbatch, t_new):
    assert t_new & (t_new - 1) == 0 and t_new <= 8
    tok = pl.BlockSpec((None, t_new, B_WIDTH), lambda b: (b, 0, 0))
    cache_specs = [pl.BlockSpec((None,) + c.shape[1:], lambda b: (b, 0, 0, 0, 0)) for c in caches_t]
    o = pl.pallas_call(
        functools.partial(_attn_sample_kernel, t_new=t_new, slopes=_alibi_slopes()),
        grid=(dbatch,),
        in_specs=[tok, tok, tok] + cache_specs,
        out_specs=pl.BlockSpec((None, t_new, GROUP_WIDTH), lambda b: (b, 0, 0)),
        out_shape=jax.ShapeDtypeStruct((dbatch, t_new, GROUP_WIDTH), BF16),
        compiler_params=_params(),
        name="attn_sample",
    )(q, k_new, v_new, *caches_t)
    return o.reshape(dbatch * t_new, GROUP_WIDTH)


def _kv_tail_kernel(k_ref, v_ref, o_ref):
    o_ref[0] = k_ref[...].T
    o_ref[1] = v_ref[...].T


def _kv_tail(k, v, gi, bsz, seq):
    keep = min(DILATION_GROUPS[gi][0], seq)
    pb = min(keep, 512)
    first = (seq - keep) // pb
    src = pl.BlockSpec((None, pb, GROUP_WIDTH), lambda b, j: (b, first + j, gi))
    return pl.pallas_call(
        _kv_tail_kernel,
        grid=(bsz, keep // pb),
        in_specs=[src, src],
        out_specs=pl.BlockSpec((None, 2, GROUP_WIDTH, pb), lambda b, j: (b, 0, 0, j)),
        out_shape=jax.ShapeDtypeStruct((bsz, 2, GROUP_WIDTH, keep), F32),
        compiler_params=_params(2),
        name=f"kv_tail_w{DILATION_GROUPS[gi][0]}",
    )(k, v)


def _kv_sample_kernel(*refs, t_new, dbatch):
    in_refs, out_refs = refs[:2 * N_GROUPS_B], refs[2 * N_GROUPS_B:]
    for g, o_ref in enumerate(out_refs):
        k_ref, v_ref = in_refs[2 * g], in_refs[2 * g + 1]
        for t in range(t_new):
            rows = pl.ds(t, dbatch, stride=t_new)
            o_ref[t, 0] = k_ref[rows, :].T
            o_ref[t, 1] = v_ref[rows, :].T


def _kv_sample(k, v, dbatch, t_new):
    src = lambda g: pl.BlockSpec((dbatch * t_new, LANES), lambda s: (0, g * PAIRS + s))
    return pl.pallas_call(
        functools.partial(_kv_sample_kernel, t_new=t_new, dbatch=dbatch),
        grid=(PAIRS,),
        in_specs=[src(g) for g in range(N_GROUPS_B) for _ in range(2)],
        out_specs=[pl.BlockSpec((t_new, 2, LANES, dbatch), lambda s: (0, 0, s, 0))] * N_GROUPS_B,
        out_shape=[jax.ShapeDtypeStruct((t_new, 2, GROUP_WIDTH, dbatch), F32)] * N_GROUPS_B,
        compiler_params=_params(),
        name="kv_sample",
    )(*[t for _ in range(N_GROUPS_B) for t in (k, v)])


def _topk_route(logits):
    tm = logits.shape[0]
    lane = lax.broadcasted_iota(jnp.int32, logits.shape, 1).astype(F32)
    slot = lax.broadcasted_iota(jnp.int32, (tm, TOP_K), 1)
    val_out = jnp.zeros((tm, TOP_K), F32)
    work = logits
    ids, top = [], None
    for r in range(TOP_K):
        mx = jnp.max(work, axis=-1, keepdims=True)
        ix = jnp.min(jnp.where(work == mx, lane, float(N_EXPERTS)), axis=-1, keepdims=True)
        top = mx if top is None else top
        ids.append(ix)
        val_out = jnp.where(slot == r, jnp.exp(mx - top), val_out)
        work = jnp.where(lane == ix, -jnp.inf, work)
    gates = val_out / jnp.sum(val_out, axis=-1, keepdims=True)
    return ids, gates


def _finish_kernel(*refs, n_attn, n_alias, n_valid):
    n_in = 5 + n_attn + (n_attn if n_attn > 1 else 0) + 11
    ins, outs = refs[:n_in], refs[n_in + n_alias:]
    step = pl.program_id(0)

    @pl.when(step < n_valid)
    def _():
        _finish_tile(ins, outs, n_attn)

    @pl.when(step >= n_valid)
    def _():
        for ref in outs[:5]:
            ref[...] = jnp.zeros_like(ref)


def _finish_tile(ins, outs, n_attn):
    (h_ref, u_ref, va_ref, wm_ref, bias_ref), refs = ins[:5], ins[5:]
    o_refs, refs = refs[:n_attn], refs[n_attn:]
    n_lse = n_attn if n_attn > 1 else 0
    l_refs, refs = refs[:n_lse], refs[n_lse:]
    ga_ref, gb_ref, wa_ref, wb_ref, wo_ref, gmoe_ref, wrh_ref, wrl_ref, br_ref, tri_ref, cnt_in_ref = refs
    h1_ref, n2_ref, idx_ref, gate_ref, pos_ref, cnt_out_ref, carry_ref = outs
    tm = h_ref.shape[0]

    chunks = []
    for c in range(tm // CHUNK):
        rows = slice(c * CHUNK, (c + 1) * CHUNK)
        va_c = va_ref[rows, :].astype(BF16)
        mixed = jnp.concatenate(
            [_dot(wm_ref[g], va_c[:, g * A_GROUP_WIDTH:(g + 1) * A_GROUP_WIDTH]) for g in range(A_GROUPS)], axis=1)
        chunks.append((u_ref[rows, :].astype(F32) * (mixed + bias_ref[...])).astype(BF16))
    branch_a = _dot(jnp.concatenate(chunks, axis=0), wa_ref[...])

    if n_attn > 1:
        lses = [r[...] for r in l_refs]
        mx = functools.reduce(jnp.maximum, lses)
        ws = [jnp.exp(l - mx) for l in lses]
        wsum = functools.reduce(jnp.add, ws)
        o_b = (sum(w * o_ref[...].astype(F32) for w, o_ref in zip(ws, o_refs)) / wsum).astype(BF16)
    else:
        o_b = o_refs[0][...]
    branch_b = _dot(o_b, wb_ref[...])

    mix = ga_ref[...].astype(F32) * branch_a + gb_ref[...].astype(F32) * branch_b
    h1 = h_ref[...] + _dot(mix.astype(BF16), wo_ref[...])
    h1_ref[...] = h1

    n2 = (h1 * _rms(h1)) * gmoe_ref[...]
    n2_ref[...] = _pack_bf16_pairs(n2)
    n_hi = n2.astype(BF16)
    n_lo = (n2 - n_hi.astype(F32)).astype(BF16)
    logits = _dot(n_hi, wrh_ref[...]) + _dot(n_lo, wrh_ref[...]) + _dot(n_hi, wrl_ref[...]) + br_ref[...]
    ids, gates = _topk_route(logits)
    gate_ref[...] = gates

    @pl.when(pl.program_id(0) == 0)
    def _():
        carry_ref[...] = cnt_in_ref[...]

    lane = lax.broadcasted_iota(jnp.int32, logits.shape, 1).astype(F32)
    slot = lax.broadcasted_iota(jnp.int32, (tm, TOP_K), 1)
    sel = sum(jnp.where(lane == ix, 1.0, 0.0) for ix in ids)
    before = _dot(tri_ref[...], sel.astype(BF16)) + carry_ref[...]
    idx_out = jnp.zeros((tm, TOP_K), F32)
    pos_out = jnp.zeros((tm, TOP_K), F32)
    for r, ix in enumerate(ids):
        idx_out = jnp.where(slot == r, ix, idx_out)
        rank = jnp.sum(jnp.where(lane == ix, before, 0.0), axis=-1, keepdims=True)
        pos_out = jnp.where(slot == r, rank, pos_out)
    idx_ref[...] = idx_out.astype(jnp.int32)
    pos_ref[...] = pos_out.astype(jnp.int32)
    carry_ref[...] = carry_ref[...] + jnp.sum(sel, axis=0, keepdims=True)
    cnt_out_ref[...] = carry_ref[...]


def _finish(h, u, va, wm, bias_full, attn, ga, gb, consts, cnt_in, n_total, tile_offset, prev_outs):
    n_tok = h.shape[0]
    tm = TOKEN_TILE
    n_valid = n_tok // tm
    n_steps = n_valid if prev_outs is not None else n_total // tm - tile_offset
    row = lambda w: pl.BlockSpec((tm, w), lambda i: (jnp.minimum(i, n_valid - 1), 0))
    orow = lambda w: pl.BlockSpec((tm, w), lambda i: (i + tile_offset, 0))
    n_attn = len(attn)
    args = [h, u, va, wm, bias_full] + [o for o, _ in attn]
    specs = [row(D_MODEL), row(A_WIDTH), row(A_WIDTH), _resident(wm.shape), _resident(bias_full.shape)]
    specs += [row(GROUP_WIDTH)] * n_attn
    if n_attn > 1:
        args += [l for _, l in attn]
        specs += [row(GROUP_WIDTH)] * n_attn
    args += [ga, gb] + list(consts) + [cnt_in]
    specs += [row(D_MODEL), row(D_MODEL)] + [_resident(t.shape) for t in consts] + [_resident(cnt_in.shape)]
    aliases = {}
    if prev_outs is not None:
        for k, t in enumerate(prev_outs):
            aliases[len(args)] = k
            args.append(t)
            specs.append(pl.BlockSpec(memory_space=pl.ANY))
    widths = [(D_MODEL, F32), (D_MODEL // 2, jnp.int32), (TOP_K, jnp.int32), (TOP_K, F32), (TOP_K, jnp.int32)]
    outs = pl.pallas_call(
        functools.partial(_finish_kernel, n_attn=n_attn, n_alias=len(aliases), n_valid=n_valid),
        grid=(n_steps,),
        in_specs=specs,
        out_specs=[orow(w) for w, _ in widths] + [_resident(cnt_in.shape)],
        out_shape=[jax.ShapeDtypeStruct((n_total, w), dt) for w, dt in widths]
                  + [jax.ShapeDtypeStruct(cnt_in.shape, F32)],
        scratch_shapes=[pltpu.VMEM(cnt_in.shape, F32)],
        input_output_aliases=aliases,
        compiler_params=_params(),
        name=f"finish_{n_attn}",
    )(*args)
    return outs[:5], outs[5]


def _sc_gather_rows(table, idx):
    m = idx.shape[0]
    width = table.shape[1]
    per_worker = m // SC_WORKERS
    n_chunks = per_worker // SC_ROWS
    mesh = plsc.VectorSubcoreMesh(core_axis_name="c", subcore_axis_name="s",
                                  num_cores=SC_CORES, num_subcores=SC_SUBCORES)

    assert n_chunks % 2 == 0

    @functools.partial(
        pl.kernel, mesh=mesh,
        out_type=jax.ShapeDtypeStruct((m, width), table.dtype),
        scratch_types=[pltpu.VMEM((n_chunks, SC_ROWS), jnp.int32),
                       pltpu.VMEM((SC_ROWS, width), table.dtype),
                       pltpu.VMEM((SC_ROWS, width), table.dtype),
                       pltpu.SemaphoreType.DMA,
                       pltpu.SemaphoreType.DMA],
        name="sc_gather_rows",
    )
    def gather(table_hbm, idx_hbm, out_hbm, idx_v, rows_a, rows_b, sem_a, sem_b):
        wid = lax.axis_index("s") * SC_CORES + lax.axis_index("c")
        base = wid * per_worker
        pltpu.sync_copy(idx_hbm.at[wid], idx_v)

        def fetch(c, rows, sem):
            return pltpu.make_async_copy(table_hbm.at[idx_v.at[c]], rows, sem)

        def put(c, rows):
            off = pl.multiple_of(base + c * SC_ROWS, SC_ROWS)
            pltpu.sync_copy(rows, out_hbm.at[pl.ds(off, SC_ROWS)])

        fetch(0, rows_a, sem_a).start()

        @pl.loop(0, n_chunks, step=2)
        def _(c):
            fetch(c, rows_a, sem_a).wait()
            fetch(c + 1, rows_b, sem_b).start()
            put(c, rows_a)
            fetch(c + 1, rows_b, sem_b).wait()

            @pl.when(c + 2 < n_chunks)
            def _():
                fetch(c + 2, rows_a, sem_a).start()

            put(c + 1, rows_b)

    return gather(table, idx.reshape(SC_WORKERS, n_chunks, SC_ROWS))


def _sc_scatter_rows(src, dest4, n_out):
    n_src, width = src.shape
    top_k = dest4.shape[1]
    per_worker = n_src // SC_WORKERS
    rows = SC_SCATTER_ROWS
    n_chunks = per_worker // rows
    assert per_worker % rows == 0 and n_chunks % 2 == 0
    dest = dest4.reshape(SC_WORKERS, n_chunks, rows, top_k).transpose(0, 1, 3, 2)
    mesh = plsc.VectorSubcoreMesh(core_axis_name="c", subcore_axis_name="s",
                                  num_cores=SC_CORES, num_subcores=SC_SUBCORES)

    @functools.partial(
        pl.kernel, mesh=mesh,
        out_type=jax.ShapeDtypeStruct((n_out, width), src.dtype),
        scratch_types=[pltpu.VMEM((n_chunks, top_k, rows), jnp.int32),
                       pltpu.VMEM((rows, width), src.dtype),
                       pltpu.VMEM((rows, width), src.dtype),
                       pltpu.SemaphoreType.DMA,
                       pltpu.SemaphoreType.DMA,
                       pltpu.SemaphoreType.DMA],
        name="sc_scatter_rows",
    )
    def scatter(src_hbm, dest_hbm, out_hbm, idx_v, rows_a, rows_b, sem_a, sem_b, sem_w):
        wid = lax.axis_index("s") * SC_CORES + lax.axis_index("c")
        base = wid * per_worker
        pltpu.sync_copy(dest_hbm.at[wid], idx_v)

        def load(c, buf, sem):
            off = pl.multiple_of(base + c * rows, rows)
            return pltpu.make_async_copy(src_hbm.at[pl.ds(off, rows)], buf, sem)

        def spread(c, buf):
            copies = [pltpu.make_async_copy(buf, out_hbm.at[idx_v.at[c, k]], sem_w) for k in range(top_k)]
            for cp in copies:
                cp.start()
            for cp in copies:
                cp.wait()

        load(0, rows_a, sem_a).start()

        @pl.loop(0, n_chunks, step=2)
        def _(c):
            load(c, rows_a, sem_a).wait()
            load(c + 1, rows_b, sem_b).start()
            spread(c, rows_a)
            load(c + 1, rows_b, sem_b).wait()

            @pl.when(c + 2 < n_chunks)
            def _():
                load(c + 2, rows_a, sem_a).start()

            spread(c + 1, rows_b)

    return scatter(src, dest)


def _expert_kernel(be_ref, slot_ref, next_ref, nused_ref, x_ref, wgu_hbm, wd_hbm, bg_ref, bl_ref, bd_ref, sel_ref, y_ref,
                   wgu_f, wd_f, wg_s, wl_s, wd_s, sems):
    i = pl.program_id(0)
    active = i < nused_ref[0]
    expert = be_ref[i]
    fresh = (i == 0) | (expert != be_ref[jnp.maximum(i - 1, 0)])

    def fetch(e, slot):
        return (pltpu.make_async_copy(wgu_hbm.at[e], wgu_f.at[slot], sems.at[slot, 0]),
                pltpu.make_async_copy(wd_hbm.at[e], wd_f.at[slot], sems.at[slot, 1]))

    @pl.when(active & fresh)
    def _():
        slot = slot_ref[i]

        @pl.when(i == 0)
        def _():
            for cp in fetch(expert, slot):
                cp.start()

        for cp in fetch(expert, slot):
            cp.wait()
        nxt = next_ref[i]

        @pl.when(nxt >= 0)
        def _():
            for cp in fetch(nxt, 1 - slot):
                cp.start()

        for t in range(D_FF // MXU_DIM):
            src = wgu_f[slot, :, 2 * t * MXU_DIM:2 * (t + 1) * MXU_DIM].astype(BF16)
            both = _dot(src, sel_ref[...]).astype(BF16)
            wg_s[:, t * MXU_DIM:(t + 1) * MXU_DIM] = both[:, :MXU_DIM]
            wl_s[:, t * MXU_DIM:(t + 1) * MXU_DIM] = both[:, MXU_DIM:]
        wd_s[...] = wd_f[slot].astype(BF16)

    @pl.when(active)
    def _():
        x = _unpack_bf16_pairs(x_ref[...]).astype(BF16)
        h_glu = jnp.minimum(_dot(x, wg_s[...]) + bg_ref[...], SWIGLU_LIMIT)
        h_lin = jnp.clip(_dot(x, wl_s[...]) + bl_ref[...], -SWIGLU_LIMIT, SWIGLU_LIMIT)
        act = h_glu * _sigmoid(SWIGLU_ALPHA * h_glu) * (h_lin + 1.0)
        y_ref[...] = _pack_bf16_pairs(_dot(act.astype(BF16), wd_s[...]) + bd_ref[...])

    @pl.when(jnp.logical_not(active))
    def _():
        y_ref[...] = jnp.zeros_like(y_ref)


def _experts(xb, block_expert, block_slot, block_next, n_used, w_gate_up, w_down, b_glu, b_lin, b_down, sel):
    n_slots = xb.shape[0]
    n_blocks = n_slots // EXPERT_BLOCK
    by_expert = lambda k, n: pl.BlockSpec((None, k, n), lambda i, be, sl, nx, nu: (be[i], 0, 0))
    blk = pl.BlockSpec((EXPERT_BLOCK, D_MODEL // 2), lambda i, be, sl, nx, nu: (i, 0))
    hbm = pl.BlockSpec(memory_space=pl.ANY)
    return pl.pallas_call(
        _expert_kernel,
        grid_spec=pltpu.PrefetchScalarGridSpec(
            num_scalar_prefetch=4, grid=(n_blocks,),
            in_specs=[blk, hbm, hbm, by_expert(1, D_FF), by_expert(1, D_FF), by_expert(1, D_MODEL),
                      pl.BlockSpec(sel.shape, lambda i, be, sl, nx, nu: (0, 0), pipeline_mode=pl.Buffered(1))],
            out_specs=blk,
            scratch_shapes=[pltpu.VMEM((2, D_MODEL, 2 * D_FF), F32), pltpu.VMEM((2, D_FF, D_MODEL), F32),
                            pltpu.VMEM((D_MODEL, D_FF), BF16), pltpu.VMEM((D_MODEL, D_FF), BF16),
                            pltpu.VMEM((D_FF, D_MODEL), BF16), pltpu.SemaphoreType.DMA((2, 2))]),
        out_shape=jax.ShapeDtypeStruct((n_slots, D_MODEL // 2), jnp.int32),
        compiler_params=_params(),
        name="experts",
    )(block_expert, block_slot, block_next, n_used, xb, w_gate_up, w_down, b_glu, b_lin, b_down, sel)


def _final_kernel(h1_ref, yg_ref, gate_ref, p_ref, gple_ref, wg_ref, wp_ref, out_ref):
    h2 = h1_ref[...]
    gates = gate_ref[...]
    for k in range(TOP_K):
        h2 = h2 + gates[:, k:k + 1] * _unpack_bf16_pairs(yg_ref[k])
    n3 = ((h2 * _rms(h2)) * gple_ref[...]).astype(BF16)
    gate = _sigmoid(_dot(n3, wg_ref[...]))
    out_ref[...] = h2 + gate * _dot(p_ref[...].astype(BF16), wp_ref[...])


def _final(h1, yg, gates, p, tile_offset, g_ple, w_ple_gate, w_ple_proj):
    n_tok = p.shape[0]
    tm = TOKEN_TILE
    return pl.pallas_call(
        _final_kernel,
        grid=(n_tok // tm,),
        in_specs=[pl.BlockSpec((tm, D_MODEL), lambda i: (i + tile_offset, 0)),
                  pl.BlockSpec((TOP_K, tm, D_MODEL // 2), lambda i: (0, i + tile_offset, 0)),
                  pl.BlockSpec((tm, TOP_K), lambda i: (i + tile_offset, 0)),
                  pl.BlockSpec((tm, PLE_DIM), lambda i: (i, 0)),
                  _resident(g_ple.shape), _resident(w_ple_gate.shape), _resident(w_ple_proj.shape)],
        out_specs=pl.BlockSpec((tm, D_MODEL), lambda i: (i, 0)),
        out_shape=jax.ShapeDtypeStruct((n_tok, D_MODEL), F32),
        compiler_params=_params(),
        name="final",
    )(h1, yg, gates, p, g_ple, w_ple_gate, w_ple_proj)


def _routing_tables(idx4, pos4, counts, n_slots):
    counts = counts.reshape(N_EXPERTS).astype(jnp.int32)
    pcounts = (counts + EXPERT_BLOCK - 1) // EXPERT_BLOCK * EXPERT_BLOCK
    pends = jnp.cumsum(pcounts)
    pstarts = pends - pcounts
    experts = jnp.arange(N_EXPERTS, dtype=jnp.int32)
    start4 = jnp.sum(jnp.where(idx4[:, :, None] == experts, pstarts, 0), axis=-1)
    dest4 = (start4 + pos4).astype(jnp.int32)
    n_blocks = n_slots // EXPERT_BLOCK
    block_start = jnp.arange(n_blocks, dtype=jnp.int32) * EXPERT_BLOCK
    block_expert = jnp.minimum(jnp.sum(block_start[:, None] >= pends[None, :], axis=1), N_EXPERTS - 1).astype(jnp.int32)
    n_used = (pends[-1] // EXPERT_BLOCK).astype(jnp.int32).reshape(1)
    used = counts > 0
    slot_e = (jnp.cumsum(used.astype(jnp.int32)) - 1) & 1
    later_used = used[None, :] & (experts[None, :] > experts[:, None])
    next_e = jnp.min(jnp.where(later_used, experts[None, :], N_EXPERTS), axis=1)
    next_e = jnp.where(next_e == N_EXPERTS, -1, next_e)
    of_block = block_expert[:, None] == experts[None, :]
    block_slot = jnp.sum(jnp.where(of_block, slot_e[None, :], 0), axis=1).astype(jnp.int32)
    block_next = jnp.sum(jnp.where(of_block, next_e[None, :], 0), axis=1).astype(jnp.int32)
    return dest4, block_expert, block_slot, block_next, n_used


def kernel(x_prompt, x_sample, cache_kv_w128, cache_kv_w512, cache_kv_w2048, p_prompt, p_sample, g_mix, w_in, g_v, g_q, g_k, w_spatial, b_spatial, w_branch_a, w_branch_b, w_out, g_moe, w_router, b_router, w_gate_up, b_gate_up, w_down, b_down, g_ple, w_ple_gate, w_ple_proj):
    bsz, seq, _ = x_prompt.shape
    dbatch, t_new, _ = x_sample.shape
    assert g_mix.shape[0] == 1
    caches = (cache_kv_w128, cache_kv_w512, cache_kv_w2048)
    l = 0
    n_p, n_s = bsz * seq, dbatch * t_new
    n_tok = n_p + n_s
    assert n_p % TOKEN_TILE == 0 and n_s % TOKEN_TILE == 0

    row2 = lambda t: t.reshape(1, -1)
    w_in_bf = w_in[l].astype(BF16)
    g_q_t = jnp.tile(g_q[l], B_HEADS).reshape(1, B_WIDTH)
    g_k_t = jnp.tile(g_k[l], B_HEADS).reshape(1, B_WIDTH)
    hid = np.arange(MXU_DIM) // HEAD_DIM
    hsum = jnp.asarray(hid[:, None] == hid[None, :], BF16)
    tril = jnp.tril(jnp.ones((CHUNK, CHUNK), bool))
    wm_prompt = jnp.where(tril[None], w_spatial[l], 0).astype(BF16)
    bias_prompt = jnp.repeat(b_spatial[l].T, A_GROUP_WIDTH, axis=1)
    reps = CHUNK // t_new
    small = jnp.where(tril[None, :t_new, :t_new], w_spatial[l][:, :t_new, :t_new], 0)
    wm_sample = jnp.einsum("ab,gij->gaibj", jnp.eye(reps, dtype=F32), small).reshape(A_GROUPS, CHUNK, CHUNK).astype(BF16)
    bias_sample = jnp.tile(bias_prompt[:t_new], (reps, 1))
    wr_hi = w_router[l].astype(BF16)
    wr_lo = (w_router[l] - wr_hi.astype(F32)).astype(BF16)
    tri = jnp.asarray(np.tril(np.ones((TOKEN_TILE, TOKEN_TILE), np.float32), -1), BF16)
    consts = (w_branch_a[l].astype(BF16), w_branch_b[l].astype(BF16), w_out[l].astype(BF16), row2(g_moe[l]),
              wr_hi, wr_lo, row2(b_router[l]), tri)
    sel_np = np.zeros((2 * MXU_DIM, 2 * MXU_DIM), np.float32)
    sel_np[2 * np.arange(MXU_DIM), np.arange(MXU_DIM)] = 1.0
    sel_np[2 * np.arange(MXU_DIM) + 1, MXU_DIM + np.arange(MXU_DIM)] = 1.0
    sel = jnp.asarray(sel_np, BF16)

    proj = functools.partial(_project, g_mix=row2(g_mix[l]), w_in_bf=w_in_bf, g_v=row2(g_v[l]),
                             g_q_t=g_q_t, g_k_t=g_k_t, hsum=hsum)

    xp = x_prompt.reshape(n_p, D_MODEL)
    u_p, va_p, q_p, k_p, v_p, ga_p, gb_p = proj(xp, va_dtype=BF16)
    seq3 = lambda t: t.reshape(bsz, seq, B_WIDTH)
    attn_p = [_attend_prompt(seq3(q_p), seq3(k_p), seq3(v_p), gi, bsz, seq) for gi in range(N_GROUPS_B)]
    zero_counts = jnp.zeros((1, N_EXPERTS), F32)
    outs_p, cnt_p = _finish(xp, u_p, va_p, wm_prompt, bias_prompt, attn_p, ga_p, gb_p, consts, zero_counts,
                            n_tok, 0, None)

    xs = x_sample.reshape(n_s, D_MODEL)
    u_s, va_s, q_s, k_s, v_s, ga_s, gb_s = proj(xs, va_dtype=F32)
    tok3 = lambda t: t.reshape(dbatch, t_new, B_WIDTH)
    caches_t = [jnp.transpose(c[l], (0, 2, 3, 4, 1)) for c in caches]
    o_s = _attend_sample(tok3(q_s), tok3(k_s), tok3(v_s), caches_t, dbatch, t_new)
    (h1, n2, idx4, gates, pos4), counts = _finish(xs, u_s, va_s, wm_sample, bias_sample, [(o_s, None)], ga_s, gb_s,
                                                  consts, cnt_p, n_tok, n_p // TOKEN_TILE, outs_p)

    n_blocks = -(-n_tok * TOP_K // EXPERT_BLOCK) + N_EXPERTS
    n_slots = n_blocks * EXPERT_BLOCK
    dest4, block_expert, block_slot, block_next, n_used = _routing_tables(idx4, pos4, counts, n_slots)
    xb = _sc_scatter_rows(n2, dest4, n_slots)
    yb = _experts(xb, block_expert, block_slot, block_next, n_used, w_gate_up[l], w_down[l],
                  b_gate_up[l][:, None, 0::2], b_gate_up[l][:, None, 1::2], b_down[l][:, None, :], sel)
    yg = _sc_gather_rows(yb, dest4.T.reshape(-1)).reshape(TOP_K, n_tok, D_MODEL // 2)
    fin = functools.partial(_final, g_ple=row2(g_ple[l]), w_ple_gate=w_ple_gate[l].astype(BF16),
                            w_ple_proj=w_ple_proj[l].astype(BF16))
    y_prompt = fin(h1, yg, gates, p_prompt[l].reshape(n_p, PLE_DIM), 0).reshape(bsz, seq, D_MODEL)
    y_sample = fin(h1, yg, gates, p_sample[l].reshape(n_s, PLE_DIM), n_p // TOKEN_TILE).reshape(dbatch, t_new, D_MODEL)

    kv_prompt = []
    for gi in range(N_GROUPS_B):
        t = _kv_tail(seq3(k_p), seq3(v_p), gi, bsz, seq)
        keep = t.shape[-1]
        kv_prompt.append(jnp.transpose(t.reshape(bsz, 2, H_SLOT, HEAD_DIM, keep), (0, 4, 1, 2, 3))[None])
    kvs = _kv_sample(k_s, v_s, dbatch, t_new)
    kv_sample = [jnp.transpose(kvs[gi].reshape(t_new, 2, H_SLOT, HEAD_DIM, dbatch), (4, 0, 1, 2, 3))[None]
                 for gi in range(N_GROUPS_B)]
    va_out = va_s.reshape(1, dbatch, t_new, A_WIDTH)
    return (y_prompt, y_sample, *kv_prompt, *kv_sample, va_out)
```

```python
import functools

import numpy as np
import jax
import jax.numpy as jnp
from jax import lax
from jax.experimental import pallas as pl
from jax.experimental.pallas import tpu as pltpu
from jax.experimental.pallas import tpu_sc as plsc

F32 = jnp.float32
BF16 = jnp.bfloat16

D_MODEL = 1024
A_WIDTH = 1024
A_GROUPS = 4
A_GROUP_WIDTH = A_WIDTH // A_GROUPS
CHUNK = 128
HEAD_DIM = 64
H_SLOT = 8
GROUP_WIDTH = H_SLOT * HEAD_DIM
DILATION_GROUPS = ((128, 1), (512, 4), (2048, 16))
N_GROUPS_B = len(DILATION_GROUPS)
B_HEADS = H_SLOT * N_GROUPS_B
B_WIDTH = B_HEADS * HEAD_DIM
N_EXPERTS = 32
TOP_K = 4
D_FF = 1024
SWIGLU_ALPHA = 1.702
SWIGLU_LIMIT = 7.0
PLE_DIM = 256
RMS_EPS = 1e-6
NEG_BIG = -1e30
LOG2_E = float(np.log2(np.e))
LN_2 = float(np.log(2.0))

LANES = 128
SUBLANES = 8
MXU_DIM = 256
HEADS_PER_VREG = LANES // HEAD_DIM
PAIRS = GROUP_WIDTH // LANES
RESIDUE_UNROLL = 4
TOKEN_TILE = 512
EXPERT_BLOCK = 256
VMEM_LIMIT = 56 * 1024 * 1024

SC_CORES = 2
SC_SUBCORES = 16
SC_WORKERS = SC_CORES * SC_SUBCORES
SC_ROWS = 32
SC_SCATTER_ROWS = 16

_COL_SPLITS = np.cumsum([0, A_WIDTH, A_WIDTH, B_WIDTH, B_WIDTH, B_WIDTH, D_MODEL, D_MODEL]).tolist()


def _alibi_slopes():
    return np.exp2(-8.0 * np.arange(1, B_HEADS + 1, dtype=np.float32) / B_HEADS).astype(np.float32)


def _sigmoid(x):
    return 1.0 / (1.0 + jnp.exp(-x))


def _rms(x):
    return lax.rsqrt(jnp.mean(x * x, axis=-1, keepdims=True) + RMS_EPS)


def _dot(a, b):
    return jnp.dot(a, b, preferred_element_type=F32)


def _dot_nt(a, b):
    return lax.dot_general(a, b, (((1,), (1,)), ((), ())), preferred_element_type=F32)


def _pack_bf16_pairs(x):
    w = x.shape[1] // 2
    lo = lax.bitcast_convert_type(x[:, :w].astype(BF16).astype(F32), jnp.uint32) >> 16
    hi = lax.bitcast_convert_type(x[:, w:].astype(BF16).astype(F32), jnp.uint32) & jnp.uint32(0xFFFF0000)
    return lax.bitcast_convert_type(lo | hi, jnp.int32)


def _unpack_bf16_pairs(p):
    u = lax.bitcast_convert_type(p, jnp.uint32)
    lo = lax.bitcast_convert_type(u << 16, F32)
    hi = lax.bitcast_convert_type(u & jnp.uint32(0xFFFF0000), F32)
    return jnp.concatenate([lo, hi], axis=1)


def _resident(shape):
    nd = len(shape)
    return pl.BlockSpec(shape, lambda *_: (0,) * nd, pipeline_mode=pl.Buffered(1))


def _params(n_axes=1):
    return pltpu.CompilerParams(dimension_semantics=("arbitrary",) * n_axes, vmem_limit_bytes=VMEM_LIMIT)


def _proj_kernel(x_ref, gmix_ref, w_ref, gv_ref, gq_ref, gk_ref, hsum_ref,
                 u_ref, va_ref, q_ref, k_ref, v_ref, ga_ref, gb_ref):
    x = x_ref[...]
    n = ((x * _rms(x)) * gmix_ref[...]).astype(BF16)

    def section(i):
        return _dot(n, w_ref[:, _COL_SPLITS[i]:_COL_SPLITS[i + 1]])

    u_ref[...] = jax.nn.gelu(section(0)).astype(u_ref.dtype)
    va = jax.nn.gelu(section(1))
    va_ref[...] = ((va * _rms(va)) * gv_ref[...]).astype(va_ref.dtype)

    def head_norm(z, g_ref, scale):
        parts = []
        for c in range(B_WIDTH // MXU_DIM):
            zc = z[:, c * MXU_DIM:(c + 1) * MXU_DIM]
            ss = _dot((zc * zc).astype(BF16), hsum_ref[...])
            parts.append(zc * lax.rsqrt(ss * (1.0 / HEAD_DIM) + RMS_EPS))
        return jnp.concatenate(parts, axis=1) * (g_ref[...] * scale)

    q_ref[...] = head_norm(section(2), gq_ref, HEAD_DIM ** -0.5 * LOG2_E)
    k_ref[...] = head_norm(section(3), gk_ref, 1.0)
    v_ref[...] = section(4)
    ga_ref[...] = _sigmoid(section(5)).astype(ga_ref.dtype)
    gb_ref[...] = _sigmoid(section(6)).astype(gb_ref.dtype)


def _project(x, g_mix, w_in_bf, g_v, g_q_t, g_k_t, hsum, va_dtype):
    n_tok = x.shape[0]
    tm = TOKEN_TILE
    row = lambda w: pl.BlockSpec((tm, w), lambda i: (i, 0))
    outs = [(A_WIDTH, BF16), (A_WIDTH, va_dtype), (B_WIDTH, F32), (B_WIDTH, F32), (B_WIDTH, F32),
            (D_MODEL, BF16), (D_MODEL, BF16)]
    return pl.pallas_call(
        _proj_kernel,
        grid=(n_tok // tm,),
        in_specs=[row(D_MODEL), _resident(g_mix.shape), _resident(w_in_bf.shape), _resident(g_v.shape),
                  _resident(g_q_t.shape), _resident(g_k_t.shape), _resident(hsum.shape)],
        out_specs=[row(w) for w, _ in outs],
        out_shape=[jax.ShapeDtypeStruct((n_tok, w), dt) for w, dt in outs],
        compiler_params=_params(),
        name="project",
    )(x, g_mix, w_in_bf, g_v, g_q_t, g_k_t, hsum)


def _band_bias(band, dil, slopes):
    qi = jnp.arange(band, dtype=jnp.int32)[:, None]
    kj = jnp.arange(2 * band, dtype=jnp.int32)[None, :]
    dist = qi + band - kj
    in_band = (dist >= 0) & (dist <= band)
    valid = jnp.stack([in_band & (kj >= band), in_band])
    penalty = (jnp.asarray(slopes, F32) * LOG2_E)[:, None, None] * (dist * dil).astype(F32)[None]
    return jnp.where(valid[:, None], -penalty[None], NEG_BIG)


def _pair_attention(q2, k, v, bias_pair):
    band = q2.shape[0]
    first = lax.broadcasted_iota(jnp.int32, (band, LANES), 1) < HEAD_DIM
    qs = jnp.concatenate([jnp.where(first, q2, 0.0), jnp.where(first, 0.0, q2)], axis=0).astype(BF16)
    s = _dot_nt(qs, k) + bias_pair.reshape(HEADS_PER_VREG * band, 2 * band)
    m = jnp.max(s, axis=-1, keepdims=True)
    e = jnp.exp2(s - m).astype(BF16)
    v_ones = jnp.concatenate([v, jnp.ones_like(v)], axis=1)
    r = _dot(e, v_ones)
    den = r[:, LANES:]
    o2 = r[:, :LANES] / den
    lse = m * LN_2 + jnp.log(den)
    return jnp.where(first, o2[:band], o2[band:]), jnp.where(first, lse[:band], lse[band:])


def _attn_dense_kernel(q_ref, kp_ref, kc_ref, vp_ref, vc_ref, bias_ref, o_ref, l_ref):
    for p in range(PAIRS):
        cols = slice(p * LANES, (p + 1) * LANES)
        k = jnp.concatenate([kp_ref[:, cols], kc_ref[:, cols]], axis=0).astype(BF16)
        v = jnp.concatenate([vp_ref[:, cols], vc_ref[:, cols]], axis=0).astype(BF16)
        bias2 = bias_ref[p * HEADS_PER_VREG:(p + 1) * HEADS_PER_VREG]
        o_pair, l_pair = _pair_attention(q_ref[:, cols], k, v, bias2)
        o_ref[:, cols] = o_pair.astype(o_ref.dtype)
        l_ref[:, cols] = l_pair


def _attn_dilated_kernel(q_ref, kp_ref, kc_ref, vp_ref, vc_ref, bias_ref, *rest, band, dil, n_others):
    others, rest = rest[:2 * n_others], rest[2 * n_others:]
    if n_others:
        out_ref, o_ref, l_ref = rest
    else:
        o_ref, l_ref = rest

    def residues(it, carry):
        for u in range(RESIDUE_UNROLL):
            rows = pl.ds(it * RESIDUE_UNROLL + u, band, stride=dil)
            k = jnp.concatenate([kp_ref[rows, :], kc_ref[rows, :]], axis=0).astype(BF16)
            v = jnp.concatenate([vp_ref[rows, :], vc_ref[rows, :]], axis=0).astype(BF16)
            o_pair, l_pair = _pair_attention(q_ref[rows, :], k, v, bias_ref[...])
            o_ref[rows, :] = o_pair
            l_ref[rows, :] = l_pair
        return carry

    lax.fori_loop(0, dil // RESIDUE_UNROLL, residues, 0)

    if n_others:
        outs = [others[2 * g][...].astype(F32) for g in range(n_others)] + [o_ref[...]]
        lses = [others[2 * g + 1][...] for g in range(n_others)] + [l_ref[...]]
        mx = functools.reduce(jnp.maximum, lses)
        ws = [jnp.exp(l - mx) for l in lses]
        wsum = functools.reduce(jnp.add, ws)
        out_ref[...] = (sum(w * o for w, o in zip(ws, outs)) / wsum).astype(out_ref.dtype)


def _attend_prompt(q, k, v, gi, bsz, seq, others=()):
    win, dil = DILATION_GROUPS[gi]
    band = win // dil
    bias = _band_bias(band, dil, _alibi_slopes()[gi * H_SLOT:(gi + 1) * H_SLOT])
    if dil == 1:
        cur = pl.BlockSpec((None, band, GROUP_WIDTH), lambda b, j: (b, j, gi))
        prev = pl.BlockSpec((None, band, GROUP_WIDTH), lambda b, j: (b, jnp.maximum(j - 1, 0), gi))
        tab = pl.BlockSpec((None, H_SLOT, band, 2 * band), lambda b, j: (jnp.minimum(j, 1), 0, 0, 0))
        out = pl.BlockSpec((None, band, GROUP_WIDTH), lambda b, j: (b, j, 0))
        o, lse = pl.pallas_call(
            _attn_dense_kernel,
            grid=(bsz, seq // band),
            in_specs=[cur, prev, cur, prev, cur, tab],
            out_specs=[out, out],
            out_shape=[jax.ShapeDtypeStruct((bsz, seq, GROUP_WIDTH), BF16),
                       jax.ShapeDtypeStruct((bsz, seq, GROUP_WIDTH), F32)],
            compiler_params=_params(2),
            name=f"attn_prompt_w{win}",
        )(q, k, k, v, v, bias)
    else:
        col = lambda p: gi * PAIRS + p
        cur = pl.BlockSpec((None, win, LANES), lambda b, j, p: (b, j, col(p)))
        prev = pl.BlockSpec((None, win, LANES), lambda b, j, p: (b, jnp.maximum(j - 1, 0), col(p)))
        tab = pl.BlockSpec((None, HEADS_PER_VREG, band, 2 * band), lambda b, j, p: (jnp.minimum(j, 1), p, 0, 0))
        out = pl.BlockSpec((None, win, LANES), lambda b, j, p: (b, j, p))
        kern = functools.partial(_attn_dilated_kernel, band=band, dil=dil, n_others=len(others))
        common = dict(grid=(bsz, seq // win, PAIRS), compiler_params=_params(3), name=f"attn_prompt_w{win}")
        if others:
            extra = [t.reshape(bsz, seq, GROUP_WIDTH) for pair in others for t in pair]
            combined = pl.pallas_call(
                kern,
                in_specs=[cur, prev, cur, prev, cur, tab] + [out] * len(extra),
                out_specs=out,
                out_shape=jax.ShapeDtypeStruct((bsz, seq, GROUP_WIDTH), BF16),
                scratch_shapes=[pltpu.VMEM((win, LANES), F32), pltpu.VMEM((win, LANES), F32)],
                **common,
            )(q, k, k, v, v, bias, *extra)
            return combined.reshape(bsz * seq, GROUP_WIDTH)
        o, lse = pl.pallas_call(
            kern,
            in_specs=[cur, prev, cur, prev, cur, tab],
            out_specs=[out, out],
            out_shape=[jax.ShapeDtypeStruct((bsz, seq, GROUP_WIDTH), F32)] * 2,
            **common,
        )(q, k, k, v, v, bias)
    return o.reshape(bsz * seq, GROUP_WIDTH), lse.reshape(bsz * seq, GROUP_WIDTH)


def _attn_sample_kernel(q_ref, kn_ref, vn_ref, c0_ref, c1_ref, c2_ref, o_ref, *, t_new, slopes):
    rows = H_SLOT * t_new
    pad_new = 16
    t_shift = t_new.bit_length() - 1
    caches = (c0_ref, c1_ref, c2_ref)
    row_id = lax.broadcasted_iota(jnp.int32, (rows, GROUP_WIDTH), 0)
    col_id = lax.broadcasted_iota(jnp.int32, (rows, GROUP_WIDTH), 1)
    own_head = (row_id >> t_shift) == (col_id >> (HEAD_DIM.bit_length() - 1))
    head_of_row = lax.broadcasted_iota(jnp.int32, (rows, 1), 0) >> t_shift
    zpad = jnp.zeros((pad_new - t_new, GROUP_WIDTH), F32)
    outs, lses = [], []
    for g, (win, dil) in enumerate(DILATION_GROUPS):
        buf_len = caches[g].shape[-1]
        cols = slice(g * GROUP_WIDTH, (g + 1) * GROUP_WIDTH)

        def mask_of(n_keys, first_pos, n_real):
            i_q = lax.broadcasted_iota(jnp.int32, (rows, n_keys), 0) & (t_new - 1)
            key = lax.broadcasted_iota(jnp.int32, (rows, n_keys), 1)
            delta = buf_len + i_q - (first_pos + key)
            ok = (delta >= 0) & ((delta & (dil - 1)) == 0) & (delta <= win) & (key < n_real)
            return ok, delta.astype(F32)

        kt = caches[g][0].reshape(GROUP_WIDTH, buf_len).astype(BF16)
        vt = caches[g][1].reshape(GROUP_WIDTH, buf_len).astype(BF16)
        qblk = jnp.where(own_head, jnp.concatenate([q_ref[:, cols]] * H_SLOT, axis=0), 0.0).astype(BF16)
        kn = jnp.concatenate([kn_ref[:, cols], zpad], axis=0).astype(BF16)
        vn = jnp.concatenate([vn_ref[:, cols], zpad], axis=0).astype(BF16)
        slope_rows = jnp.zeros((rows, 1), F32)
        for h in range(H_SLOT):
            slope_rows = jnp.where(head_of_row == h, float(slopes[g * H_SLOT + h]) * LOG2_E, slope_rows)
        ok_c, delta_c = mask_of(buf_len, 0, buf_len)
        ok_n, delta_n = mask_of(pad_new, buf_len, t_new)
        sc = jnp.where(ok_c, _dot(qblk, kt) - slope_rows * delta_c, NEG_BIG)
        sn = jnp.where(ok_n, _dot_nt(qblk, kn) - slope_rows * delta_n, NEG_BIG)
        m = jnp.maximum(jnp.max(sc, axis=-1, keepdims=True), jnp.max(sn, axis=-1, keepdims=True))
        ec = jnp.exp2(sc - m)
        en = jnp.exp2(sn - m)
        den = jnp.sum(ec, axis=-1, keepdims=True) + jnp.sum(en, axis=-1, keepdims=True)
        outs.append((_dot_nt(ec.astype(BF16), vt) + _dot(en.astype(BF16), vn)) * (1.0 / den))
        lses.append(m * LN_2 + jnp.log(den))
    mx = functools.reduce(jnp.maximum, lses)
    ws = [jnp.exp(l - mx) for l in lses]
    wsum = functools.reduce(jnp.add, ws)
    comb = sum((w / wsum) * o for w, o in zip(ws, outs))
    comb = jnp.where(own_head, comb, 0.0)
    o_tok = sum(comb[h * t_new:(h + 1) * t_new, :] for h in range(H_SLOT))
    o_ref[...] = o_tok.astype(o_ref.dtype)


def _attend_sample(q, k_new, v_new, caches_t, dbatch, t_new):
    assert t_new & (t_new - 1) == 0 and t_new <= 8
    tok = pl.BlockSpec((None, t_new, B_WIDTH), lambda b: (b, 0, 0))
    cache_specs = [pl.BlockSpec((None,) + c.shape[1:], lambda b: (b, 0, 0, 0, 0)) for c in caches_t]
    o = pl.pallas_call(
        functools.partial(_attn_sample_kernel, t_new=t_new, slopes=_alibi_slopes()),
        grid=(dbatch,),
        in_specs=[tok, tok, tok] + cache_specs,
        out_specs=pl.BlockSpec((None, t_new, GROUP_WIDTH), lambda b: (b, 0, 0)),
        out_shape=jax.ShapeDtypeStruct((dbatch, t_new, GROUP_WIDTH), BF16),
        compiler_params=_params(),
        name="attn_sample",
    )(q, k_new, v_new, *caches_t)
    return o.reshape(dbatch * t_new, GROUP_WIDTH)


def _kv_tail_kernel(k_ref, v_ref, o_ref):
    o_ref[0] = k_ref[...].T
    o_ref[1] = v_ref[...].T


def _kv_tail(k, v, gi, bsz, seq):
    keep = min(DILATION_GROUPS[gi][0], seq)
    pb = min(keep, 512)
    first = (seq - keep) // pb
    src = pl.BlockSpec((None, pb, GROUP_WIDTH), lambda b, j: (b, first + j, gi))
    return pl.pallas_call(
        _kv_tail_kernel,
        grid=(bsz, keep // pb),
        in_specs=[src, src],
        out_specs=pl.BlockSpec((None, 2, GROUP_WIDTH, pb), lambda b, j: (b, 0, 0, j)),
        out_shape=jax.ShapeDtypeStruct((bsz, 2, GROUP_WIDTH, keep), F32),
        compiler_params=_params(2),
        name=f"kv_tail_w{DILATION_GROUPS[gi][0]}",
    )(k, v)


def _kv_sample_kernel(*refs, t_new, dbatch):
    in_refs, out_refs = refs[:2 * N_GROUPS_B], refs[2 * N_GROUPS_B:]
    for g, o_ref in enumerate(out_refs):
        k_ref, v_ref = in_refs[2 * g], in_refs[2 * g + 1]
        for t in range(t_new):
            rows = pl.ds(t, dbatch, stride=t_new)
            o_ref[t, 0] = k_ref[rows, :].T
            o_ref[t, 1] = v_ref[rows, :].T


def _kv_sample(k, v, dbatch, t_new):
    src = lambda g: pl.BlockSpec((dbatch * t_new, LANES), lambda s: (0, g * PAIRS + s))
    return pl.pallas_call(
        functools.partial(_kv_sample_kernel, t_new=t_new, dbatch=dbatch),
        grid=(PAIRS,),
        in_specs=[src(g) for g in range(N_GROUPS_B) for _ in range(2)],
        out_specs=[pl.BlockSpec((t_new, 2, LANES, dbatch), lambda s: (0, 0, s, 0))] * N_GROUPS_B,
        out_shape=[jax.ShapeDtypeStruct((t_new, 2, GROUP_WIDTH, dbatch), F32)] * N_GROUPS_B,
        compiler_params=_params(),
        name="kv_sample",
    )(*[t for _ in range(N_GROUPS_B) for t in (k, v)])


def _topk_route(logits):
    tm = logits.shape[0]
    lane = lax.broadcasted_iota(jnp.int32, logits.shape, 1).astype(F32)
    slot = lax.broadcasted_iota(jnp.int32, (tm, TOP_K), 1)
    val_out = jnp.zeros((tm, TOP_K), F32)
    work = logits
    ids, top = [], None
    for r in range(TOP_K):
        mx = jnp.max(work, axis=-1, keepdims=True)
        ix = jnp.min(jnp.where(work == mx, lane, float(N_EXPERTS)), axis=-1, keepdims=True)
        top = mx if top is None else top
        ids.append(ix)
        val_out = jnp.where(slot == r, jnp.exp(mx - top), val_out)
        work = jnp.where(lane == ix, -jnp.inf, work)
    gates = val_out / jnp.sum(val_out, axis=-1, keepdims=True)
    return ids, gates


_FINISH_INPUTS = 17


def _finish_kernel(*refs, n_alias, n_valid):
    ins, outs = refs[:_FINISH_INPUTS], refs[_FINISH_INPUTS + n_alias:]
    step = pl.program_id(0)

    @pl.when(step < n_valid)
    def _():
        _finish_tile(ins, outs)

    @pl.when(step >= n_valid)
    def _():
        for ref in outs[:5]:
            ref[...] = jnp.zeros_like(ref)


def _finish_tile(ins, outs):
    (h_ref, u_ref, va_ref, wm_ref, bias_ref, ob_ref, ga_ref, gb_ref, wa_ref, wb_ref, wo_ref, gmoe_ref,
     wrh_ref, wrl_ref, br_ref, tri_ref, cnt_in_ref) = ins
    h1_ref, n2_ref, idx_ref, gate_ref, pos_ref, cnt_out_ref, carry_ref = outs
    tm = h_ref.shape[0]

    chunks = []
    for c in range(tm // CHUNK):
        rows = slice(c * CHUNK, (c + 1) * CHUNK)
        va_c = va_ref[rows, :].astype(BF16)
        mixed = jnp.concatenate(
            [_dot(wm_ref[g], va_c[:, g * A_GROUP_WIDTH:(g + 1) * A_GROUP_WIDTH]) for g in range(A_GROUPS)], axis=1)
        chunks.append((u_ref[rows, :].astype(F32) * (mixed + bias_ref[...])).astype(BF16))
    branch_a = _dot(jnp.concatenate(chunks, axis=0), wa_ref[...])

    branch_b = _dot(ob_ref[...], wb_ref[...])

    mix = ga_ref[...].astype(F32) * branch_a + gb_ref[...].astype(F32) * branch_b
    h1 = h_ref[...] + _dot(mix.astype(BF16), wo_ref[...])
    h1_ref[...] = h1

    n2 = (h1 * _rms(h1)) * gmoe_ref[...]
    n2_ref[...] = _pack_bf16_pairs(n2)
    n_hi = n2.astype(BF16)
    n_lo = (n2 - n_hi.astype(F32)).astype(BF16)
    logits = _dot(n_hi, wrh_ref[...]) + _dot(n_lo, wrh_ref[...]) + _dot(n_hi, wrl_ref[...]) + br_ref[...]
    ids, gates = _topk_route(logits)
    gate_ref[...] = gates

    @pl.when(pl.program_id(0) == 0)
    def _():
        carry_ref[...] = cnt_in_ref[...]

    lane = lax.broadcasted_iota(jnp.int32, logits.shape, 1).astype(F32)
    slot = lax.broadcasted_iota(jnp.int32, (tm, TOP_K), 1)
    sel = sum(jnp.where(lane == ix, 1.0, 0.0) for ix in ids)
    before = _dot(tri_ref[...], sel.astype(BF16)) + carry_ref[...]
    idx_out = jnp.zeros((tm, TOP_K), F32)
    pos_out = jnp.zeros((tm, TOP_K), F32)
    for r, ix in enumerate(ids):
        idx_out = jnp.where(slot == r, ix, idx_out)
        rank = jnp.sum(jnp.where(lane == ix, before, 0.0), axis=-1, keepdims=True)
        pos_out = jnp.where(slot == r, rank, pos_out)
    idx_ref[...] = idx_out.astype(jnp.int32)
    pos_ref[...] = pos_out.astype(jnp.int32)
    carry_ref[...] = carry_ref[...] + jnp.sum(sel, axis=0, keepdims=True)
    cnt_out_ref[...] = carry_ref[...]


def _finish(h, u, va, wm, bias_full, o_b, ga, gb, consts, cnt_in, n_total, tile_offset, prev_outs):
    n_tok = h.shape[0]
    tm = TOKEN_TILE
    n_valid = n_tok // tm
    n_steps = n_valid if prev_outs is not None else n_total // tm - tile_offset
    row = lambda w: pl.BlockSpec((tm, w), lambda i: (jnp.minimum(i, n_valid - 1), 0))
    orow = lambda w: pl.BlockSpec((tm, w), lambda i: (i + tile_offset, 0))
    args = [h, u, va, wm, bias_full, o_b, ga, gb] + list(consts) + [cnt_in]
    specs = [row(D_MODEL), row(A_WIDTH), row(A_WIDTH), _resident(wm.shape), _resident(bias_full.shape),
             row(GROUP_WIDTH), row(D_MODEL), row(D_MODEL)]
    specs += [_resident(t.shape) for t in consts] + [_resident(cnt_in.shape)]
    assert len(args) == _FINISH_INPUTS
    aliases = {}
    if prev_outs is not None:
        for k, t in enumerate(prev_outs):
            aliases[len(args)] = k
            args.append(t)
            specs.append(pl.BlockSpec(memory_space=pl.ANY))
    widths = [(D_MODEL, F32), (D_MODEL // 2, jnp.int32), (TOP_K, jnp.int32), (TOP_K, F32), (TOP_K, jnp.int32)]
    outs = pl.pallas_call(
        functools.partial(_finish_kernel, n_alias=len(aliases), n_valid=n_valid),
        grid=(n_steps,),
        in_specs=specs,
        out_specs=[orow(w) for w, _ in widths] + [_resident(cnt_in.shape)],
        out_shape=[jax.ShapeDtypeStruct((n_total, w), dt) for w, dt in widths]
                  + [jax.ShapeDtypeStruct(cnt_in.shape, F32)],
        scratch_shapes=[pltpu.VMEM(cnt_in.shape, F32)],
        input_output_aliases=aliases,
        compiler_params=_params(),
        name="finish",
    )(*args)
    return outs[:5], outs[5]


def _sc_gather_rows(table, idx):
    m = idx.shape[0]
    width = table.shape[1]
    per_worker = m // SC_WORKERS
    n_chunks = per_worker // SC_ROWS
    mesh = plsc.VectorSubcoreMesh(core_axis_name="c", subcore_axis_name="s",
                                  num_cores=SC_CORES, num_subcores=SC_SUBCORES)

    assert n_chunks % 2 == 0

    @functools.partial(
        pl.kernel, mesh=mesh,
        out_type=jax.ShapeDtypeStruct((m, width), table.dtype),
        scratch_types=[pltpu.VMEM((n_chunks, SC_ROWS), jnp.int32),
                       pltpu.VMEM((SC_ROWS, width), table.dtype),
                       pltpu.VMEM((SC_ROWS, width), table.dtype),
                       pltpu.SemaphoreType.DMA,
                       pltpu.SemaphoreType.DMA],
        name="sc_gather_rows",
    )
    def gather(table_hbm, idx_hbm, out_hbm, idx_v, rows_a, rows_b, sem_a, sem_b):
        wid = lax.axis_index("s") * SC_CORES + lax.axis_index("c")
        base = wid * per_worker
        pltpu.sync_copy(idx_hbm.at[wid], idx_v)

        def fetch(c, rows, sem):
            return pltpu.make_async_copy(table_hbm.at[idx_v.at[c]], rows, sem)

        def put(c, rows):
            off = pl.multiple_of(base + c * SC_ROWS, SC_ROWS)
            pltpu.sync_copy(rows, out_hbm.at[pl.ds(off, SC_ROWS)])

        fetch(0, rows_a, sem_a).start()

        @pl.loop(0, n_chunks, step=2)
        def _(c):
            fetch(c, rows_a, sem_a).wait()
            fetch(c + 1, rows_b, sem_b).start()
            put(c, rows_a)
            fetch(c + 1, rows_b, sem_b).wait()

            @pl.when(c + 2 < n_chunks)
            def _():
                fetch(c + 2, rows_a, sem_a).start()

            put(c + 1, rows_b)

    return gather(table, idx.reshape(SC_WORKERS, n_chunks, SC_ROWS))


def _sc_scatter_rows(src, dest4, n_out):
    n_src, width = src.shape
    top_k = dest4.shape[1]
    per_worker = n_src // SC_WORKERS
    rows = SC_SCATTER_ROWS
    n_chunks = per_worker // rows
    assert per_worker % rows == 0 and n_chunks % 2 == 0
    dest = dest4.reshape(SC_WORKERS, n_chunks, rows, top_k).transpose(0, 1, 3, 2)
    mesh = plsc.VectorSubcoreMesh(core_axis_name="c", subcore_axis_name="s",
                                  num_cores=SC_CORES, num_subcores=SC_SUBCORES)

    @functools.partial(
        pl.kernel, mesh=mesh,
        out_type=jax.ShapeDtypeStruct((n_out, width), src.dtype),
        scratch_types=[pltpu.VMEM((n_chunks, top_k, rows), jnp.int32),
                       pltpu.VMEM((rows, width), src.dtype),
                       pltpu.VMEM((rows, width), src.dtype),
                       pltpu.SemaphoreType.DMA,
                       pltpu.SemaphoreType.DMA,
                       pltpu.SemaphoreType.DMA],
        name="sc_scatter_rows",
    )
    def scatter(src_hbm, dest_hbm, out_hbm, idx_v, rows_a, rows_b, sem_a, sem_b, sem_w):
        wid = lax.axis_index("s") * SC_CORES + lax.axis_index("c")
        base = wid * per_worker
        pltpu.sync_copy(dest_hbm.at[wid], idx_v)

        def load(c, buf, sem):
            off = pl.multiple_of(base + c * rows, rows)
            return pltpu.make_async_copy(src_hbm.at[pl.ds(off, rows)], buf, sem)

        def spread(c, buf):
            copies = [pltpu.make_async_copy(buf, out_hbm.at[idx_v.at[c, k]], sem_w) for k in range(top_k)]
            for cp in copies:
                cp.start()
            for cp in copies:
                cp.wait()

        load(0, rows_a, sem_a).start()

        @pl.loop(0, n_chunks, step=2)
        def _(c):
            load(c, rows_a, sem_a).wait()
            load(c + 1, rows_b, sem_b).start()
            spread(c, rows_a)
            load(c + 1, rows_b, sem_b).wait()

            @pl.when(c + 2 < n_chunks)
            def _():
                load(c + 2, rows_a, sem_a).start()

            spread(c + 1, rows_b)

    return scatter(src, dest)


def _expert_kernel(be_ref, slot_ref, next_ref, nused_ref, x_ref, wgu_hbm, wd_hbm, bg_ref, bl_ref, bd_ref, sel_ref, y_ref,
                   wgu_f, wd_f, wg_s, wl_s, wd_s, sems):
    i = pl.program_id(0)
    active = i < nused_ref[0]
    expert = be_ref[i]
    fresh = (i == 0) | (expert != be_ref[jnp.maximum(i - 1, 0)])

    def fetch(e, slot):
        return (pltpu.make_async_copy(wgu_hbm.at[e], wgu_f.at[slot], sems.at[slot, 0]),
                pltpu.make_async_copy(wd_hbm.at[e], wd_f.at[slot], sems.at[slot, 1]))

    @pl.when(active & fresh)
    def _():
        slot = slot_ref[i]

        @pl.when(i == 0)
        def _():
            for cp in fetch(expert, slot):
                cp.start()

        for cp in fetch(expert, slot):
            cp.wait()
        nxt = next_ref[i]

        @pl.when(nxt >= 0)
        def _():
            for cp in fetch(nxt, 1 - slot):
                cp.start()

        for t in range(D_FF // MXU_DIM):
            src = wgu_f[slot, :, 2 * t * MXU_DIM:2 * (t + 1) * MXU_DIM].astype(BF16)
            both = _dot(src, sel_ref[...]).astype(BF16)
            wg_s[:, t * MXU_DIM:(t + 1) * MXU_DIM] = both[:, :MXU_DIM]
            wl_s[:, t * MXU_DIM:(t + 1) * MXU_DIM] = both[:, MXU_DIM:]
        wd_s[...] = wd_f[slot].astype(BF16)

    @pl.when(active)
    def _():
        x = _unpack_bf16_pairs(x_ref[...]).astype(BF16)
        h_glu = jnp.minimum(_dot(x, wg_s[...]) + bg_ref[...], SWIGLU_LIMIT)
        h_lin = jnp.clip(_dot(x, wl_s[...]) + bl_ref[...], -SWIGLU_LIMIT, SWIGLU_LIMIT)
        act = h_glu * _sigmoid(SWIGLU_ALPHA * h_glu) * (h_lin + 1.0)
        y_ref[...] = _pack_bf16_pairs(_dot(act.astype(BF16), wd_s[...]) + bd_ref[...])

    @pl.when(jnp.logical_not(active))
    def _():
        y_ref[...] = jnp.zeros_like(y_ref)


def _experts(xb, block_expert, block_slot, block_next, n_used, w_gate_up, w_down, b_glu, b_lin, b_down, sel):
    n_slots = xb.shape[0]
    n_blocks = n_slots // EXPERT_BLOCK
    by_expert = lambda k, n: pl.BlockSpec((None, k, n), lambda i, be, sl, nx, nu: (be[i], 0, 0))
    blk = pl.BlockSpec((EXPERT_BLOCK, D_MODEL // 2), lambda i, be, sl, nx, nu: (i, 0))
    hbm = pl.BlockSpec(memory_space=pl.ANY)
    return pl.pallas_call(
        _expert_kernel,
        grid_spec=pltpu.PrefetchScalarGridSpec(
            num_scalar_prefetch=4, grid=(n_blocks,),
            in_specs=[blk, hbm, hbm, by_expert(1, D_FF), by_expert(1, D_FF), by_expert(1, D_MODEL),
                      pl.BlockSpec(sel.shape, lambda i, be, sl, nx, nu: (0, 0), pipeline_mode=pl.Buffered(1))],
            out_specs=blk,
            scratch_shapes=[pltpu.VMEM((2, D_MODEL, 2 * D_FF), F32), pltpu.VMEM((2, D_FF, D_MODEL), F32),
                            pltpu.VMEM((D_MODEL, D_FF), BF16), pltpu.VMEM((D_MODEL, D_FF), BF16),
                            pltpu.VMEM((D_FF, D_MODEL), BF16), pltpu.SemaphoreType.DMA((2, 2))]),
        out_shape=jax.ShapeDtypeStruct((n_slots, D_MODEL // 2), jnp.int32),
        compiler_params=_params(),
        name="experts",
    )(block_expert, block_slot, block_next, n_used, xb, w_gate_up, w_down, b_glu, b_lin, b_down, sel)


def _final_kernel(h1_ref, yg_ref, gate_ref, p_ref, gple_ref, wg_ref, wp_ref, out_ref):
    h2 = h1_ref[...]
    gates = gate_ref[...]
    for k in range(TOP_K):
        h2 = h2 + gates[:, k:k + 1] * _unpack_bf16_pairs(yg_ref[k])
    n3 = ((h2 * _rms(h2)) * gple_ref[...]).astype(BF16)
    gate = _sigmoid(_dot(n3, wg_ref[...]))
    out_ref[...] = h2 + gate * _dot(p_ref[...].astype(BF16), wp_ref[...])


def _final(h1, yg, gates, p, tile_offset, g_ple, w_ple_gate, w_ple_proj):
    n_tok = p.shape[0]
    tm = TOKEN_TILE
    return pl.pallas_call(
        _final_kernel,
        grid=(n_tok // tm,),
        in_specs=[pl.BlockSpec((tm, D_MODEL), lambda i: (i + tile_offset, 0)),
                  pl.BlockSpec((TOP_K, tm, D_MODEL // 2), lambda i: (0, i + tile_offset, 0)),
                  pl.BlockSpec((tm, TOP_K), lambda i: (i + tile_offset, 0)),
                  pl.BlockSpec((tm, PLE_DIM), lambda i: (i, 0)),
                  _resident(g_ple.shape), _resident(w_ple_gate.shape), _resident(w_ple_proj.shape)],
        out_specs=pl.BlockSpec((tm, D_MODEL), lambda i: (i, 0)),
        out_shape=jax.ShapeDtypeStruct((n_tok, D_MODEL), F32),
        compiler_params=_params(),
        name="final",
    )(h1, yg, gates, p, g_ple, w_ple_gate, w_ple_proj)


def _routing_tables(idx4, pos4, counts, n_slots):
    counts = counts.reshape(N_EXPERTS).astype(jnp.int32)
    pcounts = (counts + EXPERT_BLOCK - 1) // EXPERT_BLOCK * EXPERT_BLOCK
    pends = jnp.cumsum(pcounts)
    pstarts = pends - pcounts
    experts = jnp.arange(N_EXPERTS, dtype=jnp.int32)
    start4 = jnp.sum(jnp.where(idx4[:, :, None] == experts, pstarts, 0), axis=-1)
    dest4 = (start4 + pos4).astype(jnp.int32)
    n_blocks = n_slots // EXPERT_BLOCK
    block_start = jnp.arange(n_blocks, dtype=jnp.int32) * EXPERT_BLOCK
    block_expert = jnp.minimum(jnp.sum(block_start[:, None] >= pends[None, :], axis=1), N_EXPERTS - 1).astype(jnp.int32)
    n_used = (pends[-1] // EXPERT_BLOCK).astype(jnp.int32).reshape(1)
    used = counts > 0
    slot_e = (jnp.cumsum(used.astype(jnp.int32)) - 1) & 1
    later_used = used[None, :] & (experts[None, :] > experts[:, None])
    next_e = jnp.min(jnp.where(later_used, experts[None, :], N_EXPERTS), axis=1)
    next_e = jnp.where(next_e == N_EXPERTS, -1, next_e)
    of_block = block_expert[:, None] == experts[None, :]
    block_slot = jnp.sum(jnp.where(of_block, slot_e[None, :], 0), axis=1).astype(jnp.int32)
    block_next = jnp.sum(jnp.where(of_block, next_e[None, :], 0), axis=1).astype(jnp.int32)
    return dest4, block_expert, block_slot, block_next, n_used


def kernel(x_prompt, x_sample, cache_kv_w128, cache_kv_w512, cache_kv_w2048, p_prompt, p_sample, g_mix, w_in, g_v, g_q, g_k, w_spatial, b_spatial, w_branch_a, w_branch_b, w_out, g_moe, w_router, b_router, w_gate_up, b_gate_up, w_down, b_down, g_ple, w_ple_gate, w_ple_proj):
    bsz, seq, _ = x_prompt.shape
    dbatch, t_new, _ = x_sample.shape
    assert g_mix.shape[0] == 1
    caches = (cache_kv_w128, cache_kv_w512, cache_kv_w2048)
    l = 0
    n_p, n_s = bsz * seq, dbatch * t_new
    n_tok = n_p + n_s
    assert n_p % TOKEN_TILE == 0 and n_s % TOKEN_TILE == 0

    row2 = lambda t: t.reshape(1, -1)
    w_in_bf = w_in[l].astype(BF16)
    g_q_t = jnp.tile(g_q[l], B_HEADS).reshape(1, B_WIDTH)
    g_k_t = jnp.tile(g_k[l], B_HEADS).reshape(1, B_WIDTH)
    hid = np.arange(MXU_DIM) // HEAD_DIM
    hsum = jnp.asarray(hid[:, None] == hid[None, :], BF16)
    tril = jnp.tril(jnp.ones((CHUNK, CHUNK), bool))
    wm_prompt = jnp.where(tril[None], w_spatial[l], 0).astype(BF16)
    bias_prompt = jnp.repeat(b_spatial[l].T, A_GROUP_WIDTH, axis=1)
    reps = CHUNK // t_new
    small = jnp.where(tril[None, :t_new, :t_new], w_spatial[l][:, :t_new, :t_new], 0)
    wm_sample = jnp.einsum("ab,gij->gaibj", jnp.eye(reps, dtype=F32), small).reshape(A_GROUPS, CHUNK, CHUNK).astype(BF16)
    bias_sample = jnp.tile(bias_prompt[:t_new], (reps, 1))
    wr_hi = w_router[l].astype(BF16)
    wr_lo = (w_router[l] - wr_hi.astype(F32)).astype(BF16)
    tri = jnp.asarray(np.tril(np.ones((TOKEN_TILE, TOKEN_TILE), np.float32), -1), BF16)
    consts = (w_branch_a[l].astype(BF16), w_branch_b[l].astype(BF16), w_out[l].astype(BF16), row2(g_moe[l]),
              wr_hi, wr_lo, row2(b_router[l]), tri)
    sel_np = np.zeros((2 * MXU_DIM, 2 * MXU_DIM), np.float32)
    sel_np[2 * np.arange(MXU_DIM), np.arange(MXU_DIM)] = 1.0
    sel_np[2 * np.arange(MXU_DIM) + 1, MXU_DIM + np.arange(MXU_DIM)] = 1.0
    sel = jnp.asarray(sel_np, BF16)

    proj = functools.partial(_project, g_mix=row2(g_mix[l]), w_in_bf=w_in_bf, g_v=row2(g_v[l]),
                             g_q_t=g_q_t, g_k_t=g_k_t, hsum=hsum)

    xp = x_prompt.reshape(n_p, D_MODEL)
    u_p, va_p, q_p, k_p, v_p, ga_p, gb_p = proj(xp, va_dtype=BF16)
    seq3 = lambda t: t.reshape(bsz, seq, B_WIDTH)
    attn_p = [_attend_prompt(seq3(q_p), seq3(k_p), seq3(v_p), gi, bsz, seq) for gi in range(N_GROUPS_B - 1)]
    o_p = _attend_prompt(seq3(q_p), seq3(k_p), seq3(v_p), N_GROUPS_B - 1, bsz, seq, others=attn_p)
    zero_counts = jnp.zeros((1, N_EXPERTS), F32)
    outs_p, cnt_p = _finish(xp, u_p, va_p, wm_prompt, bias_prompt, o_p, ga_p, gb_p, consts, zero_counts,
                            n_tok, 0, None)

    xs = x_sample.reshape(n_s, D_MODEL)
    u_s, va_s, q_s, k_s, v_s, ga_s, gb_s = proj(xs, va_dtype=F32)
    tok3 = lambda t: t.reshape(dbatch, t_new, B_WIDTH)
    caches_t = [jnp.transpose(c[l], (0, 2, 3, 4, 1)) for c in caches]
    o_s = _attend_sample(tok3(q_s), tok3(k_s), tok3(v_s), caches_t, dbatch, t_new)
    (h1, n2, idx4, gates, pos4), counts = _finish(xs, u_s, va_s, wm_sample, bias_sample, o_s, ga_s, gb_s,
                                                  consts, cnt_p, n_tok, n_p // TOKEN_TILE, outs_p)

    n_blocks = -(-n_tok * TOP_K // EXPERT_BLOCK) + N_EXPERTS
    n_slots = n_blocks * EXPERT_BLOCK
    dest4, block_expert, block_slot, block_next, n_used = _routing_tables(idx4, pos4, counts, n_slots)
    xb = _sc_scatter_rows(n2, dest4, n_slots)
    yb = _experts(xb, block_expert, block_slot, block_next, n_used, w_gate_up[l], w_down[l],
                  b_gate_up[l][:, None, 0::2], b_gate_up[l][:, None, 1::2], b_down[l][:, None, :], sel)
    yg = _sc_gather_rows(yb, dest4.T.reshape(-1)).reshape(TOP_K, n_tok, D_MODEL // 2)
    fin = functools.partial(_final, g_ple=row2(g_ple[l]), w_ple_gate=w_ple_gate[l].astype(BF16),
                            w_ple_proj=w_ple_proj[l].astype(BF16))
    y_prompt = fin(h1, yg, gates, p_prompt[l].reshape(n_p, PLE_DIM), 0).reshape(bsz, seq, D_MODEL)
    y_sample = fin(h1, yg, gates, p_sample[l].reshape(n_s, PLE_DIM), n_p // TOKEN_TILE).reshape(dbatch, t_new, D_MODEL)

    kv_prompt = []
    for gi in range(N_GROUPS_B):
        t = _kv_tail(seq3(k_p), seq3(v_p), gi, bsz, seq)
        keep = t.shape[-1]
        kv_prompt.append(jnp.transpose(t.reshape(bsz, 2, H_SLOT, HEAD_DIM, keep), (0, 4, 1, 2, 3))[None])
    kvs = _kv_sample(k_s, v_s, dbatch, t_new)
    kv_sample = [jnp.transpose(kvs[gi].reshape(t_new, 2, H_SLOT, HEAD_DIM, dbatch), (4, 0, 1, 2, 3))[None]
                 for gi in range(N_GROUPS_B)]
    va_out = va_s.reshape(1, dbatch, t_new, A_WIDTH)
    return (y_prompt, y_sample, *kv_prompt, *kv_sample, va_out)
```

```python
import functools

import numpy as np
import jax
import jax.numpy as jnp
from jax import lax
from jax.experimental import pallas as pl
from jax.experimental.pallas import tpu as pltpu
from jax.experimental.pallas import tpu_sc as plsc

F32 = jnp.float32
BF16 = jnp.bfloat16

D_MODEL = 1024
A_WIDTH = 1024
A_GROUPS = 4
A_GROUP_WIDTH = A_WIDTH // A_GROUPS
CHUNK = 128
HEAD_DIM = 64
H_SLOT = 8
GROUP_WIDTH = H_SLOT * HEAD_DIM
DILATION_GROUPS = ((128, 1), (512, 4), (2048, 16))
N_GROUPS_B = len(DILATION_GROUPS)
B_HEADS = H_SLOT * N_GROUPS_B
B_WIDTH = B_HEADS * HEAD_DIM
N_EXPERTS = 32
TOP_K = 4
D_FF = 1024
SWIGLU_ALPHA = 1.702
SWIGLU_LIMIT = 7.0
PLE_DIM = 256
RMS_EPS = 1e-6
NEG_BIG = -1e30
LOG2_E = float(np.log2(np.e))
LN_2 = float(np.log(2.0))

LANES = 128
SUBLANES = 8
MXU_DIM = 256
HEADS_PER_VREG = LANES // HEAD_DIM
PAIRS = GROUP_WIDTH // LANES
RESIDUE_UNROLL = 8
ATTN_STEP_ROWS = 1024
TOKEN_TILE = 512
EXPERT_BLOCK = 256
VMEM_LIMIT = 56 * 1024 * 1024

SC_CORES = 2
SC_SUBCORES = 16
SC_WORKERS = SC_CORES * SC_SUBCORES
SC_ROWS = 32
SC_SCATTER_ROWS = 16

_COL_SPLITS = np.cumsum([0, A_WIDTH, A_WIDTH, B_WIDTH, B_WIDTH, B_WIDTH, D_MODEL, D_MODEL]).tolist()


def _alibi_slopes():
    return np.exp2(-8.0 * np.arange(1, B_HEADS + 1, dtype=np.float32) / B_HEADS).astype(np.float32)


def _sigmoid(x):
    return 1.0 / (1.0 + jnp.exp(-x))


def _rms(x):
    return lax.rsqrt(jnp.mean(x * x, axis=-1, keepdims=True) + RMS_EPS)


def _dot(a, b):
    return jnp.dot(a, b, preferred_element_type=F32)


def _dot_nt(a, b):
    return lax.dot_general(a, b, (((1,), (1,)), ((), ())), preferred_element_type=F32)


def _pack_bf16_pairs(x):
    w = x.shape[1] // 2
    lo = lax.bitcast_convert_type(x[:, :w].astype(BF16).astype(F32), jnp.uint32) >> 16
    hi = lax.bitcast_convert_type(x[:, w:].astype(BF16).astype(F32), jnp.uint32) & jnp.uint32(0xFFFF0000)
    return lax.bitcast_convert_type(lo | hi, jnp.int32)


def _unpack_bf16_pairs(p):
    u = lax.bitcast_convert_type(p, jnp.uint32)
    lo = lax.bitcast_convert_type(u << 16, F32)
    hi = lax.bitcast_convert_type(u & jnp.uint32(0xFFFF0000), F32)
    return jnp.concatenate([lo, hi], axis=1)


def _resident(shape):
    nd = len(shape)
    return pl.BlockSpec(shape, lambda *_: (0,) * nd, pipeline_mode=pl.Buffered(1))


def _params(n_axes=1):
    return pltpu.CompilerParams(dimension_semantics=("arbitrary",) * n_axes, vmem_limit_bytes=VMEM_LIMIT)


def _proj_kernel(x_ref, gmix_ref, w_ref, gv_ref, gq_ref, gk_ref, hsum_ref,
                 u_ref, va_ref, q_ref, k_ref, v_ref, ga_ref, gb_ref):
    x = x_ref[...]
    n = ((x * _rms(x)) * gmix_ref[...]).astype(BF16)

    def section(i):
        return _dot(n, w_ref[:, _COL_SPLITS[i]:_COL_SPLITS[i + 1]])

    u_ref[...] = jax.nn.gelu(section(0)).astype(u_ref.dtype)
    va = jax.nn.gelu(section(1))
    va_ref[...] = ((va * _rms(va)) * gv_ref[...]).astype(va_ref.dtype)

    def head_norm(z, g_ref, scale):
        parts = []
        for c in range(B_WIDTH // MXU_DIM):
            zc = z[:, c * MXU_DIM:(c + 1) * MXU_DIM]
            ss = _dot((zc * zc).astype(BF16), hsum_ref[...])
            parts.append(zc * lax.rsqrt(ss * (1.0 / HEAD_DIM) + RMS_EPS))
        return jnp.concatenate(parts, axis=1) * (g_ref[...] * scale)

    q_ref[...] = head_norm(section(2), gq_ref, HEAD_DIM ** -0.5 * LOG2_E)
    k_ref[...] = head_norm(section(3), gk_ref, 1.0)
    v_ref[...] = section(4)
    ga_ref[...] = _sigmoid(section(5)).astype(ga_ref.dtype)
    gb_ref[...] = _sigmoid(section(6)).astype(gb_ref.dtype)


def _project(x, g_mix, w_in_bf, g_v, g_q_t, g_k_t, hsum, va_dtype):
    n_tok = x.shape[0]
    tm = TOKEN_TILE
    row = lambda w: pl.BlockSpec((tm, w), lambda i: (i, 0))
    outs = [(A_WIDTH, BF16), (A_WIDTH, va_dtype), (B_WIDTH, F32), (B_WIDTH, F32), (B_WIDTH, F32),
            (D_MODEL, BF16), (D_MODEL, BF16)]
    return pl.pallas_call(
        _proj_kernel,
        grid=(n_tok // tm,),
        in_specs=[row(D_MODEL), _resident(g_mix.shape), _resident(w_in_bf.shape), _resident(g_v.shape),
                  _resident(g_q_t.shape), _resident(g_k_t.shape), _resident(hsum.shape)],
        out_specs=[row(w) for w, _ in outs],
        out_shape=[jax.ShapeDtypeStruct((n_tok, w), dt) for w, dt in outs],
        compiler_params=_params(),
        name="project",
    )(x, g_mix, w_in_bf, g_v, g_q_t, g_k_t, hsum)


def _band_bias(band, dil, slopes):
    qi = jnp.arange(band, dtype=jnp.int32)[:, None]
    kj = jnp.arange(2 * band, dtype=jnp.int32)[None, :]
    dist = qi + band - kj
    in_band = (dist >= 0) & (dist <= band)
    valid = jnp.stack([in_band & (kj >= band), in_band])
    penalty = (jnp.asarray(slopes, F32) * LOG2_E)[:, None, None] * (dist * dil).astype(F32)[None]
    return jnp.where(valid[:, None], -penalty[None], NEG_BIG)


def _pair_attention(q2, k, v, bias_pair):
    band = q2.shape[0]
    first = lax.broadcasted_iota(jnp.int32, (band, LANES), 1) < HEAD_DIM
    qs = jnp.concatenate([jnp.where(first, q2, 0.0), jnp.where(first, 0.0, q2)], axis=0).astype(BF16)
    s = _dot_nt(qs, k) + bias_pair.reshape(HEADS_PER_VREG * band, 2 * band)
    m = jnp.max(s, axis=-1, keepdims=True)
    e = jnp.exp2(s - m).astype(BF16)
    v_ones = jnp.concatenate([v, jnp.ones_like(v)], axis=1)
    r = _dot(e, v_ones)
    den = r[:, LANES:]
    o2 = r[:, :LANES] / den
    lse = m * LN_2 + jnp.log(den)
    return jnp.where(first, o2[:band], o2[band:]), jnp.where(first, lse[:band], lse[band:])


def _attn_dense_kernel(q_ref, kp_ref, kc_ref, vp_ref, vc_ref, bias0_ref, bias_ref, o_ref, l_ref, *, band, n_sub):
    for i in range(n_sub):
        rows = slice(i * band, (i + 1) * band)
        before = slice((i - 1) * band, i * band)
        for p in range(PAIRS):
            cols = slice(p * LANES, (p + 1) * LANES)
            k_prev = kc_ref[before, cols] if i else kp_ref[:, cols]
            v_prev = vc_ref[before, cols] if i else vp_ref[:, cols]
            k = jnp.concatenate([k_prev, kc_ref[rows, cols]], axis=0).astype(BF16)
            v = jnp.concatenate([v_prev, vc_ref[rows, cols]], axis=0).astype(BF16)
            table = bias_ref if i else bias0_ref
            bias2 = table[p * HEADS_PER_VREG:(p + 1) * HEADS_PER_VREG]
            o_pair, l_pair = _pair_attention(q_ref[rows, cols], k, v, bias2)
            o_ref[rows, cols] = o_pair.astype(o_ref.dtype)
            l_ref[rows, cols] = l_pair


def _attn_dilated_kernel(q_ref, kp_ref, kc_ref, vp_ref, vc_ref, bias0_ref, bias_ref, *rest,
                         band, dil, n_sub, n_others):
    others, rest = rest[:2 * n_others], rest[2 * n_others:]
    if n_others:
        out_ref, o_ref, l_ref = rest
    else:
        o_ref, l_ref = rest
    win = band * dil
    unroll = min(RESIDUE_UNROLL, dil)

    def residues(it, carry):
        for i in range(n_sub):
            for u in range(unroll):
                r = it * unroll + u
                rows = pl.ds(i * win + r, band, stride=dil)
                before = pl.ds((i - 1) * win + r, band, stride=dil) if i else pl.ds(r, band, stride=dil)
                k_prev = kc_ref[before, :] if i else kp_ref[before, :]
                v_prev = vc_ref[before, :] if i else vp_ref[before, :]
                k = jnp.concatenate([k_prev, kc_ref[rows, :]], axis=0).astype(BF16)
                v = jnp.concatenate([v_prev, vc_ref[rows, :]], axis=0).astype(BF16)
                table = bias_ref if i else bias0_ref
                o_pair, l_pair = _pair_attention(q_ref[rows, :], k, v, table[...])
                o_ref[rows, :] = o_pair
                l_ref[rows, :] = l_pair
        return carry

    lax.fori_loop(0, dil // unroll, residues, 0)

    if n_others:
        outs = [others[2 * g][...].astype(F32) for g in range(n_others)] + [o_ref[...]]
        lses = [others[2 * g + 1][...] for g in range(n_others)] + [l_ref[...]]
        mx = functools.reduce(jnp.maximum, lses)
        ws = [jnp.exp(l - mx) for l in lses]
        wsum = functools.reduce(jnp.add, ws)
        out_ref[...] = (sum(w * o for w, o in zip(ws, outs)) / wsum).astype(out_ref.dtype)


def _attend_prompt(q, k, v, gi, bsz, seq, others=()):
    win, dil = DILATION_GROUPS[gi]
    band = win // dil
    bias = _band_bias(band, dil, _alibi_slopes()[gi * H_SLOT:(gi + 1) * H_SLOT])
    n_sub = max(1, min(ATTN_STEP_ROWS // win, seq // win))
    step = n_sub * win
    if dil == 1:
        cur = pl.BlockSpec((None, step, GROUP_WIDTH), lambda b, j: (b, j, gi))
        prev = pl.BlockSpec((None, win, GROUP_WIDTH), lambda b, j: (b, jnp.maximum(j * n_sub - 1, 0), gi))
        tab0 = pl.BlockSpec((None, H_SLOT, band, 2 * band), lambda b, j: (jnp.minimum(j, 1), 0, 0, 0))
        tab = pl.BlockSpec((None, H_SLOT, band, 2 * band), lambda b, j: (1, 0, 0, 0))
        out = pl.BlockSpec((None, step, GROUP_WIDTH), lambda b, j: (b, j, 0))
        o, lse = pl.pallas_call(
            functools.partial(_attn_dense_kernel, band=band, n_sub=n_sub),
            grid=(bsz, seq // step),
            in_specs=[cur, prev, cur, prev, cur, tab0, tab],
            out_specs=[out, out],
            out_shape=[jax.ShapeDtypeStruct((bsz, seq, GROUP_WIDTH), BF16),
                       jax.ShapeDtypeStruct((bsz, seq, GROUP_WIDTH), F32)],
            compiler_params=_params(2),
            name=f"attn_prompt_w{win}",
        )(q, k, k, v, v, bias, bias)
    else:
        col = lambda p: gi * PAIRS + p
        cur = pl.BlockSpec((None, step, LANES), lambda b, j, p: (b, j, col(p)))
        prev = pl.BlockSpec((None, win, LANES), lambda b, j, p: (b, jnp.maximum(j * n_sub - 1, 0), col(p)))
        tab0 = pl.BlockSpec((None, HEADS_PER_VREG, band, 2 * band), lambda b, j, p: (jnp.minimum(j, 1), p, 0, 0))
        tab = pl.BlockSpec((None, HEADS_PER_VREG, band, 2 * band), lambda b, j, p: (1, p, 0, 0))
        out = pl.BlockSpec((None, step, LANES), lambda b, j, p: (b, j, p))
        kern = functools.partial(_attn_dilated_kernel, band=band, dil=dil, n_sub=n_sub, n_others=len(others))
        common = dict(grid=(bsz, seq // step, PAIRS), compiler_params=_params(3), name=f"attn_prompt_w{win}")
        ins = [cur, prev, cur, prev, cur, tab0, tab]
        if others:
            extra = [t.reshape(bsz, seq, GROUP_WIDTH) for pair in others for t in pair]
            combined = pl.pallas_call(
                kern,
                in_specs=ins + [out] * len(extra),
                out_specs=out,
                out_shape=jax.ShapeDtypeStruct((bsz, seq, GROUP_WIDTH), BF16),
                scratch_shapes=[pltpu.VMEM((step, LANES), F32), pltpu.VMEM((step, LANES), F32)],
                **common,
            )(q, k, k, v, v, bias, bias, *extra)
            return combined.reshape(bsz * seq, GROUP_WIDTH)
        o, lse = pl.pallas_call(
            kern,
            in_specs=ins,
            out_specs=[out, out],
            out_shape=[jax.ShapeDtypeStruct((bsz, seq, GROUP_WIDTH), F32)] * 2,
            **common,
        )(q, k, k, v, v, bias, bias)
    return o.reshape(bsz * seq, GROUP_WIDTH), lse.reshape(bsz * seq, GROUP_WIDTH)


def _attn_sample_kernel(q_ref, kn_ref, vn_ref, c0_ref, c1_ref, c2_ref, o_ref, *, t_new, slopes):
    rows = H_SLOT * t_new
    pad_new = 16
    t_shift = t_new.bit_length() - 1
    caches = (c0_ref, c1_ref, c2_ref)
    row_id = lax.broadcasted_iota(jnp.int32, (rows, GROUP_WIDTH), 0)
    col_id = lax.broadcasted_iota(jnp.int32, (rows, GROUP_WIDTH), 1)
    own_head = (row_id >> t_shift) == (col_id >> (HEAD_DIM.bit_length() - 1))
    head_of_row = lax.broadcasted_iota(jnp.int32, (rows, 1), 0) >> t_shift
    zpad = jnp.zeros((pad_new - t_new, GROUP_WIDTH), F32)
    outs, lses = [], []
    for g, (win, dil) in enumerate(DILATION_GROUPS):
        buf_len = caches[g].shape[-1]
        cols = slice(g * GROUP_WIDTH, (g + 1) * GROUP_WIDTH)

        def mask_of(n_keys, first_pos, n_real):
            i_q = lax.broadcasted_iota(jnp.int32, (rows, n_keys), 0) & (t_new - 1)
            key = lax.broadcasted_iota(jnp.int32, (rows, n_keys), 1)
            delta = buf_len + i_q - (first_pos + key)
            ok = (delta >= 0) & ((delta & (dil - 1)) == 0) & (delta <= win) & (key < n_real)
            return ok, delta.astype(F32)

        kt = caches[g][0].reshape(GROUP_WIDTH, buf_len).astype(BF16)
        vt = caches[g][1].reshape(GROUP_WIDTH, buf_len).astype(BF16)
        qblk = jnp.where(own_head, jnp.concatenate([q_ref[:, cols]] * H_SLOT, axis=0), 0.0).astype(BF16)
        kn = jnp.concatenate([kn_ref[:, cols], zpad], axis=0).astype(BF16)
        vn = jnp.concatenate([vn_ref[:, cols], zpad], axis=0).astype(BF16)
        slope_rows = jnp.zeros((rows, 1), F32)
        for h in range(H_SLOT):
            slope_rows = jnp.where(head_of_row == h, float(slopes[g * H_SLOT + h]) * LOG2_E, slope_rows)
        ok_c, delta_c = mask_of(buf_len, 0, buf_len)
        ok_n, delta_n = mask_of(pad_new, buf_len, t_new)
        sc = jnp.where(ok_c, _dot(qblk, kt) - slope_rows * delta_c, NEG_BIG)
        sn = jnp.where(ok_n, _dot_nt(qblk, kn) - slope_rows * delta_n, NEG_BIG)
        m = jnp.maximum(jnp.max(sc, axis=-1, keepdims=True), jnp.max(sn, axis=-1, keepdims=True))
        ec = jnp.exp2(sc - m)
        en = jnp.exp2(sn - m)
        den = jnp.sum(ec, axis=-1, keepdims=True) + jnp.sum(en, axis=-1, keepdims=True)
        outs.append((_dot_nt(ec.astype(BF16), vt) + _dot(en.astype(BF16), vn)) * (1.0 / den))
        lses.append(m * LN_2 + jnp.log(den))
    mx = functools.reduce(jnp.maximum, lses)
    ws = [jnp.exp(l - mx) for l in lses]
    wsum = functools.reduce(jnp.add, ws)
    comb = sum((w / wsum) * o for w, o in zip(ws, outs))
    comb = jnp.where(own_head, comb, 0.0)
    o_tok = sum(comb[h * t_new:(h + 1) * t_new, :] for h in range(H_SLOT))
    o_ref[...] = o_tok.astype(o_ref.dtype)


def _attend_sample(q, k_new, v_new, caches_t, dbatch, t_new):
    assert t_new & (t_new - 1) == 0 and t_new <= 8
    tok = pl.BlockSpec((None, t_new, B_WIDTH), lambda b: (b, 0, 0))
    cache_specs = [pl.BlockSpec((None,) + c.shape[1:], lambda b: (b, 0, 0, 0, 0)) for c in caches_t]
    o = pl.pallas_call(
        functools.partial(_attn_sample_kernel, t_new=t_new, slopes=_alibi_slopes()),
        grid=(dbatch,),
        in_specs=[tok, tok, tok] + cache_specs,
        out_specs=pl.BlockSpec((None, t_new, GROUP_WIDTH), lambda b: (b, 0, 0)),
        out_shape=jax.ShapeDtypeStruct((dbatch, t_new, GROUP_WIDTH), BF16),
        compiler_params=_params(),
        name="attn_sample",
    )(q, k_new, v_new, *caches_t)
    return o.reshape(dbatch * t_new, GROUP_WIDTH)


def _kv_tail_kernel(k_ref, v_ref, o_ref):
    o_ref[0] = k_ref[...].T
    o_ref[1] = v_ref[...].T


def _kv_tail(k, v, gi, bsz, seq):
    keep = min(DILATION_GROUPS[gi][0], seq)
    pb = min(keep, 512)
    first = (seq - keep) // pb
    src = pl.BlockSpec((None, pb, GROUP_WIDTH), lambda b, j: (b, first + j, gi))
    return pl.pallas_call(
        _kv_tail_kernel,
        grid=(bsz, keep // pb),
        in_specs=[src, src],
        out_specs=pl.BlockSpec((None, 2, GROUP_WIDTH, pb), lambda b, j: (b, 0, 0, j)),
        out_shape=jax.ShapeDtypeStruct((bsz, 2, GROUP_WIDTH, keep), F32),
        compiler_params=_params(2),
        name=f"kv_tail_w{DILATION_GROUPS[gi][0]}",
    )(k, v)


def _kv_sample_kernel(*refs, t_new, dbatch):
    in_refs, out_refs = refs[:2 * N_GROUPS_B], refs[2 * N_GROUPS_B:]
    for g, o_ref in enumerate(out_refs):
        k_ref, v_ref = in_refs[2 * g], in_refs[2 * g + 1]
        for t in range(t_new):
            rows = pl.ds(t, dbatch, stride=t_new)
            o_ref[t, 0] = k_ref[rows, :].T
            o_ref[t, 1] = v_ref[rows, :].T


def _kv_sample(k, v, dbatch, t_new):
    src = lambda g: pl.BlockSpec((dbatch * t_new, LANES), lambda s: (0, g * PAIRS + s))
    return pl.pallas_call(
        functools.partial(_kv_sample_kernel, t_new=t_new, dbatch=dbatch),
        grid=(PAIRS,),
        in_specs=[src(g) for g in range(N_GROUPS_B) for _ in range(2)],
        out_specs=[pl.BlockSpec((t_new, 2, LANES, dbatch), lambda s: (0, 0, s, 0))] * N_GROUPS_B,
        out_shape=[jax.ShapeDtypeStruct((t_new, 2, GROUP_WIDTH, dbatch), F32)] * N_GROUPS_B,
        compiler_params=_params(),
        name="kv_sample",
    )(*[t for _ in range(N_GROUPS_B) for t in (k, v)])


def _topk_route(logits):
    tm = logits.shape[0]
    lane = lax.broadcasted_iota(jnp.int32, logits.shape, 1).astype(F32)
    slot = lax.broadcasted_iota(jnp.int32, (tm, TOP_K), 1)
    val_out = jnp.zeros((tm, TOP_K), F32)
    work = logits
    ids, top = [], None
    for r in range(TOP_K):
        mx = jnp.max(work, axis=-1, keepdims=True)
        ix = jnp.min(jnp.where(work == mx, lane, float(N_EXPERTS)), axis=-1, keepdims=True)
        top = mx if top is None else top
        ids.append(ix)
        val_out = jnp.where(slot == r, jnp.exp(mx - top), val_out)
        work = jnp.where(lane == ix, -jnp.inf, work)
    gates = val_out / jnp.sum(val_out, axis=-1, keepdims=True)
    return ids, gates


_FINISH_INPUTS = 17


def _finish_kernel(*refs, n_alias, n_valid):
    ins, outs = refs[:_FINISH_INPUTS], refs[_FINISH_INPUTS + n_alias:]
    step = pl.program_id(0)

    @pl.when(step < n_valid)
    def _():
        _finish_tile(ins, outs)

    @pl.when(step >= n_valid)
    def _():
        for ref in outs[:5]:
            ref[...] = jnp.zeros_like(ref)


def _finish_tile(ins, outs):
    (h_ref, u_ref, va_ref, wm_ref, bias_ref, ob_ref, ga_ref, gb_ref, wa_ref, wb_ref, wo_ref, gmoe_ref,
     wrh_ref, wrl_ref, br_ref, tri_ref, cnt_in_ref) = ins
    h1_ref, n2_ref, idx_ref, gate_ref, pos_ref, cnt_out_ref, carry_ref = outs
    tm = h_ref.shape[0]

    chunks = []
    for c in range(tm // CHUNK):
        rows = slice(c * CHUNK, (c + 1) * CHUNK)
        va_c = va_ref[rows, :].astype(BF16)
        mixed = jnp.concatenate(
            [_dot(wm_ref[g], va_c[:, g * A_GROUP_WIDTH:(g + 1) * A_GROUP_WIDTH]) for g in range(A_GROUPS)], axis=1)
        chunks.append((u_ref[rows, :].astype(F32) * (mixed + bias_ref[...])).astype(BF16))
    branch_a = _dot(jnp.concatenate(chunks, axis=0), wa_ref[...])

    branch_b = _dot(ob_ref[...], wb_ref[...])

    mix = ga_ref[...].astype(F32) * branch_a + gb_ref[...].astype(F32) * branch_b
    h1 = h_ref[...] + _dot(mix.astype(BF16), wo_ref[...])
    h1_ref[...] = h1

    n2 = (h1 * _rms(h1)) * gmoe_ref[...]
    n2_ref[...] = _pack_bf16_pairs(n2)
    n_hi = n2.astype(BF16)
    n_lo = (n2 - n_hi.astype(F32)).astype(BF16)
    logits = _dot(n_hi, wrh_ref[...]) + _dot(n_lo, wrh_ref[...]) + _dot(n_hi, wrl_ref[...]) + br_ref[...]
    ids, gates = _topk_route(logits)
    gate_ref[...] = gates

    @pl.when(pl.program_id(0) == 0)
    def _():
        carry_ref[...] = cnt_in_ref[...]

    lane = lax.broadcasted_iota(jnp.int32, logits.shape, 1).astype(F32)
    slot = lax.broadcasted_iota(jnp.int32, (tm, TOP_K), 1)
    sel = sum(jnp.where(lane == ix, 1.0, 0.0) for ix in ids)
    before = _dot(tri_ref[...], sel.astype(BF16)) + carry_ref[...]
    idx_out = jnp.zeros((tm, TOP_K), F32)
    pos_out = jnp.zeros((tm, TOP_K), F32)
    for r, ix in enumerate(ids):
        idx_out = jnp.where(slot == r, ix, idx_out)
        rank = jnp.sum(jnp.where(lane == ix, before, 0.0), axis=-1, keepdims=True)
        pos_out = jnp.where(slot == r, rank, pos_out)
    idx_ref[...] = idx_out.astype(jnp.int32)
    pos_ref[...] = pos_out.astype(jnp.int32)
    carry_ref[...] = carry_ref[...] + jnp.sum(sel, axis=0, keepdims=True)
    cnt_out_ref[...] = carry_ref[...]


def _finish(h, u, va, wm, bias_full, o_b, ga, gb, consts, cnt_in, n_total, tile_offset, prev_outs):
    n_tok = h.shape[0]
    tm = TOKEN_TILE
    n_valid = n_tok // tm
    n_steps = n_valid if prev_outs is not None else n_total // tm - tile_offset
    row = lambda w: pl.BlockSpec((tm, w), lambda i: (jnp.minimum(i, n_valid - 1), 0))
    orow = lambda w: pl.BlockSpec((tm, w), lambda i: (i + tile_offset, 0))
    args = [h, u, va, wm, bias_full, o_b, ga, gb] + list(consts) + [cnt_in]
    specs = [row(D_MODEL), row(A_WIDTH), row(A_WIDTH), _resident(wm.shape), _resident(bias_full.shape),
             row(GROUP_WIDTH), row(D_MODEL), row(D_MODEL)]
    specs += [_resident(t.shape) for t in consts] + [_resident(cnt_in.shape)]
    assert len(args) == _FINISH_INPUTS
    aliases = {}
    if prev_outs is not None:
        for k, t in enumerate(prev_outs):
            aliases[len(args)] = k
            args.append(t)
            specs.append(pl.BlockSpec(memory_space=pl.ANY))
    widths = [(D_MODEL, F32), (D_MODEL // 2, jnp.int32), (TOP_K, jnp.int32), (TOP_K, F32), (TOP_K, jnp.int32)]
    outs = pl.pallas_call(
        functools.partial(_finish_kernel, n_alias=len(aliases), n_valid=n_valid),
        grid=(n_steps,),
        in_specs=specs,
        out_specs=[orow(w) for w, _ in widths] + [_resident(cnt_in.shape)],
        out_shape=[jax.ShapeDtypeStruct((n_total, w), dt) for w, dt in widths]
                  + [jax.ShapeDtypeStruct(cnt_in.shape, F32)],
        scratch_shapes=[pltpu.VMEM(cnt_in.shape, F32)],
        input_output_aliases=aliases,
        compiler_params=_params(),
        name="finish",
    )(*args)
    return outs[:5], outs[5]


def _sc_gather_rows(table, idx):
    m = idx.shape[0]
    width = table.shape[1]
    per_worker = m // SC_WORKERS
    n_chunks = per_worker // SC_ROWS
    mesh = plsc.VectorSubcoreMesh(core_axis_name="c", subcore_axis_name="s",
                                  num_cores=SC_CORES, num_subcores=SC_SUBCORES)

    assert n_chunks % 2 == 0

    @functools.partial(
        pl.kernel, mesh=mesh,
        out_type=jax.ShapeDtypeStruct((m, width), table.dtype),
        scratch_types=[pltpu.VMEM((n_chunks, SC_ROWS), jnp.int32),
                       pltpu.VMEM((SC_ROWS, width), table.dtype),
                       pltpu.VMEM((SC_ROWS, width), table.dtype),
                       pltpu.SemaphoreType.DMA,
                       pltpu.SemaphoreType.DMA],
        name="sc_gather_rows",
    )
    def gather(table_hbm, idx_hbm, out_hbm, idx_v, rows_a, rows_b, sem_a, sem_b):
        wid = lax.axis_index("s") * SC_CORES + lax.axis_index("c")
        base = wid * per_worker
        pltpu.sync_copy(idx_hbm.at[wid], idx_v)

        def fetch(c, rows, sem):
            return pltpu.make_async_copy(table_hbm.at[idx_v.at[c]], rows, sem)

        def put(c, rows):
            off = pl.multiple_of(base + c * SC_ROWS, SC_ROWS)
            pltpu.sync_copy(rows, out_hbm.at[pl.ds(off, SC_ROWS)])

        fetch(0, rows_a, sem_a).start()

        @pl.loop(0, n_chunks, step=2)
        def _(c):
            fetch(c, rows_a, sem_a).wait()
            fetch(c + 1, rows_b, sem_b).start()
            put(c, rows_a)
            fetch(c + 1, rows_b, sem_b).wait()

            @pl.when(c + 2 < n_chunks)
            def _():
                fetch(c + 2, rows_a, sem_a).start()

            put(c + 1, rows_b)

    return gather(table, idx.reshape(SC_WORKERS, n_chunks, SC_ROWS))


def _sc_scatter_rows(src, dest4, n_out):
    n_src, width = src.shape
    top_k = dest4.shape[1]
    per_worker = n_src // SC_WORKERS
    rows = SC_SCATTER_ROWS
    n_chunks = per_worker // rows
    assert per_worker % rows == 0 and n_chunks % 2 == 0
    dest = dest4.reshape(SC_WORKERS, n_chunks, rows, top_k).transpose(0, 1, 3, 2)
    mesh = plsc.VectorSubcoreMesh(core_axis_name="c", subcore_axis_name="s",
                                  num_cores=SC_CORES, num_subcores=SC_SUBCORES)

    @functools.partial(
        pl.kernel, mesh=mesh,
        out_type=jax.ShapeDtypeStruct((n_out, width), src.dtype),
        scratch_types=[pltpu.VMEM((n_chunks, top_k, rows), jnp.int32),
                       pltpu.VMEM((rows, width), src.dtype),
                       pltpu.VMEM((rows, width), src.dtype),
                       pltpu.SemaphoreType.DMA,
                       pltpu.SemaphoreType.DMA,
                       pltpu.SemaphoreType.DMA],
        name="sc_scatter_rows",
    )
    def scatter(src_hbm, dest_hbm, out_hbm, idx_v, rows_a, rows_b, sem_a, sem_b, sem_w):
        wid = lax.axis_index("s") * SC_CORES + lax.axis_index("c")
        base = wid * per_worker
        pltpu.sync_copy(dest_hbm.at[wid], idx_v)

        def load(c, buf, sem):
            off = pl.multiple_of(base + c * rows, rows)
            return pltpu.make_async_copy(src_hbm.at[pl.ds(off, rows)], buf, sem)

        def spread(c, buf):
            copies = [pltpu.make_async_copy(buf, out_hbm.at[idx_v.at[c, k]], sem_w) for k in range(top_k)]
            for cp in copies:
                cp.start()
            for cp in copies:
                cp.wait()

        load(0, rows_a, sem_a).start()

        @pl.loop(0, n_chunks, step=2)
        def _(c):
            load(c, rows_a, sem_a).wait()
            load(c + 1, rows_b, sem_b).start()
            spread(c, rows_a)
            load(c + 1, rows_b, sem_b).wait()

            @pl.when(c + 2 < n_chunks)
            def _():
                load(c + 2, rows_a, sem_a).start()

            spread(c + 1, rows_b)

    return scatter(src, dest)


def _expert_kernel(be_ref, slot_ref, next_ref, nused_ref, x_ref, wgu_hbm, wd_hbm, bg_ref, bl_ref, bd_ref, sel_ref, y_ref,
                   wgu_f, wd_f, wg_s, wl_s, wd_s, sems):
    i = pl.program_id(0)
    active = i < nused_ref[0]
    expert = be_ref[i]
    fresh = (i == 0) | (expert != be_ref[jnp.maximum(i - 1, 0)])

    def fetch(e, slot):
        return (pltpu.make_async_copy(wgu_hbm.at[e], wgu_f.at[slot], sems.at[slot, 0]),
                pltpu.make_async_copy(wd_hbm.at[e], wd_f.at[slot], sems.at[slot, 1]))

    @pl.when(active & fresh)
    def _():
        slot = slot_ref[i]

        @pl.when(i == 0)
        def _():
            for cp in fetch(expert, slot):
                cp.start()

        for cp in fetch(expert, slot):
            cp.wait()
        nxt = next_ref[i]

        @pl.when(nxt >= 0)
        def _():
            for cp in fetch(nxt, 1 - slot):
                cp.start()

        for t in range(D_FF // MXU_DIM):
            src = wgu_f[slot, :, 2 * t * MXU_DIM:2 * (t + 1) * MXU_DIM].astype(BF16)
            both = _dot(src, sel_ref[...]).astype(BF16)
            wg_s[:, t * MXU_DIM:(t + 1) * MXU_DIM] = both[:, :MXU_DIM]
            wl_s[:, t * MXU_DIM:(t + 1) * MXU_DIM] = both[:, MXU_DIM:]
        wd_s[...] = wd_f[slot].astype(BF16)

    @pl.when(active)
    def _():
        x = _unpack_bf16_pairs(x_ref[...]).astype(BF16)
        h_glu = jnp.minimum(_dot(x, wg_s[...]) + bg_ref[...], SWIGLU_LIMIT)
        h_lin = jnp.clip(_dot(x, wl_s[...]) + bl_ref[...], -SWIGLU_LIMIT, SWIGLU_LIMIT)
        act = h_glu * _sigmoid(SWIGLU_ALPHA * h_glu) * (h_lin + 1.0)
        y_ref[...] = _pack_bf16_pairs(_dot(act.astype(BF16), wd_s[...]) + bd_ref[...])

    @pl.when(jnp.logical_not(active))
    def _():
        y_ref[...] = jnp.zeros_like(y_ref)


def _experts(xb, block_expert, block_slot, block_next, n_used, w_gate_up, w_down, b_glu, b_lin, b_down, sel):
    n_slots = xb.shape[0]
    n_blocks = n_slots // EXPERT_BLOCK
    by_expert = lambda k, n: pl.BlockSpec((None, k, n), lambda i, be, sl, nx, nu: (be[i], 0, 0))
    blk = pl.BlockSpec((EXPERT_BLOCK, D_MODEL // 2), lambda i, be, sl, nx, nu: (i, 0))
    hbm = pl.BlockSpec(memory_space=pl.ANY)
    return pl.pallas_call(
        _expert_kernel,
        grid_spec=pltpu.PrefetchScalarGridSpec(
            num_scalar_prefetch=4, grid=(n_blocks,),
            in_specs=[blk, hbm, hbm, by_expert(1, D_FF), by_expert(1, D_FF), by_expert(1, D_MODEL),
                      pl.BlockSpec(sel.shape, lambda i, be, sl, nx, nu: (0, 0), pipeline_mode=pl.Buffered(1))],
            out_specs=blk,
            scratch_shapes=[pltpu.VMEM((2, D_MODEL, 2 * D_FF), F32), pltpu.VMEM((2, D_FF, D_MODEL), F32),
                            pltpu.VMEM((D_MODEL, D_FF), BF16), pltpu.VMEM((D_MODEL, D_FF), BF16),
                            pltpu.VMEM((D_FF, D_MODEL), BF16), pltpu.SemaphoreType.DMA((2, 2))]),
        out_shape=jax.ShapeDtypeStruct((n_slots, D_MODEL // 2), jnp.int32),
        compiler_params=_params(),
        name="experts",
    )(block_expert, block_slot, block_next, n_used, xb, w_gate_up, w_down, b_glu, b_lin, b_down, sel)


def _final_kernel(h1_ref, yg_ref, gate_ref, p_ref, gple_ref, wg_ref, wp_ref, out_ref):
    h2 = h1_ref[...]
    gates = gate_ref[...]
    for k in range(TOP_K):
        h2 = h2 + gates[:, k:k + 1] * _unpack_bf16_pairs(yg_ref[k])
    n3 = ((h2 * _rms(h2)) * gple_ref[...]).astype(BF16)
    gate = _sigmoid(_dot(n3, wg_ref[...]))
    out_ref[...] = h2 + gate * _dot(p_ref[...].astype(BF16), wp_ref[...])


def _final(h1, yg, gates, p, tile_offset, g_ple, w_ple_gate, w_ple_proj):
    n_tok = p.shape[0]
    tm = TOKEN_TILE
    return pl.pallas_call(
        _final_kernel,
        grid=(n_tok // tm,),
        in_specs=[pl.BlockSpec((tm, D_MODEL), lambda i: (i + tile_offset, 0)),
                  pl.BlockSpec((TOP_K, tm, D_MODEL // 2), lambda i: (0, i + tile_offset, 0)),
                  pl.BlockSpec((tm, TOP_K), lambda i: (i + tile_offset, 0)),
                  pl.BlockSpec((tm, PLE_DIM), lambda i: (i, 0)),
                  _resident(g_ple.shape), _resident(w_ple_gate.shape), _resident(w_ple_proj.shape)],
        out_specs=pl.BlockSpec((tm, D_MODEL), lambda i: (i, 0)),
        out_shape=jax.ShapeDtypeStruct((n_tok, D_MODEL), F32),
        compiler_params=_params(),
        name="final",
    )(h1, yg, gates, p, g_ple, w_ple_gate, w_ple_proj)


def _routing_tables(idx4, pos4, counts, n_slots):
    counts = counts.reshape(N_EXPERTS).astype(jnp.int32)
    pcounts = (counts + EXPERT_BLOCK - 1) // EXPERT_BLOCK * EXPERT_BLOCK
    pends = jnp.cumsum(pcounts)
    pstarts = pends - pcounts
    experts = jnp.arange(N_EXPERTS, dtype=jnp.int32)
    start4 = jnp.sum(jnp.where(idx4[:, :, None] == experts, pstarts, 0), axis=-1)
    dest4 = (start4 + pos4).astype(jnp.int32)
    n_blocks = n_slots // EXPERT_BLOCK
    block_start = jnp.arange(n_blocks, dtype=jnp.int32) * EXPERT_BLOCK
    block_expert = jnp.minimum(jnp.sum(block_start[:, None] >= pends[None, :], axis=1), N_EXPERTS - 1).astype(jnp.int32)
    n_used = (pends[-1] // EXPERT_BLOCK).astype(jnp.int32).reshape(1)
    used = counts > 0
    slot_e = (jnp.cumsum(used.astype(jnp.int32)) - 1) & 1
    later_used = used[None, :] & (experts[None, :] > experts[:, None])
    next_e = jnp.min(jnp.where(later_used, experts[None, :], N_EXPERTS), axis=1)
    next_e = jnp.where(next_e == N_EXPERTS, -1, next_e)
    of_block = block_expert[:, None] == experts[None, :]
    block_slot = jnp.sum(jnp.where(of_block, slot_e[None, :], 0), axis=1).astype(jnp.int32)
    block_next = jnp.sum(jnp.where(of_block, next_e[None, :], 0), axis=1).astype(jnp.int32)
    return dest4, block_expert, block_slot, block_next, n_used


def kernel(x_prompt, x_sample, cache_kv_w128, cache_kv_w512, cache_kv_w2048, p_prompt, p_sample, g_mix, w_in, g_v, g_q, g_k, w_spatial, b_spatial, w_branch_a, w_branch_b, w_out, g_moe, w_router, b_router, w_gate_up, b_gate_up, w_down, b_down, g_ple, w_ple_gate, w_ple_proj):
    bsz, seq, _ = x_prompt.shape
    dbatch, t_new, _ = x_sample.shape
    assert g_mix.shape[0] == 1
    caches = (cache_kv_w128, cache_kv_w512, cache_kv_w2048)
    l = 0
    n_p, n_s = bsz * seq, dbatch * t_new
    n_tok = n_p + n_s
    assert n_p % TOKEN_TILE == 0 and n_s % TOKEN_TILE == 0

    row2 = lambda t: t.reshape(1, -1)
    w_in_bf = w_in[l].astype(BF16)
    g_q_t = jnp.tile(g_q[l], B_HEADS).reshape(1, B_WIDTH)
    g_k_t = jnp.tile(g_k[l], B_HEADS).reshape(1, B_WIDTH)
    hid = np.arange(MXU_DIM) // HEAD_DIM
    hsum = jnp.asarray(hid[:, None] == hid[None, :], BF16)
    tril = jnp.tril(jnp.ones((CHUNK, CHUNK), bool))
    wm_prompt = jnp.where(tril[None], w_spatial[l], 0).astype(BF16)
    bias_prompt = jnp.repeat(b_spatial[l].T, A_GROUP_WIDTH, axis=1)
    reps = CHUNK // t_new
    small = jnp.where(tril[None, :t_new, :t_new], w_spatial[l][:, :t_new, :t_new], 0)
    wm_sample = jnp.einsum("ab,gij->gaibj", jnp.eye(reps, dtype=F32), small).reshape(A_GROUPS, CHUNK, CHUNK).astype(BF16)
    bias_sample = jnp.tile(bias_prompt[:t_new], (reps, 1))
    wr_hi = w_router[l].astype(BF16)
    wr_lo = (w_router[l] - wr_hi.astype(F32)).astype(BF16)
    tri = jnp.asarray(np.tril(np.ones((TOKEN_TILE, TOKEN_TILE), np.float32), -1), BF16)
    consts = (w_branch_a[l].astype(BF16), w_branch_b[l].astype(BF16), w_out[l].astype(BF16), row2(g_moe[l]),
              wr_hi, wr_lo, row2(b_router[l]), tri)
    sel_np = np.zeros((2 * MXU_DIM, 2 * MXU_DIM), np.float32)
    sel_np[2 * np.arange(MXU_DIM), np.arange(MXU_DIM)] = 1.0
    sel_np[2 * np.arange(MXU_DIM) + 1, MXU_DIM + np.arange(MXU_DIM)] = 1.0
    sel = jnp.asarray(sel_np, BF16)

    proj = functools.partial(_project, g_mix=row2(g_mix[l]), w_in_bf=w_in_bf, g_v=row2(g_v[l]),
                             g_q_t=g_q_t, g_k_t=g_k_t, hsum=hsum)

    xp = x_prompt.reshape(n_p, D_MODEL)
    u_p, va_p, q_p, k_p, v_p, ga_p, gb_p = proj(xp, va_dtype=BF16)
    seq3 = lambda t: t.reshape(bsz, seq, B_WIDTH)
    attn_p = [_attend_prompt(seq3(q_p), seq3(k_p), seq3(v_p), gi, bsz, seq) for gi in range(N_GROUPS_B - 1)]
    o_p = _attend_prompt(seq3(q_p), seq3(k_p), seq3(v_p), N_GROUPS_B - 1, bsz, seq, others=attn_p)
    zero_counts = jnp.zeros((1, N_EXPERTS), F32)
    outs_p, cnt_p = _finish(xp, u_p, va_p, wm_prompt, bias_prompt, o_p, ga_p, gb_p, consts, zero_counts,
                            n_tok, 0, None)

    xs = x_sample.reshape(n_s, D_MODEL)
    u_s, va_s, q_s, k_s, v_s, ga_s, gb_s = proj(xs, va_dtype=F32)
    tok3 = lambda t: t.reshape(dbatch, t_new, B_WIDTH)
    caches_t = [jnp.transpose(c[l], (0, 2, 3, 4, 1)) for c in caches]
    o_s = _attend_sample(tok3(q_s), tok3(k_s), tok3(v_s), caches_t, dbatch, t_new)
    (h1, n2, idx4, gates, pos4), counts = _finish(xs, u_s, va_s, wm_sample, bias_sample, o_s, ga_s, gb_s,
                                                  consts, cnt_p, n_tok, n_p // TOKEN_TILE, outs_p)

    n_blocks = -(-n_tok * TOP_K // EXPERT_BLOCK) + N_EXPERTS
    n_slots = n_blocks * EXPERT_BLOCK
    dest4, block_expert, block_slot, block_next, n_used = _routing_tables(idx4, pos4, counts, n_slots)
    xb = _sc_scatter_rows(n2, dest4, n_slots)
    yb = _experts(xb, block_expert, block_slot, block_next, n_used, w_gate_up[l], w_down[l],
                  b_gate_up[l][:, None, 0::2], b_gate_up[l][:, None, 1::2], b_down[l][:, None, :], sel)
    yg = _sc_gather_rows(yb, dest4.T.reshape(-1)).reshape(TOP_K, n_tok, D_MODEL // 2)
    fin = functools.partial(_final, g_ple=row2(g_ple[l]), w_ple_gate=w_ple_gate[l].astype(BF16),
                            w_ple_proj=w_ple_proj[l].astype(BF16))
    y_prompt = fin(h1, yg, gates, p_prompt[l].reshape(n_p, PLE_DIM), 0).reshape(bsz, seq, D_MODEL)
    y_sample = fin(h1, yg, gates, p_sample[l].reshape(n_s, PLE_DIM), n_p // TOKEN_TILE).reshape(dbatch, t_new, D_MODEL)

    kv_prompt = []
    for gi in range(N_GROUPS_B):
        t = _kv_tail(seq3(k_p), seq3(v_p), gi, bsz, seq)
        keep = t.shape[-1]
        kv_prompt.append(jnp.transpose(t.reshape(bsz, 2, H_SLOT, HEAD_DIM, keep), (0, 4, 1, 2, 3))[None])
    kvs = _kv_sample(k_s, v_s, dbatch, t_new)
    kv_sample = [jnp.transpose(kvs[gi].reshape(t_new, 2, H_SLOT, HEAD_DIM, dbatch), (4, 0, 1, 2, 3))[None]
                 for gi in range(N_GROUPS_B)]
    va_out = va_s.reshape(1, dbatch, t_new, A_WIDTH)
    return (y_prompt, y_sample, *kv_prompt, *kv_sample, va_out)
```

```python
import functools

import numpy as np
import jax
import jax.numpy as jnp
from jax import lax
from jax.experimental import pallas as pl
from jax.experimental.pallas import tpu as pltpu
from jax.experimental.pallas import tpu_sc as plsc

F32 = jnp.float32
BF16 = jnp.bfloat16

D_MODEL = 1024
A_WIDTH = 1024
A_GROUPS = 4
A_GROUP_WIDTH = A_WIDTH // A_GROUPS
CHUNK = 128
HEAD_DIM = 64
H_SLOT = 8
GROUP_WIDTH = H_SLOT * HEAD_DIM
DILATION_GROUPS = ((128, 1), (512, 4), (2048, 16))
N_GROUPS_B = len(DILATION_GROUPS)
B_HEADS = H_SLOT * N_GROUPS_B
B_WIDTH = B_HEADS * HEAD_DIM
N_EXPERTS = 32
TOP_K = 4
D_FF = 1024
SWIGLU_ALPHA = 1.702
SWIGLU_LIMIT = 7.0
PLE_DIM = 256
RMS_EPS = 1e-6
NEG_BIG = -1e30
LOG2_E = float(np.log2(np.e))
LN_2 = float(np.log(2.0))

LANES = 128
SUBLANES = 8
MXU_DIM = 256
HEADS_PER_VREG = LANES // HEAD_DIM
PAIRS = GROUP_WIDTH // LANES
RESIDUE_UNROLL = 8
SAMPLE_SEQS_PER_STEP = 2
ATTN_STEP_ROWS = 1024
TOKEN_TILE = 512
EXPERT_BLOCK = 256
VMEM_LIMIT = 56 * 1024 * 1024

SC_CORES = 2
SC_SUBCORES = 16
SC_WORKERS = SC_CORES * SC_SUBCORES
SC_ROWS = 32
SC_SCATTER_ROWS = 16

_COL_SPLITS = np.cumsum([0, A_WIDTH, A_WIDTH, B_WIDTH, B_WIDTH, B_WIDTH, D_MODEL, D_MODEL]).tolist()


def _alibi_slopes():
    return np.exp2(-8.0 * np.arange(1, B_HEADS + 1, dtype=np.float32) / B_HEADS).astype(np.float32)


def _sigmoid(x):
    return 1.0 / (1.0 + jnp.exp(-x))


def _rms(x):
    return lax.rsqrt(jnp.mean(x * x, axis=-1, keepdims=True) + RMS_EPS)


def _dot(a, b):
    return jnp.dot(a, b, preferred_element_type=F32)


def _dot_nt(a, b):
    return lax.dot_general(a, b, (((1,), (1,)), ((), ())), preferred_element_type=F32)


def _pack_bf16_pairs(x):
    w = x.shape[1] // 2
    lo = lax.bitcast_convert_type(x[:, :w].astype(BF16).astype(F32), jnp.uint32) >> 16
    hi = lax.bitcast_convert_type(x[:, w:].astype(BF16).astype(F32), jnp.uint32) & jnp.uint32(0xFFFF0000)
    return lax.bitcast_convert_type(lo | hi, jnp.int32)


def _unpack_bf16_pairs(p):
    u = lax.bitcast_convert_type(p, jnp.uint32)
    lo = lax.bitcast_convert_type(u << 16, F32)
    hi = lax.bitcast_convert_type(u & jnp.uint32(0xFFFF0000), F32)
    return jnp.concatenate([lo, hi], axis=1)


def _resident(shape):
    nd = len(shape)
    return pl.BlockSpec(shape, lambda *_: (0,) * nd, pipeline_mode=pl.Buffered(1))


def _params(n_axes=1):
    return pltpu.CompilerParams(dimension_semantics=("arbitrary",) * n_axes, vmem_limit_bytes=VMEM_LIMIT)


def _proj_kernel(x_ref, gmix_ref, w_ref, gv_ref, gq_ref, gk_ref, hsum_ref,
                 u_ref, va_ref, q_ref, k_ref, v_ref, ga_ref, gb_ref):
    x = x_ref[...]
    n = ((x * _rms(x)) * gmix_ref[...]).astype(BF16)

    def section(i):
        return _dot(n, w_ref[:, _COL_SPLITS[i]:_COL_SPLITS[i + 1]])

    u_ref[...] = jax.nn.gelu(section(0)).astype(u_ref.dtype)
    va = jax.nn.gelu(section(1))
    va_ref[...] = ((va * _rms(va)) * gv_ref[...]).astype(va_ref.dtype)

    def head_norm(z, g_ref, scale):
        parts = []
        for c in range(B_WIDTH // MXU_DIM):
            zc = z[:, c * MXU_DIM:(c + 1) * MXU_DIM]
            ss = _dot((zc * zc).astype(BF16), hsum_ref[...])
            parts.append(zc * lax.rsqrt(ss * (1.0 / HEAD_DIM) + RMS_EPS))
        return jnp.concatenate(parts, axis=1) * (g_ref[...] * scale)

    q_ref[...] = head_norm(section(2), gq_ref, HEAD_DIM ** -0.5 * LOG2_E)
    k_ref[...] = head_norm(section(3), gk_ref, 1.0)
    v_ref[...] = section(4)
    ga_ref[...] = _sigmoid(section(5)).astype(ga_ref.dtype)
    gb_ref[...] = _sigmoid(section(6)).astype(gb_ref.dtype)


def _project(x, g_mix, w_in_bf, g_v, g_q_t, g_k_t, hsum, va_dtype):
    n_tok = x.shape[0]
    tm = TOKEN_TILE
    row = lambda w: pl.BlockSpec((tm, w), lambda i: (i, 0))
    outs = [(A_WIDTH, BF16), (A_WIDTH, va_dtype), (B_WIDTH, F32), (B_WIDTH, F32), (B_WIDTH, F32),
            (D_MODEL, BF16), (D_MODEL, BF16)]
    return pl.pallas_call(
        _proj_kernel,
        grid=(n_tok // tm,),
        in_specs=[row(D_MODEL), _resident(g_mix.shape), _resident(w_in_bf.shape), _resident(g_v.shape),
                  _resident(g_q_t.shape), _resident(g_k_t.shape), _resident(hsum.shape)],
        out_specs=[row(w) for w, _ in outs],
        out_shape=[jax.ShapeDtypeStruct((n_tok, w), dt) for w, dt in outs],
        compiler_params=_params(),
        name="project",
    )(x, g_mix, w_in_bf, g_v, g_q_t, g_k_t, hsum)


def _band_bias(band, dil, slopes):
    qi = jnp.arange(band, dtype=jnp.int32)[:, None]
    kj = jnp.arange(2 * band, dtype=jnp.int32)[None, :]
    dist = qi + band - kj
    in_band = (dist >= 0) & (dist <= band)
    valid = jnp.stack([in_band & (kj >= band), in_band])
    penalty = (jnp.asarray(slopes, F32) * LOG2_E)[:, None, None] * (dist * dil).astype(F32)[None]
    return jnp.where(valid[:, None], -penalty[None], NEG_BIG)


def _pair_attention(q2, k, v, bias_pair):
    band = q2.shape[0]
    first = lax.broadcasted_iota(jnp.int32, (band, LANES), 1) < HEAD_DIM
    qs = jnp.concatenate([jnp.where(first, q2, 0.0), jnp.where(first, 0.0, q2)], axis=0).astype(BF16)
    s = _dot_nt(qs, k) + bias_pair.reshape(HEADS_PER_VREG * band, 2 * band)
    m = jnp.max(s, axis=-1, keepdims=True)
    e = jnp.exp2(s - m).astype(BF16)
    v_ones = jnp.concatenate([v, jnp.ones_like(v)], axis=1)
    r = _dot(e, v_ones)
    den = r[:, LANES:]
    o2 = r[:, :LANES] / den
    lse = m * LN_2 + jnp.log(den)
    return jnp.where(first, o2[:band], o2[band:]), jnp.where(first, lse[:band], lse[band:])


def _attn_dense_kernel(q_ref, kp_ref, kc_ref, vp_ref, vc_ref, bias0_ref, bias_ref, o_ref, l_ref, *, band, n_sub):
    for i in range(n_sub):
        rows = slice(i * band, (i + 1) * band)
        before = slice((i - 1) * band, i * band)
        for p in range(PAIRS):
            cols = slice(p * LANES, (p + 1) * LANES)
            k_prev = kc_ref[before, cols] if i else kp_ref[:, cols]
            v_prev = vc_ref[before, cols] if i else vp_ref[:, cols]
            k = jnp.concatenate([k_prev, kc_ref[rows, cols]], axis=0).astype(BF16)
            v = jnp.concatenate([v_prev, vc_ref[rows, cols]], axis=0).astype(BF16)
            table = bias_ref if i else bias0_ref
            bias2 = table[p * HEADS_PER_VREG:(p + 1) * HEADS_PER_VREG]
            o_pair, l_pair = _pair_attention(q_ref[rows, cols], k, v, bias2)
            o_ref[rows, cols] = o_pair.astype(o_ref.dtype)
            l_ref[rows, cols] = l_pair


def _attn_dilated_kernel(q_ref, kp_ref, kc_ref, vp_ref, vc_ref, bias0_ref, bias_ref, *rest,
                         band, dil, n_sub, n_others):
    others, rest = rest[:2 * n_others], rest[2 * n_others:]
    if n_others:
        out_ref, o_ref, l_ref = rest
    else:
        o_ref, l_ref = rest
    win = band * dil
    unroll = min(RESIDUE_UNROLL, dil)

    def residues(it, carry):
        for i in range(n_sub):
            for u in range(unroll):
                r = it * unroll + u
                rows = pl.ds(i * win + r, band, stride=dil)
                before = pl.ds((i - 1) * win + r, band, stride=dil) if i else pl.ds(r, band, stride=dil)
                k_prev = kc_ref[before, :] if i else kp_ref[before, :]
                v_prev = vc_ref[before, :] if i else vp_ref[before, :]
                k = jnp.concatenate([k_prev, kc_ref[rows, :]], axis=0).astype(BF16)
                v = jnp.concatenate([v_prev, vc_ref[rows, :]], axis=0).astype(BF16)
                table = bias_ref if i else bias0_ref
                o_pair, l_pair = _pair_attention(q_ref[rows, :], k, v, table[...])
                o_ref[rows, :] = o_pair
                l_ref[rows, :] = l_pair
        return carry

    lax.fori_loop(0, dil // unroll, residues, 0)

    if n_others:
        outs = [others[2 * g][...].astype(F32) for g in range(n_others)] + [o_ref[...]]
        lses = [others[2 * g + 1][...] for g in range(n_others)] + [l_ref[...]]
        mx = functools.reduce(jnp.maximum, lses)
        ws = [jnp.exp(l - mx) for l in lses]
        wsum = functools.reduce(jnp.add, ws)
        out_ref[...] = (sum(w * o for w, o in zip(ws, outs)) / wsum).astype(out_ref.dtype)


def _attend_prompt(q, k, v, gi, bsz, seq, others=()):
    win, dil = DILATION_GROUPS[gi]
    band = win // dil
    bias = _band_bias(band, dil, _alibi_slopes()[gi * H_SLOT:(gi + 1) * H_SLOT])
    step_rows = ATTN_STEP_ROWS if dil == 1 else 2 * ATTN_STEP_ROWS
    n_sub = max(1, min(step_rows // win, seq // win))
    step = n_sub * win
    if dil == 1:
        cur = pl.BlockSpec((None, step, GROUP_WIDTH), lambda b, j: (b, j, gi))
        prev = pl.BlockSpec((None, win, GROUP_WIDTH), lambda b, j: (b, jnp.maximum(j * n_sub - 1, 0), gi))
        tab0 = pl.BlockSpec((None, H_SLOT, band, 2 * band), lambda b, j: (jnp.minimum(j, 1), 0, 0, 0))
        tab = pl.BlockSpec((None, H_SLOT, band, 2 * band), lambda b, j: (1, 0, 0, 0))
        out = pl.BlockSpec((None, step, GROUP_WIDTH), lambda b, j: (b, j, 0))
        o, lse = pl.pallas_call(
            functools.partial(_attn_dense_kernel, band=band, n_sub=n_sub),
            grid=(bsz, seq // step),
            in_specs=[cur, prev, cur, prev, cur, tab0, tab],
            out_specs=[out, out],
            out_shape=[jax.ShapeDtypeStruct((bsz, seq, GROUP_WIDTH), BF16),
                       jax.ShapeDtypeStruct((bsz, seq, GROUP_WIDTH), F32)],
            compiler_params=_params(2),
            name=f"attn_prompt_w{win}",
        )(q, k, k, v, v, bias, bias)
    else:
        col = lambda p: gi * PAIRS + p
        cur = pl.BlockSpec((None, step, LANES), lambda b, j, p: (b, j, col(p)))
        prev = pl.BlockSpec((None, win, LANES), lambda b, j, p: (b, jnp.maximum(j * n_sub - 1, 0), col(p)))
        tab0 = pl.BlockSpec((None, HEADS_PER_VREG, band, 2 * band), lambda b, j, p: (jnp.minimum(j, 1), p, 0, 0))
        tab = pl.BlockSpec((None, HEADS_PER_VREG, band, 2 * band), lambda b, j, p: (1, p, 0, 0))
        out = pl.BlockSpec((None, step, LANES), lambda b, j, p: (b, j, p))
        kern = functools.partial(_attn_dilated_kernel, band=band, dil=dil, n_sub=n_sub, n_others=len(others))
        common = dict(grid=(bsz, seq // step, PAIRS), compiler_params=_params(3), name=f"attn_prompt_w{win}")
        ins = [cur, prev, cur, prev, cur, tab0, tab]
        if others:
            extra = [t.reshape(bsz, seq, GROUP_WIDTH) for pair in others for t in pair]
            combined = pl.pallas_call(
                kern,
                in_specs=ins + [out] * len(extra),
                out_specs=out,
                out_shape=jax.ShapeDtypeStruct((bsz, seq, GROUP_WIDTH), BF16),
                scratch_shapes=[pltpu.VMEM((step, LANES), F32), pltpu.VMEM((step, LANES), F32)],
                **common,
            )(q, k, k, v, v, bias, bias, *extra)
            return combined.reshape(bsz * seq, GROUP_WIDTH)
        o, lse = pl.pallas_call(
            kern,
            in_specs=ins,
            out_specs=[out, out],
            out_shape=[jax.ShapeDtypeStruct((bsz, seq, GROUP_WIDTH), F32)] * 2,
            **common,
        )(q, k, k, v, v, bias, bias)
    return o.reshape(bsz * seq, GROUP_WIDTH), lse.reshape(bsz * seq, GROUP_WIDTH)


def _attn_sample_kernel(q_ref, kn_ref, vn_ref, c0_ref, c1_ref, c2_ref, o_ref, *, t_new, slopes):
    for b in range(q_ref.shape[0]):
        _attn_sample_one(q_ref.at[b], kn_ref.at[b], vn_ref.at[b], c0_ref.at[b], c1_ref.at[b], c2_ref.at[b],
                         o_ref.at[b], t_new=t_new, slopes=slopes)


def _attn_sample_one(q_ref, kn_ref, vn_ref, c0_ref, c1_ref, c2_ref, o_ref, *, t_new, slopes):
    rows = H_SLOT * t_new
    pad_new = 16
    t_shift = t_new.bit_length() - 1
    caches = (c0_ref, c1_ref, c2_ref)
    row_id = lax.broadcasted_iota(jnp.int32, (rows, GROUP_WIDTH), 0)
    col_id = lax.broadcasted_iota(jnp.int32, (rows, GROUP_WIDTH), 1)
    own_head = (row_id >> t_shift) == (col_id >> (HEAD_DIM.bit_length() - 1))
    head_of_row = lax.broadcasted_iota(jnp.int32, (rows, 1), 0) >> t_shift
    zpad = jnp.zeros((pad_new - t_new, GROUP_WIDTH), F32)
    outs, lses = [], []
    for g, (win, dil) in enumerate(DILATION_GROUPS):
        buf_len = caches[g].shape[-1]
        cols = slice(g * GROUP_WIDTH, (g + 1) * GROUP_WIDTH)

        def mask_of(n_keys, first_pos, n_real):
            i_q = lax.broadcasted_iota(jnp.int32, (rows, n_keys), 0) & (t_new - 1)
            key = lax.broadcasted_iota(jnp.int32, (rows, n_keys), 1)
            delta = buf_len + i_q - (first_pos + key)
            ok = (delta >= 0) & ((delta & (dil - 1)) == 0) & (delta <= win) & (key < n_real)
            return ok, delta.astype(F32)

        kt = caches[g][0].reshape(GROUP_WIDTH, buf_len).astype(BF16)
        vt = caches[g][1].reshape(GROUP_WIDTH, buf_len).astype(BF16)
        qblk = jnp.where(own_head, jnp.concatenate([q_ref[:, cols]] * H_SLOT, axis=0), 0.0).astype(BF16)
        kn = jnp.concatenate([kn_ref[:, cols], zpad], axis=0).astype(BF16)
        vn = jnp.concatenate([vn_ref[:, cols], zpad], axis=0).astype(BF16)
        slope_rows = jnp.zeros((rows, 1), F32)
        for h in range(H_SLOT):
            slope_rows = jnp.where(head_of_row == h, float(slopes[g * H_SLOT + h]) * LOG2_E, slope_rows)
        ok_c, delta_c = mask_of(buf_len, 0, buf_len)
        ok_n, delta_n = mask_of(pad_new, buf_len, t_new)
        sc = jnp.where(ok_c, _dot(qblk, kt) - slope_rows * delta_c, NEG_BIG)
        sn = jnp.where(ok_n, _dot_nt(qblk, kn) - slope_rows * delta_n, NEG_BIG)
        m = jnp.maximum(jnp.max(sc, axis=-1, keepdims=True), jnp.max(sn, axis=-1, keepdims=True))
        ec = jnp.exp2(sc - m)
        en = jnp.exp2(sn - m)
        den = jnp.sum(ec, axis=-1, keepdims=True) + jnp.sum(en, axis=-1, keepdims=True)
        outs.append((_dot_nt(ec.astype(BF16), vt) + _dot(en.astype(BF16), vn)) * (1.0 / den))
        lses.append(m * LN_2 + jnp.log(den))
    mx = functools.reduce(jnp.maximum, lses)
    ws = [jnp.exp(l - mx) for l in lses]
    wsum = functools.reduce(jnp.add, ws)
    comb = sum((w / wsum) * o for w, o in zip(ws, outs))
    comb = jnp.where(own_head, comb, 0.0)
    o_tok = sum(comb[h * t_new:(h + 1) * t_new, :] for h in range(H_SLOT))
    o_ref[...] = o_tok.astype(o_ref.dtype)


def _attend_sample(q, k_new, v_new, caches_t, dbatch, t_new):
    assert t_new & (t_new - 1) == 0 and t_new <= 8
    nb = SAMPLE_SEQS_PER_STEP if dbatch % SAMPLE_SEQS_PER_STEP == 0 else 1
    tok = pl.BlockSpec((nb, t_new, B_WIDTH), lambda b: (b, 0, 0))
    cache_specs = [pl.BlockSpec((nb,) + c.shape[1:], lambda b: (b, 0, 0, 0, 0)) for c in caches_t]
    o = pl.pallas_call(
        functools.partial(_attn_sample_kernel, t_new=t_new, slopes=_alibi_slopes()),
        grid=(dbatch // nb,),
        in_specs=[tok, tok, tok] + cache_specs,
        out_specs=pl.BlockSpec((nb, t_new, GROUP_WIDTH), lambda b: (b, 0, 0)),
        out_shape=jax.ShapeDtypeStruct((dbatch, t_new, GROUP_WIDTH), BF16),
        compiler_params=_params(),
        name="attn_sample",
    )(q, k_new, v_new, *caches_t)
    return o.reshape(dbatch * t_new, GROUP_WIDTH)


def _kv_tail_kernel(k_ref, v_ref, o_ref):
    o_ref[0] = k_ref[...].T
    o_ref[1] = v_ref[...].T


def _kv_tail(k, v, gi, bsz, seq):
    keep = min(DILATION_GROUPS[gi][0], seq)
    pb = min(keep, 512)
    first = (seq - keep) // pb
    src = pl.BlockSpec((None, pb, GROUP_WIDTH), lambda b, j: (b, first + j, gi))
    return pl.pallas_call(
        _kv_tail_kernel,
        grid=(bsz, keep // pb),
        in_specs=[src, src],
        out_specs=pl.BlockSpec((None, 2, GROUP_WIDTH, pb), lambda b, j: (b, 0, 0, j)),
        out_shape=jax.ShapeDtypeStruct((bsz, 2, GROUP_WIDTH, keep), F32),
        compiler_params=_params(2),
        name=f"kv_tail_w{DILATION_GROUPS[gi][0]}",
    )(k, v)


def _kv_sample_kernel(*refs, t_new, dbatch):
    in_refs, out_refs = refs[:2 * N_GROUPS_B], refs[2 * N_GROUPS_B:]
    for g, o_ref in enumerate(out_refs):
        k_ref, v_ref = in_refs[2 * g], in_refs[2 * g + 1]
        for t in range(t_new):
            rows = pl.ds(t, dbatch, stride=t_new)
            o_ref[t, 0] = k_ref[rows, :].T
            o_ref[t, 1] = v_ref[rows, :].T


def _kv_sample(k, v, dbatch, t_new):
    src = lambda g: pl.BlockSpec((dbatch * t_new, LANES), lambda s: (0, g * PAIRS + s))
    return pl.pallas_call(
        functools.partial(_kv_sample_kernel, t_new=t_new, dbatch=dbatch),
        grid=(PAIRS,),
        in_specs=[src(g) for g in range(N_GROUPS_B) for _ in range(2)],
        out_specs=[pl.BlockSpec((t_new, 2, LANES, dbatch), lambda s: (0, 0, s, 0))] * N_GROUPS_B,
        out_shape=[jax.ShapeDtypeStruct((t_new, 2, GROUP_WIDTH, dbatch), F32)] * N_GROUPS_B,
        compiler_params=_params(),
        name="kv_sample",
    )(*[t for _ in range(N_GROUPS_B) for t in (k, v)])


def _topk_route(logits):
    tm = logits.shape[0]
    lane = lax.broadcasted_iota(jnp.int32, logits.shape, 1).astype(F32)
    slot = lax.broadcasted_iota(jnp.int32, (tm, TOP_K), 1)
    val_out = jnp.zeros((tm, TOP_K), F32)
    work = logits
    ids, top = [], None
    for r in range(TOP_K):
        mx = jnp.max(work, axis=-1, keepdims=True)
        ix = jnp.min(jnp.where(work == mx, lane, float(N_EXPERTS)), axis=-1, keepdims=True)
        top = mx if top is None else top
        ids.append(ix)
        val_out = jnp.where(slot == r, jnp.exp(mx - top), val_out)
        work = jnp.where(lane == ix, -jnp.inf, work)
    gates = val_out / jnp.sum(val_out, axis=-1, keepdims=True)
    return ids, gates


_FINISH_INPUTS = 17


def _finish_kernel(*refs, n_alias, n_valid):
    ins, outs = refs[:_FINISH_INPUTS], refs[_FINISH_INPUTS + n_alias:]
    step = pl.program_id(0)

    @pl.when(step < n_valid)
    def _():
        _finish_tile(ins, outs)

    @pl.when(step >= n_valid)
    def _():
        for ref in outs[:5]:
            ref[...] = jnp.zeros_like(ref)


def _finish_tile(ins, outs):
    (h_ref, u_ref, va_ref, wm_ref, bias_ref, ob_ref, ga_ref, gb_ref, wa_ref, wb_ref, wo_ref, gmoe_ref,
     wrh_ref, wrl_ref, br_ref, tri_ref, cnt_in_ref) = ins
    h1_ref, n2_ref, idx_ref, gate_ref, pos_ref, cnt_out_ref, carry_ref = outs
    tm = h_ref.shape[0]

    chunks = []
    for c in range(tm // CHUNK):
        rows = slice(c * CHUNK, (c + 1) * CHUNK)
        va_c = va_ref[rows, :].astype(BF16)
        mixed = jnp.concatenate(
            [_dot(wm_ref[g], va_c[:, g * A_GROUP_WIDTH:(g + 1) * A_GROUP_WIDTH]) for g in range(A_GROUPS)], axis=1)
        chunks.append((u_ref[rows, :].astype(F32) * (mixed + bias_ref[...])).astype(BF16))
    branch_a = _dot(jnp.concatenate(chunks, axis=0), wa_ref[...])

    branch_b = _dot(ob_ref[...], wb_ref[...])

    mix = ga_ref[...].astype(F32) * branch_a + gb_ref[...].astype(F32) * branch_b
    h1 = h_ref[...] + _dot(mix.astype(BF16), wo_ref[...])
    h1_ref[...] = h1

    n2 = (h1 * _rms(h1)) * gmoe_ref[...]
    n2_ref[...] = _pack_bf16_pairs(n2)
    n_hi = n2.astype(BF16)
    n_lo = (n2 - n_hi.astype(F32)).astype(BF16)
    logits = _dot(n_hi, wrh_ref[...]) + _dot(n_lo, wrh_ref[...]) + _dot(n_hi, wrl_ref[...]) + br_ref[...]
    ids, gates = _topk_route(logits)
    gate_ref[...] = gates

    @pl.when(pl.program_id(0) == 0)
    def _():
        carry_ref[...] = cnt_in_ref[...]

    lane = lax.broadcasted_iota(jnp.int32, logits.shape, 1).astype(F32)
    slot = lax.broadcasted_iota(jnp.int32, (tm, TOP_K), 1)
    sel = sum(jnp.where(lane == ix, 1.0, 0.0) for ix in ids)
    before = _dot(tri_ref[...], sel.astype(BF16)) + carry_ref[...]
    idx_out = jnp.zeros((tm, TOP_K), F32)
    pos_out = jnp.zeros((tm, TOP_K), F32)
    for r, ix in enumerate(ids):
        idx_out = jnp.where(slot == r, ix, idx_out)
        rank = jnp.sum(jnp.where(lane == ix, before, 0.0), axis=-1, keepdims=True)
        pos_out = jnp.where(slot == r, rank, pos_out)
    idx_ref[...] = idx_out.astype(jnp.int32)
    pos_ref[...] = pos_out.astype(jnp.int32)
    carry_ref[...] = carry_ref[...] + jnp.sum(sel, axis=0, keepdims=True)
    cnt_out_ref[...] = carry_ref[...]


def _finish(h, u, va, wm, bias_full, o_b, ga, gb, consts, cnt_in, n_total, tile_offset, prev_outs):
    n_tok = h.shape[0]
    tm = TOKEN_TILE
    n_valid = n_tok // tm
    n_steps = n_valid if prev_outs is not None else n_total // tm - tile_offset
    row = lambda w: pl.BlockSpec((tm, w), lambda i: (jnp.minimum(i, n_valid - 1), 0))
    orow = lambda w: pl.BlockSpec((tm, w), lambda i: (i + tile_offset, 0))
    args = [h, u, va, wm, bias_full, o_b, ga, gb] + list(consts) + [cnt_in]
    specs = [row(D_MODEL), row(A_WIDTH), row(A_WIDTH), _resident(wm.shape), _resident(bias_full.shape),
             row(GROUP_WIDTH), row(D_MODEL), row(D_MODEL)]
    specs += [_resident(t.shape) for t in consts] + [_resident(cnt_in.shape)]
    assert len(args) == _FINISH_INPUTS
    aliases = {}
    if prev_outs is not None:
        for k, t in enumerate(prev_outs):
            aliases[len(args)] = k
            args.append(t)
            specs.append(pl.BlockSpec(memory_space=pl.ANY))
    widths = [(D_MODEL, F32), (D_MODEL // 2, jnp.int32), (TOP_K, jnp.int32), (TOP_K, F32), (TOP_K, jnp.int32)]
    outs = pl.pallas_call(
        functools.partial(_finish_kernel, n_alias=len(aliases), n_valid=n_valid),
        grid=(n_steps,),
        in_specs=specs,
        out_specs=[orow(w) for w, _ in widths] + [_resident(cnt_in.shape)],
        out_shape=[jax.ShapeDtypeStruct((n_total, w), dt) for w, dt in widths]
                  + [jax.ShapeDtypeStruct(cnt_in.shape, F32)],
        scratch_shapes=[pltpu.VMEM(cnt_in.shape, F32)],
        input_output_aliases=aliases,
        compiler_params=_params(),
        name="finish",
    )(*args)
    return outs[:5], outs[5]


def _sc_gather_rows(table, idx):
    m = idx.shape[0]
    width = table.shape[1]
    per_worker = m // SC_WORKERS
    n_chunks = per_worker // SC_ROWS
    mesh = plsc.VectorSubcoreMesh(core_axis_name="c", subcore_axis_name="s",
                                  num_cores=SC_CORES, num_subcores=SC_SUBCORES)

    assert n_chunks % 2 == 0

    @functools.partial(
        pl.kernel, mesh=mesh,
        out_type=jax.ShapeDtypeStruct((m, width), table.dtype),
        scratch_types=[pltpu.VMEM((n_chunks, SC_ROWS), jnp.int32),
                       pltpu.VMEM((SC_ROWS, width), table.dtype),
                       pltpu.VMEM((SC_ROWS, width), table.dtype),
                       pltpu.SemaphoreType.DMA,
                       pltpu.SemaphoreType.DMA],
        name="sc_gather_rows",
    )
    def gather(table_hbm, idx_hbm, out_hbm, idx_v, rows_a, rows_b, sem_a, sem_b):
        wid = lax.axis_index("s") * SC_CORES + lax.axis_index("c")
        base = wid * per_worker
        pltpu.sync_copy(idx_hbm.at[wid], idx_v)

        def fetch(c, rows, sem):
            return pltpu.make_async_copy(table_hbm.at[idx_v.at[c]], rows, sem)

        def put(c, rows):
            off = pl.multiple_of(base + c * SC_ROWS, SC_ROWS)
            pltpu.sync_copy(rows, out_hbm.at[pl.ds(off, SC_ROWS)])

        fetch(0, rows_a, sem_a).start()

        @pl.loop(0, n_chunks, step=2)
        def _(c):
            fetch(c, rows_a, sem_a).wait()
            fetch(c + 1, rows_b, sem_b).start()
            put(c, rows_a)
            fetch(c + 1, rows_b, sem_b).wait()

            @pl.when(c + 2 < n_chunks)
            def _():
                fetch(c + 2, rows_a, sem_a).start()

            put(c + 1, rows_b)

    return gather(table, idx.reshape(SC_WORKERS, n_chunks, SC_ROWS))


def _sc_scatter_rows(src, dest4, n_out):
    n_src, width = src.shape
    top_k = dest4.shape[1]
    per_worker = n_src // SC_WORKERS
    rows = SC_SCATTER_ROWS
    n_chunks = per_worker // rows
    assert per_worker % rows == 0 and n_chunks % 2 == 0
    dest = dest4.reshape(SC_WORKERS, n_chunks, rows, top_k).transpose(0, 1, 3, 2)
    mesh = plsc.VectorSubcoreMesh(core_axis_name="c", subcore_axis_name="s",
                                  num_cores=SC_CORES, num_subcores=SC_SUBCORES)

    @functools.partial(
        pl.kernel, mesh=mesh,
        out_type=jax.ShapeDtypeStruct((n_out, width), src.dtype),
        scratch_types=[pltpu.VMEM((n_chunks, top_k, rows), jnp.int32),
                       pltpu.VMEM((rows, width), src.dtype),
                       pltpu.VMEM((rows, width), src.dtype),
                       pltpu.SemaphoreType.DMA,
                       pltpu.SemaphoreType.DMA,
                       pltpu.SemaphoreType.DMA],
        name="sc_scatter_rows",
    )
    def scatter(src_hbm, dest_hbm, out_hbm, idx_v, rows_a, rows_b, sem_a, sem_b, sem_w):
        wid = lax.axis_index("s") * SC_CORES + lax.axis_index("c")
        base = wid * per_worker
        pltpu.sync_copy(dest_hbm.at[wid], idx_v)

        def load(c, buf, sem):
            off = pl.multiple_of(base + c * rows, rows)
            return pltpu.make_async_copy(src_hbm.at[pl.ds(off, rows)], buf, sem)

        def spread(c, buf):
            copies = [pltpu.make_async_copy(buf, out_hbm.at[idx_v.at[c, k]], sem_w) for k in range(top_k)]
            for cp in copies:
                cp.start()
            for cp in copies:
                cp.wait()

        load(0, rows_a, sem_a).start()

        @pl.loop(0, n_chunks, step=2)
        def _(c):
            load(c, rows_a, sem_a).wait()
            load(c + 1, rows_b, sem_b).start()
            spread(c, rows_a)
            load(c + 1, rows_b, sem_b).wait()

            @pl.when(c + 2 < n_chunks)
            def _():
                load(c + 2, rows_a, sem_a).start()

            spread(c + 1, rows_b)

    return scatter(src, dest)


def _expert_kernel(be_ref, slot_ref, next_ref, nused_ref, x_ref, wgu_hbm, wd_hbm, bg_ref, bl_ref, bd_ref, sel_ref, y_ref,
                   wgu_f, wd_f, wg_s, wl_s, wd_s, sems):
    i = pl.program_id(0)
    active = i < nused_ref[0]
    expert = be_ref[i]
    fresh = (i == 0) | (expert != be_ref[jnp.maximum(i - 1, 0)])

    def fetch(e, slot):
        return (pltpu.make_async_copy(wgu_hbm.at[e], wgu_f.at[slot], sems.at[slot, 0]),
                pltpu.make_async_copy(wd_hbm.at[e], wd_f.at[slot], sems.at[slot, 1]))

    @pl.when(active & fresh)
    def _():
        slot = slot_ref[i]

        @pl.when(i == 0)
        def _():
            for cp in fetch(expert, slot):
                cp.start()

        for cp in fetch(expert, slot):
            cp.wait()
        nxt = next_ref[i]

        @pl.when(nxt >= 0)
        def _():
            for cp in fetch(nxt, 1 - slot):
                cp.start()

        for t in range(D_FF // MXU_DIM):
            src = wgu_f[slot, :, 2 * t * MXU_DIM:2 * (t + 1) * MXU_DIM].astype(BF16)
            both = _dot(src, sel_ref[...]).astype(BF16)
            wg_s[:, t * MXU_DIM:(t + 1) * MXU_DIM] = both[:, :MXU_DIM]
            wl_s[:, t * MXU_DIM:(t + 1) * MXU_DIM] = both[:, MXU_DIM:]
        wd_s[...] = wd_f[slot].astype(BF16)

    @pl.when(active)
    def _():
        x = _unpack_bf16_pairs(x_ref[...]).astype(BF16)
        h_glu = jnp.minimum(_dot(x, wg_s[...]) + bg_ref[...], SWIGLU_LIMIT)
        h_lin = jnp.clip(_dot(x, wl_s[...]) + bl_ref[...], -SWIGLU_LIMIT, SWIGLU_LIMIT)
        act = h_glu * _sigmoid(SWIGLU_ALPHA * h_glu) * (h_lin + 1.0)
        y_ref[...] = _pack_bf16_pairs(_dot(act.astype(BF16), wd_s[...]) + bd_ref[...])

    @pl.when(jnp.logical_not(active))
    def _():
        y_ref[...] = jnp.zeros_like(y_ref)


def _experts(xb, block_expert, block_slot, block_next, n_used, w_gate_up, w_down, b_glu, b_lin, b_down, sel):
    n_slots = xb.shape[0]
    n_blocks = n_slots // EXPERT_BLOCK
    by_expert = lambda k, n: pl.BlockSpec((None, k, n), lambda i, be, sl, nx, nu: (be[i], 0, 0))
    blk = pl.BlockSpec((EXPERT_BLOCK, D_MODEL // 2), lambda i, be, sl, nx, nu: (i, 0))
    hbm = pl.BlockSpec(memory_space=pl.ANY)
    return pl.pallas_call(
        _expert_kernel,
        grid_spec=pltpu.PrefetchScalarGridSpec(
            num_scalar_prefetch=4, grid=(n_blocks,),
            in_specs=[blk, hbm, hbm, by_expert(1, D_FF), by_expert(1, D_FF), by_expert(1, D_MODEL),
                      pl.BlockSpec(sel.shape, lambda i, be, sl, nx, nu: (0, 0), pipeline_mode=pl.Buffered(1))],
            out_specs=blk,
            scratch_shapes=[pltpu.VMEM((2, D_MODEL, 2 * D_FF), F32), pltpu.VMEM((2, D_FF, D_MODEL), F32),
                            pltpu.VMEM((D_MODEL, D_FF), BF16), pltpu.VMEM((D_MODEL, D_FF), BF16),
                            pltpu.VMEM((D_FF, D_MODEL), BF16), pltpu.SemaphoreType.DMA((2, 2))]),
        out_shape=jax.ShapeDtypeStruct((n_slots, D_MODEL // 2), jnp.int32),
        compiler_params=_params(),
        name="experts",
    )(block_expert, block_slot, block_next, n_used, xb, w_gate_up, w_down, b_glu, b_lin, b_down, sel)


def _final_kernel(h1_ref, yg_ref, gate_ref, p_ref, gple_ref, wg_ref, wp_ref, out_ref):
    h2 = h1_ref[...]
    gates = gate_ref[...]
    for k in range(TOP_K):
        h2 = h2 + gates[:, k:k + 1] * _unpack_bf16_pairs(yg_ref[k])
    n3 = ((h2 * _rms(h2)) * gple_ref[...]).astype(BF16)
    gate = _sigmoid(_dot(n3, wg_ref[...]))
    out_ref[...] = h2 + gate * _dot(p_ref[...].astype(BF16), wp_ref[...])


def _final(h1, yg, gates, p, tile_offset, g_ple, w_ple_gate, w_ple_proj):
    n_tok = p.shape[0]
    tm = TOKEN_TILE
    return pl.pallas_call(
        _final_kernel,
        grid=(n_tok // tm,),
        in_specs=[pl.BlockSpec((tm, D_MODEL), lambda i: (i + tile_offset, 0)),
                  pl.BlockSpec((TOP_K, tm, D_MODEL // 2), lambda i: (0, i + tile_offset, 0)),
                  pl.BlockSpec((tm, TOP_K), lambda i: (i + tile_offset, 0)),
                  pl.BlockSpec((tm, PLE_DIM), lambda i: (i, 0)),
                  _resident(g_ple.shape), _resident(w_ple_gate.shape), _resident(w_ple_proj.shape)],
        out_specs=pl.BlockSpec((tm, D_MODEL), lambda i: (i, 0)),
        out_shape=jax.ShapeDtypeStruct((n_tok, D_MODEL), F32),
        compiler_params=_params(),
        name="final",
    )(h1, yg, gates, p, g_ple, w_ple_gate, w_ple_proj)


def _routing_tables(idx4, pos4, counts, n_slots):
    counts = counts.reshape(N_EXPERTS).astype(jnp.int32)
    pcounts = (counts + EXPERT_BLOCK - 1) // EXPERT_BLOCK * EXPERT_BLOCK
    pends = jnp.cumsum(pcounts)
    pstarts = pends - pcounts
    experts = jnp.arange(N_EXPERTS, dtype=jnp.int32)
    start4 = jnp.sum(jnp.where(idx4[:, :, None] == experts, pstarts, 0), axis=-1)
    dest4 = (start4 + pos4).astype(jnp.int32)
    n_blocks = n_slots // EXPERT_BLOCK
    block_start = jnp.arange(n_blocks, dtype=jnp.int32) * EXPERT_BLOCK
    block_expert = jnp.minimum(jnp.sum(block_start[:, None] >= pends[None, :], axis=1), N_EXPERTS - 1).astype(jnp.int32)
    n_used = (pends[-1] // EXPERT_BLOCK).astype(jnp.int32).reshape(1)
    used = counts > 0
    slot_e = (jnp.cumsum(used.astype(jnp.int32)) - 1) & 1
    later_used = used[None, :] & (experts[None, :] > experts[:, None])
    next_e = jnp.min(jnp.where(later_used, experts[None, :], N_EXPERTS), axis=1)
    next_e = jnp.where(next_e == N_EXPERTS, -1, next_e)
    of_block = block_expert[:, None] == experts[None, :]
    block_slot = jnp.sum(jnp.where(of_block, slot_e[None, :], 0), axis=1).astype(jnp.int32)
    block_next = jnp.sum(jnp.where(of_block, next_e[None, :], 0), axis=1).astype(jnp.int32)
    return dest4, block_expert, block_slot, block_next, n_used


def kernel(x_prompt, x_sample, cache_kv_w128, cache_kv_w512, cache_kv_w2048, p_prompt, p_sample, g_mix, w_in, g_v, g_q, g_k, w_spatial, b_spatial, w_branch_a, w_branch_b, w_out, g_moe, w_router, b_router, w_gate_up, b_gate_up, w_down, b_down, g_ple, w_ple_gate, w_ple_proj):
    bsz, seq, _ = x_prompt.shape
    dbatch, t_new, _ = x_sample.shape
    assert g_mix.shape[0] == 1
    caches = (cache_kv_w128, cache_kv_w512, cache_kv_w2048)
    l = 0
    n_p, n_s = bsz * seq, dbatch * t_new
    n_tok = n_p + n_s
    assert n_p % TOKEN_TILE == 0 and n_s % TOKEN_TILE == 0

    row2 = lambda t: t.reshape(1, -1)
    w_in_bf = w_in[l].astype(BF16)
    g_q_t = jnp.tile(g_q[l], B_HEADS).reshape(1, B_WIDTH)
    g_k_t = jnp.tile(g_k[l], B_HEADS).reshape(1, B_WIDTH)
    hid = np.arange(MXU_DIM) // HEAD_DIM
    hsum = jnp.asarray(hid[:, None] == hid[None, :], BF16)
    tril = jnp.tril(jnp.ones((CHUNK, CHUNK), bool))
    wm_prompt = jnp.where(tril[None], w_spatial[l], 0).astype(BF16)
    bias_prompt = jnp.repeat(b_spatial[l].T, A_GROUP_WIDTH, axis=1)
    reps = CHUNK // t_new
    small = jnp.where(tril[None, :t_new, :t_new], w_spatial[l][:, :t_new, :t_new], 0)
    wm_sample = jnp.einsum("ab,gij->gaibj", jnp.eye(reps, dtype=F32), small).reshape(A_GROUPS, CHUNK, CHUNK).astype(BF16)
    bias_sample = jnp.tile(bias_prompt[:t_new], (reps, 1))
    wr_hi = w_router[l].astype(BF16)
    wr_lo = (w_router[l] - wr_hi.astype(F32)).astype(BF16)
    tri = jnp.asarray(np.tril(np.ones((TOKEN_TILE, TOKEN_TILE), np.float32), -1), BF16)
    consts = (w_branch_a[l].astype(BF16), w_branch_b[l].astype(BF16), w_out[l].astype(BF16), row2(g_moe[l]),
              wr_hi, wr_lo, row2(b_router[l]), tri)
    sel_np = np.zeros((2 * MXU_DIM, 2 * MXU_DIM), np.float32)
    sel_np[2 * np.arange(MXU_DIM), np.arange(MXU_DIM)] = 1.0
    sel_np[2 * np.arange(MXU_DIM) + 1, MXU_DIM + np.arange(MXU_DIM)] = 1.0
    sel = jnp.asarray(sel_np, BF16)

    proj = functools.partial(_project, g_mix=row2(g_mix[l]), w_in_bf=w_in_bf, g_v=row2(g_v[l]),
                             g_q_t=g_q_t, g_k_t=g_k_t, hsum=hsum)

    xp = x_prompt.reshape(n_p, D_MODEL)
    u_p, va_p, q_p, k_p, v_p, ga_p, gb_p = proj(xp, va_dtype=BF16)
    seq3 = lambda t: t.reshape(bsz, seq, B_WIDTH)
    attn_p = [_attend_prompt(seq3(q_p), seq3(k_p), seq3(v_p), gi, bsz, seq) for gi in range(N_GROUPS_B - 1)]
    o_p = _attend_prompt(seq3(q_p), seq3(k_p), seq3(v_p), N_GROUPS_B - 1, bsz, seq, others=attn_p)
    zero_counts = jnp.zeros((1, N_EXPERTS), F32)
    outs_p, cnt_p = _finish(xp, u_p, va_p, wm_prompt, bias_prompt, o_p, ga_p, gb_p, consts, zero_counts,
                            n_tok, 0, None)

    xs = x_sample.reshape(n_s, D_MODEL)
    u_s, va_s, q_s, k_s, v_s, ga_s, gb_s = proj(xs, va_dtype=F32)
    tok3 = lambda t: t.reshape(dbatch, t_new, B_WIDTH)
    caches_t = [jnp.transpose(c[l], (0, 2, 3, 4, 1)) for c in caches]
    o_s = _attend_sample(tok3(q_s), tok3(k_s), tok3(v_s), caches_t, dbatch, t_new)
    (h1, n2, idx4, gates, pos4), counts = _finish(xs, u_s, va_s, wm_sample, bias_sample, o_s, ga_s, gb_s,
                                                  consts, cnt_p, n_tok, n_p // TOKEN_TILE, outs_p)

    n_blocks = -(-n_tok * TOP_K // EXPERT_BLOCK) + N_EXPERTS
    n_slots = n_blocks * EXPERT_BLOCK
    dest4, block_expert, block_slot, block_next, n_used = _routing_tables(idx4, pos4, counts, n_slots)
    xb = _sc_scatter_rows(n2, dest4, n_slots)
    yb = _experts(xb, block_expert, block_slot, block_next, n_used, w_gate_up[l], w_down[l],
                  b_gate_up[l][:, None, 0::2], b_gate_up[l][:, None, 1::2], b_down[l][:, None, :], sel)
    yg = _sc_gather_rows(yb, dest4.T.reshape(-1)).reshape(TOP_K, n_tok, D_MODEL // 2)
    fin = functools.partial(_final, g_ple=row2(g_ple[l]), w_ple_gate=w_ple_gate[l].astype(BF16),
                            w_ple_proj=w_ple_proj[l].astype(BF16))
    y_prompt = fin(h1, yg, gates, p_prompt[l].reshape(n_p, PLE_DIM), 0).reshape(bsz, seq, D_MODEL)
    y_sample = fin(h1, yg, gates, p_sample[l].reshape(n_s, PLE_DIM), n_p // TOKEN_TILE).reshape(dbatch, t_new, D_MODEL)

    kv_prompt = []
    for gi in range(N_GROUPS_B):
        t = _kv_tail(seq3(k_p), seq3(v_p), gi, bsz, seq)
        keep = t.shape[-1]
        kv_prompt.append(jnp.transpose(t.reshape(bsz, 2, H_SLOT, HEAD_DIM, keep), (0, 4, 1, 2, 3))[None])
    kvs = _kv_sample(k_s, v_s, dbatch, t_new)
    kv_sample = [jnp.transpose(kvs[gi].reshape(t_new, 2, H_SLOT, HEAD_DIM, dbatch), (4, 0, 1, 2, 3))[None]
                 for gi in range(N_GROUPS_B)]
    va_out = va_s.reshape(1, dbatch, t_new, A_WIDTH)
    return (y_prompt, y_sample, *kv_prompt, *kv_sample, va_out)
```

```python
import functools

import numpy as np
import jax
import jax.numpy as jnp
from jax import lax
from jax.experimental import pallas as pl
from jax.experimental.pallas import tpu as pltpu
from jax.experimental.pallas import tpu_sc as plsc

F32 = jnp.float32
BF16 = jnp.bfloat16

D_MODEL = 1024
A_WIDTH = 1024
A_GROUPS = 4
A_GROUP_WIDTH = A_WIDTH // A_GROUPS
CHUNK = 128
HEAD_DIM = 64
H_SLOT = 8
GROUP_WIDTH = H_SLOT * HEAD_DIM
DILATION_GROUPS = ((128, 1), (512, 4), (2048, 16))
N_GROUPS_B = len(DILATION_GROUPS)
B_HEADS = H_SLOT * N_GROUPS_B
B_WIDTH = B_HEADS * HEAD_DIM
N_EXPERTS = 32
TOP_K = 4
D_FF = 1024
SWIGLU_ALPHA = 1.702
SWIGLU_LIMIT = 7.0
PLE_DIM = 256
RMS_EPS = 1e-6
NEG_BIG = -1e30
LOG2_E = float(np.log2(np.e))
LN_2 = float(np.log(2.0))

LANES = 128
SUBLANES = 8
MXU_DIM = 256
HEADS_PER_VREG = LANES // HEAD_DIM
PAIRS = GROUP_WIDTH // LANES
RESIDUE_UNROLL = 8
SAMPLE_SEQS_PER_STEP = 2
ATTN_STEP_ROWS = 1024
TOKEN_TILE = 512
FINISH_TILE = 512
EXPERT_BLOCK = 256
VMEM_LIMIT = 56 * 1024 * 1024

SC_CORES = 2
SC_SUBCORES = 16
SC_WORKERS = SC_CORES * SC_SUBCORES
SC_ROWS = 32
SC_SCATTER_ROWS = 16

_COL_SPLITS = np.cumsum([0, A_WIDTH, A_WIDTH, B_WIDTH, B_WIDTH, B_WIDTH, D_MODEL, D_MODEL]).tolist()


def _alibi_slopes():
    return np.exp2(-8.0 * np.arange(1, B_HEADS + 1, dtype=np.float32) / B_HEADS).astype(np.float32)


def _sigmoid(x):
    return 1.0 / (1.0 + jnp.exp(-x))


def _rms(x):
    return lax.rsqrt(jnp.mean(x * x, axis=-1, keepdims=True) + RMS_EPS)


def _dot(a, b):
    return jnp.dot(a, b, preferred_element_type=F32)


def _dot_nt(a, b):
    return lax.dot_general(a, b, (((1,), (1,)), ((), ())), preferred_element_type=F32)


def _pack_bf16_pairs(x):
    w = x.shape[1] // 2
    lo = lax.bitcast_convert_type(x[:, :w].astype(BF16).astype(F32), jnp.uint32) >> 16
    hi = lax.bitcast_convert_type(x[:, w:].astype(BF16).astype(F32), jnp.uint32) & jnp.uint32(0xFFFF0000)
    return lax.bitcast_convert_type(lo | hi, jnp.int32)


def _unpack_bf16_pairs(p):
    u = lax.bitcast_convert_type(p, jnp.uint32)
    lo = lax.bitcast_convert_type(u << 16, F32)
    hi = lax.bitcast_convert_type(u & jnp.uint32(0xFFFF0000), F32)
    return jnp.concatenate([lo, hi], axis=1)


def _resident(shape):
    nd = len(shape)
    return pl.BlockSpec(shape, lambda *_: (0,) * nd, pipeline_mode=pl.Buffered(1))


def _params(n_axes=1):
    return pltpu.CompilerParams(dimension_semantics=("arbitrary",) * n_axes, vmem_limit_bytes=VMEM_LIMIT)


def _proj_kernel(x_ref, gmix_ref, w_ref, gv_ref, gq_ref, gk_ref, hsum_ref,
                 u_ref, va_ref, q_ref, k_ref, v_ref, ga_ref, gb_ref):
    x = x_ref[...]
    n = ((x * _rms(x)) * gmix_ref[...]).astype(BF16)

    def section(i):
        return _dot(n, w_ref[:, _COL_SPLITS[i]:_COL_SPLITS[i + 1]])

    u_ref[...] = jax.nn.gelu(section(0)).astype(u_ref.dtype)
    va = jax.nn.gelu(section(1))
    va_ref[...] = ((va * _rms(va)) * gv_ref[...]).astype(va_ref.dtype)

    def head_norm(z, g_ref, scale):
        parts = []
        for c in range(B_WIDTH // MXU_DIM):
            zc = z[:, c * MXU_DIM:(c + 1) * MXU_DIM]
            ss = _dot((zc * zc).astype(BF16), hsum_ref[...])
            parts.append(zc * lax.rsqrt(ss * (1.0 / HEAD_DIM) + RMS_EPS))
        return jnp.concatenate(parts, axis=1) * (g_ref[...] * scale)

    q_ref[...] = head_norm(section(2), gq_ref, HEAD_DIM ** -0.5 * LOG2_E)
    k_ref[...] = head_norm(section(3), gk_ref, 1.0)
    v_ref[...] = section(4)
    ga_ref[...] = _sigmoid(section(5)).astype(ga_ref.dtype)
    gb_ref[...] = _sigmoid(section(6)).astype(gb_ref.dtype)


def _project(x, g_mix, w_in_bf, g_v, g_q_t, g_k_t, hsum, va_dtype):
    n_tok = x.shape[0]
    tm = TOKEN_TILE
    row = lambda w: pl.BlockSpec((tm, w), lambda i: (i, 0))
    outs = [(A_WIDTH, BF16), (A_WIDTH, va_dtype), (B_WIDTH, F32), (B_WIDTH, F32), (B_WIDTH, F32),
            (D_MODEL, BF16), (D_MODEL, BF16)]
    return pl.pallas_call(
        _proj_kernel,
        grid=(n_tok // tm,),
        in_specs=[row(D_MODEL), _resident(g_mix.shape), _resident(w_in_bf.shape), _resident(g_v.shape),
                  _resident(g_q_t.shape), _resident(g_k_t.shape), _resident(hsum.shape)],
        out_specs=[row(w) for w, _ in outs],
        out_shape=[jax.ShapeDtypeStruct((n_tok, w), dt) for w, dt in outs],
        compiler_params=_params(),
        name="project",
    )(x, g_mix, w_in_bf, g_v, g_q_t, g_k_t, hsum)


def _band_bias(band, dil, slopes):
    qi = jnp.arange(band, dtype=jnp.int32)[:, None]
    kj = jnp.arange(2 * band, dtype=jnp.int32)[None, :]
    dist = qi + band - kj
    in_band = (dist >= 0) & (dist <= band)
    valid = jnp.stack([in_band & (kj >= band), in_band])
    penalty = (jnp.asarray(slopes, F32) * LOG2_E)[:, None, None] * (dist * dil).astype(F32)[None]
    return jnp.where(valid[:, None], -penalty[None], NEG_BIG)


def _pair_attention(q2, k, v, bias_pair):
    band = q2.shape[0]
    first = lax.broadcasted_iota(jnp.int32, (band, LANES), 1) < HEAD_DIM
    qs = jnp.concatenate([jnp.where(first, q2, 0.0), jnp.where(first, 0.0, q2)], axis=0).astype(BF16)
    s = _dot_nt(qs, k) + bias_pair.reshape(HEADS_PER_VREG * band, 2 * band)
    m = jnp.max(s, axis=-1, keepdims=True)
    e = jnp.exp2(s - m).astype(BF16)
    v_ones = jnp.concatenate([v, jnp.ones_like(v)], axis=1)
    r = _dot(e, v_ones)
    den = r[:, LANES:]
    o2 = r[:, :LANES] / den
    lse = m * LN_2 + jnp.log(den)
    return jnp.where(first, o2[:band], o2[band:]), jnp.where(first, lse[:band], lse[band:])


def _attn_dense_kernel(q_ref, kp_ref, kc_ref, vp_ref, vc_ref, bias0_ref, bias_ref, o_ref, l_ref, *, band, n_sub):
    for i in range(n_sub):
        rows = slice(i * band, (i + 1) * band)
        before = slice((i - 1) * band, i * band)
        for p in range(PAIRS):
            cols = slice(p * LANES, (p + 1) * LANES)
            k_prev = kc_ref[before, cols] if i else kp_ref[:, cols]
            v_prev = vc_ref[before, cols] if i else vp_ref[:, cols]
            k = jnp.concatenate([k_prev, kc_ref[rows, cols]], axis=0).astype(BF16)
            v = jnp.concatenate([v_prev, vc_ref[rows, cols]], axis=0).astype(BF16)
            table = bias_ref if i else bias0_ref
            bias2 = table[p * HEADS_PER_VREG:(p + 1) * HEADS_PER_VREG]
            o_pair, l_pair = _pair_attention(q_ref[rows, cols], k, v, bias2)
            o_ref[rows, cols] = o_pair.astype(o_ref.dtype)
            l_ref[rows, cols] = l_pair


def _attn_dilated_kernel(q_ref, kp_ref, kc_ref, vp_ref, vc_ref, bias0_ref, bias_ref, *rest,
                         band, dil, n_sub, n_others):
    others, rest = rest[:2 * n_others], rest[2 * n_others:]
    if n_others:
        out_ref, o_ref, l_ref = rest
    else:
        o_ref, l_ref = rest
    win = band * dil
    unroll = min(RESIDUE_UNROLL, dil)

    def residues(it, carry):
        for i in range(n_sub):
            for u in range(unroll):
                r = it * unroll + u
                rows = pl.ds(i * win + r, band, stride=dil)
                before = pl.ds((i - 1) * win + r, band, stride=dil) if i else pl.ds(r, band, stride=dil)
                k_prev = kc_ref[before, :] if i else kp_ref[before, :]
                v_prev = vc_ref[before, :] if i else vp_ref[before, :]
                k = jnp.concatenate([k_prev, kc_ref[rows, :]], axis=0).astype(BF16)
                v = jnp.concatenate([v_prev, vc_ref[rows, :]], axis=0).astype(BF16)
                table = bias_ref if i else bias0_ref
                o_pair, l_pair = _pair_attention(q_ref[rows, :], k, v, table[...])
                o_ref[rows, :] = o_pair
                l_ref[rows, :] = l_pair
        return carry

    lax.fori_loop(0, dil // unroll, residues, 0)

    if n_others:
        outs = [others[2 * g][...].astype(F32) for g in range(n_others)] + [o_ref[...]]
        lses = [others[2 * g + 1][...] for g in range(n_others)] + [l_ref[...]]
        mx = functools.reduce(jnp.maximum, lses)
        ws = [jnp.exp(l - mx) for l in lses]
        wsum = functools.reduce(jnp.add, ws)
        out_ref[...] = (sum(w * o for w, o in zip(ws, outs)) / wsum).astype(out_ref.dtype)


def _attend_prompt(q, k, v, gi, bsz, seq, others=()):
    win, dil = DILATION_GROUPS[gi]
    band = win // dil
    bias = _band_bias(band, dil, _alibi_slopes()[gi * H_SLOT:(gi + 1) * H_SLOT])
    step_rows = ATTN_STEP_ROWS if dil == 1 else 2 * ATTN_STEP_ROWS
    n_sub = max(1, min(step_rows // win, seq // win))
    step = n_sub * win
    if dil == 1:
        cur = pl.BlockSpec((None, step, GROUP_WIDTH), lambda b, j: (b, j, gi))
        prev = pl.BlockSpec((None, win, GROUP_WIDTH), lambda b, j: (b, jnp.maximum(j * n_sub - 1, 0), gi))
        tab0 = pl.BlockSpec((None, H_SLOT, band, 2 * band), lambda b, j: (jnp.minimum(j, 1), 0, 0, 0))
        tab = pl.BlockSpec((None, H_SLOT, band, 2 * band), lambda b, j: (1, 0, 0, 0))
        out = pl.BlockSpec((None, step, GROUP_WIDTH), lambda b, j: (b, j, 0))
        o, lse = pl.pallas_call(
            functools.partial(_attn_dense_kernel, band=band, n_sub=n_sub),
            grid=(bsz, seq // step),
            in_specs=[cur, prev, cur, prev, cur, tab0, tab],
            out_specs=[out, out],
            out_shape=[jax.ShapeDtypeStruct((bsz, seq, GROUP_WIDTH), BF16),
                       jax.ShapeDtypeStruct((bsz, seq, GROUP_WIDTH), F32)],
            compiler_params=_params(2),
            name=f"attn_prompt_w{win}",
        )(q, k, k, v, v, bias, bias)
    else:
        col = lambda p: gi * PAIRS + p
        cur = pl.BlockSpec((None, step, LANES), lambda b, j, p: (b, j, col(p)))
        prev = pl.BlockSpec((None, win, LANES), lambda b, j, p: (b, jnp.maximum(j * n_sub - 1, 0), col(p)))
        tab0 = pl.BlockSpec((None, HEADS_PER_VREG, band, 2 * band), lambda b, j, p: (jnp.minimum(j, 1), p, 0, 0))
        tab = pl.BlockSpec((None, HEADS_PER_VREG, band, 2 * band), lambda b, j, p: (1, p, 0, 0))
        out = pl.BlockSpec((None, step, LANES), lambda b, j, p: (b, j, p))
        kern = functools.partial(_attn_dilated_kernel, band=band, dil=dil, n_sub=n_sub, n_others=len(others))
        common = dict(grid=(bsz, seq // step, PAIRS), compiler_params=_params(3), name=f"attn_prompt_w{win}")
        ins = [cur, prev, cur, prev, cur, tab0, tab]
        if others:
            extra = [t.reshape(bsz, seq, GROUP_WIDTH) for pair in others for t in pair]
            combined = pl.pallas_call(
                kern,
                in_specs=ins + [out] * len(extra),
                out_specs=out,
                out_shape=jax.ShapeDtypeStruct((bsz, seq, GROUP_WIDTH), BF16),
                scratch_shapes=[pltpu.VMEM((step, LANES), F32), pltpu.VMEM((step, LANES), F32)],
                **common,
            )(q, k, k, v, v, bias, bias, *extra)
            return combined.reshape(bsz * seq, GROUP_WIDTH)
        o, lse = pl.pallas_call(
            kern,
            in_specs=ins,
            out_specs=[out, out],
            out_shape=[jax.ShapeDtypeStruct((bsz, seq, GROUP_WIDTH), F32)] * 2,
            **common,
        )(q, k, k, v, v, bias, bias)
    return o.reshape(bsz * seq, GROUP_WIDTH), lse.reshape(bsz * seq, GROUP_WIDTH)


def _attn_sample_kernel(q_ref, kn_ref, vn_ref, c0_ref, c1_ref, c2_ref, o_ref, *, t_new, slopes):
    for b in range(q_ref.shape[0]):
        _attn_sample_one(q_ref.at[b], kn_ref.at[b], vn_ref.at[b], c0_ref.at[b], c1_ref.at[b], c2_ref.at[b],
                         o_ref.at[b], t_new=t_new, slopes=slopes)


def _attn_sample_one(q_ref, kn_ref, vn_ref, c0_ref, c1_ref, c2_ref, o_ref, *, t_new, slopes):
    rows = H_SLOT * t_new
    pad_new = 16
    t_shift = t_new.bit_length() - 1
    caches = (c0_ref, c1_ref, c2_ref)
    row_id = lax.broadcasted_iota(jnp.int32, (rows, GROUP_WIDTH), 0)
    col_id = lax.broadcasted_iota(jnp.int32, (rows, GROUP_WIDTH), 1)
    own_head = (row_id >> t_shift) == (col_id >> (HEAD_DIM.bit_length() - 1))
    head_of_row = lax.broadcasted_iota(jnp.int32, (rows, 1), 0) >> t_shift
    zpad = jnp.zeros((pad_new - t_new, GROUP_WIDTH), F32)
    outs, lses = [], []
    for g, (win, dil) in enumerate(DILATION_GROUPS):
        buf_len = caches[g].shape[-1]
        cols = slice(g * GROUP_WIDTH, (g + 1) * GROUP_WIDTH)

        def mask_of(n_keys, first_pos, n_real):
            i_q = lax.broadcasted_iota(jnp.int32, (rows, n_keys), 0) & (t_new - 1)
            key = lax.broadcasted_iota(jnp.int32, (rows, n_keys), 1)
            delta = buf_len + i_q - (first_pos + key)
            ok = (delta >= 0) & ((delta & (dil - 1)) == 0) & (delta <= win) & (key < n_real)
            return ok, delta.astype(F32)

        kt = caches[g][0].reshape(GROUP_WIDTH, buf_len).astype(BF16)
        vt = caches[g][1].reshape(GROUP_WIDTH, buf_len).astype(BF16)
        qblk = jnp.where(own_head, jnp.concatenate([q_ref[:, cols]] * H_SLOT, axis=0), 0.0).astype(BF16)
        kn = jnp.concatenate([kn_ref[:, cols], zpad], axis=0).astype(BF16)
        vn = jnp.concatenate([vn_ref[:, cols], zpad], axis=0).astype(BF16)
        slope_rows = jnp.zeros((rows, 1), F32)
        for h in range(H_SLOT):
            slope_rows = jnp.where(head_of_row == h, float(slopes[g * H_SLOT + h]) * LOG2_E, slope_rows)
        ok_c, delta_c = mask_of(buf_len, 0, buf_len)
        ok_n, delta_n = mask_of(pad_new, buf_len, t_new)
        sc = jnp.where(ok_c, _dot(qblk, kt) - slope_rows * delta_c, NEG_BIG)
        sn = jnp.where(ok_n, _dot_nt(qblk, kn) - slope_rows * delta_n, NEG_BIG)
        m = jnp.maximum(jnp.max(sc, axis=-1, keepdims=True), jnp.max(sn, axis=-1, keepdims=True))
        ec = jnp.exp2(sc - m)
        en = jnp.exp2(sn - m)
        den = jnp.sum(ec, axis=-1, keepdims=True) + jnp.sum(en, axis=-1, keepdims=True)
        outs.append((_dot_nt(ec.astype(BF16), vt) + _dot(en.astype(BF16), vn)) * (1.0 / den))
        lses.append(m * LN_2 + jnp.log(den))
    mx = functools.reduce(jnp.maximum, lses)
    ws = [jnp.exp(l - mx) for l in lses]
    wsum = functools.reduce(jnp.add, ws)
    comb = sum((w / wsum) * o for w, o in zip(ws, outs))
    comb = jnp.where(own_head, comb, 0.0)
    o_tok = sum(comb[h * t_new:(h + 1) * t_new, :] for h in range(H_SLOT))
    o_ref[...] = o_tok.astype(o_ref.dtype)


def _attend_sample(q, k_new, v_new, caches_t, dbatch, t_new):
    assert t_new & (t_new - 1) == 0 and t_new <= 8
    nb = SAMPLE_SEQS_PER_STEP if dbatch % SAMPLE_SEQS_PER_STEP == 0 else 1
    tok = pl.BlockSpec((nb, t_new, B_WIDTH), lambda b: (b, 0, 0))
    cache_specs = [pl.BlockSpec((nb,) + c.shape[1:], lambda b: (b, 0, 0, 0, 0)) for c in caches_t]
    o = pl.pallas_call(
        functools.partial(_attn_sample_kernel, t_new=t_new, slopes=_alibi_slopes()),
        grid=(dbatch // nb,),
        in_specs=[tok, tok, tok] + cache_specs,
        out_specs=pl.BlockSpec((nb, t_new, GROUP_WIDTH), lambda b: (b, 0, 0)),
        out_shape=jax.ShapeDtypeStruct((dbatch, t_new, GROUP_WIDTH), BF16),
        compiler_params=_params(),
        name="attn_sample",
    )(q, k_new, v_new, *caches_t)
    return o.reshape(dbatch * t_new, GROUP_WIDTH)


def _kv_tail_kernel(k_ref, v_ref, o_ref):
    o_ref[0] = k_ref[...].T
    o_ref[1] = v_ref[...].T


def _kv_tail(k, v, gi, bsz, seq):
    keep = min(DILATION_GROUPS[gi][0], seq)
    pb = min(keep, 512)
    first = (seq - keep) // pb
    src = pl.BlockSpec((None, pb, GROUP_WIDTH), lambda b, j: (b, first + j, gi))
    return pl.pallas_call(
        _kv_tail_kernel,
        grid=(bsz, keep // pb),
        in_specs=[src, src],
        out_specs=pl.BlockSpec((None, 2, GROUP_WIDTH, pb), lambda b, j: (b, 0, 0, j)),
        out_shape=jax.ShapeDtypeStruct((bsz, 2, GROUP_WIDTH, keep), F32),
        compiler_params=_params(2),
        name=f"kv_tail_w{DILATION_GROUPS[gi][0]}",
    )(k, v)


def _kv_sample_kernel(*refs, t_new, dbatch):
    in_refs, out_refs = refs[:2 * N_GROUPS_B], refs[2 * N_GROUPS_B:]
    for g, o_ref in enumerate(out_refs):
        k_ref, v_ref = in_refs[2 * g], in_refs[2 * g + 1]
        for t in range(t_new):
            rows = pl.ds(t, dbatch, stride=t_new)
            o_ref[t, 0] = k_ref[rows, :].T
            o_ref[t, 1] = v_ref[rows, :].T


def _kv_sample(k, v, dbatch, t_new):
    src = lambda g: pl.BlockSpec((dbatch * t_new, LANES), lambda s: (0, g * PAIRS + s))
    return pl.pallas_call(
        functools.partial(_kv_sample_kernel, t_new=t_new, dbatch=dbatch),
        grid=(PAIRS,),
        in_specs=[src(g) for g in range(N_GROUPS_B) for _ in range(2)],
        out_specs=[pl.BlockSpec((t_new, 2, LANES, dbatch), lambda s: (0, 0, s, 0))] * N_GROUPS_B,
        out_shape=[jax.ShapeDtypeStruct((t_new, 2, GROUP_WIDTH, dbatch), F32)] * N_GROUPS_B,
        compiler_params=_params(),
        name="kv_sample",
    )(*[t for _ in range(N_GROUPS_B) for t in (k, v)])


def _topk_route(logits):
    tm = logits.shape[0]
    lane = lax.broadcasted_iota(jnp.int32, logits.shape, 1).astype(F32)
    slot = lax.broadcasted_iota(jnp.int32, (tm, TOP_K), 1)
    val_out = jnp.zeros((tm, TOP_K), F32)
    work = logits
    ids, top = [], None
    for r in range(TOP_K):
        mx = jnp.max(work, axis=-1, keepdims=True)
        ix = jnp.min(jnp.where(work == mx, lane, float(N_EXPERTS)), axis=-1, keepdims=True)
        top = mx if top is None else top
        ids.append(ix)
        val_out = jnp.where(slot == r, jnp.exp(mx - top), val_out)
        work = jnp.where(lane == ix, -jnp.inf, work)
    gates = val_out / jnp.sum(val_out, axis=-1, keepdims=True)
    return ids, gates


_FINISH_INPUTS = 17


def _finish_kernel(*refs, n_alias, n_valid):
    ins, outs = refs[:_FINISH_INPUTS], refs[_FINISH_INPUTS + n_alias:]
    step = pl.program_id(0)

    @pl.when(step < n_valid)
    def _():
        _finish_tile(ins, outs)

    @pl.when(step >= n_valid)
    def _():
        for ref in outs[:5]:
            ref[...] = jnp.zeros_like(ref)


def _finish_tile(ins, outs):
    (h_ref, u_ref, va_ref, wm_ref, bias_ref, ob_ref, ga_ref, gb_ref, wa_ref, wb_ref, wo_ref, gmoe_ref,
     wrh_ref, wrl_ref, br_ref, tri_ref, cnt_in_ref) = ins
    h1_ref, n2_ref, idx_ref, gate_ref, pos_ref, cnt_out_ref, carry_ref = outs
    tm = h_ref.shape[0]

    chunks = []
    for c in range(tm // CHUNK):
        rows = slice(c * CHUNK, (c + 1) * CHUNK)
        va_c = va_ref[rows, :].astype(BF16)
        mixed = jnp.concatenate(
            [_dot(wm_ref[g], va_c[:, g * A_GROUP_WIDTH:(g + 1) * A_GROUP_WIDTH]) for g in range(A_GROUPS)], axis=1)
        chunks.append((u_ref[rows, :].astype(F32) * (mixed + bias_ref[...])).astype(BF16))
    branch_a = _dot(jnp.concatenate(chunks, axis=0), wa_ref[...])

    branch_b = _dot(ob_ref[...], wb_ref[...])

    mix = ga_ref[...].astype(F32) * branch_a + gb_ref[...].astype(F32) * branch_b
    h1 = h_ref[...] + _dot(mix.astype(BF16), wo_ref[...])
    h1_ref[...] = h1

    n2 = (h1 * _rms(h1)) * gmoe_ref[...]
    n2_ref[...] = _pack_bf16_pairs(n2)
    n_hi = n2.astype(BF16)
    n_lo = (n2 - n_hi.astype(F32)).astype(BF16)
    logits = _dot(n_hi, wrh_ref[...]) + _dot(n_lo, wrh_ref[...]) + _dot(n_hi, wrl_ref[...]) + br_ref[...]
    ids, gates = _topk_route(logits)
    gate_ref[...] = gates

    @pl.when(pl.program_id(0) == 0)
    def _():
        carry_ref[...] = cnt_in_ref[...]

    lane = lax.broadcasted_iota(jnp.int32, logits.shape, 1).astype(F32)
    slot = lax.broadcasted_iota(jnp.int32, (tm, TOP_K), 1)
    sel = sum(jnp.where(lane == ix, 1.0, 0.0) for ix in ids)
    before = _dot(tri_ref[...], sel.astype(BF16)) + carry_ref[...]
    idx_out = jnp.zeros((tm, TOP_K), F32)
    pos_out = jnp.zeros((tm, TOP_K), F32)
    for r, ix in enumerate(ids):
        idx_out = jnp.where(slot == r, ix, idx_out)
        rank = jnp.sum(jnp.where(lane == ix, before, 0.0), axis=-1, keepdims=True)
        pos_out = jnp.where(slot == r, rank, pos_out)
    idx_ref[...] = idx_out.astype(jnp.int32)
    pos_ref[...] = pos_out.astype(jnp.int32)
    carry_ref[...] = carry_ref[...] + jnp.sum(sel, axis=0, keepdims=True)
    cnt_out_ref[...] = carry_ref[...]


def _finish(h, u, va, wm, bias_full, o_b, ga, gb, consts, cnt_in, n_total, tile_offset, prev_outs):
    n_tok = h.shape[0]
    tm = FINISH_TILE
    n_valid = n_tok // tm
    n_steps = n_valid if prev_outs is not None else n_total // tm - tile_offset
    row = lambda w: pl.BlockSpec((tm, w), lambda i: (jnp.minimum(i, n_valid - 1), 0))
    orow = lambda w: pl.BlockSpec((tm, w), lambda i: (i + tile_offset, 0))
    args = [h, u, va, wm, bias_full, o_b, ga, gb] + list(consts) + [cnt_in]
    specs = [row(D_MODEL), row(A_WIDTH), row(A_WIDTH), _resident(wm.shape), _resident(bias_full.shape),
             row(GROUP_WIDTH), row(D_MODEL), row(D_MODEL)]
    specs += [_resident(t.shape) for t in consts] + [_resident(cnt_in.shape)]
    assert len(args) == _FINISH_INPUTS
    aliases = {}
    if prev_outs is not None:
        for k, t in enumerate(prev_outs):
            aliases[len(args)] = k
            args.append(t)
            specs.append(pl.BlockSpec(memory_space=pl.ANY))
    widths = [(D_MODEL, F32), (D_MODEL // 2, jnp.int32), (TOP_K, jnp.int32), (TOP_K, F32), (TOP_K, jnp.int32)]
    outs = pl.pallas_call(
        functools.partial(_finish_kernel, n_alias=len(aliases), n_valid=n_valid),
        grid=(n_steps,),
        in_specs=specs,
        out_specs=[orow(w) for w, _ in widths] + [_resident(cnt_in.shape)],
        out_shape=[jax.ShapeDtypeStruct((n_total, w), dt) for w, dt in widths]
                  + [jax.ShapeDtypeStruct(cnt_in.shape, F32)],
        scratch_shapes=[pltpu.VMEM(cnt_in.shape, F32)],
        input_output_aliases=aliases,
        compiler_params=_params(),
        name="finish",
    )(*args)
    return outs[:5], outs[5]


def _sc_gather_rows(table, idx):
    m = idx.shape[0]
    width = table.shape[1]
    per_worker = m // SC_WORKERS
    n_chunks = per_worker // SC_ROWS
    mesh = plsc.VectorSubcoreMesh(core_axis_name="c", subcore_axis_name="s",
                                  num_cores=SC_CORES, num_subcores=SC_SUBCORES)

    assert n_chunks % 2 == 0

    @functools.partial(
        pl.kernel, mesh=mesh,
        out_type=jax.ShapeDtypeStruct((m, width), table.dtype),
        scratch_types=[pltpu.VMEM((n_chunks, SC_ROWS), jnp.int32),
                       pltpu.VMEM((SC_ROWS, width), table.dtype),
                       pltpu.VMEM((SC_ROWS, width), table.dtype),
                       pltpu.SemaphoreType.DMA,
                       pltpu.SemaphoreType.DMA],
        name="sc_gather_rows",
    )
    def gather(table_hbm, idx_hbm, out_hbm, idx_v, rows_a, rows_b, sem_a, sem_b):
        wid = lax.axis_index("s") * SC_CORES + lax.axis_index("c")
        base = wid * per_worker
        pltpu.sync_copy(idx_hbm.at[wid], idx_v)

        def fetch(c, rows, sem):
            return pltpu.make_async_copy(table_hbm.at[idx_v.at[c]], rows, sem)

        def put(c, rows):
            off = pl.multiple_of(base + c * SC_ROWS, SC_ROWS)
            pltpu.sync_copy(rows, out_hbm.at[pl.ds(off, SC_ROWS)])

        fetch(0, rows_a, sem_a).start()

        @pl.loop(0, n_chunks, step=2)
        def _(c):
            fetch(c, rows_a, sem_a).wait()
            fetch(c + 1, rows_b, sem_b).start()
            put(c, rows_a)
            fetch(c + 1, rows_b, sem_b).wait()

            @pl.when(c + 2 < n_chunks)
            def _():
                fetch(c + 2, rows_a, sem_a).start()

            put(c + 1, rows_b)

    return gather(table, idx.reshape(SC_WORKERS, n_chunks, SC_ROWS))


def _sc_scatter_rows(src, dest4, n_out):
    n_src, width = src.shape
    top_k = dest4.shape[1]
    per_worker = n_src // SC_WORKERS
    rows = SC_SCATTER_ROWS
    n_chunks = per_worker // rows
    assert per_worker % rows == 0 and n_chunks % 2 == 0
    dest = dest4.reshape(SC_WORKERS, n_chunks, rows, top_k).transpose(0, 1, 3, 2)
    mesh = plsc.VectorSubcoreMesh(core_axis_name="c", subcore_axis_name="s",
                                  num_cores=SC_CORES, num_subcores=SC_SUBCORES)

    @functools.partial(
        pl.kernel, mesh=mesh,
        out_type=jax.ShapeDtypeStruct((n_out, width), src.dtype),
        scratch_types=[pltpu.VMEM((n_chunks, top_k, rows), jnp.int32),
                       pltpu.VMEM((rows, width), src.dtype),
                       pltpu.VMEM((rows, width), src.dtype),
                       pltpu.SemaphoreType.DMA,
                       pltpu.SemaphoreType.DMA,
                       pltpu.SemaphoreType.DMA],
        name="sc_scatter_rows",
    )
    def scatter(src_hbm, dest_hbm, out_hbm, idx_v, rows_a, rows_b, sem_a, sem_b, sem_w):
        wid = lax.axis_index("s") * SC_CORES + lax.axis_index("c")
        base = wid * per_worker
        pltpu.sync_copy(dest_hbm.at[wid], idx_v)

        def load(c, buf, sem):
            off = pl.multiple_of(base + c * rows, rows)
            return pltpu.make_async_copy(src_hbm.at[pl.ds(off, rows)], buf, sem)

        def spread(c, buf):
            copies = [pltpu.make_async_copy(buf, out_hbm.at[idx_v.at[c, k]], sem_w) for k in range(top_k)]
            for cp in copies:
                cp.start()
            for cp in copies:
                cp.wait()

        load(0, rows_a, sem_a).start()

        @pl.loop(0, n_chunks, step=2)
        def _(c):
            load(c, rows_a, sem_a).wait()
            load(c + 1, rows_b, sem_b).start()
            spread(c, rows_a)
            load(c + 1, rows_b, sem_b).wait()

            @pl.when(c + 2 < n_chunks)
            def _():
                load(c + 2, rows_a, sem_a).start()

            spread(c + 1, rows_b)

    return scatter(src, dest)


def _expert_kernel(be_ref, slot_ref, next_ref, nused_ref, x_ref, wgu_hbm, wd_hbm, bg_ref, bl_ref, bd_ref, sel_ref, y_ref,
                   wgu_f, wd_f, wg_s, wl_s, wd_s, sems):
    i = pl.program_id(0)
    active = i < nused_ref[0]
    expert = be_ref[i]
    fresh = (i == 0) | (expert != be_ref[jnp.maximum(i - 1, 0)])

    def fetch(e, slot):
        return (pltpu.make_async_copy(wgu_hbm.at[e], wgu_f.at[slot], sems.at[slot, 0]),
                pltpu.make_async_copy(wd_hbm.at[e], wd_f.at[slot], sems.at[slot, 1]))

    @pl.when(active & fresh)
    def _():
        slot = slot_ref[i]

        @pl.when(i == 0)
        def _():
            for cp in fetch(expert, slot):
                cp.start()

        for cp in fetch(expert, slot):
            cp.wait()
        nxt = next_ref[i]

        @pl.when(nxt >= 0)
        def _():
            for cp in fetch(nxt, 1 - slot):
                cp.start()

        half = MXU_DIM // 2
        for t in range(2 * D_FF // MXU_DIM):
            src = wgu_f[slot, :, t * MXU_DIM:(t + 1) * MXU_DIM].astype(BF16)
            both = _dot(src, sel_ref[...]).astype(BF16)
            wg_s[:, t * half:(t + 1) * half] = both[:, :half]
            wl_s[:, t * half:(t + 1) * half] = both[:, half:]
        wd_s[...] = wd_f[slot].astype(BF16)

    @pl.when(active)
    def _():
        x = _unpack_bf16_pairs(x_ref[...]).astype(BF16)
        h_glu = jnp.minimum(_dot(x, wg_s[...]) + bg_ref[...], SWIGLU_LIMIT)
        h_lin = jnp.clip(_dot(x, wl_s[...]) + bl_ref[...], -SWIGLU_LIMIT, SWIGLU_LIMIT)
        act = h_glu * _sigmoid(SWIGLU_ALPHA * h_glu) * (h_lin + 1.0)
        y_ref[...] = _pack_bf16_pairs(_dot(act.astype(BF16), wd_s[...]) + bd_ref[...])

    @pl.when(jnp.logical_not(active))
    def _():
        y_ref[...] = jnp.zeros_like(y_ref)


def _experts(xb, block_expert, block_slot, block_next, n_used, w_gate_up, w_down, b_glu, b_lin, b_down, sel):
    n_slots = xb.shape[0]
    n_blocks = n_slots // EXPERT_BLOCK
    by_expert = lambda k, n: pl.BlockSpec((None, k, n), lambda i, be, sl, nx, nu: (be[i], 0, 0))
    blk = pl.BlockSpec((EXPERT_BLOCK, D_MODEL // 2), lambda i, be, sl, nx, nu: (i, 0))
    hbm = pl.BlockSpec(memory_space=pl.ANY)
    return pl.pallas_call(
        _expert_kernel,
        grid_spec=pltpu.PrefetchScalarGridSpec(
            num_scalar_prefetch=4, grid=(n_blocks,),
            in_specs=[blk, hbm, hbm, by_expert(1, D_FF), by_expert(1, D_FF), by_expert(1, D_MODEL),
                      pl.BlockSpec(sel.shape, lambda i, be, sl, nx, nu: (0, 0), pipeline_mode=pl.Buffered(1))],
            out_specs=blk,
            scratch_shapes=[pltpu.VMEM((2, D_MODEL, 2 * D_FF), F32), pltpu.VMEM((2, D_FF, D_MODEL), F32),
                            pltpu.VMEM((D_MODEL, D_FF), BF16), pltpu.VMEM((D_MODEL, D_FF), BF16),
                            pltpu.VMEM((D_FF, D_MODEL), BF16), pltpu.SemaphoreType.DMA((2, 2))]),
        out_shape=jax.ShapeDtypeStruct((n_slots, D_MODEL // 2), jnp.int32),
        compiler_params=_params(),
        name="experts",
    )(block_expert, block_slot, block_next, n_used, xb, w_gate_up, w_down, b_glu, b_lin, b_down, sel)


def _final_kernel(h1_ref, yg_ref, gate_ref, p_ref, gple_ref, wg_ref, wp_ref, out_ref):
    h2 = h1_ref[...]
    gates = gate_ref[...]
    for k in range(TOP_K):
        h2 = h2 + gates[:, k:k + 1] * _unpack_bf16_pairs(yg_ref[k])
    n3 = ((h2 * _rms(h2)) * gple_ref[...]).astype(BF16)
    gate = _sigmoid(_dot(n3, wg_ref[...]))
    out_ref[...] = h2 + gate * _dot(p_ref[...].astype(BF16), wp_ref[...])


def _final(h1, yg, gates, p, tile_offset, g_ple, w_ple_gate, w_ple_proj):
    n_tok = p.shape[0]
    tm = TOKEN_TILE
    return pl.pallas_call(
        _final_kernel,
        grid=(n_tok // tm,),
        in_specs=[pl.BlockSpec((tm, D_MODEL), lambda i: (i + tile_offset, 0)),
                  pl.BlockSpec((TOP_K, tm, D_MODEL // 2), lambda i: (0, i + tile_offset, 0)),
                  pl.BlockSpec((tm, TOP_K), lambda i: (i + tile_offset, 0)),
                  pl.BlockSpec((tm, PLE_DIM), lambda i: (i, 0)),
                  _resident(g_ple.shape), _resident(w_ple_gate.shape), _resident(w_ple_proj.shape)],
        out_specs=pl.BlockSpec((tm, D_MODEL), lambda i: (i, 0)),
        out_shape=jax.ShapeDtypeStruct((n_tok, D_MODEL), F32),
        compiler_params=_params(),
        name="final",
    )(h1, yg, gates, p, g_ple, w_ple_gate, w_ple_proj)


def _routing_tables(idx4, pos4, counts, n_slots):
    counts = counts.reshape(N_EXPERTS).astype(jnp.int32)
    pcounts = (counts + EXPERT_BLOCK - 1) // EXPERT_BLOCK * EXPERT_BLOCK
    pends = jnp.cumsum(pcounts)
    pstarts = pends - pcounts
    experts = jnp.arange(N_EXPERTS, dtype=jnp.int32)
    start4 = jnp.sum(jnp.where(idx4[:, :, None] == experts, pstarts, 0), axis=-1)
    dest4 = (start4 + pos4).astype(jnp.int32)
    n_blocks = n_slots // EXPERT_BLOCK
    block_start = jnp.arange(n_blocks, dtype=jnp.int32) * EXPERT_BLOCK
    block_expert = jnp.minimum(jnp.sum(block_start[:, None] >= pends[None, :], axis=1), N_EXPERTS - 1).astype(jnp.int32)
    n_used = (pends[-1] // EXPERT_BLOCK).astype(jnp.int32).reshape(1)
    used = counts > 0
    slot_e = (jnp.cumsum(used.astype(jnp.int32)) - 1) & 1
    later_used = used[None, :] & (experts[None, :] > experts[:, None])
    next_e = jnp.min(jnp.where(later_used, experts[None, :], N_EXPERTS), axis=1)
    next_e = jnp.where(next_e == N_EXPERTS, -1, next_e)
    of_block = block_expert[:, None] == experts[None, :]
    block_slot = jnp.sum(jnp.where(of_block, slot_e[None, :], 0), axis=1).astype(jnp.int32)
    block_next = jnp.sum(jnp.where(of_block, next_e[None, :], 0), axis=1).astype(jnp.int32)
    return dest4, block_expert, block_slot, block_next, n_used


def kernel(x_prompt, x_sample, cache_kv_w128, cache_kv_w512, cache_kv_w2048, p_prompt, p_sample, g_mix, w_in, g_v, g_q, g_k, w_spatial, b_spatial, w_branch_a, w_branch_b, w_out, g_moe, w_router, b_router, w_gate_up, b_gate_up, w_down, b_down, g_ple, w_ple_gate, w_ple_proj):
    bsz, seq, _ = x_prompt.shape
    dbatch, t_new, _ = x_sample.shape
    assert g_mix.shape[0] == 1
    caches = (cache_kv_w128, cache_kv_w512, cache_kv_w2048)
    l = 0
    n_p, n_s = bsz * seq, dbatch * t_new
    n_tok = n_p + n_s
    assert n_p % TOKEN_TILE == 0 and n_s % TOKEN_TILE == 0

    row2 = lambda t: t.reshape(1, -1)
    w_in_bf = w_in[l].astype(BF16)
    g_q_t = jnp.tile(g_q[l], B_HEADS).reshape(1, B_WIDTH)
    g_k_t = jnp.tile(g_k[l], B_HEADS).reshape(1, B_WIDTH)
    hid = np.arange(MXU_DIM) // HEAD_DIM
    hsum = jnp.asarray(hid[:, None] == hid[None, :], BF16)
    tril = jnp.tril(jnp.ones((CHUNK, CHUNK), bool))
    wm_prompt = jnp.where(tril[None], w_spatial[l], 0).astype(BF16)
    bias_prompt = jnp.repeat(b_spatial[l].T, A_GROUP_WIDTH, axis=1)
    reps = CHUNK // t_new
    small = jnp.where(tril[None, :t_new, :t_new], w_spatial[l][:, :t_new, :t_new], 0)
    wm_sample = jnp.einsum("ab,gij->gaibj", jnp.eye(reps, dtype=F32), small).reshape(A_GROUPS, CHUNK, CHUNK).astype(BF16)
    bias_sample = jnp.tile(bias_prompt[:t_new], (reps, 1))
    wr_hi = w_router[l].astype(BF16)
    wr_lo = (w_router[l] - wr_hi.astype(F32)).astype(BF16)
    tri = jnp.asarray(np.tril(np.ones((FINISH_TILE, FINISH_TILE), np.float32), -1), BF16)
    consts = (w_branch_a[l].astype(BF16), w_branch_b[l].astype(BF16), w_out[l].astype(BF16), row2(g_moe[l]),
              wr_hi, wr_lo, row2(b_router[l]), tri)
    half = MXU_DIM // 2
    sel_np = np.zeros((MXU_DIM, MXU_DIM), np.float32)
    sel_np[2 * np.arange(half), np.arange(half)] = 1.0
    sel_np[2 * np.arange(half) + 1, half + np.arange(half)] = 1.0
    sel = jnp.asarray(sel_np, BF16)

    proj = functools.partial(_project, g_mix=row2(g_mix[l]), w_in_bf=w_in_bf, g_v=row2(g_v[l]),
                             g_q_t=g_q_t, g_k_t=g_k_t, hsum=hsum)

    xp = x_prompt.reshape(n_p, D_MODEL)
    u_p, va_p, q_p, k_p, v_p, ga_p, gb_p = proj(xp, va_dtype=BF16)
    seq3 = lambda t: t.reshape(bsz, seq, B_WIDTH)
    attn_p = [_attend_prompt(seq3(q_p), seq3(k_p), seq3(v_p), gi, bsz, seq) for gi in range(N_GROUPS_B - 1)]
    o_p = _attend_prompt(seq3(q_p), seq3(k_p), seq3(v_p), N_GROUPS_B - 1, bsz, seq, others=attn_p)
    zero_counts = jnp.zeros((1, N_EXPERTS), F32)
    outs_p, cnt_p = _finish(xp, u_p, va_p, wm_prompt, bias_prompt, o_p, ga_p, gb_p, consts, zero_counts,
                            n_tok, 0, None)

    xs = x_sample.reshape(n_s, D_MODEL)
    u_s, va_s, q_s, k_s, v_s, ga_s, gb_s = proj(xs, va_dtype=F32)
    tok3 = lambda t: t.reshape(dbatch, t_new, B_WIDTH)
    caches_t = [jnp.transpose(c[l], (0, 2, 3, 4, 1)) for c in caches]
    o_s = _attend_sample(tok3(q_s), tok3(k_s), tok3(v_s), caches_t, dbatch, t_new)
    (h1, n2, idx4, gates, pos4), counts = _finish(xs, u_s, va_s, wm_sample, bias_sample, o_s, ga_s, gb_s,
                                                  consts, cnt_p, n_tok, n_p // FINISH_TILE, outs_p)

    n_blocks = -(-n_tok * TOP_K // EXPERT_BLOCK) + N_EXPERTS
    n_slots = n_blocks * EXPERT_BLOCK
    dest4, block_expert, block_slot, block_next, n_used = _routing_tables(idx4, pos4, counts, n_slots)
    xb = _sc_scatter_rows(n2, dest4, n_slots)
    yb = _experts(xb, block_expert, block_slot, block_next, n_used, w_gate_up[l], w_down[l],
                  b_gate_up[l][:, None, 0::2], b_gate_up[l][:, None, 1::2], b_down[l][:, None, :], sel)
    yg = _sc_gather_rows(yb, dest4.T.reshape(-1)).reshape(TOP_K, n_tok, D_MODEL // 2)
    fin = functools.partial(_final, g_ple=row2(g_ple[l]), w_ple_gate=w_ple_gate[l].astype(BF16),
                            w_ple_proj=w_ple_proj[l].astype(BF16))
    y_prompt = fin(h1, yg, gates, p_prompt[l].reshape(n_p, PLE_DIM), 0).reshape(bsz, seq, D_MODEL)
    y_sample = fin(h1, yg, gates, p_sample[l].reshape(n_s, PLE_DIM), n_p // TOKEN_TILE).reshape(dbatch, t_new, D_MODEL)

    kv_prompt = []
    for gi in range(N_GROUPS_B):
        t = _kv_tail(seq3(k_p), seq3(v_p), gi, bsz, seq)
        keep = t.shape[-1]
        kv_prompt.append(jnp.transpose(t.reshape(bsz, 2, H_SLOT, HEAD_DIM, keep), (0, 4, 1, 2, 3))[None])
    kvs = _kv_sample(k_s, v_s, dbatch, t_new)
    kv_sample = [jnp.transpose(kvs[gi].reshape(t_new, 2, H_SLOT, HEAD_DIM, dbatch), (4, 0, 1, 2, 3))[None]
                 for gi in range(N_GROUPS_B)]
    va_out = va_s.reshape(1, dbatch, t_new, A_WIDTH)
    return (y_prompt, y_sample, *kv_prompt, *kv_sample, va_out)
```

```python
import functools

import numpy as np
import jax
import jax.numpy as jnp
from jax import lax
from jax.experimental import pallas as pl
from jax.experimental.pallas import tpu as pltpu
from jax.experimental.pallas import tpu_sc as plsc

F32 = jnp.float32
BF16 = jnp.bfloat16

D_MODEL = 1024
A_WIDTH = 1024
A_GROUPS = 4
A_GROUP_WIDTH = A_WIDTH // A_GROUPS
CHUNK = 128
HEAD_DIM = 64
H_SLOT = 8
GROUP_WIDTH = H_SLOT * HEAD_DIM
DILATION_GROUPS = ((128, 1), (512, 4), (2048, 16))
N_GROUPS_B = len(DILATION_GROUPS)
B_HEADS = H_SLOT * N_GROUPS_B
B_WIDTH = B_HEADS * HEAD_DIM
N_EXPERTS = 32
TOP_K = 4
D_FF = 1024
SWIGLU_ALPHA = 1.702
SWIGLU_LIMIT = 7.0
PLE_DIM = 256
RMS_EPS = 1e-6
NEG_BIG = -1e30
LOG2_E = float(np.log2(np.e))
LN_2 = float(np.log(2.0))

LANES = 128
SUBLANES = 8
MXU_DIM = 256
HEADS_PER_VREG = LANES // HEAD_DIM
PAIRS = GROUP_WIDTH // LANES
RESIDUE_UNROLL = 8
SAMPLE_SEQS_PER_STEP = 2
ATTN_STEP_ROWS = 1024
TOKEN_TILE = 512
FINISH_TILE = 512
EXPERT_BLOCK = 256
VMEM_LIMIT = 56 * 1024 * 1024

SC_CORES = 2
SC_SUBCORES = 16
SC_WORKERS = SC_CORES * SC_SUBCORES
SC_ROWS = 64
SC_SCATTER_ROWS = 16

_COL_SPLITS = np.cumsum([0, A_WIDTH, A_WIDTH, B_WIDTH, B_WIDTH, B_WIDTH, D_MODEL, D_MODEL]).tolist()


def _alibi_slopes():
    return np.exp2(-8.0 * np.arange(1, B_HEADS + 1, dtype=np.float32) / B_HEADS).astype(np.float32)


def _sigmoid(x):
    return 1.0 / (1.0 + jnp.exp(-x))


def _rms(x):
    return lax.rsqrt(jnp.mean(x * x, axis=-1, keepdims=True) + RMS_EPS)


def _dot(a, b):
    return jnp.dot(a, b, preferred_element_type=F32)


def _dot_nt(a, b):
    return lax.dot_general(a, b, (((1,), (1,)), ((), ())), preferred_element_type=F32)


def _pack_bf16_pairs(x):
    w = x.shape[1] // 2
    lo = lax.bitcast_convert_type(x[:, :w].astype(BF16).astype(F32), jnp.uint32) >> 16
    hi = lax.bitcast_convert_type(x[:, w:].astype(BF16).astype(F32), jnp.uint32) & jnp.uint32(0xFFFF0000)
    return lax.bitcast_convert_type(lo | hi, jnp.int32)


def _unpack_bf16_pairs(p):
    u = lax.bitcast_convert_type(p, jnp.uint32)
    lo = lax.bitcast_convert_type(u << 16, F32)
    hi = lax.bitcast_convert_type(u & jnp.uint32(0xFFFF0000), F32)
    return jnp.concatenate([lo, hi], axis=1)


def _resident(shape):
    nd = len(shape)
    return pl.BlockSpec(shape, lambda *_: (0,) * nd, pipeline_mode=pl.Buffered(1))


def _params(n_axes=1):
    return pltpu.CompilerParams(dimension_semantics=("arbitrary",) * n_axes, vmem_limit_bytes=VMEM_LIMIT)


def _proj_kernel(x_ref, gmix_ref, w_ref, gv_ref, gq_ref, gk_ref, hsum_ref,
                 u_ref, va_ref, q_ref, k_ref, v_ref, ga_ref, gb_ref):
    x = x_ref[...]
    n = ((x * _rms(x)) * gmix_ref[...]).astype(BF16)

    def section(i):
        return _dot(n, w_ref[:, _COL_SPLITS[i]:_COL_SPLITS[i + 1]])

    u_ref[...] = jax.nn.gelu(section(0)).astype(u_ref.dtype)
    va = jax.nn.gelu(section(1))
    va_ref[...] = ((va * _rms(va)) * gv_ref[...]).astype(va_ref.dtype)

    def head_norm(z, g_ref, scale):
        parts = []
        for c in range(B_WIDTH // MXU_DIM):
            zc = z[:, c * MXU_DIM:(c + 1) * MXU_DIM]
            ss = _dot((zc * zc).astype(BF16), hsum_ref[...])
            parts.append(zc * lax.rsqrt(ss * (1.0 / HEAD_DIM) + RMS_EPS))
        return jnp.concatenate(parts, axis=1) * (g_ref[...] * scale)

    q_ref[...] = head_norm(section(2), gq_ref, HEAD_DIM ** -0.5 * LOG2_E)
    k_ref[...] = head_norm(section(3), gk_ref, 1.0)
    v_ref[...] = section(4)
    ga_ref[...] = _sigmoid(section(5)).astype(ga_ref.dtype)
    gb_ref[...] = _sigmoid(section(6)).astype(gb_ref.dtype)


def _project(x, g_mix, w_in_bf, g_v, g_q_t, g_k_t, hsum, va_dtype):
    n_tok = x.shape[0]
    tm = TOKEN_TILE
    row = lambda w: pl.BlockSpec((tm, w), lambda i: (i, 0))
    outs = [(A_WIDTH, BF16), (A_WIDTH, va_dtype), (B_WIDTH, F32), (B_WIDTH, F32), (B_WIDTH, F32),
            (D_MODEL, BF16), (D_MODEL, BF16)]
    return pl.pallas_call(
        _proj_kernel,
        grid=(n_tok // tm,),
        in_specs=[row(D_MODEL), _resident(g_mix.shape), _resident(w_in_bf.shape), _resident(g_v.shape),
                  _resident(g_q_t.shape), _resident(g_k_t.shape), _resident(hsum.shape)],
        out_specs=[row(w) for w, _ in outs],
        out_shape=[jax.ShapeDtypeStruct((n_tok, w), dt) for w, dt in outs],
        compiler_params=_params(),
        name="project",
    )(x, g_mix, w_in_bf, g_v, g_q_t, g_k_t, hsum)


def _band_bias(band, dil, slopes):
    qi = jnp.arange(band, dtype=jnp.int32)[:, None]
    kj = jnp.arange(2 * band, dtype=jnp.int32)[None, :]
    dist = qi + band - kj
    in_band = (dist >= 0) & (dist <= band)
    valid = jnp.stack([in_band & (kj >= band), in_band])
    penalty = (jnp.asarray(slopes, F32) * LOG2_E)[:, None, None] * (dist * dil).astype(F32)[None]
    return jnp.where(valid[:, None], -penalty[None], NEG_BIG)


def _pair_attention(q2, k, v, bias_pair):
    band = q2.shape[0]
    first = lax.broadcasted_iota(jnp.int32, (band, LANES), 1) < HEAD_DIM
    qs = jnp.concatenate([jnp.where(first, q2, 0.0), jnp.where(first, 0.0, q2)], axis=0).astype(BF16)
    s = _dot_nt(qs, k) + bias_pair.reshape(HEADS_PER_VREG * band, 2 * band)
    m = jnp.max(s, axis=-1, keepdims=True)
    e = jnp.exp2(s - m).astype(BF16)
    v_ones = jnp.concatenate([v, jnp.ones_like(v)], axis=1)
    r = _dot(e, v_ones)
    den = r[:, LANES:]
    o2 = r[:, :LANES] / den
    lse = m * LN_2 + jnp.log(den)
    return jnp.where(first, o2[:band], o2[band:]), jnp.where(first, lse[:band], lse[band:])


def _attn_dense_kernel(q_ref, kp_ref, kc_ref, vp_ref, vc_ref, bias0_ref, bias_ref, o_ref, l_ref, *, band, n_sub):
    for i in range(n_sub):
        rows = slice(i * band, (i + 1) * band)
        before = slice((i - 1) * band, i * band)
        for p in range(PAIRS):
            cols = slice(p * LANES, (p + 1) * LANES)
            k_prev = kc_ref[before, cols] if i else kp_ref[:, cols]
            v_prev = vc_ref[before, cols] if i else vp_ref[:, cols]
            k = jnp.concatenate([k_prev, kc_ref[rows, cols]], axis=0).astype(BF16)
            v = jnp.concatenate([v_prev, vc_ref[rows, cols]], axis=0).astype(BF16)
            table = bias_ref if i else bias0_ref
            bias2 = table[p * HEADS_PER_VREG:(p + 1) * HEADS_PER_VREG]
            o_pair, l_pair = _pair_attention(q_ref[rows, cols], k, v, bias2)
            o_ref[rows, cols] = o_pair.astype(o_ref.dtype)
            l_ref[rows, cols] = l_pair


def _attn_dilated_kernel(q_ref, kp_ref, kc_ref, vp_ref, vc_ref, bias0_ref, bias_ref, *rest,
                         band, dil, n_sub, n_others):
    others, rest = rest[:2 * n_others], rest[2 * n_others:]
    if n_others:
        out_ref, o_ref, l_ref = rest
    else:
        o_ref, l_ref = rest
    win = band * dil
    unroll = min(RESIDUE_UNROLL, dil)

    def residues(it, carry):
        for i in range(n_sub):
            for u in range(unroll):
                r = it * unroll + u
                rows = pl.ds(i * win + r, band, stride=dil)
                before = pl.ds((i - 1) * win + r, band, stride=dil) if i else pl.ds(r, band, stride=dil)
                k_prev = kc_ref[before, :] if i else kp_ref[before, :]
                v_prev = vc_ref[before, :] if i else vp_ref[before, :]
                k = jnp.concatenate([k_prev, kc_ref[rows, :]], axis=0).astype(BF16)
                v = jnp.concatenate([v_prev, vc_ref[rows, :]], axis=0).astype(BF16)
                table = bias_ref if i else bias0_ref
                o_pair, l_pair = _pair_attention(q_ref[rows, :], k, v, table[...])
                o_ref[rows, :] = o_pair
                l_ref[rows, :] = l_pair
        return carry

    lax.fori_loop(0, dil // unroll, residues, 0)

    if n_others:
        outs = [others[2 * g][...].astype(F32) for g in range(n_others)] + [o_ref[...]]
        lses = [others[2 * g + 1][...] for g in range(n_others)] + [l_ref[...]]
        mx = functools.reduce(jnp.maximum, lses)
        ws = [jnp.exp(l - mx) for l in lses]
        wsum = functools.reduce(jnp.add, ws)
        out_ref[...] = (sum(w * o for w, o in zip(ws, outs)) / wsum).astype(out_ref.dtype)


def _attend_prompt(q, k, v, gi, bsz, seq, others=()):
    win, dil = DILATION_GROUPS[gi]
    band = win // dil
    bias = _band_bias(band, dil, _alibi_slopes()[gi * H_SLOT:(gi + 1) * H_SLOT])
    step_rows = ATTN_STEP_ROWS if dil == 1 else 2 * ATTN_STEP_ROWS
    n_sub = max(1, min(step_rows // win, seq // win))
    step = n_sub * win
    if dil == 1:
        cur = pl.BlockSpec((None, step, GROUP_WIDTH), lambda b, j: (b, j, gi))
        prev = pl.BlockSpec((None, win, GROUP_WIDTH), lambda b, j: (b, jnp.maximum(j * n_sub - 1, 0), gi))
        tab0 = pl.BlockSpec((None, H_SLOT, band, 2 * band), lambda b, j: (jnp.minimum(j, 1), 0, 0, 0))
        tab = pl.BlockSpec((None, H_SLOT, band, 2 * band), lambda b, j: (1, 0, 0, 0))
        out = pl.BlockSpec((None, step, GROUP_WIDTH), lambda b, j: (b, j, 0))
        o, lse = pl.pallas_call(
            functools.partial(_attn_dense_kernel, band=band, n_sub=n_sub),
            grid=(bsz, seq // step),
            in_specs=[cur, prev, cur, prev, cur, tab0, tab],
            out_specs=[out, out],
            out_shape=[jax.ShapeDtypeStruct((bsz, seq, GROUP_WIDTH), BF16),
                       jax.ShapeDtypeStruct((bsz, seq, GROUP_WIDTH), F32)],
            compiler_params=_params(2),
            name=f"attn_prompt_w{win}",
        )(q, k, k, v, v, bias, bias)
    else:
        col = lambda p: gi * PAIRS + p
        cur = pl.BlockSpec((None, step, LANES), lambda b, j, p: (b, j, col(p)))
        prev = pl.BlockSpec((None, win, LANES), lambda b, j, p: (b, jnp.maximum(j * n_sub - 1, 0), col(p)))
        tab0 = pl.BlockSpec((None, HEADS_PER_VREG, band, 2 * band), lambda b, j, p: (jnp.minimum(j, 1), p, 0, 0))
        tab = pl.BlockSpec((None, HEADS_PER_VREG, band, 2 * band), lambda b, j, p: (1, p, 0, 0))
        out = pl.BlockSpec((None, step, LANES), lambda b, j, p: (b, j, p))
        kern = functools.partial(_attn_dilated_kernel, band=band, dil=dil, n_sub=n_sub, n_others=len(others))
        common = dict(grid=(bsz, seq // step, PAIRS), compiler_params=_params(3), name=f"attn_prompt_w{win}")
        ins = [cur, prev, cur, prev, cur, tab0, tab]
        if others:
            extra = [t.reshape(bsz, seq, GROUP_WIDTH) for pair in others for t in pair]
            combined = pl.pallas_call(
                kern,
                in_specs=ins + [out] * len(extra),
                out_specs=out,
                out_shape=jax.ShapeDtypeStruct((bsz, seq, GROUP_WIDTH), BF16),
                scratch_shapes=[pltpu.VMEM((step, LANES), F32), pltpu.VMEM((step, LANES), F32)],
                **common,
            )(q, k, k, v, v, bias, bias, *extra)
            return combined.reshape(bsz * seq, GROUP_WIDTH)
        o, lse = pl.pallas_call(
            kern,
            in_specs=ins,
            out_specs=[out, out],
            out_shape=[jax.ShapeDtypeStruct((bsz, seq, GROUP_WIDTH), F32)] * 2,
            **common,
        )(q, k, k, v, v, bias, bias)
    return o.reshape(bsz * seq, GROUP_WIDTH), lse.reshape(bsz * seq, GROUP_WIDTH)


def _attn_sample_kernel(q_ref, kn_ref, vn_ref, c0_ref, c1_ref, c2_ref, o_ref, *, t_new, slopes):
    for b in range(q_ref.shape[0]):
        _attn_sample_one(q_ref.at[b], kn_ref.at[b], vn_ref.at[b], c0_ref.at[b], c1_ref.at[b], c2_ref.at[b],
                         o_ref.at[b], t_new=t_new, slopes=slopes)


def _attn_sample_one(q_ref, kn_ref, vn_ref, c0_ref, c1_ref, c2_ref, o_ref, *, t_new, slopes):
    rows = H_SLOT * t_new
    pad_new = 16
    t_shift = t_new.bit_length() - 1
    caches = (c0_ref, c1_ref, c2_ref)
    row_id = lax.broadcasted_iota(jnp.int32, (rows, GROUP_WIDTH), 0)
    col_id = lax.broadcasted_iota(jnp.int32, (rows, GROUP_WIDTH), 1)
    own_head = (row_id >> t_shift) == (col_id >> (HEAD_DIM.bit_length() - 1))
    head_of_row = lax.broadcasted_iota(jnp.int32, (rows, 1), 0) >> t_shift
    zpad = jnp.zeros((pad_new - t_new, GROUP_WIDTH), F32)
    outs, lses = [], []
    for g, (win, dil) in enumerate(DILATION_GROUPS):
        buf_len = caches[g].shape[-1]
        cols = slice(g * GROUP_WIDTH, (g + 1) * GROUP_WIDTH)

        def mask_of(n_keys, first_pos, n_real):
            i_q = lax.broadcasted_iota(jnp.int32, (rows, n_keys), 0) & (t_new - 1)
            key = lax.broadcasted_iota(jnp.int32, (rows, n_keys), 1)
            delta = buf_len + i_q - (first_pos + key)
            ok = (delta >= 0) & ((delta & (dil - 1)) == 0) & (delta <= win) & (key < n_real)
            return ok, delta.astype(F32)

        kt = caches[g][0].reshape(GROUP_WIDTH, buf_len).astype(BF16)
        vt = caches[g][1].reshape(GROUP_WIDTH, buf_len).astype(BF16)
        qblk = jnp.where(own_head, jnp.concatenate([q_ref[:, cols]] * H_SLOT, axis=0), 0.0).astype(BF16)
        kn = jnp.concatenate([kn_ref[:, cols], zpad], axis=0).astype(BF16)
        vn = jnp.concatenate([vn_ref[:, cols], zpad], axis=0).astype(BF16)
        slope_rows = jnp.zeros((rows, 1), F32)
        for h in range(H_SLOT):
            slope_rows = jnp.where(head_of_row == h, float(slopes[g * H_SLOT + h]) * LOG2_E, slope_rows)
        ok_c, delta_c = mask_of(buf_len, 0, buf_len)
        ok_n, delta_n = mask_of(pad_new, buf_len, t_new)
        sc = jnp.where(ok_c, _dot(qblk, kt) - slope_rows * delta_c, NEG_BIG)
        sn = jnp.where(ok_n, _dot_nt(qblk, kn) - slope_rows * delta_n, NEG_BIG)
        m = jnp.maximum(jnp.max(sc, axis=-1, keepdims=True), jnp.max(sn, axis=-1, keepdims=True))
        ec = jnp.exp2(sc - m)
        en = jnp.exp2(sn - m)
        den = jnp.sum(ec, axis=-1, keepdims=True) + jnp.sum(en, axis=-1, keepdims=True)
        outs.append((_dot_nt(ec.astype(BF16), vt) + _dot(en.astype(BF16), vn)) * (1.0 / den))
        lses.append(m * LN_2 + jnp.log(den))
    mx = functools.reduce(jnp.maximum, lses)
    ws = [jnp.exp(l - mx) for l in lses]
    wsum = functools.reduce(jnp.add, ws)
    comb = sum((w / wsum) * o for w, o in zip(ws, outs))
    comb = jnp.where(own_head, comb, 0.0)
    o_tok = sum(comb[h * t_new:(h + 1) * t_new, :] for h in range(H_SLOT))
    o_ref[...] = o_tok.astype(o_ref.dtype)


def _attend_sample(q, k_new, v_new, caches_t, dbatch, t_new):
    assert t_new & (t_new - 1) == 0 and t_new <= 8
    nb = SAMPLE_SEQS_PER_STEP if dbatch % SAMPLE_SEQS_PER_STEP == 0 else 1
    tok = pl.BlockSpec((nb, t_new, B_WIDTH), lambda b: (b, 0, 0))
    cache_specs = [pl.BlockSpec((nb,) + c.shape[1:], lambda b: (b, 0, 0, 0, 0)) for c in caches_t]
    o = pl.pallas_call(
        functools.partial(_attn_sample_kernel, t_new=t_new, slopes=_alibi_slopes()),
        grid=(dbatch // nb,),
        in_specs=[tok, tok, tok] + cache_specs,
        out_specs=pl.BlockSpec((nb, t_new, GROUP_WIDTH), lambda b: (b, 0, 0)),
        out_shape=jax.ShapeDtypeStruct((dbatch, t_new, GROUP_WIDTH), BF16),
        compiler_params=_params(),
        name="attn_sample",
    )(q, k_new, v_new, *caches_t)
    return o.reshape(dbatch * t_new, GROUP_WIDTH)


def _kv_tail_kernel(k_ref, v_ref, o_ref):
    o_ref[0] = k_ref[...].T
    o_ref[1] = v_ref[...].T


def _kv_tail(k, v, gi, bsz, seq):
    keep = min(DILATION_GROUPS[gi][0], seq)
    pb = min(keep, 512)
    first = (seq - keep) // pb
    src = pl.BlockSpec((None, pb, GROUP_WIDTH), lambda b, j: (b, first + j, gi))
    return pl.pallas_call(
        _kv_tail_kernel,
        grid=(bsz, keep // pb),
        in_specs=[src, src],
        out_specs=pl.BlockSpec((None, 2, GROUP_WIDTH, pb), lambda b, j: (b, 0, 0, j)),
        out_shape=jax.ShapeDtypeStruct((bsz, 2, GROUP_WIDTH, keep), F32),
        compiler_params=_params(2),
        name=f"kv_tail_w{DILATION_GROUPS[gi][0]}",
    )(k, v)


def _kv_sample_kernel(*refs, t_new, dbatch):
    in_refs, out_refs = refs[:2 * N_GROUPS_B], refs[2 * N_GROUPS_B:]
    for g, o_ref in enumerate(out_refs):
        k_ref, v_ref = in_refs[2 * g], in_refs[2 * g + 1]
        for t in range(t_new):
            rows = pl.ds(t, dbatch, stride=t_new)
            o_ref[t, 0] = k_ref[rows, :].T
            o_ref[t, 1] = v_ref[rows, :].T


def _kv_sample(k, v, dbatch, t_new):
    src = lambda g: pl.BlockSpec((dbatch * t_new, LANES), lambda s: (0, g * PAIRS + s))
    return pl.pallas_call(
        functools.partial(_kv_sample_kernel, t_new=t_new, dbatch=dbatch),
        grid=(PAIRS,),
        in_specs=[src(g) for g in range(N_GROUPS_B) for _ in range(2)],
        out_specs=[pl.BlockSpec((t_new, 2, LANES, dbatch), lambda s: (0, 0, s, 0))] * N_GROUPS_B,
        out_shape=[jax.ShapeDtypeStruct((t_new, 2, GROUP_WIDTH, dbatch), F32)] * N_GROUPS_B,
        compiler_params=_params(),
        name="kv_sample",
    )(*[t for _ in range(N_GROUPS_B) for t in (k, v)])


def _topk_route(logits):
    tm = logits.shape[0]
    lane = lax.broadcasted_iota(jnp.int32, logits.shape, 1).astype(F32)
    slot = lax.broadcasted_iota(jnp.int32, (tm, TOP_K), 1)
    val_out = jnp.zeros((tm, TOP_K), F32)
    work = logits
    ids, top = [], None
    for r in range(TOP_K):
        mx = jnp.max(work, axis=-1, keepdims=True)
        ix = jnp.min(jnp.where(work == mx, lane, float(N_EXPERTS)), axis=-1, keepdims=True)
        top = mx if top is None else top
        ids.append(ix)
        val_out = jnp.where(slot == r, jnp.exp(mx - top), val_out)
        work = jnp.where(lane == ix, -jnp.inf, work)
    gates = val_out / jnp.sum(val_out, axis=-1, keepdims=True)
    return ids, gates


_FINISH_INPUTS = 17
_RANK_DIGITS = 3
_ROUTE_ROWS = TOP_K * (1 + _RANK_DIGITS)


def _finish_kernel(*refs, n_alias, n_valid):
    ins, outs = refs[:_FINISH_INPUTS], refs[_FINISH_INPUTS + n_alias:]
    step = pl.program_id(0)

    @pl.when(step < n_valid)
    def _():
        _finish_tile(ins, outs)

    @pl.when(step >= n_valid)
    def _():
        for ref in outs[:5]:
            ref[...] = jnp.zeros_like(ref)


def _finish_tile(ins, outs):
    (h_ref, u_ref, va_ref, wm_ref, bias_ref, ob_ref, ga_ref, gb_ref, wa_ref, wb_ref, wo_ref, gmoe_ref,
     wrh_ref, wrl_ref, br_ref, tri_ref, cnt_in_ref) = ins
    h1_ref, n2_ref, idx_ref, gate_ref, pos_ref, cnt_out_ref, carry_ref = outs
    tm = h_ref.shape[0]

    chunks = []
    for c in range(tm // CHUNK):
        rows = slice(c * CHUNK, (c + 1) * CHUNK)
        va_c = va_ref[rows, :].astype(BF16)
        mixed = jnp.concatenate(
            [_dot(wm_ref[g], va_c[:, g * A_GROUP_WIDTH:(g + 1) * A_GROUP_WIDTH]) for g in range(A_GROUPS)], axis=1)
        chunks.append((u_ref[rows, :].astype(F32) * (mixed + bias_ref[...])).astype(BF16))
    branch_a = _dot(jnp.concatenate(chunks, axis=0), wa_ref[...])

    branch_b = _dot(ob_ref[...], wb_ref[...])

    mix = ga_ref[...].astype(F32) * branch_a + gb_ref[...].astype(F32) * branch_b
    h1 = h_ref[...] + _dot(mix.astype(BF16), wo_ref[...])
    h1_ref[...] = h1

    n2 = (h1 * _rms(h1)) * gmoe_ref[...]
    n2_ref[...] = _pack_bf16_pairs(n2)
    n_hi = n2.astype(BF16)
    n_lo = (n2 - n_hi.astype(F32)).astype(BF16)
    logits = _dot(n_hi, wrh_ref[...]) + _dot(n_lo, wrh_ref[...]) + _dot(n_hi, wrl_ref[...]) + br_ref[...]
    ids, gates = _topk_route(logits)
    gate_ref[...] = gates

    @pl.when(pl.program_id(0) == 0)
    def _():
        carry_ref[...] = cnt_in_ref[...]

    lane = lax.broadcasted_iota(jnp.int32, logits.shape, 1).astype(F32)
    sel = sum(jnp.where(lane == ix, 1.0, 0.0) for ix in ids)
    before = _dot(tri_ref[...], sel.astype(BF16)) + carry_ref[...]
    ranks = [jnp.sum(jnp.where(lane == ix, before, 0.0), axis=-1, keepdims=True) for ix in ids]
    carry_ref[...] = carry_ref[...] + jnp.sum(sel, axis=0, keepdims=True)
    cnt_out_ref[...] = carry_ref[...]

    wide = lax.broadcasted_iota(jnp.int32, (tm, LANES), 1)
    cols = jnp.zeros((tm, LANES), F32)
    for r in range(TOP_K):
        rank_i = ranks[r].astype(jnp.int32)
        digits = [ids[r]] + [((rank_i >> (8 * d)) & 255).astype(F32) for d in range(_RANK_DIGITS)]
        for d, col in enumerate(digits):
            cols = jnp.where(wide == d * TOP_K + r, col, cols)
    eye = (lax.broadcasted_iota(jnp.int32, (_ROUTE_ROWS, LANES), 0)
           == lax.broadcasted_iota(jnp.int32, (_ROUTE_ROWS, LANES), 1)).astype(BF16)
    rows_t = _dot_nt(eye, cols.astype(BF16))
    idx_ref[...] = rows_t[:TOP_K].astype(jnp.int32)
    pos = sum(rows_t[(1 + d) * TOP_K:(2 + d) * TOP_K] * float(256 ** d) for d in range(_RANK_DIGITS))
    pos_ref[...] = pos.astype(jnp.int32)


def _finish(h, u, va, wm, bias_full, o_b, ga, gb, consts, cnt_in, n_total, tile_offset, prev_outs):
    n_tok = h.shape[0]
    tm = FINISH_TILE
    n_valid = n_tok // tm
    n_steps = n_valid if prev_outs is not None else n_total // tm - tile_offset
    row = lambda w: pl.BlockSpec((tm, w), lambda i: (jnp.minimum(i, n_valid - 1), 0))
    orow = lambda w: pl.BlockSpec((tm, w), lambda i: (i + tile_offset, 0))
    args = [h, u, va, wm, bias_full, o_b, ga, gb] + list(consts) + [cnt_in]
    specs = [row(D_MODEL), row(A_WIDTH), row(A_WIDTH), _resident(wm.shape), _resident(bias_full.shape),
             row(GROUP_WIDTH), row(D_MODEL), row(D_MODEL)]
    specs += [_resident(t.shape) for t in consts] + [_resident(cnt_in.shape)]
    assert len(args) == _FINISH_INPUTS
    aliases = {}
    if prev_outs is not None:
        for k, t in enumerate(prev_outs):
            aliases[len(args)] = k
            args.append(t)
            specs.append(pl.BlockSpec(memory_space=pl.ANY))
    by_choice = pl.BlockSpec((TOP_K, tm), lambda i: (0, i + tile_offset))
    tokens = lambda w, dt: jax.ShapeDtypeStruct((n_total, w), dt)
    choices = jax.ShapeDtypeStruct((TOP_K, n_total), jnp.int32)
    outs = pl.pallas_call(
        functools.partial(_finish_kernel, n_alias=len(aliases), n_valid=n_valid),
        grid=(n_steps,),
        in_specs=specs,
        out_specs=[orow(D_MODEL), orow(D_MODEL // 2), by_choice, orow(TOP_K), by_choice, _resident(cnt_in.shape)],
        out_shape=[tokens(D_MODEL, F32), tokens(D_MODEL // 2, jnp.int32), choices, tokens(TOP_K, F32), choices,
                   jax.ShapeDtypeStruct(cnt_in.shape, F32)],
        scratch_shapes=[pltpu.VMEM(cnt_in.shape, F32)],
        input_output_aliases=aliases,
        compiler_params=_params(),
        name="finish",
    )(*args)
    return outs[:5], outs[5]


def _sc_gather_rows(table, idx):
    m = idx.shape[0]
    width = table.shape[1]
    per_worker = m // SC_WORKERS
    n_chunks = per_worker // SC_ROWS
    mesh = plsc.VectorSubcoreMesh(core_axis_name="c", subcore_axis_name="s",
                                  num_cores=SC_CORES, num_subcores=SC_SUBCORES)

    assert n_chunks % 2 == 0

    @functools.partial(
        pl.kernel, mesh=mesh,
        out_type=jax.ShapeDtypeStruct((m, width), table.dtype),
        scratch_types=[pltpu.VMEM((n_chunks, SC_ROWS), jnp.int32),
                       pltpu.VMEM((SC_ROWS, width), table.dtype),
                       pltpu.VMEM((SC_ROWS, width), table.dtype),
                       pltpu.SemaphoreType.DMA,
                       pltpu.SemaphoreType.DMA],
        name="sc_gather_rows",
    )
    def gather(table_hbm, idx_hbm, out_hbm, idx_v, rows_a, rows_b, sem_a, sem_b):
        wid = lax.axis_index("s") * SC_CORES + lax.axis_index("c")
        base = wid * per_worker
        pltpu.sync_copy(idx_hbm.at[wid], idx_v)

        def fetch(c, rows, sem):
            return pltpu.make_async_copy(table_hbm.at[idx_v.at[c]], rows, sem)

        def put(c, rows):
            off = pl.multiple_of(base + c * SC_ROWS, SC_ROWS)
            pltpu.sync_copy(rows, out_hbm.at[pl.ds(off, SC_ROWS)])

        fetch(0, rows_a, sem_a).start()

        @pl.loop(0, n_chunks, step=2)
        def _(c):
            fetch(c, rows_a, sem_a).wait()
            fetch(c + 1, rows_b, sem_b).start()
            put(c, rows_a)
            fetch(c + 1, rows_b, sem_b).wait()

            @pl.when(c + 2 < n_chunks)
            def _():
                fetch(c + 2, rows_a, sem_a).start()

            put(c + 1, rows_b)

    return gather(table, idx.reshape(SC_WORKERS, n_chunks, SC_ROWS))


def _sc_scatter_rows(src, dest_t, n_out):
    n_src, width = src.shape
    top_k = dest_t.shape[0]
    per_worker = n_src // SC_WORKERS
    rows = SC_SCATTER_ROWS
    n_chunks = per_worker // rows
    assert per_worker % rows == 0 and n_chunks % 2 == 0
    dest = dest_t.reshape(top_k, SC_WORKERS, n_chunks, rows).transpose(1, 2, 0, 3)
    mesh = plsc.VectorSubcoreMesh(core_axis_name="c", subcore_axis_name="s",
                                  num_cores=SC_CORES, num_subcores=SC_SUBCORES)

    @functools.partial(
        pl.kernel, mesh=mesh,
        out_type=jax.ShapeDtypeStruct((n_out, width), src.dtype),
        scratch_types=[pltpu.VMEM((n_chunks, top_k, rows), jnp.int32),
                       pltpu.VMEM((rows, width), src.dtype),
                       pltpu.VMEM((rows, width), src.dtype),
                       pltpu.SemaphoreType.DMA,
                       pltpu.SemaphoreType.DMA,
                       pltpu.SemaphoreType.DMA],
        name="sc_scatter_rows",
    )
    def scatter(src_hbm, dest_hbm, out_hbm, idx_v, rows_a, rows_b, sem_a, sem_b, sem_w):
        wid = lax.axis_index("s") * SC_CORES + lax.axis_index("c")
        base = wid * per_worker
        pltpu.sync_copy(dest_hbm.at[wid], idx_v)

        def load(c, buf, sem):
            off = pl.multiple_of(base + c * rows, rows)
            return pltpu.make_async_copy(src_hbm.at[pl.ds(off, rows)], buf, sem)

        def spread(c, buf):
            copies = [pltpu.make_async_copy(buf, out_hbm.at[idx_v.at[c, k]], sem_w) for k in range(top_k)]
            for cp in copies:
                cp.start()
            for cp in copies:
                cp.wait()

        load(0, rows_a, sem_a).start()

        @pl.loop(0, n_chunks, step=2)
        def _(c):
            load(c, rows_a, sem_a).wait()
            load(c + 1, rows_b, sem_b).start()
            spread(c, rows_a)
            load(c + 1, rows_b, sem_b).wait()

            @pl.when(c + 2 < n_chunks)
            def _():
                load(c + 2, rows_a, sem_a).start()

            spread(c + 1, rows_b)

    return scatter(src, dest)


def _expert_kernel(be_ref, slot_ref, next_ref, nused_ref, x_ref, wgu_hbm, wd_hbm, bg_ref, bl_ref, bd_ref, sel_ref, y_ref,
                   wgu_f, wd_f, wg_s, wl_s, wd_s, sems):
    i = pl.program_id(0)
    active = i < nused_ref[0]
    expert = be_ref[i]
    fresh = (i == 0) | (expert != be_ref[jnp.maximum(i - 1, 0)])

    def fetch(e, slot):
        return (pltpu.make_async_copy(wgu_hbm.at[e], wgu_f.at[slot], sems.at[slot, 0]),
                pltpu.make_async_copy(wd_hbm.at[e], wd_f.at[slot], sems.at[slot, 1]))

    @pl.when(active & fresh)
    def _():
        slot = slot_ref[i]

        @pl.when(i == 0)
        def _():
            for cp in fetch(expert, slot):
                cp.start()

        for cp in fetch(expert, slot):
            cp.wait()
        nxt = next_ref[i]

        @pl.when(nxt >= 0)
        def _():
            for cp in fetch(nxt, 1 - slot):
                cp.start()

        half = MXU_DIM // 2
        for t in range(2 * D_FF // MXU_DIM):
            src = wgu_f[slot, :, t * MXU_DIM:(t + 1) * MXU_DIM].astype(BF16)
            both = _dot(src, sel_ref[...]).astype(BF16)
            wg_s[:, t * half:(t + 1) * half] = both[:, :half]
            wl_s[:, t * half:(t + 1) * half] = both[:, half:]
        wd_s[...] = wd_f[slot].astype(BF16)

    @pl.when(active)
    def _():
        x = _unpack_bf16_pairs(x_ref[...]).astype(BF16)
        h_glu = jnp.minimum(_dot(x, wg_s[...]) + bg_ref[...], SWIGLU_LIMIT)
        h_lin = jnp.clip(_dot(x, wl_s[...]) + bl_ref[...], -SWIGLU_LIMIT, SWIGLU_LIMIT)
        act = h_glu * _sigmoid(SWIGLU_ALPHA * h_glu) * (h_lin + 1.0)
        y_ref[...] = _pack_bf16_pairs(_dot(act.astype(BF16), wd_s[...]) + bd_ref[...])

    @pl.when(jnp.logical_not(active))
    def _():
        y_ref[...] = jnp.zeros_like(y_ref)


def _experts(xb, block_expert, block_slot, block_next, n_used, w_gate_up, w_down, b_glu, b_lin, b_down, sel):
    n_slots = xb.shape[0]
    n_blocks = n_slots // EXPERT_BLOCK
    by_expert = lambda k, n: pl.BlockSpec((None, k, n), lambda i, be, sl, nx, nu: (be[i], 0, 0))
    blk = pl.BlockSpec((EXPERT_BLOCK, D_MODEL // 2), lambda i, be, sl, nx, nu: (i, 0))
    hbm = pl.BlockSpec(memory_space=pl.ANY)
    return pl.pallas_call(
        _expert_kernel,
        grid_spec=pltpu.PrefetchScalarGridSpec(
            num_scalar_prefetch=4, grid=(n_blocks,),
            in_specs=[blk, hbm, hbm, by_expert(1, D_FF), by_expert(1, D_FF), by_expert(1, D_MODEL),
                      pl.BlockSpec(sel.shape, lambda i, be, sl, nx, nu: (0, 0), pipeline_mode=pl.Buffered(1))],
            out_specs=blk,
            scratch_shapes=[pltpu.VMEM((2, D_MODEL, 2 * D_FF), F32), pltpu.VMEM((2, D_FF, D_MODEL), F32),
                            pltpu.VMEM((D_MODEL, D_FF), BF16), pltpu.VMEM((D_MODEL, D_FF), BF16),
                            pltpu.VMEM((D_FF, D_MODEL), BF16), pltpu.SemaphoreType.DMA((2, 2))]),
        out_shape=jax.ShapeDtypeStruct((n_slots, D_MODEL // 2), jnp.int32),
        compiler_params=_params(),
        name="experts",
    )(block_expert, block_slot, block_next, n_used, xb, w_gate_up, w_down, b_glu, b_lin, b_down, sel)


def _final_kernel(h1_ref, yg_ref, gate_ref, p_ref, gple_ref, wg_ref, wp_ref, out_ref):
    h2 = h1_ref[...]
    gates = gate_ref[...]
    for k in range(TOP_K):
        h2 = h2 + gates[:, k:k + 1] * _unpack_bf16_pairs(yg_ref[k])
    n3 = ((h2 * _rms(h2)) * gple_ref[...]).astype(BF16)
    gate = _sigmoid(_dot(n3, wg_ref[...]))
    out_ref[...] = h2 + gate * _dot(p_ref[...].astype(BF16), wp_ref[...])


def _final(h1, yg, gates, p, tile_offset, g_ple, w_ple_gate, w_ple_proj):
    n_tok = p.shape[0]
    tm = TOKEN_TILE
    return pl.pallas_call(
        _final_kernel,
        grid=(n_tok // tm,),
        in_specs=[pl.BlockSpec((tm, D_MODEL), lambda i: (i + tile_offset, 0)),
                  pl.BlockSpec((TOP_K, tm, D_MODEL // 2), lambda i: (0, i + tile_offset, 0)),
                  pl.BlockSpec((tm, TOP_K), lambda i: (i + tile_offset, 0)),
                  pl.BlockSpec((tm, PLE_DIM), lambda i: (i, 0)),
                  _resident(g_ple.shape), _resident(w_ple_gate.shape), _resident(w_ple_proj.shape)],
        out_specs=pl.BlockSpec((tm, D_MODEL), lambda i: (i, 0)),
        out_shape=jax.ShapeDtypeStruct((n_tok, D_MODEL), F32),
        compiler_params=_params(),
        name="final",
    )(h1, yg, gates, p, g_ple, w_ple_gate, w_ple_proj)


def _routing_tables(idx_t, pos_t, counts, n_slots):
    counts = counts.reshape(N_EXPERTS).astype(jnp.int32)
    pcounts = (counts + EXPERT_BLOCK - 1) // EXPERT_BLOCK * EXPERT_BLOCK
    pends = jnp.cumsum(pcounts)
    pstarts = pends - pcounts
    experts = jnp.arange(N_EXPERTS, dtype=jnp.int32)
    start_t = jnp.sum(jnp.where(idx_t[None] == experts[:, None, None], pstarts[:, None, None], 0), axis=0)
    dest_t = (start_t + pos_t).astype(jnp.int32)
    n_blocks = n_slots // EXPERT_BLOCK
    block_start = jnp.arange(n_blocks, dtype=jnp.int32) * EXPERT_BLOCK
    block_expert = jnp.minimum(jnp.sum(block_start[:, None] >= pends[None, :], axis=1), N_EXPERTS - 1).astype(jnp.int32)
    n_used = (pends[-1] // EXPERT_BLOCK).astype(jnp.int32).reshape(1)
    used = counts > 0
    slot_e = (jnp.cumsum(used.astype(jnp.int32)) - 1) & 1
    later_used = used[None, :] & (experts[None, :] > experts[:, None])
    next_e = jnp.min(jnp.where(later_used, experts[None, :], N_EXPERTS), axis=1)
    next_e = jnp.where(next_e == N_EXPERTS, -1, next_e)
    of_block = block_expert[:, None] == experts[None, :]
    block_slot = jnp.sum(jnp.where(of_block, slot_e[None, :], 0), axis=1).astype(jnp.int32)
    block_next = jnp.sum(jnp.where(of_block, next_e[None, :], 0), axis=1).astype(jnp.int32)
    return dest_t, block_expert, block_slot, block_next, n_used


def kernel(x_prompt, x_sample, cache_kv_w128, cache_kv_w512, cache_kv_w2048, p_prompt, p_sample, g_mix, w_in, g_v, g_q, g_k, w_spatial, b_spatial, w_branch_a, w_branch_b, w_out, g_moe, w_router, b_router, w_gate_up, b_gate_up, w_down, b_down, g_ple, w_ple_gate, w_ple_proj):
    bsz, seq, _ = x_prompt.shape
    dbatch, t_new, _ = x_sample.shape
    assert g_mix.shape[0] == 1
    caches = (cache_kv_w128, cache_kv_w512, cache_kv_w2048)
    l = 0
    n_p, n_s = bsz * seq, dbatch * t_new
    n_tok = n_p + n_s
    assert n_p % TOKEN_TILE == 0 and n_s % TOKEN_TILE == 0

    row2 = lambda t: t.reshape(1, -1)
    w_in_bf = w_in[l].astype(BF16)
    g_q_t = jnp.tile(g_q[l], B_HEADS).reshape(1, B_WIDTH)
    g_k_t = jnp.tile(g_k[l], B_HEADS).reshape(1, B_WIDTH)
    hid = np.arange(MXU_DIM) // HEAD_DIM
    hsum = jnp.asarray(hid[:, None] == hid[None, :], BF16)
    tril = jnp.tril(jnp.ones((CHUNK, CHUNK), bool))
    wm_prompt = jnp.where(tril[None], w_spatial[l], 0).astype(BF16)
    bias_prompt = jnp.repeat(b_spatial[l].T, A_GROUP_WIDTH, axis=1)
    reps = CHUNK // t_new
    small = jnp.where(tril[None, :t_new, :t_new], w_spatial[l][:, :t_new, :t_new], 0)
    wm_sample = jnp.einsum("ab,gij->gaibj", jnp.eye(reps, dtype=F32), small).reshape(A_GROUPS, CHUNK, CHUNK).astype(BF16)
    bias_sample = jnp.tile(bias_prompt[:t_new], (reps, 1))
    wr_hi = w_router[l].astype(BF16)
    wr_lo = (w_router[l] - wr_hi.astype(F32)).astype(BF16)
    tri = jnp.asarray(np.tril(np.ones((FINISH_TILE, FINISH_TILE), np.float32), -1), BF16)
    consts = (w_branch_a[l].astype(BF16), w_branch_b[l].astype(BF16), w_out[l].astype(BF16), row2(g_moe[l]),
              wr_hi, wr_lo, row2(b_router[l]), tri)
    half = MXU_DIM // 2
    sel_np = np.zeros((MXU_DIM, MXU_DIM), np.float32)
    sel_np[2 * np.arange(half), np.arange(half)] = 1.0
    sel_np[2 * np.arange(half) + 1, half + np.arange(half)] = 1.0
    sel = jnp.asarray(sel_np, BF16)

    proj = functools.partial(_project, g_mix=row2(g_mix[l]), w_in_bf=w_in_bf, g_v=row2(g_v[l]),
                             g_q_t=g_q_t, g_k_t=g_k_t, hsum=hsum)

    xp = x_prompt.reshape(n_p, D_MODEL)
    u_p, va_p, q_p, k_p, v_p, ga_p, gb_p = proj(xp, va_dtype=BF16)
    seq3 = lambda t: t.reshape(bsz, seq, B_WIDTH)
    attn_p = [_attend_prompt(seq3(q_p), seq3(k_p), seq3(v_p), gi, bsz, seq) for gi in range(N_GROUPS_B - 1)]
    o_p = _attend_prompt(seq3(q_p), seq3(k_p), seq3(v_p), N_GROUPS_B - 1, bsz, seq, others=attn_p)
    zero_counts = jnp.zeros((1, N_EXPERTS), F32)
    outs_p, cnt_p = _finish(xp, u_p, va_p, wm_prompt, bias_prompt, o_p, ga_p, gb_p, consts, zero_counts,
                            n_tok, 0, None)

    xs = x_sample.reshape(n_s, D_MODEL)
    u_s, va_s, q_s, k_s, v_s, ga_s, gb_s = proj(xs, va_dtype=F32)
    tok3 = lambda t: t.reshape(dbatch, t_new, B_WIDTH)
    caches_t = [jnp.transpose(c[l], (0, 2, 3, 4, 1)) for c in caches]
    o_s = _attend_sample(tok3(q_s), tok3(k_s), tok3(v_s), caches_t, dbatch, t_new)
    (h1, n2, idx_t, gates, pos_t), counts = _finish(xs, u_s, va_s, wm_sample, bias_sample, o_s, ga_s, gb_s,
                                                  consts, cnt_p, n_tok, n_p // FINISH_TILE, outs_p)

    n_blocks = -(-n_tok * TOP_K // EXPERT_BLOCK) + N_EXPERTS
    n_slots = n_blocks * EXPERT_BLOCK
    dest_t, block_expert, block_slot, block_next, n_used = _routing_tables(idx_t, pos_t, counts, n_slots)
    xb = _sc_scatter_rows(n2, dest_t, n_slots)
    yb = _experts(xb, block_expert, block_slot, block_next, n_used, w_gate_up[l], w_down[l],
                  b_gate_up[l][:, None, 0::2], b_gate_up[l][:, None, 1::2], b_down[l][:, None, :], sel)
    yg = _sc_gather_rows(yb, dest_t.reshape(-1)).reshape(TOP_K, n_tok, D_MODEL // 2)
    fin = functools.partial(_final, g_ple=row2(g_ple[l]), w_ple_gate=w_ple_gate[l].astype(BF16),
                            w_ple_proj=w_ple_proj[l].astype(BF16))
    y_prompt = fin(h1, yg, gates, p_prompt[l].reshape(n_p, PLE_DIM), 0).reshape(bsz, seq, D_MODEL)
    y_sample = fin(h1, yg, gates, p_sample[l].reshape(n_s, PLE_DIM), n_p // TOKEN_TILE).reshape(dbatch, t_new, D_MODEL)

    kv_prompt = []
    for gi in range(N_GROUPS_B):
        t = _kv_tail(seq3(k_p), seq3(v_p), gi, bsz, seq)
        keep = t.shape[-1]
        kv_prompt.append(jnp.transpose(t.reshape(bsz, 2, H_SLOT, HEAD_DIM, keep), (0, 4, 1, 2, 3))[None])
    kvs = _kv_sample(k_s, v_s, dbatch, t_new)
    kv_sample = [jnp.transpose(kvs[gi].reshape(t_new, 2, H_SLOT, HEAD_DIM, dbatch), (4, 0, 1, 2, 3))[None]
                 for gi in range(N_GROUPS_B)]
    va_out = va_s.reshape(1, dbatch, t_new, A_WIDTH)
    return (y_prompt, y_sample, *kv_prompt, *kv_sample, va_out)
```

```python
import functools

import numpy as np
import jax
import jax.numpy as jnp
from jax import lax
from jax.experimental import pallas as pl
from jax.experimental.pallas import tpu as pltpu
from jax.experimental.pallas import tpu_sc as plsc

F32 = jnp.float32
BF16 = jnp.bfloat16

D_MODEL = 1024
A_WIDTH = 1024
A_GROUPS = 4
A_GROUP_WIDTH = A_WIDTH // A_GROUPS
CHUNK = 128
HEAD_DIM = 64
H_SLOT = 8
GROUP_WIDTH = H_SLOT * HEAD_DIM
DILATION_GROUPS = ((128, 1), (512, 4), (2048, 16))
N_GROUPS_B = len(DILATION_GROUPS)
B_HEADS = H_SLOT * N_GROUPS_B
B_WIDTH = B_HEADS * HEAD_DIM
N_EXPERTS = 32
TOP_K = 4
D_FF = 1024
SWIGLU_ALPHA = 1.702
SWIGLU_LIMIT = 7.0
PLE_DIM = 256
RMS_EPS = 1e-6
NEG_BIG = -1e30
LOG2_E = float(np.log2(np.e))
LN_2 = float(np.log(2.0))

LANES = 128
SUBLANES = 8
MXU_DIM = 256
HEADS_PER_VREG = LANES // HEAD_DIM
PAIRS = GROUP_WIDTH // LANES
RESIDUE_UNROLL = 8
SAMPLE_SEQS_PER_STEP = 2
ATTN_STEP_ROWS = 1024
TOKEN_TILE = 512
FINISH_TILE = 512
EXPERT_BLOCK = 256
VMEM_LIMIT = 56 * 1024 * 1024

SC_CORES = 2
SC_SUBCORES = 16
SC_WORKERS = SC_CORES * SC_SUBCORES
SC_ROWS = 64
SC_SCATTER_ROWS = 32

_COL_SPLITS = np.cumsum([0, A_WIDTH, A_WIDTH, B_WIDTH, B_WIDTH, B_WIDTH, D_MODEL, D_MODEL]).tolist()


def _alibi_slopes():
    return np.exp2(-8.0 * np.arange(1, B_HEADS + 1, dtype=np.float32) / B_HEADS).astype(np.float32)


def _sigmoid(x):
    return 1.0 / (1.0 + jnp.exp(-x))


def _rms(x):
    return lax.rsqrt(jnp.mean(x * x, axis=-1, keepdims=True) + RMS_EPS)


def _dot(a, b):
    return jnp.dot(a, b, preferred_element_type=F32)


def _dot_nt(a, b):
    return lax.dot_general(a, b, (((1,), (1,)), ((), ())), preferred_element_type=F32)


def _pack_bf16_pairs(x):
    w = x.shape[1] // 2
    lo = lax.bitcast_convert_type(x[:, :w].astype(BF16).astype(F32), jnp.uint32) >> 16
    hi = lax.bitcast_convert_type(x[:, w:].astype(BF16).astype(F32), jnp.uint32) & jnp.uint32(0xFFFF0000)
    return lax.bitcast_convert_type(lo | hi, jnp.int32)


def _unpack_bf16_pairs(p):
    u = lax.bitcast_convert_type(p, jnp.uint32)
    lo = lax.bitcast_convert_type(u << 16, F32)
    hi = lax.bitcast_convert_type(u & jnp.uint32(0xFFFF0000), F32)
    return jnp.concatenate([lo, hi], axis=1)


def _resident(shape):
    nd = len(shape)
    return pl.BlockSpec(shape, lambda *_: (0,) * nd, pipeline_mode=pl.Buffered(1))


def _params(n_axes=1):
    return pltpu.CompilerParams(dimension_semantics=("arbitrary",) * n_axes, vmem_limit_bytes=VMEM_LIMIT)


def _proj_kernel(x_ref, gmix_ref, w_ref, gv_ref, gq_ref, gk_ref, hsum_ref,
                 u_ref, va_ref, q_ref, k_ref, v_ref, ga_ref, gb_ref):
    x = x_ref[...]
    n = ((x * _rms(x)) * gmix_ref[...]).astype(BF16)

    def section(i):
        return _dot(n, w_ref[:, _COL_SPLITS[i]:_COL_SPLITS[i + 1]])

    u_ref[...] = jax.nn.gelu(section(0)).astype(u_ref.dtype)
    va = jax.nn.gelu(section(1))
    va_ref[...] = ((va * _rms(va)) * gv_ref[...]).astype(va_ref.dtype)

    def head_norm(z, g_ref, scale):
        parts = []
        for c in range(B_WIDTH // MXU_DIM):
            zc = z[:, c * MXU_DIM:(c + 1) * MXU_DIM]
            ss = _dot((zc * zc).astype(BF16), hsum_ref[...])
            parts.append(zc * lax.rsqrt(ss * (1.0 / HEAD_DIM) + RMS_EPS))
        return jnp.concatenate(parts, axis=1) * (g_ref[...] * scale)

    q_ref[...] = head_norm(section(2), gq_ref, HEAD_DIM ** -0.5 * LOG2_E)
    k_ref[...] = head_norm(section(3), gk_ref, 1.0)
    v_ref[...] = section(4)
    ga_ref[...] = _sigmoid(section(5)).astype(ga_ref.dtype)
    gb_ref[...] = _sigmoid(section(6)).astype(gb_ref.dtype)


def _project(x, g_mix, w_in_bf, g_v, g_q_t, g_k_t, hsum, va_dtype):
    n_tok = x.shape[0]
    tm = TOKEN_TILE
    row = lambda w: pl.BlockSpec((tm, w), lambda i: (i, 0))
    outs = [(A_WIDTH, BF16), (A_WIDTH, va_dtype), (B_WIDTH, F32), (B_WIDTH, F32), (B_WIDTH, F32),
            (D_MODEL, BF16), (D_MODEL, BF16)]
    return pl.pallas_call(
        _proj_kernel,
        grid=(n_tok // tm,),
        in_specs=[row(D_MODEL), _resident(g_mix.shape), _resident(w_in_bf.shape), _resident(g_v.shape),
                  _resident(g_q_t.shape), _resident(g_k_t.shape), _resident(hsum.shape)],
        out_specs=[row(w) for w, _ in outs],
        out_shape=[jax.ShapeDtypeStruct((n_tok, w), dt) for w, dt in outs],
        compiler_params=_params(),
        name="project",
    )(x, g_mix, w_in_bf, g_v, g_q_t, g_k_t, hsum)


def _band_bias(band, dil, slopes):
    qi = jnp.arange(band, dtype=jnp.int32)[:, None]
    kj = jnp.arange(2 * band, dtype=jnp.int32)[None, :]
    dist = qi + band - kj
    in_band = (dist >= 0) & (dist <= band)
    valid = jnp.stack([in_band & (kj >= band), in_band])
    penalty = (jnp.asarray(slopes, F32) * LOG2_E)[:, None, None] * (dist * dil).astype(F32)[None]
    return jnp.where(valid[:, None], -penalty[None], NEG_BIG)


def _pair_attention(q2, k, v, bias_pair):
    band = q2.shape[0]
    first = lax.broadcasted_iota(jnp.int32, (band, LANES), 1) < HEAD_DIM
    qs = jnp.concatenate([jnp.where(first, q2, 0.0), jnp.where(first, 0.0, q2)], axis=0).astype(BF16)
    s = _dot_nt(qs, k) + bias_pair.reshape(HEADS_PER_VREG * band, 2 * band)
    m = jnp.max(s, axis=-1, keepdims=True)
    e = jnp.exp2(s - m).astype(BF16)
    v_ones = jnp.concatenate([v, jnp.ones_like(v)], axis=1)
    r = _dot(e, v_ones)
    den = r[:, LANES:]
    o2 = r[:, :LANES] / den
    lse = m * LN_2 + jnp.log(den)
    return jnp.where(first, o2[:band], o2[band:]), jnp.where(first, lse[:band], lse[band:])


def _attn_dense_kernel(q_ref, kp_ref, kc_ref, vp_ref, vc_ref, bias0_ref, bias_ref, o_ref, l_ref, *, band, n_sub):
    for i in range(n_sub):
        rows = slice(i * band, (i + 1) * band)
        before = slice((i - 1) * band, i * band)
        for p in range(PAIRS):
            cols = slice(p * LANES, (p + 1) * LANES)
            k_prev = kc_ref[before, cols] if i else kp_ref[:, cols]
            v_prev = vc_ref[before, cols] if i else vp_ref[:, cols]
            k = jnp.concatenate([k_prev, kc_ref[rows, cols]], axis=0).astype(BF16)
            v = jnp.concatenate([v_prev, vc_ref[rows, cols]], axis=0).astype(BF16)
            table = bias_ref if i else bias0_ref
            bias2 = table[p * HEADS_PER_VREG:(p + 1) * HEADS_PER_VREG]
            o_pair, l_pair = _pair_attention(q_ref[rows, cols], k, v, bias2)
            o_ref[rows, cols] = o_pair.astype(o_ref.dtype)
            l_ref[rows, cols] = l_pair


def _attn_dilated_kernel(q_ref, kp_ref, kc_ref, vp_ref, vc_ref, bias0_ref, bias_ref, *rest,
                         band, dil, n_sub, n_others):
    others, rest = rest[:2 * n_others], rest[2 * n_others:]
    if n_others:
        out_ref, o_ref, l_ref = rest
    else:
        o_ref, l_ref = rest
    win = band * dil
    unroll = min(RESIDUE_UNROLL, dil)

    def residues(it, carry):
        for i in range(n_sub):
            for u in range(unroll):
                r = it * unroll + u
                rows = pl.ds(i * win + r, band, stride=dil)
                before = pl.ds((i - 1) * win + r, band, stride=dil) if i else pl.ds(r, band, stride=dil)
                k_prev = kc_ref[before, :] if i else kp_ref[before, :]
                v_prev = vc_ref[before, :] if i else vp_ref[before, :]
                k = jnp.concatenate([k_prev, kc_ref[rows, :]], axis=0).astype(BF16)
                v = jnp.concatenate([v_prev, vc_ref[rows, :]], axis=0).astype(BF16)
                table = bias_ref if i else bias0_ref
                o_pair, l_pair = _pair_attention(q_ref[rows, :], k, v, table[...])
                o_ref[rows, :] = o_pair
                l_ref[rows, :] = l_pair
        return carry

    lax.fori_loop(0, dil // unroll, residues, 0)

    if n_others:
        outs = [others[2 * g][...].astype(F32) for g in range(n_others)] + [o_ref[...]]
        lses = [others[2 * g + 1][...] for g in range(n_others)] + [l_ref[...]]
        mx = functools.reduce(jnp.maximum, lses)
        ws = [jnp.exp(l - mx) for l in lses]
        wsum = functools.reduce(jnp.add, ws)
        out_ref[...] = (sum(w * o for w, o in zip(ws, outs)) / wsum).astype(out_ref.dtype)


def _attend_prompt(q, k, v, gi, bsz, seq, others=()):
    win, dil = DILATION_GROUPS[gi]
    band = win // dil
    bias = _band_bias(band, dil, _alibi_slopes()[gi * H_SLOT:(gi + 1) * H_SLOT])
    step_rows = ATTN_STEP_ROWS if dil == 1 else 4 * ATTN_STEP_ROWS
    n_sub = max(1, min(step_rows // win, seq // win))
    step = n_sub * win
    if dil == 1:
        cur = pl.BlockSpec((None, step, GROUP_WIDTH), lambda b, j: (b, j, gi))
        prev = pl.BlockSpec((None, win, GROUP_WIDTH), lambda b, j: (b, jnp.maximum(j * n_sub - 1, 0), gi))
        tab0 = pl.BlockSpec((None, H_SLOT, band, 2 * band), lambda b, j: (jnp.minimum(j, 1), 0, 0, 0))
        tab = pl.BlockSpec((None, H_SLOT, band, 2 * band), lambda b, j: (1, 0, 0, 0))
        out = pl.BlockSpec((None, step, GROUP_WIDTH), lambda b, j: (b, j, 0))
        o, lse = pl.pallas_call(
            functools.partial(_attn_dense_kernel, band=band, n_sub=n_sub),
            grid=(bsz, seq // step),
            in_specs=[cur, prev, cur, prev, cur, tab0, tab],
            out_specs=[out, out],
            out_shape=[jax.ShapeDtypeStruct((bsz, seq, GROUP_WIDTH), BF16),
                       jax.ShapeDtypeStruct((bsz, seq, GROUP_WIDTH), F32)],
            compiler_params=_params(2),
            name=f"attn_prompt_w{win}",
        )(q, k, k, v, v, bias, bias)
    else:
        col = lambda p: gi * PAIRS + p
        cur = pl.BlockSpec((None, step, LANES), lambda b, j, p: (b, j, col(p)))
        prev = pl.BlockSpec((None, win, LANES), lambda b, j, p: (b, jnp.maximum(j * n_sub - 1, 0), col(p)))
        tab0 = pl.BlockSpec((None, HEADS_PER_VREG, band, 2 * band), lambda b, j, p: (jnp.minimum(j, 1), p, 0, 0))
        tab = pl.BlockSpec((None, HEADS_PER_VREG, band, 2 * band), lambda b, j, p: (1, p, 0, 0))
        out = pl.BlockSpec((None, step, LANES), lambda b, j, p: (b, j, p))
        kern = functools.partial(_attn_dilated_kernel, band=band, dil=dil, n_sub=n_sub, n_others=len(others))
        common = dict(grid=(bsz, seq // step, PAIRS), compiler_params=_params(3), name=f"attn_prompt_w{win}")
        ins = [cur, prev, cur, prev, cur, tab0, tab]
        if others:
            extra = [t.reshape(bsz, seq, GROUP_WIDTH) for pair in others for t in pair]
            combined = pl.pallas_call(
                kern,
                in_specs=ins + [out] * len(extra),
                out_specs=out,
                out_shape=jax.ShapeDtypeStruct((bsz, seq, GROUP_WIDTH), BF16),
                scratch_shapes=[pltpu.VMEM((step, LANES), F32), pltpu.VMEM((step, LANES), F32)],
                **common,
            )(q, k, k, v, v, bias, bias, *extra)
            return combined.reshape(bsz * seq, GROUP_WIDTH)
        o, lse = pl.pallas_call(
            kern,
            in_specs=ins,
            out_specs=[out, out],
            out_shape=[jax.ShapeDtypeStruct((bsz, seq, GROUP_WIDTH), F32)] * 2,
            **common,
        )(q, k, k, v, v, bias, bias)
    return o.reshape(bsz * seq, GROUP_WIDTH), lse.reshape(bsz * seq, GROUP_WIDTH)


def _attn_sample_kernel(q_ref, kn_ref, vn_ref, c0_ref, c1_ref, c2_ref, o_ref, *, t_new, slopes):
    for b in range(q_ref.shape[0]):
        _attn_sample_one(q_ref.at[b], kn_ref.at[b], vn_ref.at[b], c0_ref.at[b], c1_ref.at[b], c2_ref.at[b],
                         o_ref.at[b], t_new=t_new, slopes=slopes)


def _attn_sample_one(q_ref, kn_ref, vn_ref, c0_ref, c1_ref, c2_ref, o_ref, *, t_new, slopes):
    rows = H_SLOT * t_new
    pad_new = 16
    t_shift = t_new.bit_length() - 1
    caches = (c0_ref, c1_ref, c2_ref)
    row_id = lax.broadcasted_iota(jnp.int32, (rows, GROUP_WIDTH), 0)
    col_id = lax.broadcasted_iota(jnp.int32, (rows, GROUP_WIDTH), 1)
    own_head = (row_id >> t_shift) == (col_id >> (HEAD_DIM.bit_length() - 1))
    head_of_row = lax.broadcasted_iota(jnp.int32, (rows, 1), 0) >> t_shift
    zpad = jnp.zeros((pad_new - t_new, GROUP_WIDTH), F32)
    outs, lses = [], []
    for g, (win, dil) in enumerate(DILATION_GROUPS):
        buf_len = caches[g].shape[-1]
        cols = slice(g * GROUP_WIDTH, (g + 1) * GROUP_WIDTH)

        def mask_of(n_keys, first_pos, n_real):
            i_q = lax.broadcasted_iota(jnp.int32, (rows, n_keys), 0) & (t_new - 1)
            key = lax.broadcasted_iota(jnp.int32, (rows, n_keys), 1)
            delta = buf_len + i_q - (first_pos + key)
            ok = (delta >= 0) & ((delta & (dil - 1)) == 0) & (delta <= win) & (key < n_real)
            return ok, delta.astype(F32)

        kt = caches[g][0].reshape(GROUP_WIDTH, buf_len).astype(BF16)
        vt = caches[g][1].reshape(GROUP_WIDTH, buf_len).astype(BF16)
        qblk = jnp.where(own_head, jnp.concatenate([q_ref[:, cols]] * H_SLOT, axis=0), 0.0).astype(BF16)
        kn = jnp.concatenate([kn_ref[:, cols], zpad], axis=0).astype(BF16)
        vn = jnp.concatenate([vn_ref[:, cols], zpad], axis=0).astype(BF16)
        slope_rows = jnp.zeros((rows, 1), F32)
        for h in range(H_SLOT):
            slope_rows = jnp.where(head_of_row == h, float(slopes[g * H_SLOT + h]) * LOG2_E, slope_rows)
        ok_c, delta_c = mask_of(buf_len, 0, buf_len)
        ok_n, delta_n = mask_of(pad_new, buf_len, t_new)
        sc = jnp.where(ok_c, _dot(qblk, kt) - slope_rows * delta_c, NEG_BIG)
        sn = jnp.where(ok_n, _dot_nt(qblk, kn) - slope_rows * delta_n, NEG_BIG)
        m = jnp.maximum(jnp.max(sc, axis=-1, keepdims=True), jnp.max(sn, axis=-1, keepdims=True))
        ec = jnp.exp2(sc - m)
        en = jnp.exp2(sn - m)
        den = jnp.sum(ec, axis=-1, keepdims=True) + jnp.sum(en, axis=-1, keepdims=True)
        outs.append((_dot_nt(ec.astype(BF16), vt) + _dot(en.astype(BF16), vn)) * (1.0 / den))
        lses.append(m * LN_2 + jnp.log(den))
    mx = functools.reduce(jnp.maximum, lses)
    ws = [jnp.exp(l - mx) for l in lses]
    wsum = functools.reduce(jnp.add, ws)
    comb = sum((w / wsum) * o for w, o in zip(ws, outs))
    comb = jnp.where(own_head, comb, 0.0)
    o_tok = sum(comb[h * t_new:(h + 1) * t_new, :] for h in range(H_SLOT))
    o_ref[...] = o_tok.astype(o_ref.dtype)


def _attend_sample(q, k_new, v_new, caches_t, dbatch, t_new):
    assert t_new & (t_new - 1) == 0 and t_new <= 8
    nb = SAMPLE_SEQS_PER_STEP if dbatch % SAMPLE_SEQS_PER_STEP == 0 else 1
    tok = pl.BlockSpec((nb, t_new, B_WIDTH), lambda b: (b, 0, 0))
    cache_specs = [pl.BlockSpec((nb,) + c.shape[1:], lambda b: (b, 0, 0, 0, 0)) for c in caches_t]
    o = pl.pallas_call(
        functools.partial(_attn_sample_kernel, t_new=t_new, slopes=_alibi_slopes()),
        grid=(dbatch // nb,),
        in_specs=[tok, tok, tok] + cache_specs,
        out_specs=pl.BlockSpec((nb, t_new, GROUP_WIDTH), lambda b: (b, 0, 0)),
        out_shape=jax.ShapeDtypeStruct((dbatch, t_new, GROUP_WIDTH), BF16),
        compiler_params=_params(),
        name="attn_sample",
    )(q, k_new, v_new, *caches_t)
    return o.reshape(dbatch * t_new, GROUP_WIDTH)


def _kv_tail_kernel(k_ref, v_ref, o_ref):
    o_ref[0] = k_ref[...].T
    o_ref[1] = v_ref[...].T


def _kv_tail(k, v, gi, bsz, seq):
    keep = min(DILATION_GROUPS[gi][0], seq)
    pb = min(keep, 512)
    first = (seq - keep) // pb
    src = pl.BlockSpec((None, pb, GROUP_WIDTH), lambda b, j: (b, first + j, gi))
    return pl.pallas_call(
        _kv_tail_kernel,
        grid=(bsz, keep // pb),
        in_specs=[src, src],
        out_specs=pl.BlockSpec((None, 2, GROUP_WIDTH, pb), lambda b, j: (b, 0, 0, j)),
        out_shape=jax.ShapeDtypeStruct((bsz, 2, GROUP_WIDTH, keep), F32),
        compiler_params=_params(2),
        name=f"kv_tail_w{DILATION_GROUPS[gi][0]}",
    )(k, v)


def _kv_sample_kernel(*refs, t_new, dbatch):
    in_refs, out_refs = refs[:2 * N_GROUPS_B], refs[2 * N_GROUPS_B:]
    for g, o_ref in enumerate(out_refs):
        k_ref, v_ref = in_refs[2 * g], in_refs[2 * g + 1]
        for t in range(t_new):
            rows = pl.ds(t, dbatch, stride=t_new)
            o_ref[t, 0] = k_ref[rows, :].T
            o_ref[t, 1] = v_ref[rows, :].T


def _kv_sample(k, v, dbatch, t_new):
    src = lambda g: pl.BlockSpec((dbatch * t_new, LANES), lambda s: (0, g * PAIRS + s))
    return pl.pallas_call(
        functools.partial(_kv_sample_kernel, t_new=t_new, dbatch=dbatch),
        grid=(PAIRS,),
        in_specs=[src(g) for g in range(N_GROUPS_B) for _ in range(2)],
        out_specs=[pl.BlockSpec((t_new, 2, LANES, dbatch), lambda s: (0, 0, s, 0))] * N_GROUPS_B,
        out_shape=[jax.ShapeDtypeStruct((t_new, 2, GROUP_WIDTH, dbatch), F32)] * N_GROUPS_B,
        compiler_params=_params(),
        name="kv_sample",
    )(*[t for _ in range(N_GROUPS_B) for t in (k, v)])


def _topk_route(logits):
    tm = logits.shape[0]
    lane = lax.broadcasted_iota(jnp.int32, logits.shape, 1).astype(F32)
    slot = lax.broadcasted_iota(jnp.int32, (tm, TOP_K), 1)
    val_out = jnp.zeros((tm, TOP_K), F32)
    work = logits
    ids, top = [], None
    for r in range(TOP_K):
        mx = jnp.max(work, axis=-1, keepdims=True)
        ix = jnp.min(jnp.where(work == mx, lane, float(N_EXPERTS)), axis=-1, keepdims=True)
        top = mx if top is None else top
        ids.append(ix)
        val_out = jnp.where(slot == r, jnp.exp(mx - top), val_out)
        work = jnp.where(lane == ix, -jnp.inf, work)
    gates = val_out / jnp.sum(val_out, axis=-1, keepdims=True)
    return ids, gates


_FINISH_INPUTS = 17
_RANK_DIGITS = 3
_ROUTE_ROWS = TOP_K * (1 + _RANK_DIGITS)


def _finish_kernel(*refs, n_alias, n_valid):
    ins, outs = refs[:_FINISH_INPUTS], refs[_FINISH_INPUTS + n_alias:]
    step = pl.program_id(0)

    @pl.when(step < n_valid)
    def _():
        _finish_tile(ins, outs)

    @pl.when(step >= n_valid)
    def _():
        for ref in outs[:5]:
            ref[...] = jnp.zeros_like(ref)


def _finish_tile(ins, outs):
    (h_ref, u_ref, va_ref, wm_ref, bias_ref, ob_ref, ga_ref, gb_ref, wa_ref, wb_ref, wo_ref, gmoe_ref,
     wrh_ref, wrl_ref, br_ref, tri_ref, cnt_in_ref) = ins
    h1_ref, n2_ref, idx_ref, gate_ref, pos_ref, cnt_out_ref, carry_ref = outs
    tm = h_ref.shape[0]

    chunks = []
    for c in range(tm // CHUNK):
        rows = slice(c * CHUNK, (c + 1) * CHUNK)
        va_c = va_ref[rows, :].astype(BF16)
        mixed = jnp.concatenate(
            [_dot(wm_ref[g], va_c[:, g * A_GROUP_WIDTH:(g + 1) * A_GROUP_WIDTH]) for g in range(A_GROUPS)], axis=1)
        chunks.append((u_ref[rows, :].astype(F32) * (mixed + bias_ref[...])).astype(BF16))
    branch_a = _dot(jnp.concatenate(chunks, axis=0), wa_ref[...])

    branch_b = _dot(ob_ref[...], wb_ref[...])

    mix = ga_ref[...].astype(F32) * branch_a + gb_ref[...].astype(F32) * branch_b
    h1 = h_ref[...] + _dot(mix.astype(BF16), wo_ref[...])
    h1_ref[...] = h1

    n2 = (h1 * _rms(h1)) * gmoe_ref[...]
    n2_ref[...] = _pack_bf16_pairs(n2)
    n_hi = n2.astype(BF16)
    n_lo = (n2 - n_hi.astype(F32)).astype(BF16)
    logits = _dot(n_hi, wrh_ref[...]) + _dot(n_lo, wrh_ref[...]) + _dot(n_hi, wrl_ref[...]) + br_ref[...]
    ids, gates = _topk_route(logits)
    gate_ref[...] = gates

    @pl.when(pl.program_id(0) == 0)
    def _():
        carry_ref[...] = cnt_in_ref[...]

    lane = lax.broadcasted_iota(jnp.int32, logits.shape, 1).astype(F32)
    sel = sum(jnp.where(lane == ix, 1.0, 0.0) for ix in ids)
    before = _dot(tri_ref[...], sel.astype(BF16)) + carry_ref[...]
    ranks = [jnp.sum(jnp.where(lane == ix, before, 0.0), axis=-1, keepdims=True) for ix in ids]
    carry_ref[...] = carry_ref[...] + jnp.sum(sel, axis=0, keepdims=True)
    cnt_out_ref[...] = carry_ref[...]

    wide = lax.broadcasted_iota(jnp.int32, (tm, LANES), 1)
    cols = jnp.zeros((tm, LANES), F32)
    for r in range(TOP_K):
        rank_i = ranks[r].astype(jnp.int32)
        digits = [ids[r]] + [((rank_i >> (8 * d)) & 255).astype(F32) for d in range(_RANK_DIGITS)]
        for d, col in enumerate(digits):
            cols = jnp.where(wide == d * TOP_K + r, col, cols)
    eye = (lax.broadcasted_iota(jnp.int32, (_ROUTE_ROWS, LANES), 0)
           == lax.broadcasted_iota(jnp.int32, (_ROUTE_ROWS, LANES), 1)).astype(BF16)
    rows_t = _dot_nt(eye, cols.astype(BF16))
    idx_ref[...] = rows_t[:TOP_K].astype(jnp.int32)
    pos = sum(rows_t[(1 + d) * TOP_K:(2 + d) * TOP_K] * float(256 ** d) for d in range(_RANK_DIGITS))
    pos_ref[...] = pos.astype(jnp.int32)


def _finish(h, u, va, wm, bias_full, o_b, ga, gb, consts, cnt_in, n_total, tile_offset, prev_outs):
    n_tok = h.shape[0]
    tm = FINISH_TILE
    n_valid = n_tok // tm
    n_steps = n_valid if prev_outs is not None else n_total // tm - tile_offset
    row = lambda w: pl.BlockSpec((tm, w), lambda i: (jnp.minimum(i, n_valid - 1), 0))
    orow = lambda w: pl.BlockSpec((tm, w), lambda i: (i + tile_offset, 0))
    args = [h, u, va, wm, bias_full, o_b, ga, gb] + list(consts) + [cnt_in]
    specs = [row(D_MODEL), row(A_WIDTH), row(A_WIDTH), _resident(wm.shape), _resident(bias_full.shape),
             row(GROUP_WIDTH), row(D_MODEL), row(D_MODEL)]
    specs += [_resident(t.shape) for t in consts] + [_resident(cnt_in.shape)]
    assert len(args) == _FINISH_INPUTS
    aliases = {}
    if prev_outs is not None:
        for k, t in enumerate(prev_outs):
            aliases[len(args)] = k
            args.append(t)
            specs.append(pl.BlockSpec(memory_space=pl.ANY))
    by_choice = pl.BlockSpec((TOP_K, tm), lambda i: (0, i + tile_offset))
    tokens = lambda w, dt: jax.ShapeDtypeStruct((n_total, w), dt)
    choices = jax.ShapeDtypeStruct((TOP_K, n_total), jnp.int32)
    outs = pl.pallas_call(
        functools.partial(_finish_kernel, n_alias=len(aliases), n_valid=n_valid),
        grid=(n_steps,),
        in_specs=specs,
        out_specs=[orow(D_MODEL), orow(D_MODEL // 2), by_choice, orow(TOP_K), by_choice, _resident(cnt_in.shape)],
        out_shape=[tokens(D_MODEL, F32), tokens(D_MODEL // 2, jnp.int32), choices, tokens(TOP_K, F32), choices,
                   jax.ShapeDtypeStruct(cnt_in.shape, F32)],
        scratch_shapes=[pltpu.VMEM(cnt_in.shape, F32)],
        input_output_aliases=aliases,
        compiler_params=_params(),
        name="finish",
    )(*args)
    return outs[:5], outs[5]


def _sc_gather_rows(table, idx):
    m = idx.shape[0]
    width = table.shape[1]
    per_worker = m // SC_WORKERS
    n_chunks = per_worker // SC_ROWS
    mesh = plsc.VectorSubcoreMesh(core_axis_name="c", subcore_axis_name="s",
                                  num_cores=SC_CORES, num_subcores=SC_SUBCORES)

    assert n_chunks % 2 == 0

    @functools.partial(
        pl.kernel, mesh=mesh,
        out_type=jax.ShapeDtypeStruct((m, width), table.dtype),
        scratch_types=[pltpu.VMEM((n_chunks, SC_ROWS), jnp.int32),
                       pltpu.VMEM((SC_ROWS, width), table.dtype),
                       pltpu.VMEM((SC_ROWS, width), table.dtype),
                       pltpu.SemaphoreType.DMA,
                       pltpu.SemaphoreType.DMA],
        name="sc_gather_rows",
    )
    def gather(table_hbm, idx_hbm, out_hbm, idx_v, rows_a, rows_b, sem_a, sem_b):
        wid = lax.axis_index("s") * SC_CORES + lax.axis_index("c")
        base = wid * per_worker
        pltpu.sync_copy(idx_hbm.at[wid], idx_v)

        def fetch(c, rows, sem):
            return pltpu.make_async_copy(table_hbm.at[idx_v.at[c]], rows, sem)

        def put(c, rows):
            off = pl.multiple_of(base + c * SC_ROWS, SC_ROWS)
            pltpu.sync_copy(rows, out_hbm.at[pl.ds(off, SC_ROWS)])

        fetch(0, rows_a, sem_a).start()

        @pl.loop(0, n_chunks, step=2)
        def _(c):
            fetch(c, rows_a, sem_a).wait()
            fetch(c + 1, rows_b, sem_b).start()
            put(c, rows_a)
            fetch(c + 1, rows_b, sem_b).wait()

            @pl.when(c + 2 < n_chunks)
            def _():
                fetch(c + 2, rows_a, sem_a).start()

            put(c + 1, rows_b)

    return gather(table, idx.reshape(SC_WORKERS, n_chunks, SC_ROWS))


def _sc_scatter_rows(src, dest_t, n_out):
    n_src, width = src.shape
    top_k = dest_t.shape[0]
    per_worker = n_src // SC_WORKERS
    rows = SC_SCATTER_ROWS
    n_chunks = per_worker // rows
    assert per_worker % rows == 0
    dest = dest_t.reshape(top_k, SC_WORKERS, n_chunks, rows).transpose(1, 2, 0, 3)
    mesh = plsc.VectorSubcoreMesh(core_axis_name="c", subcore_axis_name="s",
                                  num_cores=SC_CORES, num_subcores=SC_SUBCORES)

    @functools.partial(
        pl.kernel, mesh=mesh,
        out_type=jax.ShapeDtypeStruct((n_out, width), src.dtype),
        scratch_types=[pltpu.VMEM((n_chunks, top_k, rows), jnp.int32),
                       pltpu.VMEM((rows, width), src.dtype),
                       pltpu.VMEM((rows, width), src.dtype),
                       pltpu.SemaphoreType.DMA,
                       pltpu.SemaphoreType.DMA,
                       pltpu.SemaphoreType.DMA],
        name="sc_scatter_rows",
    )
    def scatter(src_hbm, dest_hbm, out_hbm, idx_v, rows_a, rows_b, sem_a, sem_b, sem_w):
        wid = lax.axis_index("s") * SC_CORES + lax.axis_index("c")
        base = wid * per_worker
        pltpu.sync_copy(dest_hbm.at[wid], idx_v)

        def load(c, buf, sem):
            off = pl.multiple_of(base + c * rows, rows)
            return pltpu.make_async_copy(src_hbm.at[pl.ds(off, rows)], buf, sem)

        def spread(c, buf):
            copies = [pltpu.make_async_copy(buf, out_hbm.at[idx_v.at[c, k]], sem_w) for k in range(top_k)]
            for cp in copies:
                cp.start()
            for cp in copies:
                cp.wait()

        load(0, rows_a, sem_a).start()

        @pl.loop(0, n_chunks - 1, step=2)
        def _(c):
            load(c, rows_a, sem_a).wait()
            load(c + 1, rows_b, sem_b).start()
            spread(c, rows_a)
            load(c + 1, rows_b, sem_b).wait()

            @pl.when(c + 2 < n_chunks)
            def _():
                load(c + 2, rows_a, sem_a).start()

            spread(c + 1, rows_b)

        if n_chunks % 2:
            load(n_chunks - 1, rows_a, sem_a).wait()
            spread(n_chunks - 1, rows_a)

    return scatter(src, dest)


def _expert_kernel(be_ref, slot_ref, next_ref, nused_ref, x_ref, wgu_hbm, wd_hbm, bg_ref, bl_ref, bd_ref, sel_ref, y_ref,
                   wgu_f, wd_f, wg_s, wl_s, wd_s, sems):
    i = pl.program_id(0)
    active = i < nused_ref[0]
    expert = be_ref[i]
    fresh = (i == 0) | (expert != be_ref[jnp.maximum(i - 1, 0)])

    def fetch(e, slot):
        return (pltpu.make_async_copy(wgu_hbm.at[e], wgu_f.at[slot], sems.at[slot, 0]),
                pltpu.make_async_copy(wd_hbm.at[e], wd_f.at[slot], sems.at[slot, 1]))

    @pl.when(active & fresh)
    def _():
        slot = slot_ref[i]

        @pl.when(i == 0)
        def _():
            for cp in fetch(expert, slot):
                cp.start()

        for cp in fetch(expert, slot):
            cp.wait()
        nxt = next_ref[i]

        @pl.when(nxt >= 0)
        def _():
            for cp in fetch(nxt, 1 - slot):
                cp.start()

        half = MXU_DIM // 2
        for t in range(2 * D_FF // MXU_DIM):
            src = wgu_f[slot, :, t * MXU_DIM:(t + 1) * MXU_DIM].astype(BF16)
            both = _dot(src, sel_ref[...]).astype(BF16)
            wg_s[:, t * half:(t + 1) * half] = both[:, :half]
            wl_s[:, t * half:(t + 1) * half] = both[:, half:]
        wd_s[...] = wd_f[slot].astype(BF16)

    @pl.when(active)
    def _():
        x = _unpack_bf16_pairs(x_ref[...]).astype(BF16)
        h_glu = jnp.minimum(_dot(x, wg_s[...]) + bg_ref[...], SWIGLU_LIMIT)
        h_lin = jnp.clip(_dot(x, wl_s[...]) + bl_ref[...], -SWIGLU_LIMIT, SWIGLU_LIMIT)
        act = h_glu * _sigmoid(SWIGLU_ALPHA * h_glu) * (h_lin + 1.0)
        y_ref[...] = _pack_bf16_pairs(_dot(act.astype(BF16), wd_s[...]) + bd_ref[...])

    @pl.when(jnp.logical_not(active))
    def _():
        y_ref[...] = jnp.zeros_like(y_ref)


def _experts(xb, block_expert, block_slot, block_next, n_used, w_gate_up, w_down, b_glu, b_lin, b_down, sel):
    n_slots = xb.shape[0]
    n_blocks = n_slots // EXPERT_BLOCK
    by_expert = lambda k, n: pl.BlockSpec((None, k, n), lambda i, be, sl, nx, nu: (be[i], 0, 0))
    blk = pl.BlockSpec((EXPERT_BLOCK, D_MODEL // 2), lambda i, be, sl, nx, nu: (i, 0))
    hbm = pl.BlockSpec(memory_space=pl.ANY)
    return pl.pallas_call(
        _expert_kernel,
        grid_spec=pltpu.PrefetchScalarGridSpec(
            num_scalar_prefetch=4, grid=(n_blocks,),
            in_specs=[blk, hbm, hbm, by_expert(1, D_FF), by_expert(1, D_FF), by_expert(1, D_MODEL),
                      pl.BlockSpec(sel.shape, lambda i, be, sl, nx, nu: (0, 0), pipeline_mode=pl.Buffered(1))],
            out_specs=blk,
            scratch_shapes=[pltpu.VMEM((2, D_MODEL, 2 * D_FF), F32), pltpu.VMEM((2, D_FF, D_MODEL), F32),
                            pltpu.VMEM((D_MODEL, D_FF), BF16), pltpu.VMEM((D_MODEL, D_FF), BF16),
                            pltpu.VMEM((D_FF, D_MODEL), BF16), pltpu.SemaphoreType.DMA((2, 2))]),
        out_shape=jax.ShapeDtypeStruct((n_slots, D_MODEL // 2), jnp.int32),
        compiler_params=_params(),
        name="experts",
    )(block_expert, block_slot, block_next, n_used, xb, w_gate_up, w_down, b_glu, b_lin, b_down, sel)


def _final_kernel(h1_ref, yg_ref, gate_ref, p_ref, gple_ref, wg_ref, wp_ref, out_ref):
    h2 = h1_ref[...]
    gates = gate_ref[...]
    for k in range(TOP_K):
        h2 = h2 + gates[:, k:k + 1] * _unpack_bf16_pairs(yg_ref[k])
    n3 = ((h2 * _rms(h2)) * gple_ref[...]).astype(BF16)
    gate = _sigmoid(_dot(n3, wg_ref[...]))
    out_ref[...] = h2 + gate * _dot(p_ref[...].astype(BF16), wp_ref[...])


def _final(h1, yg, gates, p, tile_offset, g_ple, w_ple_gate, w_ple_proj):
    n_tok = p.shape[0]
    tm = TOKEN_TILE
    return pl.pallas_call(
        _final_kernel,
        grid=(n_tok // tm,),
        in_specs=[pl.BlockSpec((tm, D_MODEL), lambda i: (i + tile_offset, 0)),
                  pl.BlockSpec((TOP_K, tm, D_MODEL // 2), lambda i: (0, i + tile_offset, 0)),
                  pl.BlockSpec((tm, TOP_K), lambda i: (i + tile_offset, 0)),
                  pl.BlockSpec((tm, PLE_DIM), lambda i: (i, 0)),
                  _resident(g_ple.shape), _resident(w_ple_gate.shape), _resident(w_ple_proj.shape)],
        out_specs=pl.BlockSpec((tm, D_MODEL), lambda i: (i, 0)),
        out_shape=jax.ShapeDtypeStruct((n_tok, D_MODEL), F32),
        compiler_params=_params(),
        name="final",
    )(h1, yg, gates, p, g_ple, w_ple_gate, w_ple_proj)


def _routing_tables(idx_t, pos_t, counts, n_slots):
    counts = counts.reshape(N_EXPERTS).astype(jnp.int32)
    pcounts = (counts + EXPERT_BLOCK - 1) // EXPERT_BLOCK * EXPERT_BLOCK
    pends = jnp.cumsum(pcounts)
    pstarts = pends - pcounts
    experts = jnp.arange(N_EXPERTS, dtype=jnp.int32)
    start_t = jnp.sum(jnp.where(idx_t[None] == experts[:, None, None], pstarts[:, None, None], 0), axis=0)
    dest_t = (start_t + pos_t).astype(jnp.int32)
    n_blocks = n_slots // EXPERT_BLOCK
    block_start = jnp.arange(n_blocks, dtype=jnp.int32) * EXPERT_BLOCK
    block_expert = jnp.minimum(jnp.sum(block_start[:, None] >= pends[None, :], axis=1), N_EXPERTS - 1).astype(jnp.int32)
    n_used = (pends[-1] // EXPERT_BLOCK).astype(jnp.int32).reshape(1)
    used = counts > 0
    slot_e = (jnp.cumsum(used.astype(jnp.int32)) - 1) & 1
    later_used = used[None, :] & (experts[None, :] > experts[:, None])
    next_e = jnp.min(jnp.where(later_used, experts[None, :], N_EXPERTS), axis=1)
    next_e = jnp.where(next_e == N_EXPERTS, -1, next_e)
    of_block = block_expert[:, None] == experts[None, :]
    block_slot = jnp.sum(jnp.where(of_block, slot_e[None, :], 0), axis=1).astype(jnp.int32)
    block_next = jnp.sum(jnp.where(of_block, next_e[None, :], 0), axis=1).astype(jnp.int32)
    return dest_t, block_expert, block_slot, block_next, n_used


def kernel(x_prompt, x_sample, cache_kv_w128, cache_kv_w512, cache_kv_w2048, p_prompt, p_sample, g_mix, w_in, g_v, g_q, g_k, w_spatial, b_spatial, w_branch_a, w_branch_b, w_out, g_moe, w_router, b_router, w_gate_up, b_gate_up, w_down, b_down, g_ple, w_ple_gate, w_ple_proj):
    bsz, seq, _ = x_prompt.shape
    dbatch, t_new, _ = x_sample.shape
    assert g_mix.shape[0] == 1
    caches = (cache_kv_w128, cache_kv_w512, cache_kv_w2048)
    l = 0
    n_p, n_s = bsz * seq, dbatch * t_new
    n_tok = n_p + n_s
    assert n_p % TOKEN_TILE == 0 and n_s % TOKEN_TILE == 0

    row2 = lambda t: t.reshape(1, -1)
    w_in_bf = w_in[l].astype(BF16)
    g_q_t = jnp.tile(g_q[l], B_HEADS).reshape(1, B_WIDTH)
    g_k_t = jnp.tile(g_k[l], B_HEADS).reshape(1, B_WIDTH)
    hid = np.arange(MXU_DIM) // HEAD_DIM
    hsum = jnp.asarray(hid[:, None] == hid[None, :], BF16)
    tril = jnp.tril(jnp.ones((CHUNK, CHUNK), bool))
    wm_prompt = jnp.where(tril[None], w_spatial[l], 0).astype(BF16)
    bias_prompt = jnp.repeat(b_spatial[l].T, A_GROUP_WIDTH, axis=1)
    reps = CHUNK // t_new
    small = jnp.where(tril[None, :t_new, :t_new], w_spatial[l][:, :t_new, :t_new], 0)
    wm_sample = jnp.einsum("ab,gij->gaibj", jnp.eye(reps, dtype=F32), small).reshape(A_GROUPS, CHUNK, CHUNK).astype(BF16)
    bias_sample = jnp.tile(bias_prompt[:t_new], (reps, 1))
    wr_hi = w_router[l].astype(BF16)
    wr_lo = (w_router[l] - wr_hi.astype(F32)).astype(BF16)
    tri = jnp.asarray(np.tril(np.ones((FINISH_TILE, FINISH_TILE), np.float32), -1), BF16)
    consts = (w_branch_a[l].astype(BF16), w_branch_b[l].astype(BF16), w_out[l].astype(BF16), row2(g_moe[l]),
              wr_hi, wr_lo, row2(b_router[l]), tri)
    half = MXU_DIM // 2
    sel_np = np.zeros((MXU_DIM, MXU_DIM), np.float32)
    sel_np[2 * np.arange(half), np.arange(half)] = 1.0
    sel_np[2 * np.arange(half) + 1, half + np.arange(half)] = 1.0
    sel = jnp.asarray(sel_np, BF16)

    proj = functools.partial(_project, g_mix=row2(g_mix[l]), w_in_bf=w_in_bf, g_v=row2(g_v[l]),
                             g_q_t=g_q_t, g_k_t=g_k_t, hsum=hsum)

    xp = x_prompt.reshape(n_p, D_MODEL)
    u_p, va_p, q_p, k_p, v_p, ga_p, gb_p = proj(xp, va_dtype=BF16)
    seq3 = lambda t: t.reshape(bsz, seq, B_WIDTH)
    attn_p = [_attend_prompt(seq3(q_p), seq3(k_p), seq3(v_p), gi, bsz, seq) for gi in range(N_GROUPS_B - 1)]
    o_p = _attend_prompt(seq3(q_p), seq3(k_p), seq3(v_p), N_GROUPS_B - 1, bsz, seq, others=attn_p)
    zero_counts = jnp.zeros((1, N_EXPERTS), F32)
    outs_p, cnt_p = _finish(xp, u_p, va_p, wm_prompt, bias_prompt, o_p, ga_p, gb_p, consts, zero_counts,
                            n_tok, 0, None)

    xs = x_sample.reshape(n_s, D_MODEL)
    u_s, va_s, q_s, k_s, v_s, ga_s, gb_s = proj(xs, va_dtype=F32)
    tok3 = lambda t: t.reshape(dbatch, t_new, B_WIDTH)
    caches_t = [jnp.transpose(c[l], (0, 2, 3, 4, 1)) for c in caches]
    o_s = _attend_sample(tok3(q_s), tok3(k_s), tok3(v_s), caches_t, dbatch, t_new)
    (h1, n2, idx_t, gates, pos_t), counts = _finish(xs, u_s, va_s, wm_sample, bias_sample, o_s, ga_s, gb_s,
                                                  consts, cnt_p, n_tok, n_p // FINISH_TILE, outs_p)

    n_blocks = -(-n_tok * TOP_K // EXPERT_BLOCK) + N_EXPERTS
    n_slots = n_blocks * EXPERT_BLOCK
    dest_t, block_expert, block_slot, block_next, n_used = _routing_tables(idx_t, pos_t, counts, n_slots)
    xb = _sc_scatter_rows(n2, dest_t, n_slots)
    yb = _experts(xb, block_expert, block_slot, block_next, n_used, w_gate_up[l], w_down[l],
                  b_gate_up[l][:, None, 0::2], b_gate_up[l][:, None, 1::2], b_down[l][:, None, :], sel)
    yg = _sc_gather_rows(yb, dest_t.reshape(-1)).reshape(TOP_K, n_tok, D_MODEL // 2)
    fin = functools.partial(_final, g_ple=row2(g_ple[l]), w_ple_gate=w_ple_gate[l].astype(BF16),
                            w_ple_proj=w_ple_proj[l].astype(BF16))
    y_prompt = fin(h1, yg, gates, p_prompt[l].reshape(n_p, PLE_DIM), 0).reshape(bsz, seq, D_MODEL)
    y_sample = fin(h1, yg, gates, p_sample[l].reshape(n_s, PLE_DIM), n_p // TOKEN_TILE).reshape(dbatch, t_new, D_MODEL)

    kv_prompt = []
    for gi in range(N_GROUPS_B):
        t = _kv_tail(seq3(k_p), seq3(v_p), gi, bsz, seq)
        keep = t.shape[-1]
        kv_prompt.append(jnp.transpose(t.reshape(bsz, 2, H_SLOT, HEAD_DIM, keep), (0, 4, 1, 2, 3))[None])
    kvs = _kv_sample(k_s, v_s, dbatch, t_new)
    kv_sample = [jnp.transpose(kvs[gi].reshape(t_new, 2, H_SLOT, HEAD_DIM, dbatch), (4, 0, 1, 2, 3))[None]
                 for gi in range(N_GROUPS_B)]
    va_out = va_s.reshape(1, dbatch, t_new, A_WIDTH)
    return (y_prompt, y_sample, *kv_prompt, *kv_sample, va_out)
```

```python
import functools

import numpy as np
import jax
import jax.numpy as jnp
from jax import lax
from jax.experimental import pallas as pl
from jax.experimental.pallas import tpu as pltpu
from jax.experimental.pallas import tpu_sc as plsc

F32 = jnp.float32
BF16 = jnp.bfloat16

D_MODEL = 1024
A_WIDTH = 1024
A_GROUPS = 4
A_GROUP_WIDTH = A_WIDTH // A_GROUPS
CHUNK = 128
HEAD_DIM = 64
H_SLOT = 8
GROUP_WIDTH = H_SLOT * HEAD_DIM
DILATION_GROUPS = ((128, 1), (512, 4), (2048, 16))
N_GROUPS_B = len(DILATION_GROUPS)
B_HEADS = H_SLOT * N_GROUPS_B
B_WIDTH = B_HEADS * HEAD_DIM
N_EXPERTS = 32
TOP_K = 4
D_FF = 1024
SWIGLU_ALPHA = 1.702
SWIGLU_LIMIT = 7.0
PLE_DIM = 256
RMS_EPS = 1e-6
NEG_BIG = -1e30
LOG2_E = float(np.log2(np.e))
LN_2 = float(np.log(2.0))

LANES = 128
MXU_DIM = 256
HEADS_PER_VREG = LANES // HEAD_DIM
PAIRS = GROUP_WIDTH // LANES
RESIDUE_UNROLL = 8
SAMPLE_SEQS_PER_STEP = 2
ATTN_STEP_ROWS = 1024
TOKEN_TILE = 512
FINISH_TILE = 512
EXPERT_BLOCK = 256
VMEM_LIMIT = 56 * 1024 * 1024

SC_CORES = 2
SC_SUBCORES = 16
SC_WORKERS = SC_CORES * SC_SUBCORES
SC_ROWS = 64
SC_SCATTER_ROWS = 32

_COL_SPLITS = np.cumsum([0, A_WIDTH, A_WIDTH, B_WIDTH, B_WIDTH, B_WIDTH, D_MODEL, D_MODEL]).tolist()


def _alibi_slopes():
    return np.exp2(-8.0 * np.arange(1, B_HEADS + 1, dtype=np.float32) / B_HEADS).astype(np.float32)


def _sigmoid(x):
    return 1.0 / (1.0 + jnp.exp(-x))


def _rms(x):
    return lax.rsqrt(jnp.mean(x * x, axis=-1, keepdims=True) + RMS_EPS)


def _dot(a, b):
    return jnp.dot(a, b, preferred_element_type=F32)


def _dot_nt(a, b):
    return lax.dot_general(a, b, (((1,), (1,)), ((), ())), preferred_element_type=F32)


def _pack_bf16_pairs(x):
    w = x.shape[1] // 2
    lo = lax.bitcast_convert_type(x[:, :w].astype(BF16).astype(F32), jnp.uint32) >> 16
    hi = lax.bitcast_convert_type(x[:, w:].astype(BF16).astype(F32), jnp.uint32) & jnp.uint32(0xFFFF0000)
    return lax.bitcast_convert_type(lo | hi, jnp.int32)


def _unpack_bf16_pairs(p):
    u = lax.bitcast_convert_type(p, jnp.uint32)
    lo = lax.bitcast_convert_type(u << 16, F32)
    hi = lax.bitcast_convert_type(u & jnp.uint32(0xFFFF0000), F32)
    return jnp.concatenate([lo, hi], axis=1)


def _resident(shape):
    nd = len(shape)
    return pl.BlockSpec(shape, lambda *_: (0,) * nd, pipeline_mode=pl.Buffered(1))


def _params(n_axes=1):
    return pltpu.CompilerParams(dimension_semantics=("arbitrary",) * n_axes, vmem_limit_bytes=VMEM_LIMIT)


def _proj_kernel(x_ref, gmix_ref, w_ref, gv_ref, gq_ref, gk_ref, hsum_ref,
                 u_ref, va_ref, q_ref, k_ref, v_ref, ga_ref, gb_ref):
    x = x_ref[...]
    n = ((x * _rms(x)) * gmix_ref[...]).astype(BF16)

    def section(i):
        return _dot(n, w_ref[:, _COL_SPLITS[i]:_COL_SPLITS[i + 1]])

    u_ref[...] = jax.nn.gelu(section(0)).astype(u_ref.dtype)
    va = jax.nn.gelu(section(1))
    va_ref[...] = ((va * _rms(va)) * gv_ref[...]).astype(va_ref.dtype)

    def head_norm(z, g_ref, scale):
        parts = []
        for c in range(B_WIDTH // MXU_DIM):
            zc = z[:, c * MXU_DIM:(c + 1) * MXU_DIM]
            ss = _dot((zc * zc).astype(BF16), hsum_ref[...])
            parts.append(zc * lax.rsqrt(ss * (1.0 / HEAD_DIM) + RMS_EPS))
        return jnp.concatenate(parts, axis=1) * (g_ref[...] * scale)

    q_ref[...] = head_norm(section(2), gq_ref, HEAD_DIM ** -0.5 * LOG2_E)
    k_ref[...] = head_norm(section(3), gk_ref, 1.0)
    v_ref[...] = section(4)
    ga_ref[...] = _sigmoid(section(5)).astype(ga_ref.dtype)
    gb_ref[...] = _sigmoid(section(6)).astype(gb_ref.dtype)


def _project(x, g_mix, w_in_bf, g_v, g_q_t, g_k_t, hsum, va_dtype):
    n_tok = x.shape[0]
    tm = TOKEN_TILE
    row = lambda w: pl.BlockSpec((tm, w), lambda i: (i, 0))
    outs = [(A_WIDTH, BF16), (A_WIDTH, va_dtype), (B_WIDTH, F32), (B_WIDTH, F32), (B_WIDTH, F32),
            (D_MODEL, BF16), (D_MODEL, BF16)]
    return pl.pallas_call(
        _proj_kernel,
        grid=(n_tok // tm,),
        in_specs=[row(D_MODEL), _resident(g_mix.shape), _resident(w_in_bf.shape), _resident(g_v.shape),
                  _resident(g_q_t.shape), _resident(g_k_t.shape), _resident(hsum.shape)],
        out_specs=[row(w) for w, _ in outs],
        out_shape=[jax.ShapeDtypeStruct((n_tok, w), dt) for w, dt in outs],
        compiler_params=_params(),
        name="project",
    )(x, g_mix, w_in_bf, g_v, g_q_t, g_k_t, hsum)


def _band_bias(band, dil, slopes):
    qi = jnp.arange(band, dtype=jnp.int32)[:, None]
    kj = jnp.arange(2 * band, dtype=jnp.int32)[None, :]
    dist = qi + band - kj
    in_band = (dist >= 0) & (dist <= band)
    valid = jnp.stack([in_band & (kj >= band), in_band])
    penalty = (jnp.asarray(slopes, F32) * LOG2_E)[:, None, None] * (dist * dil).astype(F32)[None]
    return jnp.where(valid[:, None], -penalty[None], NEG_BIG)


def _pair_attention(q2, k, v, bias_pair):
    band = q2.shape[0]
    first = lax.broadcasted_iota(jnp.int32, (band, LANES), 1) < HEAD_DIM
    qs = jnp.concatenate([jnp.where(first, q2, 0.0), jnp.where(first, 0.0, q2)], axis=0).astype(BF16)
    s = _dot_nt(qs, k) + bias_pair.reshape(HEADS_PER_VREG * band, 2 * band)
    m = jnp.max(s, axis=-1, keepdims=True)
    e = jnp.exp2(s - m).astype(BF16)
    v_ones = jnp.concatenate([v, jnp.ones_like(v)], axis=1)
    r = _dot(e, v_ones)
    den = r[:, LANES:]
    o2 = r[:, :LANES] / den
    lse = m * LN_2 + jnp.log(den)
    return jnp.where(first, o2[:band], o2[band:]), jnp.where(first, lse[:band], lse[band:])


def _attn_dense_kernel(q_ref, kp_ref, kc_ref, vp_ref, vc_ref, bias0_ref, bias_ref, o_ref, l_ref, *, band, n_sub):
    for i in range(n_sub):
        rows = slice(i * band, (i + 1) * band)
        before = slice((i - 1) * band, i * band)
        for p in range(PAIRS):
            cols = slice(p * LANES, (p + 1) * LANES)
            k_prev = kc_ref[before, cols] if i else kp_ref[:, cols]
            v_prev = vc_ref[before, cols] if i else vp_ref[:, cols]
            k = jnp.concatenate([k_prev, kc_ref[rows, cols]], axis=0).astype(BF16)
            v = jnp.concatenate([v_prev, vc_ref[rows, cols]], axis=0).astype(BF16)
            table = bias_ref if i else bias0_ref
            bias2 = table[p * HEADS_PER_VREG:(p + 1) * HEADS_PER_VREG]
            o_pair, l_pair = _pair_attention(q_ref[rows, cols], k, v, bias2)
            o_ref[rows, cols] = o_pair.astype(o_ref.dtype)
            l_ref[rows, cols] = l_pair


def _attn_dilated_kernel(q_ref, kp_ref, kc_ref, vp_ref, vc_ref, bias0_ref, bias_ref, *rest,
                         band, dil, n_sub, n_others):
    others, rest = rest[:2 * n_others], rest[2 * n_others:]
    if n_others:
        out_ref, o_ref, l_ref = rest
    else:
        o_ref, l_ref = rest
    win = band * dil
    unroll = min(RESIDUE_UNROLL, dil)

    def residues(it, carry):
        for i in range(n_sub):
            for u in range(unroll):
                r = it * unroll + u
                rows = pl.ds(i * win + r, band, stride=dil)
                before = pl.ds((i - 1) * win + r, band, stride=dil) if i else pl.ds(r, band, stride=dil)
                k_prev = kc_ref[before, :] if i else kp_ref[before, :]
                v_prev = vc_ref[before, :] if i else vp_ref[before, :]
                k = jnp.concatenate([k_prev, kc_ref[rows, :]], axis=0).astype(BF16)
                v = jnp.concatenate([v_prev, vc_ref[rows, :]], axis=0).astype(BF16)
                table = bias_ref if i else bias0_ref
                o_pair, l_pair = _pair_attention(q_ref[rows, :], k, v, table[...])
                o_ref[rows, :] = o_pair
                l_ref[rows, :] = l_pair
        return carry

    lax.fori_loop(0, dil // unroll, residues, 0)

    if n_others:
        outs = [others[2 * g][...].astype(F32) for g in range(n_others)] + [o_ref[...]]
        lses = [others[2 * g + 1][...] for g in range(n_others)] + [l_ref[...]]
        mx = functools.reduce(jnp.maximum, lses)
        ws = [jnp.exp(l - mx) for l in lses]
        wsum = functools.reduce(jnp.add, ws)
        out_ref[...] = (sum(w * o for w, o in zip(ws, outs)) / wsum).astype(out_ref.dtype)


def _attend_prompt(q, k, v, gi, bsz, seq, others=()):
    win, dil = DILATION_GROUPS[gi]
    band = win // dil
    bias = _band_bias(band, dil, _alibi_slopes()[gi * H_SLOT:(gi + 1) * H_SLOT])
    step_rows = ATTN_STEP_ROWS if dil == 1 else 4 * ATTN_STEP_ROWS
    n_sub = max(1, min(step_rows // win, seq // win))
    step = n_sub * win
    if dil == 1:
        cur = pl.BlockSpec((None, step, GROUP_WIDTH), lambda b, j: (b, j, gi))
        prev = pl.BlockSpec((None, win, GROUP_WIDTH), lambda b, j: (b, jnp.maximum(j * n_sub - 1, 0), gi))
        tab0 = pl.BlockSpec((None, H_SLOT, band, 2 * band), lambda b, j: (jnp.minimum(j, 1), 0, 0, 0))
        tab = pl.BlockSpec((None, H_SLOT, band, 2 * band), lambda b, j: (1, 0, 0, 0))
        out = pl.BlockSpec((None, step, GROUP_WIDTH), lambda b, j: (b, j, 0))
        o, lse = pl.pallas_call(
            functools.partial(_attn_dense_kernel, band=band, n_sub=n_sub),
            grid=(bsz, seq // step),
            in_specs=[cur, prev, cur, prev, cur, tab0, tab],
            out_specs=[out, out],
            out_shape=[jax.ShapeDtypeStruct((bsz, seq, GROUP_WIDTH), BF16),
                       jax.ShapeDtypeStruct((bsz, seq, GROUP_WIDTH), F32)],
            compiler_params=_params(2),
            name=f"attn_prompt_w{win}",
        )(q, k, k, v, v, bias, bias)
    else:
        col = lambda p: gi * PAIRS + p
        cur = pl.BlockSpec((None, step, LANES), lambda b, j, p: (b, j, col(p)))
        prev = pl.BlockSpec((None, win, LANES), lambda b, j, p: (b, jnp.maximum(j * n_sub - 1, 0), col(p)))
        tab0 = pl.BlockSpec((None, HEADS_PER_VREG, band, 2 * band), lambda b, j, p: (jnp.minimum(j, 1), p, 0, 0))
        tab = pl.BlockSpec((None, HEADS_PER_VREG, band, 2 * band), lambda b, j, p: (1, p, 0, 0))
        out = pl.BlockSpec((None, step, LANES), lambda b, j, p: (b, j, p))
        kern = functools.partial(_attn_dilated_kernel, band=band, dil=dil, n_sub=n_sub, n_others=len(others))
        common = dict(grid=(bsz, seq // step, PAIRS), compiler_params=_params(3), name=f"attn_prompt_w{win}")
        ins = [cur, prev, cur, prev, cur, tab0, tab]
        if others:
            extra = [t.reshape(bsz, seq, GROUP_WIDTH) for pair in others for t in pair]
            combined = pl.pallas_call(
                kern,
                in_specs=ins + [out] * len(extra),
                out_specs=out,
                out_shape=jax.ShapeDtypeStruct((bsz, seq, GROUP_WIDTH), BF16),
                scratch_shapes=[pltpu.VMEM((step, LANES), F32), pltpu.VMEM((step, LANES), F32)],
                **common,
            )(q, k, k, v, v, bias, bias, *extra)
            return combined.reshape(bsz * seq, GROUP_WIDTH)
        o, lse = pl.pallas_call(
            kern,
            in_specs=ins,
            out_specs=[out, out],
            out_shape=[jax.ShapeDtypeStruct((bsz, seq, GROUP_WIDTH), F32)] * 2,
            **common,
        )(q, k, k, v, v, bias, bias)
    return o.reshape(bsz * seq, GROUP_WIDTH), lse.reshape(bsz * seq, GROUP_WIDTH)


def _attn_sample_kernel(q_ref, kn_ref, vn_ref, c0_ref, c1_ref, c2_ref, o_ref, *, t_new, slopes):
    for b in range(q_ref.shape[0]):
        _attn_sample_one(q_ref.at[b], kn_ref.at[b], vn_ref.at[b], c0_ref.at[b], c1_ref.at[b], c2_ref.at[b],
                         o_ref.at[b], t_new=t_new, slopes=slopes)


def _attn_sample_one(q_ref, kn_ref, vn_ref, c0_ref, c1_ref, c2_ref, o_ref, *, t_new, slopes):
    rows = H_SLOT * t_new
    pad_new = 16
    t_shift = t_new.bit_length() - 1
    caches = (c0_ref, c1_ref, c2_ref)
    row_id = lax.broadcasted_iota(jnp.int32, (rows, GROUP_WIDTH), 0)
    col_id = lax.broadcasted_iota(jnp.int32, (rows, GROUP_WIDTH), 1)
    own_head = (row_id >> t_shift) == (col_id >> (HEAD_DIM.bit_length() - 1))
    head_of_row = lax.broadcasted_iota(jnp.int32, (rows, 1), 0) >> t_shift
    zpad = jnp.zeros((pad_new - t_new, GROUP_WIDTH), F32)
    outs, lses = [], []
    for g, (win, dil) in enumerate(DILATION_GROUPS):
        buf_len = caches[g].shape[-1]
        cols = slice(g * GROUP_WIDTH, (g + 1) * GROUP_WIDTH)

        def mask_of(n_keys, first_pos, n_real):
            i_q = lax.broadcasted_iota(jnp.int32, (rows, n_keys), 0) & (t_new - 1)
            key = lax.broadcasted_iota(jnp.int32, (rows, n_keys), 1)
            delta = buf_len + i_q - (first_pos + key)
            ok = (delta >= 0) & ((delta & (dil - 1)) == 0) & (delta <= win) & (key < n_real)
            return ok, delta.astype(F32)

        kt = caches[g][0].reshape(GROUP_WIDTH, buf_len).astype(BF16)
        vt = caches[g][1].reshape(GROUP_WIDTH, buf_len).astype(BF16)
        qblk = jnp.where(own_head, jnp.concatenate([q_ref[:, cols]] * H_SLOT, axis=0), 0.0).astype(BF16)
        kn = jnp.concatenate([kn_ref[:, cols], zpad], axis=0).astype(BF16)
        vn = jnp.concatenate([vn_ref[:, cols], zpad], axis=0).astype(BF16)
        slope_rows = jnp.zeros((rows, 1), F32)
        for h in range(H_SLOT):
            slope_rows = jnp.where(head_of_row == h, float(slopes[g * H_SLOT + h]) * LOG2_E, slope_rows)
        ok_c, delta_c = mask_of(buf_len, 0, buf_len)
        ok_n, delta_n = mask_of(pad_new, buf_len, t_new)
        sc = jnp.where(ok_c, _dot(qblk, kt) - slope_rows * delta_c, NEG_BIG)
        sn = jnp.where(ok_n, _dot_nt(qblk, kn) - slope_rows * delta_n, NEG_BIG)
        m = jnp.maximum(jnp.max(sc, axis=-1, keepdims=True), jnp.max(sn, axis=-1, keepdims=True))
        ec = jnp.exp2(sc - m)
        en = jnp.exp2(sn - m)
        den = jnp.sum(ec, axis=-1, keepdims=True) + jnp.sum(en, axis=-1, keepdims=True)
        outs.append((_dot_nt(ec.astype(BF16), vt) + _dot(en.astype(BF16), vn)) * (1.0 / den))
        lses.append(m * LN_2 + jnp.log(den))
    mx = functools.reduce(jnp.maximum, lses)
    ws = [jnp.exp(l - mx) for l in lses]
    wsum = functools.reduce(jnp.add, ws)
    comb = sum((w / wsum) * o for w, o in zip(ws, outs))
    comb = jnp.where(own_head, comb, 0.0)
    o_tok = sum(comb[h * t_new:(h + 1) * t_new, :] for h in range(H_SLOT))
    o_ref[...] = o_tok.astype(o_ref.dtype)


def _attend_sample(q, k_new, v_new, caches_t, dbatch, t_new):
    assert t_new & (t_new - 1) == 0 and t_new <= 8
    nb = SAMPLE_SEQS_PER_STEP if dbatch % SAMPLE_SEQS_PER_STEP == 0 else 1
    tok = pl.BlockSpec((nb, t_new, B_WIDTH), lambda b: (b, 0, 0))
    cache_specs = [pl.BlockSpec((nb,) + c.shape[1:], lambda b: (b, 0, 0, 0, 0)) for c in caches_t]
    o = pl.pallas_call(
        functools.partial(_attn_sample_kernel, t_new=t_new, slopes=_alibi_slopes()),
        grid=(dbatch // nb,),
        in_specs=[tok, tok, tok] + cache_specs,
        out_specs=pl.BlockSpec((nb, t_new, GROUP_WIDTH), lambda b: (b, 0, 0)),
        out_shape=jax.ShapeDtypeStruct((dbatch, t_new, GROUP_WIDTH), BF16),
        compiler_params=_params(),
        name="attn_sample",
    )(q, k_new, v_new, *caches_t)
    return o.reshape(dbatch * t_new, GROUP_WIDTH)


def _kv_tail_kernel(k_ref, v_ref, o_ref):
    o_ref[0] = k_ref[...].T
    o_ref[1] = v_ref[...].T


def _kv_tail(k, v, gi, bsz, seq):
    keep = min(DILATION_GROUPS[gi][0], seq)
    pb = min(keep, 512)
    first = (seq - keep) // pb
    src = pl.BlockSpec((None, pb, GROUP_WIDTH), lambda b, j: (b, first + j, gi))
    return pl.pallas_call(
        _kv_tail_kernel,
        grid=(bsz, keep // pb),
        in_specs=[src, src],
        out_specs=pl.BlockSpec((None, 2, GROUP_WIDTH, pb), lambda b, j: (b, 0, 0, j)),
        out_shape=jax.ShapeDtypeStruct((bsz, 2, GROUP_WIDTH, keep), F32),
        compiler_params=_params(2),
        name=f"kv_tail_w{DILATION_GROUPS[gi][0]}",
    )(k, v)


def _kv_sample_kernel(*refs, t_new, dbatch):
    in_refs, out_refs = refs[:2 * N_GROUPS_B], refs[2 * N_GROUPS_B:]
    for g, o_ref in enumerate(out_refs):
        k_ref, v_ref = in_refs[2 * g], in_refs[2 * g + 1]
        for t in range(t_new):
            rows = pl.ds(t, dbatch, stride=t_new)
            o_ref[t, 0] = k_ref[rows, :].T
            o_ref[t, 1] = v_ref[rows, :].T


def _kv_sample(k, v, dbatch, t_new):
    src = lambda g: pl.BlockSpec((dbatch * t_new, LANES), lambda s: (0, g * PAIRS + s))
    return pl.pallas_call(
        functools.partial(_kv_sample_kernel, t_new=t_new, dbatch=dbatch),
        grid=(PAIRS,),
        in_specs=[src(g) for g in range(N_GROUPS_B) for _ in range(2)],
        out_specs=[pl.BlockSpec((t_new, 2, LANES, dbatch), lambda s: (0, 0, s, 0))] * N_GROUPS_B,
        out_shape=[jax.ShapeDtypeStruct((t_new, 2, GROUP_WIDTH, dbatch), F32)] * N_GROUPS_B,
        compiler_params=_params(),
        name="kv_sample",
    )(*[t for _ in range(N_GROUPS_B) for t in (k, v)])


def _topk_route(logits):
    tm = logits.shape[0]
    lane = lax.broadcasted_iota(jnp.int32, logits.shape, 1).astype(F32)
    slot = lax.broadcasted_iota(jnp.int32, (tm, TOP_K), 1)
    val_out = jnp.zeros((tm, TOP_K), F32)
    work = logits
    ids, top = [], None
    for r in range(TOP_K):
        mx = jnp.max(work, axis=-1, keepdims=True)
        ix = jnp.min(jnp.where(work == mx, lane, float(N_EXPERTS)), axis=-1, keepdims=True)
        top = mx if top is None else top
        ids.append(ix)
        val_out = jnp.where(slot == r, jnp.exp(mx - top), val_out)
        work = jnp.where(lane == ix, -jnp.inf, work)
    gates = val_out / jnp.sum(val_out, axis=-1, keepdims=True)
    return ids, gates


_FINISH_INPUTS = 17
_RANK_DIGITS = 3
_ROUTE_ROWS = TOP_K * (1 + _RANK_DIGITS)


def _finish_kernel(*refs, n_alias, n_valid):
    ins, outs = refs[:_FINISH_INPUTS], refs[_FINISH_INPUTS + n_alias:]
    step = pl.program_id(0)

    @pl.when(step < n_valid)
    def _():
        _finish_tile(ins, outs)

    @pl.when(step >= n_valid)
    def _():
        for ref in outs[:5]:
            ref[...] = jnp.zeros_like(ref)


def _finish_tile(ins, outs):
    (h_ref, u_ref, va_ref, wm_ref, bias_ref, ob_ref, ga_ref, gb_ref, wa_ref, wb_ref, wo_ref, gmoe_ref,
     wrh_ref, wrl_ref, br_ref, tri_ref, cnt_in_ref) = ins
    h1_ref, n2_ref, idx_ref, gate_ref, pos_ref, cnt_out_ref, carry_ref = outs
    tm = h_ref.shape[0]

    chunks = []
    for c in range(tm // CHUNK):
        rows = slice(c * CHUNK, (c + 1) * CHUNK)
        va_c = va_ref[rows, :].astype(BF16)
        mixed = jnp.concatenate(
            [_dot(wm_ref[g], va_c[:, g * A_GROUP_WIDTH:(g + 1) * A_GROUP_WIDTH]) for g in range(A_GROUPS)], axis=1)
        chunks.append((u_ref[rows, :].astype(F32) * (mixed + bias_ref[...])).astype(BF16))
    branch_a = _dot(jnp.concatenate(chunks, axis=0), wa_ref[...])

    branch_b = _dot(ob_ref[...], wb_ref[...])

    mix = ga_ref[...].astype(F32) * branch_a + gb_ref[...].astype(F32) * branch_b
    h1 = h_ref[...] + _dot(mix.astype(BF16), wo_ref[...])
    h1_ref[...] = h1

    n2 = (h1 * _rms(h1)) * gmoe_ref[...]
    n2_ref[...] = _pack_bf16_pairs(n2)
    n_hi = n2.astype(BF16)
    n_lo = (n2 - n_hi.astype(F32)).astype(BF16)
    logits = _dot(n_hi, wrh_ref[...]) + _dot(n_lo, wrh_ref[...]) + _dot(n_hi, wrl_ref[...]) + br_ref[...]
    ids, gates = _topk_route(logits)
    gate_ref[...] = gates

    @pl.when(pl.program_id(0) == 0)
    def _():
        carry_ref[...] = cnt_in_ref[...]

    lane = lax.broadcasted_iota(jnp.int32, logits.shape, 1).astype(F32)
    sel = sum(jnp.where(lane == ix, 1.0, 0.0) for ix in ids)
    before = _dot(tri_ref[...], sel.astype(BF16)) + carry_ref[...]
    ranks = [jnp.sum(jnp.where(lane == ix, before, 0.0), axis=-1, keepdims=True) for ix in ids]
    carry_ref[...] = carry_ref[...] + jnp.sum(sel, axis=0, keepdims=True)
    cnt_out_ref[...] = carry_ref[...]

    wide = lax.broadcasted_iota(jnp.int32, (tm, LANES), 1)
    cols = jnp.zeros((tm, LANES), F32)
    for r in range(TOP_K):
        rank_i = ranks[r].astype(jnp.int32)
        digits = [ids[r]] + [((rank_i >> (8 * d)) & 255).astype(F32) for d in range(_RANK_DIGITS)]
        for d, col in enumerate(digits):
            cols = jnp.where(wide == d * TOP_K + r, col, cols)
    eye = (lax.broadcasted_iota(jnp.int32, (_ROUTE_ROWS, LANES), 0)
           == lax.broadcasted_iota(jnp.int32, (_ROUTE_ROWS, LANES), 1)).astype(BF16)
    rows_t = _dot_nt(eye, cols.astype(BF16))
    idx_ref[...] = rows_t[:TOP_K].astype(jnp.int32)
    pos = sum(rows_t[(1 + d) * TOP_K:(2 + d) * TOP_K] * float(256 ** d) for d in range(_RANK_DIGITS))
    pos_ref[...] = pos.astype(jnp.int32)


def _finish(h, u, va, wm, bias_full, o_b, ga, gb, consts, cnt_in, n_total, tile_offset, prev_outs):
    n_tok = h.shape[0]
    tm = FINISH_TILE
    n_valid = n_tok // tm
    n_steps = n_valid if prev_outs is not None else n_total // tm - tile_offset
    row = lambda w: pl.BlockSpec((tm, w), lambda i: (jnp.minimum(i, n_valid - 1), 0))
    orow = lambda w: pl.BlockSpec((tm, w), lambda i: (i + tile_offset, 0))
    args = [h, u, va, wm, bias_full, o_b, ga, gb] + list(consts) + [cnt_in]
    specs = [row(D_MODEL), row(A_WIDTH), row(A_WIDTH), _resident(wm.shape), _resident(bias_full.shape),
             row(GROUP_WIDTH), row(D_MODEL), row(D_MODEL)]
    specs += [_resident(t.shape) for t in consts] + [_resident(cnt_in.shape)]
    assert len(args) == _FINISH_INPUTS
    aliases = {}
    if prev_outs is not None:
        for k, t in enumerate(prev_outs):
            aliases[len(args)] = k
            args.append(t)
            specs.append(pl.BlockSpec(memory_space=pl.ANY))
    by_choice = pl.BlockSpec((TOP_K, tm), lambda i: (0, i + tile_offset))
    tokens = lambda w, dt: jax.ShapeDtypeStruct((n_total, w), dt)
    choices = jax.ShapeDtypeStruct((TOP_K, n_total), jnp.int32)
    outs = pl.pallas_call(
        functools.partial(_finish_kernel, n_alias=len(aliases), n_valid=n_valid),
        grid=(n_steps,),
        in_specs=specs,
        out_specs=[orow(D_MODEL), orow(D_MODEL // 2), by_choice, orow(TOP_K), by_choice, _resident(cnt_in.shape)],
        out_shape=[tokens(D_MODEL, F32), tokens(D_MODEL // 2, jnp.int32), choices, tokens(TOP_K, F32), choices,
                   jax.ShapeDtypeStruct(cnt_in.shape, F32)],
        scratch_shapes=[pltpu.VMEM(cnt_in.shape, F32)],
        input_output_aliases=aliases,
        compiler_params=_params(),
        name="finish",
    )(*args)
    return outs[:5], outs[5]


def _sc_gather_rows(table, idx):
    m = idx.shape[0]
    width = table.shape[1]
    per_worker = m // SC_WORKERS
    n_chunks = per_worker // SC_ROWS
    mesh = plsc.VectorSubcoreMesh(core_axis_name="c", subcore_axis_name="s",
                                  num_cores=SC_CORES, num_subcores=SC_SUBCORES)

    assert n_chunks % 2 == 0

    @functools.partial(
        pl.kernel, mesh=mesh,
        out_type=jax.ShapeDtypeStruct((m, width), table.dtype),
        scratch_types=[pltpu.VMEM((n_chunks, SC_ROWS), jnp.int32),
                       pltpu.VMEM((SC_ROWS, width), table.dtype),
                       pltpu.VMEM((SC_ROWS, width), table.dtype),
                       pltpu.SemaphoreType.DMA,
                       pltpu.SemaphoreType.DMA],
        name="sc_gather_rows",
    )
    def gather(table_hbm, idx_hbm, out_hbm, idx_v, rows_a, rows_b, sem_a, sem_b):
        wid = lax.axis_index("s") * SC_CORES + lax.axis_index("c")
        base = wid * per_worker
        pltpu.sync_copy(idx_hbm.at[wid], idx_v)

        def fetch(c, rows, sem):
            return pltpu.make_async_copy(table_hbm.at[idx_v.at[c]], rows, sem)

        def put(c, rows):
            off = pl.multiple_of(base + c * SC_ROWS, SC_ROWS)
            pltpu.sync_copy(rows, out_hbm.at[pl.ds(off, SC_ROWS)])

        fetch(0, rows_a, sem_a).start()

        @pl.loop(0, n_chunks, step=2)
        def _(c):
            fetch(c, rows_a, sem_a).wait()
            fetch(c + 1, rows_b, sem_b).start()
            put(c, rows_a)
            fetch(c + 1, rows_b, sem_b).wait()

            @pl.when(c + 2 < n_chunks)
            def _():
                fetch(c + 2, rows_a, sem_a).start()

            put(c + 1, rows_b)

    return gather(table, idx.reshape(SC_WORKERS, n_chunks, SC_ROWS))


def _sc_scatter_rows(src, dest_t, n_out):
    n_src, width = src.shape
    top_k = dest_t.shape[0]
    per_worker = n_src // SC_WORKERS
    rows = SC_SCATTER_ROWS
    n_chunks = per_worker // rows
    assert per_worker % rows == 0
    dest = dest_t.reshape(top_k, SC_WORKERS, n_chunks, rows).transpose(1, 2, 0, 3)
    mesh = plsc.VectorSubcoreMesh(core_axis_name="c", subcore_axis_name="s",
                                  num_cores=SC_CORES, num_subcores=SC_SUBCORES)

    @functools.partial(
        pl.kernel, mesh=mesh,
        out_type=jax.ShapeDtypeStruct((n_out, width), src.dtype),
        scratch_types=[pltpu.VMEM((n_chunks, top_k, rows), jnp.int32),
                       pltpu.VMEM((rows, width), src.dtype),
                       pltpu.VMEM((rows, width), src.dtype),
                       pltpu.SemaphoreType.DMA,
                       pltpu.SemaphoreType.DMA,
                       pltpu.SemaphoreType.DMA],
        name="sc_scatter_rows",
    )
    def scatter(src_hbm, dest_hbm, out_hbm, idx_v, rows_a, rows_b, sem_a, sem_b, sem_w):
        wid = lax.axis_index("s") * SC_CORES + lax.axis_index("c")
        base = wid * per_worker
        pltpu.sync_copy(dest_hbm.at[wid], idx_v)

        def load(c, buf, sem):
            off = pl.multiple_of(base + c * rows, rows)
            return pltpu.make_async_copy(src_hbm.at[pl.ds(off, rows)], buf, sem)

        def spread(c, buf):
            copies = [pltpu.make_async_copy(buf, out_hbm.at[idx_v.at[c, k]], sem_w) for k in range(top_k)]
            for cp in copies:
                cp.start()
            for cp in copies:
                cp.wait()

        load(0, rows_a, sem_a).start()

        @pl.loop(0, n_chunks - 1, step=2)
        def _(c):
            load(c, rows_a, sem_a).wait()
            load(c + 1, rows_b, sem_b).start()
            spread(c, rows_a)
            load(c + 1, rows_b, sem_b).wait()

            @pl.when(c + 2 < n_chunks)
            def _():
                load(c + 2, rows_a, sem_a).start()

            spread(c + 1, rows_b)

        if n_chunks % 2:
            load(n_chunks - 1, rows_a, sem_a).wait()
            spread(n_chunks - 1, rows_a)

    return scatter(src, dest)


def _expert_kernel(be_ref, slot_ref, next_ref, nused_ref, x_ref, wgu_hbm, wd_hbm, bg_ref, bl_ref, bd_ref, sel_ref, y_ref,
                   wgu_f, wd_f, wg_s, wl_s, wd_s, sems):
    i = pl.program_id(0)
    active = i < nused_ref[0]
    expert = be_ref[i]
    fresh = (i == 0) | (expert != be_ref[jnp.maximum(i - 1, 0)])

    def fetch(e, slot):
        return (pltpu.make_async_copy(wgu_hbm.at[e], wgu_f.at[slot], sems.at[slot, 0]),
                pltpu.make_async_copy(wd_hbm.at[e], wd_f.at[slot], sems.at[slot, 1]))

    @pl.when(active & fresh)
    def _():
        slot = slot_ref[i]

        @pl.when(i == 0)
        def _():
            for cp in fetch(expert, slot):
                cp.start()

        for cp in fetch(expert, slot):
            cp.wait()
        nxt = next_ref[i]

        @pl.when(nxt >= 0)
        def _():
            for cp in fetch(nxt, 1 - slot):
                cp.start()

        half = MXU_DIM // 2
        for t in range(2 * D_FF // MXU_DIM):
            src = wgu_f[slot, :, t * MXU_DIM:(t + 1) * MXU_DIM].astype(BF16)
            both = _dot(src, sel_ref[...]).astype(BF16)
            wg_s[:, t * half:(t + 1) * half] = both[:, :half]
            wl_s[:, t * half:(t + 1) * half] = both[:, half:]
        wd_s[...] = wd_f[slot].astype(BF16)

    @pl.when(active)
    def _():
        x = _unpack_bf16_pairs(x_ref[...]).astype(BF16)
        h_glu = jnp.minimum(_dot(x, wg_s[...]) + bg_ref[...], SWIGLU_LIMIT)
        h_lin = jnp.clip(_dot(x, wl_s[...]) + bl_ref[...], -SWIGLU_LIMIT, SWIGLU_LIMIT)
        act = h_glu * _sigmoid(SWIGLU_ALPHA * h_glu) * (h_lin + 1.0)
        y_ref[...] = _pack_bf16_pairs(_dot(act.astype(BF16), wd_s[...]) + bd_ref[...])

    @pl.when(jnp.logical_not(active))
    def _():
        y_ref[...] = jnp.zeros_like(y_ref)


def _experts(xb, block_expert, block_slot, block_next, n_used, w_gate_up, w_down, b_glu, b_lin, b_down, sel):
    n_slots = xb.shape[0]
    n_blocks = n_slots // EXPERT_BLOCK
    by_expert = lambda k, n: pl.BlockSpec((None, k, n), lambda i, be, sl, nx, nu: (be[i], 0, 0))
    blk = pl.BlockSpec((EXPERT_BLOCK, D_MODEL // 2), lambda i, be, sl, nx, nu: (i, 0))
    hbm = pl.BlockSpec(memory_space=pl.ANY)
    return pl.pallas_call(
        _expert_kernel,
        grid_spec=pltpu.PrefetchScalarGridSpec(
            num_scalar_prefetch=4, grid=(n_blocks,),
            in_specs=[blk, hbm, hbm, by_expert(1, D_FF), by_expert(1, D_FF), by_expert(1, D_MODEL),
                      pl.BlockSpec(sel.shape, lambda i, be, sl, nx, nu: (0, 0), pipeline_mode=pl.Buffered(1))],
            out_specs=blk,
            scratch_shapes=[pltpu.VMEM((2, D_MODEL, 2 * D_FF), F32), pltpu.VMEM((2, D_FF, D_MODEL), F32),
                            pltpu.VMEM((D_MODEL, D_FF), BF16), pltpu.VMEM((D_MODEL, D_FF), BF16),
                            pltpu.VMEM((D_FF, D_MODEL), BF16), pltpu.SemaphoreType.DMA((2, 2))]),
        out_shape=jax.ShapeDtypeStruct((n_slots, D_MODEL // 2), jnp.int32),
        compiler_params=_params(),
        name="experts",
    )(block_expert, block_slot, block_next, n_used, xb, w_gate_up, w_down, b_glu, b_lin, b_down, sel)


def _final_kernel(h1_ref, yg_ref, gate_ref, p_ref, gple_ref, wg_ref, wp_ref, out_ref):
    h2 = h1_ref[...]
    gates = gate_ref[...]
    for k in range(TOP_K):
        h2 = h2 + gates[:, k:k + 1] * _unpack_bf16_pairs(yg_ref[k])
    n3 = ((h2 * _rms(h2)) * gple_ref[...]).astype(BF16)
    gate = _sigmoid(_dot(n3, wg_ref[...]))
    out_ref[...] = h2 + gate * _dot(p_ref[...].astype(BF16), wp_ref[...])


def _final(h1, yg, gates, p, tile_offset, g_ple, w_ple_gate, w_ple_proj):
    n_tok = p.shape[0]
    tm = TOKEN_TILE
    return pl.pallas_call(
        _final_kernel,
        grid=(n_tok // tm,),
        in_specs=[pl.BlockSpec((tm, D_MODEL), lambda i: (i + tile_offset, 0)),
                  pl.BlockSpec((TOP_K, tm, D_MODEL // 2), lambda i: (0, i + tile_offset, 0)),
                  pl.BlockSpec((tm, TOP_K), lambda i: (i + tile_offset, 0)),
                  pl.BlockSpec((tm, PLE_DIM), lambda i: (i, 0)),
                  _resident(g_ple.shape), _resident(w_ple_gate.shape), _resident(w_ple_proj.shape)],
        out_specs=pl.BlockSpec((tm, D_MODEL), lambda i: (i, 0)),
        out_shape=jax.ShapeDtypeStruct((n_tok, D_MODEL), F32),
        compiler_params=_params(),
        name="final",
    )(h1, yg, gates, p, g_ple, w_ple_gate, w_ple_proj)


def _routing_tables(idx_t, pos_t, counts, n_slots):
    counts = counts.reshape(N_EXPERTS).astype(jnp.int32)
    pcounts = (counts + EXPERT_BLOCK - 1) // EXPERT_BLOCK * EXPERT_BLOCK
    pends = jnp.cumsum(pcounts)
    pstarts = pends - pcounts
    experts = jnp.arange(N_EXPERTS, dtype=jnp.int32)
    start_t = jnp.sum(jnp.where(idx_t[None] == experts[:, None, None], pstarts[:, None, None], 0), axis=0)
    dest_t = (start_t + pos_t).astype(jnp.int32)
    n_blocks = n_slots // EXPERT_BLOCK
    block_start = jnp.arange(n_blocks, dtype=jnp.int32) * EXPERT_BLOCK
    block_expert = jnp.minimum(jnp.sum(block_start[:, None] >= pends[None, :], axis=1), N_EXPERTS - 1).astype(jnp.int32)
    n_used = (pends[-1] // EXPERT_BLOCK).astype(jnp.int32).reshape(1)
    used = counts > 0
    slot_e = (jnp.cumsum(used.astype(jnp.int32)) - 1) & 1
    later_used = used[None, :] & (experts[None, :] > experts[:, None])
    next_e = jnp.min(jnp.where(later_used, experts[None, :], N_EXPERTS), axis=1)
    next_e = jnp.where(next_e == N_EXPERTS, -1, next_e)
    of_block = block_expert[:, None] == experts[None, :]
    block_slot = jnp.sum(jnp.where(of_block, slot_e[None, :], 0), axis=1).astype(jnp.int32)
    block_next = jnp.sum(jnp.where(of_block, next_e[None, :], 0), axis=1).astype(jnp.int32)
    return dest_t, block_expert, block_slot, block_next, n_used


def kernel(x_prompt, x_sample, cache_kv_w128, cache_kv_w512, cache_kv_w2048, p_prompt, p_sample, g_mix, w_in, g_v, g_q, g_k, w_spatial, b_spatial, w_branch_a, w_branch_b, w_out, g_moe, w_router, b_router, w_gate_up, b_gate_up, w_down, b_down, g_ple, w_ple_gate, w_ple_proj):
    bsz, seq, _ = x_prompt.shape
    dbatch, t_new, _ = x_sample.shape
    assert g_mix.shape[0] == 1
    caches = (cache_kv_w128, cache_kv_w512, cache_kv_w2048)
    l = 0
    n_p, n_s = bsz * seq, dbatch * t_new
    n_tok = n_p + n_s
    assert n_p % TOKEN_TILE == 0 and n_s % TOKEN_TILE == 0

    row2 = lambda t: t.reshape(1, -1)
    w_in_bf = w_in[l].astype(BF16)
    g_q_t = jnp.tile(g_q[l], B_HEADS).reshape(1, B_WIDTH)
    g_k_t = jnp.tile(g_k[l], B_HEADS).reshape(1, B_WIDTH)
    hid = np.arange(MXU_DIM) // HEAD_DIM
    hsum = jnp.asarray(hid[:, None] == hid[None, :], BF16)
    tril = jnp.tril(jnp.ones((CHUNK, CHUNK), bool))
    wm_prompt = jnp.where(tril[None], w_spatial[l], 0).astype(BF16)
    bias_prompt = jnp.repeat(b_spatial[l].T, A_GROUP_WIDTH, axis=1)
    reps = CHUNK // t_new
    small = jnp.where(tril[None, :t_new, :t_new], w_spatial[l][:, :t_new, :t_new], 0)
    wm_sample = jnp.einsum("ab,gij->gaibj", jnp.eye(reps, dtype=F32), small).reshape(A_GROUPS, CHUNK, CHUNK).astype(BF16)
    bias_sample = jnp.tile(bias_prompt[:t_new], (reps, 1))
    wr_hi = w_router[l].astype(BF16)
    wr_lo = (w_router[l] - wr_hi.astype(F32)).astype(BF16)
    tri = jnp.asarray(np.tril(np.ones((FINISH_TILE, FINISH_TILE), np.float32), -1), BF16)
    consts = (w_branch_a[l].astype(BF16), w_branch_b[l].astype(BF16), w_out[l].astype(BF16), row2(g_moe[l]),
              wr_hi, wr_lo, row2(b_router[l]), tri)
    half = MXU_DIM // 2
    sel_np = np.zeros((MXU_DIM, MXU_DIM), np.float32)
    sel_np[2 * np.arange(half), np.arange(half)] = 1.0
    sel_np[2 * np.arange(half) + 1, half + np.arange(half)] = 1.0
    sel = jnp.asarray(sel_np, BF16)

    proj = functools.partial(_project, g_mix=row2(g_mix[l]), w_in_bf=w_in_bf, g_v=row2(g_v[l]),
                             g_q_t=g_q_t, g_k_t=g_k_t, hsum=hsum)

    xp = x_prompt.reshape(n_p, D_MODEL)
    u_p, va_p, q_p, k_p, v_p, ga_p, gb_p = proj(xp, va_dtype=BF16)
    seq3 = lambda t: t.reshape(bsz, seq, B_WIDTH)
    attn_p = [_attend_prompt(seq3(q_p), seq3(k_p), seq3(v_p), gi, bsz, seq) for gi in range(N_GROUPS_B - 1)]
    o_p = _attend_prompt(seq3(q_p), seq3(k_p), seq3(v_p), N_GROUPS_B - 1, bsz, seq, others=attn_p)
    zero_counts = jnp.zeros((1, N_EXPERTS), F32)
    outs_p, cnt_p = _finish(xp, u_p, va_p, wm_prompt, bias_prompt, o_p, ga_p, gb_p, consts, zero_counts,
                            n_tok, 0, None)

    xs = x_sample.reshape(n_s, D_MODEL)
    u_s, va_s, q_s, k_s, v_s, ga_s, gb_s = proj(xs, va_dtype=F32)
    tok3 = lambda t: t.reshape(dbatch, t_new, B_WIDTH)
    caches_t = [jnp.transpose(c[l], (0, 2, 3, 4, 1)) for c in caches]
    o_s = _attend_sample(tok3(q_s), tok3(k_s), tok3(v_s), caches_t, dbatch, t_new)
    (h1, n2, idx_t, gates, pos_t), counts = _finish(xs, u_s, va_s, wm_sample, bias_sample, o_s, ga_s, gb_s,
                                                  consts, cnt_p, n_tok, n_p // FINISH_TILE, outs_p)

    n_blocks = -(-n_tok * TOP_K // EXPERT_BLOCK) + N_EXPERTS
    n_slots = n_blocks * EXPERT_BLOCK
    dest_t, block_expert, block_slot, block_next, n_used = _routing_tables(idx_t, pos_t, counts, n_slots)
    xb = _sc_scatter_rows(n2, dest_t, n_slots)
    yb = _experts(xb, block_expert, block_slot, block_next, n_used, w_gate_up[l], w_down[l],
                  b_gate_up[l][:, None, 0::2], b_gate_up[l][:, None, 1::2], b_down[l][:, None, :], sel)
    yg = _sc_gather_rows(yb, dest_t.reshape(-1)).reshape(TOP_K, n_tok, D_MODEL // 2)
    fin = functools.partial(_final, g_ple=row2(g_ple[l]), w_ple_gate=w_ple_gate[l].astype(BF16),
                            w_ple_proj=w_ple_proj[l].astype(BF16))
    y_prompt = fin(h1, yg, gates, p_prompt[l].reshape(n_p, PLE_DIM), 0).reshape(bsz, seq, D_MODEL)
    y_sample = fin(h1, yg, gates, p_sample[l].reshape(n_s, PLE_DIM), n_p // TOKEN_TILE).reshape(dbatch, t_new, D_MODEL)

    kv_prompt = []
    for gi in range(N_GROUPS_B):
        t = _kv_tail(seq3(k_p), seq3(v_p), gi, bsz, seq)
        keep = t.shape[-1]
        kv_prompt.append(jnp.transpose(t.reshape(bsz, 2, H_SLOT, HEAD_DIM, keep), (0, 4, 1, 2, 3))[None])
    kvs = _kv_sample(k_s, v_s, dbatch, t_new)
    kv_sample = [jnp.transpose(kvs[gi].reshape(t_new, 2, H_SLOT, HEAD_DIM, dbatch), (4, 0, 1, 2, 3))[None]
                 for gi in range(N_GROUPS_B)]
    va_out = va_s.reshape(1, dbatch, t_new, A_WIDTH)
    return (y_prompt, y_sample, *kv_prompt, *kv_sample, va_out)
```

```python
import functools

import numpy as np
import jax
import jax.numpy as jnp
from jax import lax
from jax.experimental import pallas as pl
from jax.experimental.pallas import tpu as pltpu
from jax.experimental.pallas import tpu_sc as plsc

F32 = jnp.float32
BF16 = jnp.bfloat16

D_MODEL = 1024
A_WIDTH = 1024
A_GROUPS = 4
A_GROUP_WIDTH = A_WIDTH // A_GROUPS
CHUNK = 128
HEAD_DIM = 64
H_SLOT = 8
GROUP_WIDTH = H_SLOT * HEAD_DIM
DILATION_GROUPS = ((128, 1), (512, 4), (2048, 16))
N_GROUPS_B = len(DILATION_GROUPS)
B_HEADS = H_SLOT * N_GROUPS_B
B_WIDTH = B_HEADS * HEAD_DIM
N_EXPERTS = 32
TOP_K = 4
D_FF = 1024
SWIGLU_ALPHA = 1.702
SWIGLU_LIMIT = 7.0
PLE_DIM = 256
RMS_EPS = 1e-6
NEG_BIG = -1e30
LOG2_E = float(np.log2(np.e))
LN_2 = float(np.log(2.0))

LANES = 128
MXU_DIM = 256
HEADS_PER_VREG = LANES // HEAD_DIM
PAIRS = GROUP_WIDTH // LANES
RESIDUE_UNROLL = 8
SAMPLE_SEQS_PER_STEP = 2
ATTN_STEP_ROWS = 1024
TOKEN_TILE = 512
FINISH_TILE = 512
EXPERT_BLOCK = 256
VMEM_LIMIT = 56 * 1024 * 1024

SC_CORES = 2
SC_SUBCORES = 16
SC_WORKERS = SC_CORES * SC_SUBCORES
SC_ROWS = 64
SC_SCATTER_ROWS = 32

_COL_SPLITS = np.cumsum([0, A_WIDTH, A_WIDTH, B_WIDTH, B_WIDTH, B_WIDTH, D_MODEL, D_MODEL]).tolist()


def _alibi_slopes():
    return np.exp2(-8.0 * np.arange(1, B_HEADS + 1, dtype=np.float32) / B_HEADS).astype(np.float32)


def _sigmoid(x):
    return 1.0 / (1.0 + jnp.exp(-x))


def _rms(x):
    return lax.rsqrt(jnp.mean(x * x, axis=-1, keepdims=True) + RMS_EPS)


def _dot(a, b):
    return jnp.dot(a, b, preferred_element_type=F32)


def _dot_nt(a, b):
    return lax.dot_general(a, b, (((1,), (1,)), ((), ())), preferred_element_type=F32)


def _pack_bf16_pairs(x):
    w = x.shape[1] // 2
    lo = lax.bitcast_convert_type(x[:, :w].astype(BF16).astype(F32), jnp.uint32) >> 16
    hi = lax.bitcast_convert_type(x[:, w:].astype(BF16).astype(F32), jnp.uint32) & jnp.uint32(0xFFFF0000)
    return lax.bitcast_convert_type(lo | hi, jnp.int32)


def _unpack_bf16_pairs(p):
    u = lax.bitcast_convert_type(p, jnp.uint32)
    lo = lax.bitcast_convert_type(u << 16, F32)
    hi = lax.bitcast_convert_type(u & jnp.uint32(0xFFFF0000), F32)
    return jnp.concatenate([lo, hi], axis=1)


def _resident(shape):
    nd = len(shape)
    return pl.BlockSpec(shape, lambda *_: (0,) * nd, pipeline_mode=pl.Buffered(1))


def _params(n_axes=1):
    return pltpu.CompilerParams(dimension_semantics=("arbitrary",) * n_axes, vmem_limit_bytes=VMEM_LIMIT)


def _proj_kernel(x_ref, gmix_ref, w_ref, gv_ref, gq_ref, gk_ref, hsum_ref,
                 u_ref, va_ref, q_ref, k_ref, v_ref, ga_ref, gb_ref):
    x = x_ref[...]
    n = ((x * _rms(x)) * gmix_ref[...]).astype(BF16)

    def section(i):
        return _dot(n, w_ref[:, _COL_SPLITS[i]:_COL_SPLITS[i + 1]])

    u_ref[...] = jax.nn.gelu(section(0)).astype(u_ref.dtype)
    va = jax.nn.gelu(section(1))
    va_ref[...] = ((va * _rms(va)) * gv_ref[...]).astype(va_ref.dtype)

    def head_norm(z, g_ref, scale):
        parts = []
        for c in range(B_WIDTH // MXU_DIM):
            zc = z[:, c * MXU_DIM:(c + 1) * MXU_DIM]
            ss = _dot((zc * zc).astype(BF16), hsum_ref[...])
            parts.append(zc * lax.rsqrt(ss * (1.0 / HEAD_DIM) + RMS_EPS))
        return jnp.concatenate(parts, axis=1) * (g_ref[...] * scale)

    q_ref[...] = head_norm(section(2), gq_ref, HEAD_DIM ** -0.5 * LOG2_E)
    k_ref[...] = head_norm(section(3), gk_ref, 1.0)
    v_ref[...] = section(4)
    ga_ref[...] = _sigmoid(section(5)).astype(ga_ref.dtype)
    gb_ref[...] = _sigmoid(section(6)).astype(gb_ref.dtype)


def _project(x, g_mix, w_in_bf, g_v, g_q_t, g_k_t, hsum, va_dtype):
    n_tok = x.shape[0]
    tm = TOKEN_TILE
    row = lambda w: pl.BlockSpec((tm, w), lambda i: (i, 0))
    outs = [(A_WIDTH, BF16), (A_WIDTH, va_dtype), (B_WIDTH, F32), (B_WIDTH, F32), (B_WIDTH, F32),
            (D_MODEL, BF16), (D_MODEL, BF16)]
    return pl.pallas_call(
        _proj_kernel,
        grid=(n_tok // tm,),
        in_specs=[row(D_MODEL), _resident(g_mix.shape), _resident(w_in_bf.shape), _resident(g_v.shape),
                  _resident(g_q_t.shape), _resident(g_k_t.shape), _resident(hsum.shape)],
        out_specs=[row(w) for w, _ in outs],
        out_shape=[jax.ShapeDtypeStruct((n_tok, w), dt) for w, dt in outs],
        compiler_params=_params(),
        name="project",
    )(x, g_mix, w_in_bf, g_v, g_q_t, g_k_t, hsum)


def _band_bias(band, dil, slopes):
    qi = jnp.arange(band, dtype=jnp.int32)[:, None]
    kj = jnp.arange(2 * band, dtype=jnp.int32)[None, :]
    dist = qi + band - kj
    in_band = (dist >= 0) & (dist <= band)
    valid = jnp.stack([in_band & (kj >= band), in_band])
    penalty = (jnp.asarray(slopes, F32) * LOG2_E)[:, None, None] * (dist * dil).astype(F32)[None]
    return jnp.where(valid[:, None], -penalty[None], NEG_BIG)


def _pair_attention(q2, k, v, bias_pair):
    band = q2.shape[0]
    first = lax.broadcasted_iota(jnp.int32, (band, LANES), 1) < HEAD_DIM
    qs = jnp.concatenate([jnp.where(first, q2, 0.0), jnp.where(first, 0.0, q2)], axis=0).astype(BF16)
    s = _dot_nt(qs, k) + bias_pair.reshape(HEADS_PER_VREG * band, 2 * band)
    m = jnp.max(s, axis=-1, keepdims=True)
    e = jnp.exp2(s - m).astype(BF16)
    v_ones = jnp.concatenate([v, jnp.ones_like(v)], axis=1)
    r = _dot(e, v_ones)
    den = r[:, LANES:]
    o2 = r[:, :LANES] / den
    lse = m * LN_2 + jnp.log(den)
    return jnp.where(first, o2[:band], o2[band:]), jnp.where(first, lse[:band], lse[band:])


def _attn_dense_kernel(q_ref, kp_ref, kc_ref, vp_ref, vc_ref, bias0_ref, bias_ref, o_ref, l_ref, *, band, n_sub):
    for i in range(n_sub):
        rows = slice(i * band, (i + 1) * band)
        before = slice((i - 1) * band, i * band)
        for p in range(PAIRS):
            cols = slice(p * LANES, (p + 1) * LANES)
            k_prev = kc_ref[before, cols] if i else kp_ref[:, cols]
            v_prev = vc_ref[before, cols] if i else vp_ref[:, cols]
            k = jnp.concatenate([k_prev, kc_ref[rows, cols]], axis=0).astype(BF16)
            v = jnp.concatenate([v_prev, vc_ref[rows, cols]], axis=0).astype(BF16)
            table = bias_ref if i else bias0_ref
            bias2 = table[p * HEADS_PER_VREG:(p + 1) * HEADS_PER_VREG]
            o_pair, l_pair = _pair_attention(q_ref[rows, cols], k, v, bias2)
            o_ref[rows, cols] = o_pair.astype(o_ref.dtype)
            l_ref[rows, cols] = l_pair


def _attn_dilated_kernel(q_ref, kp_ref, kc_ref, vp_ref, vc_ref, bias0_ref, bias_ref, *rest,
                         band, dil, n_sub, n_others):
    others, rest = rest[:2 * n_others], rest[2 * n_others:]
    if n_others:
        out_ref, o_ref, l_ref = rest
    else:
        o_ref, l_ref = rest
    win = band * dil
    unroll = min(RESIDUE_UNROLL, dil)

    def residues(it, carry):
        for i in range(n_sub):
            for u in range(unroll):
                r = it * unroll + u
                rows = pl.ds(i * win + r, band, stride=dil)
                before = pl.ds((i - 1) * win + r, band, stride=dil) if i else pl.ds(r, band, stride=dil)
                k_prev = kc_ref[before, :] if i else kp_ref[before, :]
                v_prev = vc_ref[before, :] if i else vp_ref[before, :]
                k = jnp.concatenate([k_prev, kc_ref[rows, :]], axis=0).astype(BF16)
                v = jnp.concatenate([v_prev, vc_ref[rows, :]], axis=0).astype(BF16)
                table = bias_ref if i else bias0_ref
                o_pair, l_pair = _pair_attention(q_ref[rows, :], k, v, table[...])
                o_ref[rows, :] = o_pair
                l_ref[rows, :] = l_pair
        return carry

    lax.fori_loop(0, dil // unroll, residues, 0)

    if n_others:
        outs = [others[2 * g][...].astype(F32) for g in range(n_others)] + [o_ref[...]]
        lses = [others[2 * g + 1][...] for g in range(n_others)] + [l_ref[...]]
        mx = functools.reduce(jnp.maximum, lses)
        ws = [jnp.exp(l - mx) for l in lses]
        wsum = functools.reduce(jnp.add, ws)
        out_ref[...] = (sum(w * o for w, o in zip(ws, outs)) / wsum).astype(out_ref.dtype)


def _attend_prompt(q, k, v, gi, bsz, seq, others=()):
    win, dil = DILATION_GROUPS[gi]
    band = win // dil
    bias = _band_bias(band, dil, _alibi_slopes()[gi * H_SLOT:(gi + 1) * H_SLOT])
    step_rows = ATTN_STEP_ROWS if dil == 1 else 4 * ATTN_STEP_ROWS
    n_sub = max(1, min(step_rows // win, seq // win))
    step = n_sub * win
    if dil == 1:
        cur = pl.BlockSpec((None, step, GROUP_WIDTH), lambda b, j: (b, j, gi))
        prev = pl.BlockSpec((None, win, GROUP_WIDTH), lambda b, j: (b, jnp.maximum(j * n_sub - 1, 0), gi))
        tab0 = pl.BlockSpec((None, H_SLOT, band, 2 * band), lambda b, j: (jnp.minimum(j, 1), 0, 0, 0))
        tab = pl.BlockSpec((None, H_SLOT, band, 2 * band), lambda b, j: (1, 0, 0, 0))
        out = pl.BlockSpec((None, step, GROUP_WIDTH), lambda b, j: (b, j, 0))
        o, lse = pl.pallas_call(
            functools.partial(_attn_dense_kernel, band=band, n_sub=n_sub),
            grid=(bsz, seq // step),
            in_specs=[cur, prev, cur, prev, cur, tab0, tab],
            out_specs=[out, out],
            out_shape=[jax.ShapeDtypeStruct((bsz, seq, GROUP_WIDTH), BF16),
                       jax.ShapeDtypeStruct((bsz, seq, GROUP_WIDTH), F32)],
            compiler_params=_params(2),
            name=f"attn_prompt_w{win}",
        )(q, k, k, v, v, bias, bias)
    else:
        col = lambda p: gi * PAIRS + p
        cur = pl.BlockSpec((None, step, LANES), lambda b, j, p: (b, j, col(p)))
        prev = pl.BlockSpec((None, win, LANES), lambda b, j, p: (b, jnp.maximum(j * n_sub - 1, 0), col(p)))
        tab0 = pl.BlockSpec((None, HEADS_PER_VREG, band, 2 * band), lambda b, j, p: (jnp.minimum(j, 1), p, 0, 0))
        tab = pl.BlockSpec((None, HEADS_PER_VREG, band, 2 * band), lambda b, j, p: (1, p, 0, 0))
        out = pl.BlockSpec((None, step, LANES), lambda b, j, p: (b, j, p))
        kern = functools.partial(_attn_dilated_kernel, band=band, dil=dil, n_sub=n_sub, n_others=len(others))
        common = dict(grid=(bsz, seq // step, PAIRS), compiler_params=_params(3), name=f"attn_prompt_w{win}")
        ins = [cur, prev, cur, prev, cur, tab0, tab]
        if others:
            extra = [t.reshape(bsz, seq, GROUP_WIDTH) for pair in others for t in pair]
            combined = pl.pallas_call(
                kern,
                in_specs=ins + [out] * len(extra),
                out_specs=out,
                out_shape=jax.ShapeDtypeStruct((bsz, seq, GROUP_WIDTH), BF16),
                scratch_shapes=[pltpu.VMEM((step, LANES), F32), pltpu.VMEM((step, LANES), F32)],
                **common,
            )(q, k, k, v, v, bias, bias, *extra)
            return combined.reshape(bsz * seq, GROUP_WIDTH)
        o, lse = pl.pallas_call(
            kern,
            in_specs=ins,
            out_specs=[out, out],
            out_shape=[jax.ShapeDtypeStruct((bsz, seq, GROUP_WIDTH), F32)] * 2,
            **common,
        )(q, k, k, v, v, bias, bias)
    return o.reshape(bsz * seq, GROUP_WIDTH), lse.reshape(bsz * seq, GROUP_WIDTH)


def _attn_sample_kernel(q_ref, kn_ref, vn_ref, c0_ref, c1_ref, c2_ref, o_ref, *, t_new, slopes):
    for b in range(q_ref.shape[0]):
        _attn_sample_one(q_ref.at[b], kn_ref.at[b], vn_ref.at[b], c0_ref.at[b], c1_ref.at[b], c2_ref.at[b],
                         o_ref.at[b], t_new=t_new, slopes=slopes)


def _attn_sample_one(q_ref, kn_ref, vn_ref, c0_ref, c1_ref, c2_ref, o_ref, *, t_new, slopes):
    rows = H_SLOT * t_new
    pad_new = 16
    t_shift = t_new.bit_length() - 1
    caches = (c0_ref, c1_ref, c2_ref)
    row_id = lax.broadcasted_iota(jnp.int32, (rows, GROUP_WIDTH), 0)
    col_id = lax.broadcasted_iota(jnp.int32, (rows, GROUP_WIDTH), 1)
    own_head = (row_id >> t_shift) == (col_id >> (HEAD_DIM.bit_length() - 1))
    head_of_row = lax.broadcasted_iota(jnp.int32, (rows, 1), 0) >> t_shift
    zpad = jnp.zeros((pad_new - t_new, GROUP_WIDTH), F32)
    outs, lses = [], []
    for g, (win, dil) in enumerate(DILATION_GROUPS):
        buf_len = caches[g].shape[-1]
        cols = slice(g * GROUP_WIDTH, (g + 1) * GROUP_WIDTH)

        def mask_of(n_keys, first_pos, n_real):
            i_q = lax.broadcasted_iota(jnp.int32, (rows, n_keys), 0) & (t_new - 1)
            key = lax.broadcasted_iota(jnp.int32, (rows, n_keys), 1)
            delta = buf_len + i_q - (first_pos + key)
            ok = (delta >= 0) & ((delta & (dil - 1)) == 0) & (delta <= win) & (key < n_real)
            return ok, delta.astype(F32)

        kt = caches[g][0].reshape(GROUP_WIDTH, buf_len).astype(BF16)
        vt = caches[g][1].reshape(GROUP_WIDTH, buf_len).astype(BF16)
        qblk = jnp.where(own_head, jnp.concatenate([q_ref[:, cols]] * H_SLOT, axis=0), 0.0).astype(BF16)
        kn = jnp.concatenate([kn_ref[:, cols], zpad], axis=0).astype(BF16)
        vn = jnp.concatenate([vn_ref[:, cols], zpad], axis=0).astype(BF16)
        slope_rows = jnp.zeros((rows, 1), F32)
        for h in range(H_SLOT):
            slope_rows = jnp.where(head_of_row == h, float(slopes[g * H_SLOT + h]) * LOG2_E, slope_rows)
        ok_c, delta_c = mask_of(buf_len, 0, buf_len)
        ok_n, delta_n = mask_of(pad_new, buf_len, t_new)
        sc = jnp.where(ok_c, _dot(qblk, kt) - slope_rows * delta_c, NEG_BIG)
        sn = jnp.where(ok_n, _dot_nt(qblk, kn) - slope_rows * delta_n, NEG_BIG)
        m = jnp.maximum(jnp.max(sc, axis=-1, keepdims=True), jnp.max(sn, axis=-1, keepdims=True))
        ec = jnp.exp2(sc - m)
        en = jnp.exp2(sn - m)
        den = jnp.sum(ec, axis=-1, keepdims=True) + jnp.sum(en, axis=-1, keepdims=True)
        outs.append((_dot_nt(ec.astype(BF16), vt) + _dot(en.astype(BF16), vn)) * (1.0 / den))
        lses.append(m * LN_2 + jnp.log(den))
    mx = functools.reduce(jnp.maximum, lses)
    ws = [jnp.exp(l - mx) for l in lses]
    wsum = functools.reduce(jnp.add, ws)
    comb = sum((w / wsum) * o for w, o in zip(ws, outs))
    comb = jnp.where(own_head, comb, 0.0)
    o_tok = sum(comb[h * t_new:(h + 1) * t_new, :] for h in range(H_SLOT))
    o_ref[...] = o_tok.astype(o_ref.dtype)


def _attend_sample(q, k_new, v_new, caches_t, dbatch, t_new):
    assert t_new & (t_new - 1) == 0 and t_new <= 8
    nb = SAMPLE_SEQS_PER_STEP if dbatch % SAMPLE_SEQS_PER_STEP == 0 else 1
    tok = pl.BlockSpec((nb, t_new, B_WIDTH), lambda b: (b, 0, 0))
    cache_specs = [pl.BlockSpec((nb,) + c.shape[1:], lambda b: (b, 0, 0, 0, 0)) for c in caches_t]
    o = pl.pallas_call(
        functools.partial(_attn_sample_kernel, t_new=t_new, slopes=_alibi_slopes()),
        grid=(dbatch // nb,),
        in_specs=[tok, tok, tok] + cache_specs,
        out_specs=pl.BlockSpec((nb, t_new, GROUP_WIDTH), lambda b: (b, 0, 0)),
        out_shape=jax.ShapeDtypeStruct((dbatch, t_new, GROUP_WIDTH), BF16),
        compiler_params=_params(),
        name="attn_sample",
    )(q, k_new, v_new, *caches_t)
    return o.reshape(dbatch * t_new, GROUP_WIDTH)


def _kv_tail_kernel(k_ref, v_ref, o_ref):
    o_ref[0] = k_ref[...].T
    o_ref[1] = v_ref[...].T


def _kv_tail(k, v, gi, bsz, seq):
    keep = min(DILATION_GROUPS[gi][0], seq)
    pb = min(keep, 512)
    first = (seq - keep) // pb
    src = pl.BlockSpec((None, pb, GROUP_WIDTH), lambda b, j: (b, first + j, gi))
    return pl.pallas_call(
        _kv_tail_kernel,
        grid=(bsz, keep // pb),
        in_specs=[src, src],
        out_specs=pl.BlockSpec((None, 2, GROUP_WIDTH, pb), lambda b, j: (b, 0, 0, j)),
        out_shape=jax.ShapeDtypeStruct((bsz, 2, GROUP_WIDTH, keep), F32),
        compiler_params=_params(2),
        name=f"kv_tail_w{DILATION_GROUPS[gi][0]}",
    )(k, v)


def _kv_sample_kernel(*refs, t_new, dbatch):
    in_refs, out_refs = refs[:2 * N_GROUPS_B], refs[2 * N_GROUPS_B:]
    for g, o_ref in enumerate(out_refs):
        k_ref, v_ref = in_refs[2 * g], in_refs[2 * g + 1]
        for t in range(t_new):
            rows = pl.ds(t, dbatch, stride=t_new)
            o_ref[t, 0] = k_ref[rows, :].T
            o_ref[t, 1] = v_ref[rows, :].T


def _kv_sample(k, v, dbatch, t_new):
    src = lambda g: pl.BlockSpec((dbatch * t_new, LANES), lambda s: (0, g * PAIRS + s))
    return pl.pallas_call(
        functools.partial(_kv_sample_kernel, t_new=t_new, dbatch=dbatch),
        grid=(PAIRS,),
        in_specs=[src(g) for g in range(N_GROUPS_B) for _ in range(2)],
        out_specs=[pl.BlockSpec((t_new, 2, LANES, dbatch), lambda s: (0, 0, s, 0))] * N_GROUPS_B,
        out_shape=[jax.ShapeDtypeStruct((t_new, 2, GROUP_WIDTH, dbatch), F32)] * N_GROUPS_B,
        compiler_params=_params(),
        name="kv_sample",
    )(*[t for _ in range(N_GROUPS_B) for t in (k, v)])


def _topk_route(logits):
    tm = logits.shape[0]
    lane = lax.broadcasted_iota(jnp.int32, logits.shape, 1).astype(F32)
    slot = lax.broadcasted_iota(jnp.int32, (tm, TOP_K), 1)
    val_out = jnp.zeros((tm, TOP_K), F32)
    work = logits
    ids, top = [], None
    for r in range(TOP_K):
        mx = jnp.max(work, axis=-1, keepdims=True)
        ix = jnp.min(jnp.where(work == mx, lane, float(N_EXPERTS)), axis=-1, keepdims=True)
        top = mx if top is None else top
        ids.append(ix)
        val_out = jnp.where(slot == r, jnp.exp(mx - top), val_out)
        work = jnp.where(lane == ix, -jnp.inf, work)
    gates = val_out / jnp.sum(val_out, axis=-1, keepdims=True)
    return ids, gates


_FINISH_INPUTS = 17
_RANK_DIGITS = 3
_ROUTE_ROWS = TOP_K * (1 + _RANK_DIGITS)


def _finish_kernel(*refs, n_alias, n_valid):
    ins, outs = refs[:_FINISH_INPUTS], refs[_FINISH_INPUTS + n_alias:]
    step = pl.program_id(0)

    @pl.when(step < n_valid)
    def _():
        _finish_tile(ins, outs)

    @pl.when(step >= n_valid)
    def _():
        for ref in outs[:5]:
            ref[...] = jnp.zeros_like(ref)


def _finish_tile(ins, outs):
    (h_ref, u_ref, va_ref, wm_ref, bias_ref, ob_ref, ga_ref, gb_ref, wa_ref, wb_ref, wo_ref, gmoe_ref,
     wrh_ref, wrl_ref, br_ref, tri_ref, cnt_in_ref) = ins
    h1_ref, n2_ref, idx_ref, gate_ref, pos_ref, cnt_out_ref, carry_ref = outs
    tm = h_ref.shape[0]

    chunks = []
    for c in range(tm // CHUNK):
        rows = slice(c * CHUNK, (c + 1) * CHUNK)
        va_c = va_ref[rows, :].astype(BF16)
        mixed = jnp.concatenate(
            [_dot(wm_ref[g], va_c[:, g * A_GROUP_WIDTH:(g + 1) * A_GROUP_WIDTH]) for g in range(A_GROUPS)], axis=1)
        chunks.append((u_ref[rows, :].astype(F32) * (mixed + bias_ref[...])).astype(BF16))
    branch_a = _dot(jnp.concatenate(chunks, axis=0), wa_ref[...])

    branch_b = _dot(ob_ref[...], wb_ref[...])

    mix = ga_ref[...].astype(F32) * branch_a + gb_ref[...].astype(F32) * branch_b
    h1 = h_ref[...] + _dot(mix.astype(BF16), wo_ref[...])
    h1_ref[...] = h1

    n2 = (h1 * _rms(h1)) * gmoe_ref[...]
    n2_ref[...] = _pack_bf16_pairs(n2)
    n_hi = n2.astype(BF16)
    n_lo = (n2 - n_hi.astype(F32)).astype(BF16)
    logits = _dot(n_hi, wrh_ref[...]) + _dot(n_lo, wrh_ref[...]) + _dot(n_hi, wrl_ref[...]) + br_ref[...]
    ids, gates = _topk_route(logits)
    gate_ref[...] = gates

    @pl.when(pl.program_id(0) == 0)
    def _():
        carry_ref[...] = cnt_in_ref[...]

    lane = lax.broadcasted_iota(jnp.int32, logits.shape, 1).astype(F32)
    sel = sum(jnp.where(lane == ix, 1.0, 0.0) for ix in ids)
    before = _dot(tri_ref[...], sel.astype(BF16)) + carry_ref[...]
    ranks = [jnp.sum(jnp.where(lane == ix, before, 0.0), axis=-1, keepdims=True) for ix in ids]
    carry_ref[...] = carry_ref[...] + jnp.sum(sel, axis=0, keepdims=True)
    cnt_out_ref[...] = carry_ref[...]

    wide = lax.broadcasted_iota(jnp.int32, (tm, LANES), 1)
    cols = jnp.zeros((tm, LANES), F32)
    for r in range(TOP_K):
        rank_i = ranks[r].astype(jnp.int32)
        digits = [ids[r]] + [((rank_i >> (8 * d)) & 255).astype(F32) for d in range(_RANK_DIGITS)]
        for d, col in enumerate(digits):
            cols = jnp.where(wide == d * TOP_K + r, col, cols)
    eye = (lax.broadcasted_iota(jnp.int32, (_ROUTE_ROWS, LANES), 0)
           == lax.broadcasted_iota(jnp.int32, (_ROUTE_ROWS, LANES), 1)).astype(BF16)
    rows_t = _dot_nt(eye, cols.astype(BF16))
    idx_ref[...] = rows_t[:TOP_K].astype(jnp.int32)
    pos = sum(rows_t[(1 + d) * TOP_K:(2 + d) * TOP_K] * float(256 ** d) for d in range(_RANK_DIGITS))
    pos_ref[...] = pos.astype(jnp.int32)


def _finish(h, u, va, wm, bias_full, o_b, ga, gb, consts, cnt_in, n_total, tile_offset, prev_outs):
    n_tok = h.shape[0]
    tm = FINISH_TILE
    n_valid = n_tok // tm
    n_steps = n_valid if prev_outs is not None else n_total // tm - tile_offset
    row = lambda w: pl.BlockSpec((tm, w), lambda i: (jnp.minimum(i, n_valid - 1), 0))
    orow = lambda w: pl.BlockSpec((tm, w), lambda i: (i + tile_offset, 0))
    args = [h, u, va, wm, bias_full, o_b, ga, gb] + list(consts) + [cnt_in]
    specs = [row(D_MODEL), row(A_WIDTH), row(A_WIDTH), _resident(wm.shape), _resident(bias_full.shape),
             row(GROUP_WIDTH), row(D_MODEL), row(D_MODEL)]
    specs += [_resident(t.shape) for t in consts] + [_resident(cnt_in.shape)]
    assert len(args) == _FINISH_INPUTS
    aliases = {}
    if prev_outs is not None:
        for k, t in enumerate(prev_outs):
            aliases[len(args)] = k
            args.append(t)
            specs.append(pl.BlockSpec(memory_space=pl.ANY))
    by_choice = pl.BlockSpec((TOP_K, tm), lambda i: (0, i + tile_offset))
    tokens = lambda w, dt: jax.ShapeDtypeStruct((n_total, w), dt)
    choices = jax.ShapeDtypeStruct((TOP_K, n_total), jnp.int32)
    outs = pl.pallas_call(
        functools.partial(_finish_kernel, n_alias=len(aliases), n_valid=n_valid),
        grid=(n_steps,),
        in_specs=specs,
        out_specs=[orow(D_MODEL), orow(D_MODEL // 2), by_choice, orow(TOP_K), by_choice, _resident(cnt_in.shape)],
        out_shape=[tokens(D_MODEL, F32), tokens(D_MODEL // 2, jnp.int32), choices, tokens(TOP_K, F32), choices,
                   jax.ShapeDtypeStruct(cnt_in.shape, F32)],
        scratch_shapes=[pltpu.VMEM(cnt_in.shape, F32)],
        input_output_aliases=aliases,
        compiler_params=_params(),
        name="finish",
    )(*args)
    return outs[:5], outs[5]


def _sc_gather_rows(table, idx):
    m = idx.shape[0]
    width = table.shape[1]
    per_worker = m // SC_WORKERS
    n_chunks = per_worker // SC_ROWS
    mesh = plsc.VectorSubcoreMesh(core_axis_name="c", subcore_axis_name="s",
                                  num_cores=SC_CORES, num_subcores=SC_SUBCORES)

    n_buf = 3
    assert per_worker % SC_ROWS == 0 and n_chunks >= n_buf

    @functools.partial(
        pl.kernel, mesh=mesh,
        out_type=jax.ShapeDtypeStruct((m, width), table.dtype),
        scratch_types=[pltpu.VMEM((n_chunks, SC_ROWS), jnp.int32)]
                      + [pltpu.VMEM((SC_ROWS, width), table.dtype)] * n_buf
                      + [pltpu.SemaphoreType.DMA] * (2 * n_buf),
        name="sc_gather_rows",
    )
    def gather(table_hbm, idx_hbm, out_hbm, idx_v, *scratch):
        bufs, read_sems, write_sems = scratch[:n_buf], scratch[n_buf:2 * n_buf], scratch[2 * n_buf:]
        wid = lax.axis_index("s") * SC_CORES + lax.axis_index("c")
        base = wid * per_worker
        pltpu.sync_copy(idx_hbm.at[wid], idx_v)

        def fetch(c, k):
            return pltpu.make_async_copy(table_hbm.at[idx_v.at[c]], bufs[k], read_sems[k])

        def put(c, k):
            off = pl.multiple_of(base + c * SC_ROWS, SC_ROWS)
            return pltpu.make_async_copy(bufs[k], out_hbm.at[pl.ds(off, SC_ROWS)], write_sems[k])

        def stage(c, k):
            fetch(c, k).wait()
            put(c, k).start()
            put(c - 1, (k + n_buf - 1) % n_buf).wait()
            nxt = c + 2
            if isinstance(nxt, int):
                if nxt < n_chunks:
                    fetch(nxt, (k + 2) % n_buf).start()
            else:
                @pl.when(nxt < n_chunks)
                def _():
                    fetch(nxt, (k + 2) % n_buf).start()

        fetch(0, 0).start()
        fetch(1, 1).start()
        fetch(0, 0).wait()
        put(0, 0).start()
        fetch(2, 2).start()

        n_groups = (n_chunks - 1) // n_buf

        @pl.loop(0, n_groups)
        def _(g):
            first = 1 + g * n_buf
            for j in range(n_buf):
                stage(first + j, (1 + j) % n_buf)

        for c in range(1 + n_groups * n_buf, n_chunks):
            stage(c, c % n_buf)
        put(n_chunks - 1, (n_chunks - 1) % n_buf).wait()

    return gather(table, idx.reshape(SC_WORKERS, n_chunks, SC_ROWS))


def _sc_scatter_rows(src, dest_t, n_out):
    n_src, width = src.shape
    top_k = dest_t.shape[0]
    per_worker = n_src // SC_WORKERS
    rows = SC_SCATTER_ROWS
    n_chunks = per_worker // rows
    assert per_worker % rows == 0
    dest = dest_t.reshape(top_k, SC_WORKERS, n_chunks, rows).transpose(1, 2, 0, 3)
    mesh = plsc.VectorSubcoreMesh(core_axis_name="c", subcore_axis_name="s",
                                  num_cores=SC_CORES, num_subcores=SC_SUBCORES)

    @functools.partial(
        pl.kernel, mesh=mesh,
        out_type=jax.ShapeDtypeStruct((n_out, width), src.dtype),
        scratch_types=[pltpu.VMEM((n_chunks, top_k, rows), jnp.int32),
                       pltpu.VMEM((rows, width), src.dtype),
                       pltpu.VMEM((rows, width), src.dtype),
                       pltpu.SemaphoreType.DMA,
                       pltpu.SemaphoreType.DMA,
                       pltpu.SemaphoreType.DMA],
        name="sc_scatter_rows",
    )
    def scatter(src_hbm, dest_hbm, out_hbm, idx_v, rows_a, rows_b, sem_a, sem_b, sem_w):
        wid = lax.axis_index("s") * SC_CORES + lax.axis_index("c")
        base = wid * per_worker
        pltpu.sync_copy(dest_hbm.at[wid], idx_v)

        def load(c, buf, sem):
            off = pl.multiple_of(base + c * rows, rows)
            return pltpu.make_async_copy(src_hbm.at[pl.ds(off, rows)], buf, sem)

        def spread(c, buf):
            copies = [pltpu.make_async_copy(buf, out_hbm.at[idx_v.at[c, k]], sem_w) for k in range(top_k)]
            for cp in copies:
                cp.start()
            for cp in copies:
                cp.wait()

        load(0, rows_a, sem_a).start()

        @pl.loop(0, n_chunks - 1, step=2)
        def _(c):
            load(c, rows_a, sem_a).wait()
            load(c + 1, rows_b, sem_b).start()
            spread(c, rows_a)
            load(c + 1, rows_b, sem_b).wait()

            @pl.when(c + 2 < n_chunks)
            def _():
                load(c + 2, rows_a, sem_a).start()

            spread(c + 1, rows_b)

        if n_chunks % 2:
            load(n_chunks - 1, rows_a, sem_a).wait()
            spread(n_chunks - 1, rows_a)

    return scatter(src, dest)


def _expert_kernel(be_ref, slot_ref, next_ref, nused_ref, x_ref, wgu_hbm, wd_hbm, bg_ref, bl_ref, bd_ref, sel_ref, y_ref,
                   wgu_f, wd_f, wg_s, wl_s, wd_s, sems):
    i = pl.program_id(0)
    active = i < nused_ref[0]
    expert = be_ref[i]
    fresh = (i == 0) | (expert != be_ref[jnp.maximum(i - 1, 0)])

    def fetch(e, slot):
        return (pltpu.make_async_copy(wgu_hbm.at[e], wgu_f.at[slot], sems.at[slot, 0]),
                pltpu.make_async_copy(wd_hbm.at[e], wd_f.at[slot], sems.at[slot, 1]))

    @pl.when(active & fresh)
    def _():
        slot = slot_ref[i]

        @pl.when(i == 0)
        def _():
            for cp in fetch(expert, slot):
                cp.start()

        for cp in fetch(expert, slot):
            cp.wait()
        nxt = next_ref[i]

        @pl.when(nxt >= 0)
        def _():
            for cp in fetch(nxt, 1 - slot):
                cp.start()

        half = MXU_DIM // 2
        for t in range(2 * D_FF // MXU_DIM):
            src = wgu_f[slot, :, t * MXU_DIM:(t + 1) * MXU_DIM].astype(BF16)
            both = _dot(src, sel_ref[...]).astype(BF16)
            wg_s[:, t * half:(t + 1) * half] = both[:, :half]
            wl_s[:, t * half:(t + 1) * half] = both[:, half:]
        wd_s[...] = wd_f[slot].astype(BF16)

    @pl.when(active)
    def _():
        x = _unpack_bf16_pairs(x_ref[...]).astype(BF16)
        h_glu = jnp.minimum(_dot(x, wg_s[...]) + bg_ref[...], SWIGLU_LIMIT)
        h_lin = jnp.clip(_dot(x, wl_s[...]) + bl_ref[...], -SWIGLU_LIMIT, SWIGLU_LIMIT)
        act = h_glu * _sigmoid(SWIGLU_ALPHA * h_glu) * (h_lin + 1.0)
        y_ref[...] = _pack_bf16_pairs(_dot(act.astype(BF16), wd_s[...]) + bd_ref[...])

    @pl.when(jnp.logical_not(active))
    def _():
        y_ref[...] = jnp.zeros_like(y_ref)


def _experts(xb, block_expert, block_slot, block_next, n_used, w_gate_up, w_down, b_glu, b_lin, b_down, sel):
    n_slots = xb.shape[0]
    n_blocks = n_slots // EXPERT_BLOCK
    by_expert = lambda k, n: pl.BlockSpec((None, k, n), lambda i, be, sl, nx, nu: (be[i], 0, 0))
    blk = pl.BlockSpec((EXPERT_BLOCK, D_MODEL // 2), lambda i, be, sl, nx, nu: (i, 0))
    hbm = pl.BlockSpec(memory_space=pl.ANY)
    return pl.pallas_call(
        _expert_kernel,
        grid_spec=pltpu.PrefetchScalarGridSpec(
            num_scalar_prefetch=4, grid=(n_blocks,),
            in_specs=[blk, hbm, hbm, by_expert(1, D_FF), by_expert(1, D_FF), by_expert(1, D_MODEL),
                      pl.BlockSpec(sel.shape, lambda i, be, sl, nx, nu: (0, 0), pipeline_mode=pl.Buffered(1))],
            out_specs=blk,
            scratch_shapes=[pltpu.VMEM((2, D_MODEL, 2 * D_FF), F32), pltpu.VMEM((2, D_FF, D_MODEL), F32),
                            pltpu.VMEM((D_MODEL, D_FF), BF16), pltpu.VMEM((D_MODEL, D_FF), BF16),
                            pltpu.VMEM((D_FF, D_MODEL), BF16), pltpu.SemaphoreType.DMA((2, 2))]),
        out_shape=jax.ShapeDtypeStruct((n_slots, D_MODEL // 2), jnp.int32),
        compiler_params=_params(),
        name="experts",
    )(block_expert, block_slot, block_next, n_used, xb, w_gate_up, w_down, b_glu, b_lin, b_down, sel)


def _final_kernel(h1_ref, yg_ref, gate_ref, p_ref, gple_ref, wg_ref, wp_ref, out_ref):
    h2 = h1_ref[...]
    gates = gate_ref[...]
    for k in range(TOP_K):
        h2 = h2 + gates[:, k:k + 1] * _unpack_bf16_pairs(yg_ref[k])
    n3 = ((h2 * _rms(h2)) * gple_ref[...]).astype(BF16)
    gate = _sigmoid(_dot(n3, wg_ref[...]))
    out_ref[...] = h2 + gate * _dot(p_ref[...].astype(BF16), wp_ref[...])


def _final(h1, yg, gates, p, tile_offset, g_ple, w_ple_gate, w_ple_proj):
    n_tok = p.shape[0]
    tm = TOKEN_TILE
    return pl.pallas_call(
        _final_kernel,
        grid=(n_tok // tm,),
        in_specs=[pl.BlockSpec((tm, D_MODEL), lambda i: (i + tile_offset, 0)),
                  pl.BlockSpec((TOP_K, tm, D_MODEL // 2), lambda i: (0, i + tile_offset, 0)),
                  pl.BlockSpec((tm, TOP_K), lambda i: (i + tile_offset, 0)),
                  pl.BlockSpec((tm, PLE_DIM), lambda i: (i, 0)),
                  _resident(g_ple.shape), _resident(w_ple_gate.shape), _resident(w_ple_proj.shape)],
        out_specs=pl.BlockSpec((tm, D_MODEL), lambda i: (i, 0)),
        out_shape=jax.ShapeDtypeStruct((n_tok, D_MODEL), F32),
        compiler_params=_params(),
        name="final",
    )(h1, yg, gates, p, g_ple, w_ple_gate, w_ple_proj)


def _routing_tables(idx_t, pos_t, counts, n_slots):
    counts = counts.reshape(N_EXPERTS).astype(jnp.int32)
    pcounts = (counts + EXPERT_BLOCK - 1) // EXPERT_BLOCK * EXPERT_BLOCK
    pends = jnp.cumsum(pcounts)
    pstarts = pends - pcounts
    experts = jnp.arange(N_EXPERTS, dtype=jnp.int32)
    start_t = jnp.sum(jnp.where(idx_t[None] == experts[:, None, None], pstarts[:, None, None], 0), axis=0)
    dest_t = (start_t + pos_t).astype(jnp.int32)
    n_blocks = n_slots // EXPERT_BLOCK
    block_start = jnp.arange(n_blocks, dtype=jnp.int32) * EXPERT_BLOCK
    block_expert = jnp.minimum(jnp.sum(block_start[:, None] >= pends[None, :], axis=1), N_EXPERTS - 1).astype(jnp.int32)
    n_used = (pends[-1] // EXPERT_BLOCK).astype(jnp.int32).reshape(1)
    used = counts > 0
    slot_e = (jnp.cumsum(used.astype(jnp.int32)) - 1) & 1
    later_used = used[None, :] & (experts[None, :] > experts[:, None])
    next_e = jnp.min(jnp.where(later_used, experts[None, :], N_EXPERTS), axis=1)
    next_e = jnp.where(next_e == N_EXPERTS, -1, next_e)
    of_block = block_expert[:, None] == experts[None, :]
    block_slot = jnp.sum(jnp.where(of_block, slot_e[None, :], 0), axis=1).astype(jnp.int32)
    block_next = jnp.sum(jnp.where(of_block, next_e[None, :], 0), axis=1).astype(jnp.int32)
    return dest_t, block_expert, block_slot, block_next, n_used


def kernel(x_prompt, x_sample, cache_kv_w128, cache_kv_w512, cache_kv_w2048, p_prompt, p_sample, g_mix, w_in, g_v, g_q, g_k, w_spatial, b_spatial, w_branch_a, w_branch_b, w_out, g_moe, w_router, b_router, w_gate_up, b_gate_up, w_down, b_down, g_ple, w_ple_gate, w_ple_proj):
    bsz, seq, _ = x_prompt.shape
    dbatch, t_new, _ = x_sample.shape
    assert g_mix.shape[0] == 1
    caches = (cache_kv_w128, cache_kv_w512, cache_kv_w2048)
    l = 0
    n_p, n_s = bsz * seq, dbatch * t_new
    n_tok = n_p + n_s
    assert n_p % TOKEN_TILE == 0 and n_s % TOKEN_TILE == 0

    row2 = lambda t: t.reshape(1, -1)
    w_in_bf = w_in[l].astype(BF16)
    g_q_t = jnp.tile(g_q[l], B_HEADS).reshape(1, B_WIDTH)
    g_k_t = jnp.tile(g_k[l], B_HEADS).reshape(1, B_WIDTH)
    hid = np.arange(MXU_DIM) // HEAD_DIM
    hsum = jnp.asarray(hid[:, None] == hid[None, :], BF16)
    tril = jnp.tril(jnp.ones((CHUNK, CHUNK), bool))
    wm_prompt = jnp.where(tril[None], w_spatial[l], 0).astype(BF16)
    bias_prompt = jnp.repeat(b_spatial[l].T, A_GROUP_WIDTH, axis=1)
    reps = CHUNK // t_new
    small = jnp.where(tril[None, :t_new, :t_new], w_spatial[l][:, :t_new, :t_new], 0)
    wm_sample = jnp.einsum("ab,gij->gaibj", jnp.eye(reps, dtype=F32), small).reshape(A_GROUPS, CHUNK, CHUNK).astype(BF16)
    bias_sample = jnp.tile(bias_prompt[:t_new], (reps, 1))
    wr_hi = w_router[l].astype(BF16)
    wr_lo = (w_router[l] - wr_hi.astype(F32)).astype(BF16)
    tri = jnp.asarray(np.tril(np.ones((FINISH_TILE, FINISH_TILE), np.float32), -1), BF16)
    consts = (w_branch_a[l].astype(BF16), w_branch_b[l].astype(BF16), w_out[l].astype(BF16), row2(g_moe[l]),
              wr_hi, wr_lo, row2(b_router[l]), tri)
    half = MXU_DIM // 2
    sel_np = np.zeros((MXU_DIM, MXU_DIM), np.float32)
    sel_np[2 * np.arange(half), np.arange(half)] = 1.0
    sel_np[2 * np.arange(half) + 1, half + np.arange(half)] = 1.0
    sel = jnp.asarray(sel_np, BF16)

    proj = functools.partial(_project, g_mix=row2(g_mix[l]), w_in_bf=w_in_bf, g_v=row2(g_v[l]),
                             g_q_t=g_q_t, g_k_t=g_k_t, hsum=hsum)

    xp = x_prompt.reshape(n_p, D_MODEL)
    u_p, va_p, q_p, k_p, v_p, ga_p, gb_p = proj(xp, va_dtype=BF16)
    seq3 = lambda t: t.reshape(bsz, seq, B_WIDTH)
    attn_p = [_attend_prompt(seq3(q_p), seq3(k_p), seq3(v_p), gi, bsz, seq) for gi in range(N_GROUPS_B - 1)]
    o_p = _attend_prompt(seq3(q_p), seq3(k_p), seq3(v_p), N_GROUPS_B - 1, bsz, seq, others=attn_p)
    zero_counts = jnp.zeros((1, N_EXPERTS), F32)
    outs_p, cnt_p = _finish(xp, u_p, va_p, wm_prompt, bias_prompt, o_p, ga_p, gb_p, consts, zero_counts,
                            n_tok, 0, None)

    xs = x_sample.reshape(n_s, D_MODEL)
    u_s, va_s, q_s, k_s, v_s, ga_s, gb_s = proj(xs, va_dtype=F32)
    tok3 = lambda t: t.reshape(dbatch, t_new, B_WIDTH)
    caches_t = [jnp.transpose(c[l], (0, 2, 3, 4, 1)) for c in caches]
    o_s = _attend_sample(tok3(q_s), tok3(k_s), tok3(v_s), caches_t, dbatch, t_new)
    (h1, n2, idx_t, gates, pos_t), counts = _finish(xs, u_s, va_s, wm_sample, bias_sample, o_s, ga_s, gb_s,
                                                  consts, cnt_p, n_tok, n_p // FINISH_TILE, outs_p)

    n_blocks = -(-n_tok * TOP_K // EXPERT_BLOCK) + N_EXPERTS
    n_slots = n_blocks * EXPERT_BLOCK
    dest_t, block_expert, block_slot, block_next, n_used = _routing_tables(idx_t, pos_t, counts, n_slots)
    xb = _sc_scatter_rows(n2, dest_t, n_slots)
    yb = _experts(xb, block_expert, block_slot, block_next, n_used, w_gate_up[l], w_down[l],
                  b_gate_up[l][:, None, 0::2], b_gate_up[l][:, None, 1::2], b_down[l][:, None, :], sel)
    yg = _sc_gather_rows(yb, dest_t.reshape(-1)).reshape(TOP_K, n_tok, D_MODEL // 2)
    fin = functools.partial(_final, g_ple=row2(g_ple[l]), w_ple_gate=w_ple_gate[l].astype(BF16),
                            w_ple_proj=w_ple_proj[l].astype(BF16))
    y_prompt = fin(h1, yg, gates, p_prompt[l].reshape(n_p, PLE_DIM), 0).reshape(bsz, seq, D_MODEL)
    y_sample = fin(h1, yg, gates, p_sample[l].reshape(n_s, PLE_DIM), n_p // TOKEN_TILE).reshape(dbatch, t_new, D_MODEL)

    kv_prompt = []
    for gi in range(N_GROUPS_B):
        t = _kv_tail(seq3(k_p), seq3(v_p), gi, bsz, seq)
        keep = t.shape[-1]
        kv_prompt.append(jnp.transpose(t.reshape(bsz, 2, H_SLOT, HEAD_DIM, keep), (0, 4, 1, 2, 3))[None])
    kvs = _kv_sample(k_s, v_s, dbatch, t_new)
    kv_sample = [jnp.transpose(kvs[gi].reshape(t_new, 2, H_SLOT, HEAD_DIM, dbatch), (4, 0, 1, 2, 3))[None]
                 for gi in range(N_GROUPS_B)]
    va_out = va_s.reshape(1, dbatch, t_new, A_WIDTH)
    return (y_prompt, y_sample, *kv_prompt, *kv_sample, va_out)
```
